```python
import jax, jax.numpy as jnp
from jax import lax
import numpy as np

D_MODEL = 1024
BATCH = 16
SEQ = 2048
DEPTH = 1
DEC_BATCH = 8
DEC_SEQ = 2048
PAST_LEN = 128

A_HEADS = 8
A_KV_HEADS = 2
A_GROUP = A_HEADS // A_KV_HEADS
A_HEAD_DIM = 64
A_WIDTH = A_HEADS * A_HEAD_DIM
A_KV_WIDTH = A_KV_HEADS * A_HEAD_DIM
WINDOW = 128
BLOCK = 128
ROPE_DIM = A_HEAD_DIM // 4
ROPE_THETA = 500000.0

M_HEADS = 4
M_HEAD_DIM = 128
M_WIDTH = M_HEADS * M_HEAD_DIM
M_CHUNK = 128
CONV_W = 3

OFF_AQ = 0
OFF_AK = OFF_AQ + A_WIDTH
OFF_AV = OFF_AK + A_KV_WIDTH
OFF_MQ = OFF_AV + A_KV_WIDTH
OFF_MK = OFF_MQ + M_WIDTH
OFF_MV = OFF_MK + M_WIDTH
OFF_MO = OFF_MV + M_WIDTH
OFF_MG = OFF_MO + M_WIDTH
OFF_BG = OFF_MG + 4 * M_HEADS
IN_TOTAL = OFF_BG + 2 * D_MODEL

N_GROUPS = 4
EXPERTS_PER_GROUP = 4
N_EXPERTS = N_GROUPS * EXPERTS_PER_GROUP
TOP_K = 2
D_EXPERT = 512

EPS = 1e-6
NEG_BIG = -1e30
F32 = jnp.float32

kernel_name = "hybrid_bidir_swa_mlstm_hmoe_encoder"


def rms_norm(x, g):
    xf = x.astype(F32)
    y = xf * lax.rsqrt(jnp.mean(xf * xf, axis=-1, keepdims=True) + EPS)
    return (y * g.astype(F32)).astype(x.dtype)


def partial_rope(x, pos):
    half = ROPE_DIM // 2
    inv_freq = 1.0 / (ROPE_THETA ** (jnp.arange(half, dtype=F32) * 2.0 / ROPE_DIM))
    ang = pos[:, None] * inv_freq[None, :]
    cos = jnp.cos(ang)[None, :, None, :]
    sin = jnp.sin(ang)[None, :, None, :]
    xr = x[..., :ROPE_DIM].astype(F32)
    x1, x2 = xr[..., :half], xr[..., half:]
    rot = jnp.concatenate([x1 * cos - x2 * sin, x2 * cos + x1 * sin], axis=-1).astype(x.dtype)
    return jnp.concatenate([rot, x[..., ROPE_DIM:]], axis=-1)


def windowed_gqa(q, k, v, sink):
    B, S = q.shape[0], q.shape[1]
    nb = S // BLOCK
    qb = q.reshape(B, nb, BLOCK, A_KV_HEADS, A_GROUP, A_HEAD_DIM)

    def band(t):
        tp = jnp.pad(t, ((0, 0), (BLOCK, BLOCK), (0, 0), (0, 0)))
        tb = tp.reshape(B, nb + 2, BLOCK, A_KV_HEADS, A_HEAD_DIM)
        return jnp.concatenate([tb[:, :-2], tb[:, 1:-1], tb[:, 2:]], axis=2)

    kb, vb = band(k), band(v)
    s = jnp.einsum('bnqhgd,bnkhd->bnhgqk', qb, kb).astype(F32) * (A_HEAD_DIM ** -0.5)
    blk = jnp.arange(nb)[:, None]
    qpos = blk * BLOCK + jnp.arange(BLOCK)[None, :]
    kpos = (blk - 1) * BLOCK + jnp.arange(3 * BLOCK)[None, :]
    rel = kpos[:, None, :] - qpos[:, :, None]
    valid = (jnp.abs(rel) <= WINDOW) & (kpos[:, None, :] >= 0) & (kpos[:, None, :] < S)
    s = jnp.where(valid[None, :, None, None], s, NEG_BIG)
    sk = sink.astype(F32).reshape(1, 1, A_KV_HEADS, A_GROUP, 1, 1)
    m = jnp.maximum(jnp.max(s, axis=-1, keepdims=True), sk)
    p = jnp.exp(s - m)
    p = p / (jnp.sum(p, axis=-1, keepdims=True) + jnp.exp(sk - m))
    o = jnp.einsum('bnhgqk,bnkhd->bnqhgd', p.astype(v.dtype), vb)
    return o.reshape(B, S, A_WIDTH)


def centred_dwconv(u, w):
    S = u.shape[1]
    pad = CONV_W // 2
    up = jnp.pad(u, ((0, 0), (pad, pad), (0, 0)))
    out = up[:, 0:S] * w[0]
    for j in range(1, CONV_W):
        out = out + up[:, j:j + S] * w[j]
    return out


def mlstm_direction(q, k, v, i_pre, logf):
    B, H, S, dh = q.shape
    nc = S // M_CHUNK

    def to_chunks(t):
        return jnp.moveaxis(t.reshape(B, H, nc, M_CHUNK, *t.shape[3:]), 2, 0)

    xs = tuple(to_chunks(t) for t in (q, k, v, i_pre, logf))
    lower = jnp.tril(jnp.ones((M_CHUNK, M_CHUNK), dtype=bool))

    def step(carry, inp):
        C, n, m = carry
        qc, kc, vc, ic, fc = inp
        b = jnp.cumsum(fc, axis=-1)
        log_w = jnp.where(lower, b[..., :, None] - b[..., None, :] + ic[..., None, :], -jnp.inf)
        log_inter = b + m[..., None]
        m_t = jnp.maximum(log_inter, jnp.max(log_w, axis=-1))
        w_inter = jnp.exp(log_inter - m_t)
        a = jnp.einsum('bhtd,bhsd->bhts', qc, kc) * jnp.exp(log_w - m_t[..., None])
        num = w_inter[..., None] * jnp.einsum('bhtd,bhed->bhte', qc, C) + jnp.einsum('bhts,bhse->bhte', a, vc)
        den = w_inter * jnp.einsum('bhtd,bhd->bht', qc, n) + jnp.sum(a, axis=-1)
        h = num / jnp.maximum(jnp.abs(den), jnp.exp(-m_t))[..., None]
        b_last = b[..., -1]
        log_g = b_last[..., None] - b + ic
        m_new = jnp.maximum(b_last + m, jnp.max(log_g, axis=-1))
        decay = jnp.exp(b_last + m - m_new)
        wk = jnp.exp(log_g - m_new[..., None])
        C_new = decay[..., None, None] * C + jnp.einsum('bhse,bhsd->bhed', wk[..., None] * vc, kc)
        n_new = decay[..., None] * n + jnp.einsum('bhs,bhsd->bhd', wk, kc)
        return (C_new, n_new, m_new), h

    init = (jnp.zeros((B, H, dh, dh), F32), jnp.zeros((B, H, dh), F32), jnp.zeros((B, H), F32))
    _, hs = lax.scan(step, init, xs)
    return jnp.moveaxis(hs, 0, 2).reshape(B, H, S, dh)


def hier_moe(h, rg_w, rg_b, re_w, re_b, w_gate, w_up, w_down):
    B, S, D = h.shape
    t = h.reshape(-1, D)
    g_logits = (t @ rg_w).astype(F32) + rg_b.astype(F32)
    g_prob = jax.nn.softmax(g_logits, axis=-1)
    grp = jnp.argmax(g_logits, axis=-1)
    p_grp = jnp.take_along_axis(g_prob, grp[:, None], axis=-1)
    e_logits = ((t @ re_w).astype(F32) + re_b.astype(F32)).reshape(-1, N_GROUPS, EXPERTS_PER_GROUP)
    e_in_grp = jnp.take_along_axis(e_logits, grp[:, None, None], axis=1)[:, 0]
    top_v, top_i = lax.top_k(e_in_grp, TOP_K)
    top_w = jax.nn.softmax(top_v, axis=-1) * p_grp
    eid = grp[:, None] * EXPERTS_PER_GROUP + top_i
    dense_w = jnp.sum(jax.nn.one_hot(eid, N_EXPERTS, dtype=F32) * top_w[..., None], axis=1).astype(t.dtype)
    out = jnp.zeros_like(t)
    for e in range(N_EXPERTS):
        he = jax.nn.silu(t @ w_gate[e]) * (t @ w_up[e])
        out = out + dense_w[:, e:e + 1] * (he @ w_down[e])
    return out.reshape(B, S, D)


def encoder_layer(x, c, ada_w, ada_b, norm1_g, w_in, conv_w, m_gate_b, attn_sink, head_norm_g,
                  w_up_attn, w_up_mlstm, w_out, norm2_g, rg_w, rg_b, re_w, re_b, w_gate, w_up, w_down):
    B, S, _ = x.shape
    mod = jax.nn.silu(c) @ ada_w + ada_b
    sh1, sc1, g1, sh2, sc2, g2 = jnp.split(mod[:, None, :], 6, axis=-1)

    h = rms_norm(x, norm1_g) * (1 + sc1) + sh1
    proj = h @ w_in
    pos = jnp.arange(S, dtype=F32)

    q = partial_rope(proj[..., OFF_AQ:OFF_AK].reshape(B, S, A_HEADS, A_HEAD_DIM), pos)
    k = partial_rope(proj[..., OFF_AK:OFF_AV].reshape(B, S, A_KV_HEADS, A_HEAD_DIM), pos)
    v = proj[..., OFF_AV:OFF_MQ].reshape(B, S, A_KV_HEADS, A_HEAD_DIM)
    attn = windowed_gqa(q, k, v, attn_sink)

    qk = jax.nn.silu(centred_dwconv(proj[..., OFF_MQ:OFF_MV], conv_w))

    def heads(t):
        return t.reshape(B, S, M_HEADS, M_HEAD_DIM).transpose(0, 2, 1, 3).astype(F32)

    mq = heads(qk[..., :M_WIDTH])
    mk = heads(qk[..., M_WIDTH:]) * (M_HEAD_DIM ** -0.5)
    mv = heads(proj[..., OFF_MV:OFF_MO])
    o_gate = jax.nn.sigmoid(proj[..., OFF_MO:OFF_MG])
    gates = (proj[..., OFF_MG:OFF_BG].astype(F32) + m_gate_b.astype(F32)).transpose(0, 2, 1)
    i_fw, f_fw, i_bw, f_bw = jnp.split(gates, 4, axis=1)
    h_fw = mlstm_direction(mq, mk, mv, i_fw, jax.nn.log_sigmoid(f_fw))
    fl = lambda t: jnp.flip(t, axis=2)
    h_bw = fl(mlstm_direction(fl(mq), fl(mk), fl(mv), fl(i_bw), fl(jax.nn.log_sigmoid(f_bw))))
    hm = (h_fw + h_bw).transpose(0, 2, 1, 3)
    hm = rms_norm(hm, head_norm_g.reshape(M_HEADS, M_HEAD_DIM)).reshape(B, S, M_WIDTH).astype(x.dtype)
    mlstm_out = o_gate * hm

    bg = jax.nn.sigmoid(proj[..., OFF_BG:])
    merged = bg[..., :D_MODEL] * (attn @ w_up_attn) + bg[..., D_MODEL:] * (mlstm_out @ w_up_mlstm)
    x = x + g1 * (merged @ w_out)

    h2 = rms_norm(x, norm2_g) * (1 + sc2) + sh2
    x = x + g2 * hier_moe(h2, rg_w, rg_b, re_w, re_b, w_gate, w_up, w_down)
    return x


def setup_inputs(seed: int = 0) -> dict:
    key = jax.random.key(seed)
    ks = jax.random.split(key, 32)
    D = D_MODEL

    def nrm(k, shape, scale):
        return jax.random.normal(k, shape, F32) * scale

    gate_b = jnp.concatenate([
        nrm(ks[9], (DEPTH, M_HEADS), 0.1),
        3.0 + 3.0 * jax.random.uniform(ks[10], (DEPTH, M_HEADS), F32),
        nrm(ks[11], (DEPTH, M_HEADS), 0.1),
        3.0 + 3.0 * jax.random.uniform(ks[12], (DEPTH, M_HEADS), F32),
    ], axis=-1)
    return {
        'x_prompt': nrm(ks[0], (BATCH, SEQ, D), 1.0),
        'x_sample': nrm(ks[1], (DEC_BATCH, DEC_SEQ, D), 1.0),
        'c_prompt': nrm(ks[2], (BATCH, D), 1.0),
        'c_sample': nrm(ks[3], (DEC_BATCH, D), 1.0),
        'ada_w': nrm(ks[4], (DEPTH, D, 6 * D), 0.5 * D ** -0.5),
        'ada_b': nrm(ks[5], (DEPTH, 6 * D), 0.02),
        'norm1_g': 1.0 + nrm(ks[6], (DEPTH, D), 0.02),
        'w_in': nrm(ks[7], (DEPTH, D, IN_TOTAL), D ** -0.5),
        'conv_w': nrm(ks[8], (DEPTH, CONV_W, 2 * M_WIDTH), CONV_W ** -0.5),
        'm_gate_b': gate_b,
        'attn_sink': nrm(ks[13], (DEPTH, A_HEADS), 0.5),
        'head_norm_g': 1.0 + nrm(ks[14], (DEPTH, M_WIDTH), 0.02),
        'w_up_attn': nrm(ks[15], (DEPTH, A_WIDTH, D), A_WIDTH ** -0.5),
        'w_up_mlstm': nrm(ks[16], (DEPTH, M_WIDTH, D), M_WIDTH ** -0.5),
        'w_out': nrm(ks[17], (DEPTH, D, D), D ** -0.5),
        'norm2_g': 1.0 + nrm(ks[18], (DEPTH, D), 0.02),
        'rg_w': nrm(ks[19], (DEPTH, D, N_GROUPS), D ** -0.5),
        'rg_b': nrm(ks[20], (DEPTH, N_GROUPS), 0.01),
        're_w': nrm(ks[21], (DEPTH, D, N_EXPERTS), D ** -0.5),
        're_b': nrm(ks[22], (DEPTH, N_EXPERTS), 0.01),
        'w_gate': nrm(ks[23], (DEPTH, N_EXPERTS, D, D_EXPERT), D ** -0.5),
        'w_up': nrm(ks[24], (DEPTH, N_EXPERTS, D, D_EXPERT), D ** -0.5),
        'w_down': nrm(ks[25], (DEPTH, N_EXPERTS, D_EXPERT, D), D_EXPERT ** -0.5),
        'final_norm_g': 1.0 + nrm(ks[26], (D,), 0.02),
    }


def reference(x_prompt, x_sample, c_prompt, c_sample, ada_w, ada_b, norm1_g, w_in, conv_w, m_gate_b,
              attn_sink, head_norm_g, w_up_attn, w_up_mlstm, w_out, norm2_g, rg_w, rg_b, re_w, re_b,
              w_gate, w_up, w_down, final_norm_g):
    layer_params = (ada_w, ada_b, norm1_g, w_in, conv_w, m_gate_b, attn_sink, head_norm_g,
                    w_up_attn, w_up_mlstm, w_out, norm2_g, rg_w, rg_b, re_w, re_b, w_gate, w_up, w_down)

    def trunk(x, c):
        for l in range(DEPTH):
            x = encoder_layer(x, c, *[p[l] for p in layer_params])
        return rms_norm(x, final_norm_g)

    y_prompt = trunk(x_prompt, c_prompt)
    y_sample = trunk(x_sample, c_sample)
    return (y_prompt, y_sample)
```

```python
import functools

import jax
import jax.numpy as jnp
from jax import lax
from jax.experimental import pallas as pl
from jax.experimental.pallas import tpu as pltpu

D_MODEL = 1024
A_HEADS = 8
A_KV_HEADS = 2
A_GROUP = A_HEADS // A_KV_HEADS
A_HEAD_DIM = 64
A_WIDTH = A_HEADS * A_HEAD_DIM
A_KV_WIDTH = A_KV_HEADS * A_HEAD_DIM
WINDOW = 128
BLOCK = 128
ROPE_DIM = A_HEAD_DIM // 4
ROPE_THETA = 500000.0
M_HEADS = 4
M_HEAD_DIM = 128
M_WIDTH = M_HEADS * M_HEAD_DIM
M_CHUNK = 128
CONV_W = 3
OFF_AQ = 0
OFF_AK = OFF_AQ + A_WIDTH
OFF_AV = OFF_AK + A_KV_WIDTH
OFF_MQ = OFF_AV + A_KV_WIDTH
OFF_MK = OFF_MQ + M_WIDTH
OFF_MV = OFF_MK + M_WIDTH
OFF_MO = OFF_MV + M_WIDTH
OFF_MG = OFF_MO + M_WIDTH
N_MGATES = 4 * M_HEADS
OFF_BG = OFF_MG + N_MGATES
IN_TOTAL = OFF_BG + 2 * D_MODEL
N_GROUPS = 4
EXPERTS_PER_GROUP = 4
N_EXPERTS = N_GROUPS * EXPERTS_PER_GROUP
D_EXPERT = 512
EPS = 1e-6
NEG_BIG = -1e30
F32 = jnp.float32
BF16 = jnp.bfloat16

LANES = 128
SUBLANES = 8
VMEM_LIMIT_BYTES = 56 * 1024 * 1024

MAIN_AQ = 0
MAIN_KV = A_WIDTH
MAIN_MQ = MAIN_KV + 2 * A_KV_WIDTH
MAIN_MK = MAIN_MQ + M_WIDTH
MAIN_MV = MAIN_MK + M_WIDTH
MAIN_MO = MAIN_MV + M_WIDTH
MAIN_BG = MAIN_MO + M_WIDTH
MAIN_TOTAL = MAIN_BG + 2 * D_MODEL

ROUTER_G_LANE = N_EXPERTS


def _sigmoid(z):
    return 1.0 / (1.0 + jnp.exp(-z))


def _log_sigmoid(z):
    return jnp.minimum(z, 0.0) - jnp.log(1.0 + jnp.exp(-jnp.abs(z)))


def _is_forget(gate_idx):
    return jnp.bitwise_and(jnp.right_shift(gate_idx, 2), 1) == 1


def _dot(a, b):
    return jnp.dot(a, b, preferred_element_type=F32)


def _dot_nt(a, b):
    return lax.dot_general(a, b, (((1,), (1,)), ((), ())), preferred_element_type=F32)


def _dot_tn(a, b):
    return lax.dot_general(a, b, (((0,), (0,)), ((), ())), preferred_element_type=F32)


def _dot_f32(a, b):
    return jnp.dot(a, b, preferred_element_type=F32, precision=lax.Precision.HIGHEST)


def _compiler_params(semantics):
    return pltpu.CompilerParams(dimension_semantics=semantics, vmem_limit_bytes=VMEM_LIMIT_BYTES)


def _ada_kernel(c_ref, w_ref, b_ref, o_ref):
    c = c_ref[...]
    o_ref[...] = _dot_f32(c * _sigmoid(c), w_ref[...]) + b_ref[...]


def _ada_mod(c, ada_w, ada_b):
    B, D = c.shape
    n = ada_w.shape[1] // D
    return pl.pallas_call(
        _ada_kernel,
        grid=(n,),
        in_specs=[
            pl.BlockSpec((B, D), lambda j: (0, 0)),
            pl.BlockSpec((D, D), lambda j: (0, j)),
            pl.BlockSpec((1, D), lambda j: (0, j)),
        ],
        out_specs=pl.BlockSpec((B, D), lambda j: (0, j)),
        out_shape=jax.ShapeDtypeStruct((B, n * D), F32),
        compiler_params=_compiler_params(("arbitrary",)),
        name="ada_mod",
    )(c, ada_w, ada_b.reshape(1, -1))


def _rms_mod(x, g, scale, shift):
    ms = jnp.mean(x * x, axis=-1, keepdims=True)
    return (x * lax.rsqrt(ms + EPS) * g) * (1.0 + scale) + shift


def _rope_block(xb, cos, sin_lo, sin_hi):
    half = ROPE_DIM // 2
    return xb * cos + pltpu.roll(xb, LANES - half, axis=1) * sin_lo + pltpu.roll(xb, half, axis=1) * sin_hi


def _in_proj_kernel(x_ref, mod_ref, g_ref, w_ref, wg_ref, wgt_ref, gb_row_ref, gb_col_ref,
                    cos_ref, slo_ref, shi_ref,
                    q_ref, kx_ref, vx_ref, mq_ref, mk_ref, mv_ref, og_ref, bg_ref, gcol_ref, grow_ref):
    x = x_ref[0]
    tm = x.shape[0]
    h = _rms_mod(x, g_ref[...], mod_ref[0, 1:2, :], mod_ref[0, 0:1, :])
    hb = h.astype(BF16)
    cos, slo, shi = cos_ref[...], slo_ref[...], shi_ref[...]
    lane = lax.broadcasted_iota(jnp.int32, (tm, LANES), 1)
    left = lane < A_HEAD_DIM

    pj = _dot(hb, w_ref[:, MAIN_AQ:MAIN_AQ + A_WIDTH])
    for j in range(A_WIDTH // LANES):
        blk = _rope_block(pj[:, j * LANES:(j + 1) * LANES], cos, slo, shi)
        q_ref[0, :, j * LANES:(j + 1) * LANES] = (blk * (A_HEAD_DIM ** -0.5)).astype(BF16)

    pj = _dot(hb, w_ref[:, MAIN_KV:MAIN_KV + 2 * A_KV_WIDTH])
    kk = _rope_block(pj[:, 0:LANES], cos, slo, shi)
    vv = pj[:, LANES:2 * LANES]
    for src, dst in ((kk, kx_ref), (vv, vx_ref)):
        swapped = pltpu.roll(src, A_HEAD_DIM, axis=1)
        zero = jnp.zeros_like(src)
        dst[0, :, 0 * LANES:1 * LANES] = jnp.where(left, src, zero).astype(BF16)
        dst[0, :, 1 * LANES:2 * LANES] = jnp.where(left, zero, swapped).astype(BF16)
        dst[0, :, 2 * LANES:3 * LANES] = jnp.where(left, swapped, zero).astype(BF16)
        dst[0, :, 3 * LANES:4 * LANES] = jnp.where(left, zero, src).astype(BF16)

    mq_ref[0] = _dot(hb, w_ref[:, MAIN_MQ:MAIN_MQ + M_WIDTH]).astype(BF16)
    mk_ref[0] = _dot(hb, w_ref[:, MAIN_MK:MAIN_MK + M_WIDTH]).astype(BF16)
    mv_ref[0] = _dot(hb, w_ref[:, MAIN_MV:MAIN_MV + M_WIDTH]).astype(BF16)
    og_ref[0] = _sigmoid(_dot(hb, w_ref[:, MAIN_MO:MAIN_MO + M_WIDTH])).astype(BF16)
    for j in range(2 * D_MODEL // M_WIDTH):
        lo = MAIN_BG + j * M_WIDTH
        bg_ref[0, :, j * M_WIDTH:(j + 1) * M_WIDTH] = _sigmoid(_dot(hb, w_ref[:, lo:lo + M_WIDTH])).astype(BF16)

    gc = _dot(hb, wg_ref[...]) + gb_row_ref[...]
    gidx = lax.broadcasted_iota(jnp.int32, gc.shape, 1)
    gcol_ref[0] = jnp.where(_is_forget(gidx), _log_sigmoid(gc), gc)
    gr = _dot_nt(wgt_ref[...], hb) + gb_col_ref[...]
    ridx = lax.broadcasted_iota(jnp.int32, gr.shape, 0)
    gr = jnp.where(_is_forget(ridx), _log_sigmoid(gr), gr)
    for c in range(tm // M_CHUNK):
        grow_ref[0, c] = gr[:, c * M_CHUNK:(c + 1) * M_CHUNK]


def _in_proj(x, mod, norm_g, w_main, w_g, w_gt, gate_b, rope_tabs, tm):
    B, S, D = x.shape
    nt = S // tm
    cos, slo, shi = rope_tabs
    tok = lambda w: pl.BlockSpec((1, tm, w), lambda b, i: (b, i, 0))
    const2 = lambda a: pl.BlockSpec(a.shape, lambda b, i: (0, 0))
    tab = pl.BlockSpec((tm, LANES), lambda b, i: (i, 0))
    out_shapes = (
        jax.ShapeDtypeStruct((B, S, A_WIDTH), BF16),
        jax.ShapeDtypeStruct((B, S, 4 * LANES), BF16),
        jax.ShapeDtypeStruct((B, S, 4 * LANES), BF16),
        jax.ShapeDtypeStruct((B, S, M_WIDTH), BF16),
        jax.ShapeDtypeStruct((B, S, M_WIDTH), BF16),
        jax.ShapeDtypeStruct((B, S, M_WIDTH), BF16),
        jax.ShapeDtypeStruct((B, S, M_WIDTH), BF16),
        jax.ShapeDtypeStruct((B, S, 2 * D_MODEL), BF16),
        jax.ShapeDtypeStruct((B, S, N_MGATES), F32),
        jax.ShapeDtypeStruct((B, S // M_CHUNK, N_MGATES, M_CHUNK), F32),
    )
    out_specs = (
        tok(A_WIDTH), tok(4 * LANES), tok(4 * LANES), tok(M_WIDTH), tok(M_WIDTH), tok(M_WIDTH), tok(M_WIDTH),
        tok(2 * D_MODEL), tok(N_MGATES),
        pl.BlockSpec((1, tm // M_CHUNK, N_MGATES, M_CHUNK), lambda b, i: (b, i, 0, 0)),
    )
    gb_row = gate_b.reshape(1, N_MGATES)
    gb_col = gate_b.reshape(N_MGATES, 1)
    g2 = norm_g.reshape(1, D)
    return pl.pallas_call(
        _in_proj_kernel,
        grid=(B, nt),
        in_specs=[
            tok(D),
            pl.BlockSpec((1, 6, D), lambda b, i: (b, 0, 0)),
            const2(g2), const2(w_main), const2(w_g), const2(w_gt), const2(gb_row), const2(gb_col),
            tab, tab, tab,
        ],
        out_specs=out_specs,
        out_shape=out_shapes,
        compiler_params=_compiler_params(("parallel", "parallel")),
        name="in_proj",
    )(x, mod, g2, w_main, w_g, w_gt, gb_row, gb_col, cos, slo, shi)


def _attn_kernel(sink_ref, q_ref, kx_ref, vx_ref, o_ref):
    S = q_ref.shape[1]
    nb = S // BLOCK
    kw = 3 * BLOCK
    qi = lax.broadcasted_iota(jnp.int32, (BLOCK, kw), 0)
    ki = lax.broadcasted_iota(jnp.int32, (BLOCK, kw), 1)
    rel0 = ki - qi

    def block(n, carry):
        q0 = pl.multiple_of(n * BLOCK, BLOCK)
        k0 = pl.multiple_of(jnp.clip((n - 1) * BLOCK, 0, S - kw), BLOCK)
        rel = rel0 + (k0 - q0)
        valid = jnp.abs(rel) <= WINDOW
        for hk in range(A_KV_HEADS):
            k_l = kx_ref[0, pl.ds(k0, kw), (2 * hk) * LANES:(2 * hk + 1) * LANES]
            k_r = kx_ref[0, pl.ds(k0, kw), (2 * hk + 1) * LANES:(2 * hk + 2) * LANES]
            v_l = vx_ref[0, pl.ds(k0, kw), (2 * hk) * LANES:(2 * hk + 1) * LANES]
            v_r = vx_ref[0, pl.ds(k0, kw), (2 * hk + 1) * LANES:(2 * hk + 2) * LANES]
            for j in range(A_GROUP // 2):
                col = (hk * (A_GROUP // 2) + j) * LANES
                qp = q_ref[0, pl.ds(q0, BLOCK), col:col + LANES]
                acc = None
                for side, (kk, vv) in enumerate(((k_l, v_l), (k_r, v_r))):
                    sk = sink_ref[hk * A_GROUP + 2 * j + side]
                    s = jnp.where(valid, _dot_nt(qp, kk), NEG_BIG)
                    m = jnp.maximum(jnp.max(s, axis=-1, keepdims=True), sk)
                    p = jnp.exp(s - m)
                    den = jnp.sum(p, axis=-1, keepdims=True) + jnp.exp(sk - m)
                    o = _dot((p / den).astype(BF16), vv)
                    acc = o if acc is None else acc + o
                o_ref[0, pl.ds(q0, BLOCK), col:col + LANES] = acc.astype(BF16)
        return carry

    lax.fori_loop(0, nb, block, 0)


def _attention(q, kx, vx, sink):
    B, S, _ = q.shape
    seq = lambda w: pl.BlockSpec((1, S, w), lambda b: (b, 0, 0))
    return pl.pallas_call(
        _attn_kernel,
        grid=(B,),
        in_specs=[pl.BlockSpec(memory_space=pltpu.SMEM), seq(A_WIDTH), seq(4 * LANES), seq(4 * LANES)],
        out_specs=seq(A_WIDTH),
        out_shape=jax.ShapeDtypeStruct((B, S, A_WIDTH), BF16),
        compiler_params=_compiler_params(("parallel",)),
        name="window_attn",
    )(sink, q, kx, vx)


def _conv_silu(u_ref, w_ref, pad_ref, dst_ref, scale):
    S = u_ref.shape[1]
    pad_ref[0:SUBLANES, :] = jnp.zeros((SUBLANES, LANES), F32)
    pad_ref[S + SUBLANES:S + 2 * SUBLANES, :] = jnp.zeros((SUBLANES, LANES), F32)
    pad_ref[SUBLANES:S + SUBLANES, :] = u_ref[0].astype(F32)
    w0, w1, w2 = w_ref[0:1, :], w_ref[1:2, :], w_ref[2:3, :]
    for c in range(S // M_CHUNK):
        base = SUBLANES + c * M_CHUNK
        y = (pad_ref[base - 1:base - 1 + M_CHUNK, :] * w0 + pad_ref[base:base + M_CHUNK, :] * w1
             + pad_ref[base + 1:base + 1 + M_CHUNK, :] * w2)
        y = y * _sigmoid(y)
        dst_ref[c * M_CHUNK:(c + 1) * M_CHUNK, :] = y * scale


def _mlstm_chunk(qf, kf, vf, icol, irow, bcol, brow, blast, tri, ct_ref, n_ref, m_ref):
    m = m_ref[0:1, 0:1]
    ct = ct_ref[...]
    nrow = n_ref[0:1, :]
    qb, kb = qf.astype(BF16), kf.astype(BF16)
    log_w = bcol - brow + irow
    log_inter = bcol + m
    m_t = jnp.maximum(log_inter, jnp.max(jnp.where(tri, log_w, NEG_BIG), axis=-1, keepdims=True))
    w_inter = jnp.exp(log_inter - m_t)
    a = _dot_nt(qb, kb) * jnp.where(tri, jnp.exp(log_w - m_t), 0.0)
    num = w_inter * _dot(qb, ct.astype(BF16)) + _dot(a.astype(BF16), vf.astype(BF16))
    den = w_inter * jnp.sum(qf * nrow, axis=-1, keepdims=True) + jnp.sum(a, axis=-1, keepdims=True)
    h = num / jnp.maximum(jnp.abs(den), jnp.exp(-m_t))
    log_g = blast - bcol + icol
    m_new = jnp.maximum(blast + m, jnp.max(log_g, axis=0, keepdims=True))
    decay = jnp.exp(blast + m - m_new)
    wk = jnp.exp(log_g - m_new)
    ct_ref[...] = decay * ct + _dot_tn(kb, (wk * vf).astype(BF16))
    n_ref[0:1, :] = decay * nrow + jnp.sum(wk * kf, axis=0, keepdims=True)
    m_ref[0:1, 0:1] = m_new
    return h


def _mlstm_kernel(mq_ref, mk_ref, mv_ref, og_ref, gcol_ref, grow_ref, cwq_ref, cwk_ref, hg_ref, o_ref,
                  pad_ref, qs_ref, ks_ref, hacc_ref, ct_ref, n_ref, m_ref):
    S = mq_ref.shape[1]
    nc = S // M_CHUNK
    L = M_CHUNK
    _conv_silu(mq_ref, cwq_ref, pad_ref, qs_ref, 1.0)
    _conv_silu(mk_ref, cwk_ref, pad_ref, ks_ref, M_HEAD_DIM ** -0.5)
    ct_ref[...] = jnp.zeros(ct_ref.shape, F32)
    n_ref[...] = jnp.zeros(n_ref.shape, F32)
    m_ref[...] = jnp.zeros(m_ref.shape, F32)

    ti = lax.broadcasted_iota(jnp.int32, (L, L), 0)
    si = lax.broadcasted_iota(jnp.int32, (L, L), 1)
    lower = si <= ti
    upper = si >= ti
    lower_f = lower.astype(F32)
    upper_f = upper.astype(F32)

    def direction(d, c):
        r0 = pl.multiple_of(c * L, L)
        gc = gcol_ref[0, 0, pl.ds(r0, L), :]
        gr = grow_ref[0, 0, c]
        if d == 0:
            tri, tri_f, tri_ft = lower, lower_f, upper_f
        else:
            tri, tri_f, tri_ft = upper, upper_f, lower_f
        bcol = _dot_f32(tri_f, gc)[:, 2 * d + 1:2 * d + 2]
        brow = _dot_f32(gr, tri_ft)[2 * d + 1:2 * d + 2, :]
        blast = brow[:, L - 1:L] if d == 0 else brow[:, 0:1]
        icol = gc[:, 2 * d:2 * d + 1]
        irow = gr[2 * d:2 * d + 1, :]
        qf = qs_ref[pl.ds(r0, L), :]
        kf = ks_ref[pl.ds(r0, L), :]
        vf = mv_ref[0, pl.ds(r0, L), :].astype(F32)
        return _mlstm_chunk(qf, kf, vf, icol, irow, bcol, brow, blast, tri,
                            ct_ref.at[d], n_ref.at[d], m_ref.at[d])

    def finalize(c, h):
        r0 = pl.multiple_of(c * L, L)
        y = h * lax.rsqrt(jnp.mean(h * h, axis=-1, keepdims=True) + EPS) * hg_ref[...]
        o_ref[0, pl.ds(r0, L), :] = (og_ref[0, pl.ds(r0, L), :].astype(F32) * y).astype(BF16)

    def first_half(c, carry):
        for d, cc in ((0, c), (1, nc - 1 - c)):
            h = direction(d, cc)
            hacc_ref[pl.ds(pl.multiple_of(cc * L, L), L), :] = h
        return carry

    def second_half(c, carry):
        for d, cc in ((0, c), (1, nc - 1 - c)):
            h = direction(d, cc)
            finalize(cc, h + hacc_ref[pl.ds(pl.multiple_of(cc * L, L), L), :])
        return carry

    lax.fori_loop(0, nc // 2, first_half, 0)
    lax.fori_loop(nc // 2, nc, second_half, 0)


def _mlstm(mq, mk, mv, og, gcol_h, grow_h, conv_wq, conv_wk, head_g):
    B, S, _ = mq.shape
    nc = S // M_CHUNK
    head = pl.BlockSpec((1, S, M_HEAD_DIM), lambda b, h: (b, 0, h))
    cw = pl.BlockSpec((CONV_W, M_HEAD_DIM), lambda b, h: (0, h))
    return pl.pallas_call(
        _mlstm_kernel,
        grid=(B, M_HEADS),
        in_specs=[
            head, head, head, head,
            pl.BlockSpec((1, 1, S, 4), lambda b, h: (b, h, 0, 0)),
            pl.BlockSpec((1, 1, nc, SUBLANES, M_CHUNK), lambda b, h: (b, h, 0, 0, 0)),
            cw, cw,
            pl.BlockSpec((1, M_HEAD_DIM), lambda b, h: (0, h)),
        ],
        out_specs=head,
        out_shape=jax.ShapeDtypeStruct((B, S, M_WIDTH), BF16),
        scratch_shapes=[
            pltpu.VMEM((S + 2 * SUBLANES, LANES), F32),
            pltpu.VMEM((S, M_HEAD_DIM), F32),
            pltpu.VMEM((S, M_HEAD_DIM), F32),
            pltpu.VMEM((S, M_HEAD_DIM), F32),
            pltpu.VMEM((2, M_HEAD_DIM, M_HEAD_DIM), F32),
            pltpu.VMEM((2, SUBLANES, M_HEAD_DIM), F32),
            pltpu.VMEM((2, SUBLANES, LANES), F32),
        ],
        compiler_params=_compiler_params(("parallel", "parallel")),
        name="mlstm",
    )(mq, mk, mv, og, gcol_h, grow_h, conv_wq, conv_wk, head_g)


def _route(logits):
    lane = lax.broadcasted_iota(jnp.int32, logits.shape, 1)
    lane_f = lane.astype(F32)
    big = float(LANES)
    is_g = (lane >= ROUTER_G_LANE) & (lane < ROUTER_G_LANE + N_GROUPS)
    gl = jnp.where(is_g, logits, NEG_BIG)
    gmax = jnp.max(gl, axis=-1, keepdims=True)
    gsum = jnp.sum(jnp.where(is_g, jnp.exp(gl - gmax), 0.0), axis=-1, keepdims=True)
    p_grp = 1.0 / gsum
    grp = jnp.min(jnp.where(is_g & (gl == gmax), lane_f - ROUTER_G_LANE, big), axis=-1, keepdims=True)
    in_grp = (lane < N_EXPERTS) & (jnp.right_shift(lane, 2).astype(F32) == grp)
    el = jnp.where(in_grp, logits, NEG_BIG)
    v1 = jnp.max(el, axis=-1, keepdims=True)
    i1 = jnp.min(jnp.where(in_grp & (el == v1), lane_f, big), axis=-1, keepdims=True)
    rest = in_grp & (lane_f != i1)
    el2 = jnp.where(rest, logits, NEG_BIG)
    v2 = jnp.max(el2, axis=-1, keepdims=True)
    i2 = jnp.min(jnp.where(rest & (el2 == v2), lane_f, big), axis=-1, keepdims=True)
    e21 = jnp.exp(v2 - v1)
    w1 = p_grp / (1.0 + e21)
    w2 = p_grp * e21 / (1.0 + e21)
    return jnp.where(lane_f == i1, w1, 0.0) + jnp.where(lane_f == i2, w2, 0.0)


def _merge_kernel(attn_ref, mo_ref, bg_ref, x_ref, mod_ref, wua_ref, wum_ref, wo_ref, g2_ref, wr_ref, br_ref,
                  x1_ref, h2_ref, dw_ref):
    up_a = _dot(attn_ref[0], wua_ref[...])
    up_m = _dot(mo_ref[0], wum_ref[...])
    merged = (bg_ref[0, :, 0:D_MODEL].astype(F32) * up_a + bg_ref[0, :, D_MODEL:2 * D_MODEL].astype(F32) * up_m)
    x1 = x_ref[0] + mod_ref[0, 2:3, :] * _dot(merged.astype(BF16), wo_ref[...])
    x1_ref[0] = x1
    h2 = _rms_mod(x1, g2_ref[...], mod_ref[0, 4:5, :], mod_ref[0, 3:4, :])
    h2_ref[0] = h2.astype(BF16)
    logits = _dot_f32(h2, wr_ref[...]) + br_ref[...]
    dw_ref[0] = _route(logits)[:, 0:N_EXPERTS]


def _merge(attn, mo, bg, x, mod, wua, wum, wo, norm_g, w_router, b_router, tm):
    B, S, D = x.shape
    tok = lambda w: pl.BlockSpec((1, tm, w), lambda b, i: (b, i, 0))
    const2 = lambda a: pl.BlockSpec(a.shape, lambda b, i: (0, 0))
    g2 = norm_g.reshape(1, D)
    return pl.pallas_call(
        _merge_kernel,
        grid=(B, S // tm),
        in_specs=[
            tok(A_WIDTH), tok(M_WIDTH), tok(2 * D), tok(D),
            pl.BlockSpec((1, 6, D), lambda b, i: (b, 0, 0)),
            const2(wua), const2(wum), const2(wo), const2(g2), const2(w_router), const2(b_router),
        ],
        out_specs=(tok(D), tok(D), tok(N_EXPERTS)),
        out_shape=(
            jax.ShapeDtypeStruct((B, S, D), F32),
            jax.ShapeDtypeStruct((B, S, D), BF16),
            jax.ShapeDtypeStruct((B, S, N_EXPERTS), F32),
        ),
        compiler_params=_compiler_params(("parallel", "parallel")),
        name="merge_route",
    )(attn, mo, bg, x, mod, wua, wum, wo, g2, w_router, b_router)


def _moe_kernel(t_ref, dw_ref, x1_ref, mod_ref, wg_ref, wu_ref, wd_ref, fg_ref, o_ref, acc_ref):
    e = pl.program_id(2)

    @pl.when(e == 0)
    def _():
        acc_ref[...] = jnp.zeros(acc_ref.shape, F32)

    t = t_ref[0]
    g = _dot(t, wg_ref[0])
    u = _dot(t, wu_ref[0])
    he = (g * _sigmoid(g)) * u
    ye = _dot(he.astype(BF16), wd_ref[0])
    dw = dw_ref[0]
    lane = lax.broadcasted_iota(jnp.int32, dw.shape, 1)
    col = jnp.sum(jnp.where(lane == e, dw, 0.0), axis=-1, keepdims=True)
    acc_ref[...] += col * ye

    @pl.when(e == pl.num_programs(2) - 1)
    def _():
        xo = x1_ref[0] + mod_ref[0, 5:6, :] * acc_ref[...]
        ms = jnp.mean(xo * xo, axis=-1, keepdims=True)
        o_ref[0] = xo * lax.rsqrt(ms + EPS) * fg_ref[...]


def _moe(t, dw, x1, mod, wg, wu, wd, final_g, tm):
    B, S, D = x1.shape
    tok = lambda w: pl.BlockSpec((1, tm, w), lambda b, i, e: (b, i, 0))
    fg = final_g.reshape(1, D)
    return pl.pallas_call(
        _moe_kernel,
        grid=(B, S // tm, N_EXPERTS),
        in_specs=[
            tok(D), tok(N_EXPERTS), tok(D),
            pl.BlockSpec((1, 6, D), lambda b, i, e: (b, 0, 0)),
            pl.BlockSpec((1, D, D_EXPERT), lambda b, i, e: (e, 0, 0)),
            pl.BlockSpec((1, D, D_EXPERT), lambda b, i, e: (e, 0, 0)),
            pl.BlockSpec((1, D_EXPERT, D), lambda b, i, e: (e, 0, 0)),
            pl.BlockSpec((1, D), lambda b, i, e: (0, 0)),
        ],
        out_specs=tok(D),
        out_shape=jax.ShapeDtypeStruct((B, S, D), F32),
        scratch_shapes=[pltpu.VMEM((tm, D), F32)],
        compiler_params=_compiler_params(("parallel", "parallel", "arbitrary")),
        name="moe",
    )(t, dw, x1, mod, wg, wu, wd, fg)


def _rope_tables(S):
    half = ROPE_DIM // 2
    inv_freq = 1.0 / (ROPE_THETA ** (jnp.arange(half, dtype=F32) * 2.0 / ROPE_DIM))
    ang = jnp.arange(S, dtype=F32)[:, None] * inv_freq[None, :]
    cos, sin = jnp.cos(ang), jnp.sin(ang)
    zeros = jnp.zeros((S, A_HEAD_DIM - ROPE_DIM), F32)
    z8 = jnp.zeros((S, half), F32)
    cos_h = jnp.concatenate([cos, cos, jnp.ones_like(zeros)], axis=-1)
    slo_h = jnp.concatenate([-sin, z8, zeros], axis=-1)
    shi_h = jnp.concatenate([z8, sin, zeros], axis=-1)
    rep = LANES // A_HEAD_DIM
    return tuple(jnp.tile(t, (1, rep)) for t in (cos_h, slo_h, shi_h))


def _layer(x, c, p, tabs):
    B, S, D = x.shape
    mod = _ada_mod(c, p["ada_w"], p["ada_b"]).reshape(B, 6, D)
    q, kx, vx, mq, mk, mv, og, bg, gcol, grow = _in_proj(
        x, mod, p["norm1_g"], p["w_main"], p["w_g"], p["w_gt"], p["m_gate_b"], tabs, tm=512)
    attn = _attention(q, kx, vx, p["attn_sink"])
    nc = S // M_CHUNK
    gcol_h = gcol.reshape(B, S, 4, M_HEADS).transpose(0, 3, 1, 2)
    grow_h = grow.reshape(B, nc, 4, M_HEADS, M_CHUNK).transpose(0, 3, 1, 2, 4)
    grow_h = jnp.pad(grow_h, ((0, 0), (0, 0), (0, 0), (0, SUBLANES - 4), (0, 0)))
    mo = _mlstm(mq, mk, mv, og, gcol_h, grow_h, p["conv_wq"], p["conv_wk"], p["head_norm_g"])
    x1, h2, dw = _merge(attn, mo, bg, x, mod, p["w_up_attn"], p["w_up_mlstm"], p["w_out"], p["norm2_g"],
                        p["w_router"], p["b_router"], tm=512)
    return _moe(h2, dw, x1, mod, p["w_gate"], p["w_up"], p["w_down"], p["final_norm_g"], tm=min(1024, S))


def kernel(x_prompt, x_sample, c_prompt, c_sample, ada_w, ada_b, norm1_g, w_in, conv_w, m_gate_b, attn_sink,
           head_norm_g, w_up_attn, w_up_mlstm, w_out, norm2_g, rg_w, rg_b, re_w, re_b, w_gate, w_up, w_down,
           final_norm_g):
    assert ada_w.shape[0] == 1, "single-layer trunk"
    w_in0 = w_in[0]
    w_g = w_in0[:, OFF_MG:OFF_BG]
    pad = LANES - N_EXPERTS - N_GROUPS
    p = dict(
        ada_w=ada_w[0], ada_b=ada_b[0], norm1_g=norm1_g[0],
        w_main=jnp.concatenate([w_in0[:, :OFF_MG], w_in0[:, OFF_BG:]], axis=1).astype(BF16),
        w_g=w_g.astype(BF16), w_gt=w_g.T.astype(BF16), m_gate_b=m_gate_b[0], attn_sink=attn_sink[0],
        conv_wq=conv_w[0, :, :M_WIDTH], conv_wk=conv_w[0, :, M_WIDTH:],
        head_norm_g=head_norm_g[0].reshape(1, M_WIDTH),
        w_up_attn=w_up_attn[0].astype(BF16), w_up_mlstm=w_up_mlstm[0].astype(BF16), w_out=w_out[0].astype(BF16),
        norm2_g=norm2_g[0],
        w_router=jnp.pad(jnp.concatenate([re_w[0], rg_w[0]], axis=1), ((0, 0), (0, pad))),
        b_router=jnp.pad(jnp.concatenate([re_b[0], rg_b[0]]), (0, pad)).reshape(1, LANES),
        w_gate=w_gate[0].astype(BF16), w_up=w_up[0].astype(BF16), w_down=w_down[0].astype(BF16),
        final_norm_g=final_norm_g,
    )
    tabs = _rope_tables(x_prompt.shape[1])
    return (_layer(x_prompt, c_prompt, p, tabs), _layer(x_sample, c_sample, p, tabs))
```

```python
import functools

import jax
import jax.numpy as jnp
from jax import lax
from jax.experimental import pallas as pl
from jax.experimental.pallas import tpu as pltpu

D_MODEL = 1024
A_HEADS = 8
A_KV_HEADS = 2
A_GROUP = A_HEADS // A_KV_HEADS
A_HEAD_DIM = 64
A_WIDTH = A_HEADS * A_HEAD_DIM
A_KV_WIDTH = A_KV_HEADS * A_HEAD_DIM
WINDOW = 128
BLOCK = 128
ROPE_DIM = A_HEAD_DIM // 4
ROPE_THETA = 500000.0
M_HEADS = 4
M_HEAD_DIM = 128
M_WIDTH = M_HEADS * M_HEAD_DIM
M_CHUNK = 128
CONV_W = 3
OFF_AQ = 0
OFF_AK = OFF_AQ + A_WIDTH
OFF_AV = OFF_AK + A_KV_WIDTH
OFF_MQ = OFF_AV + A_KV_WIDTH
OFF_MK = OFF_MQ + M_WIDTH
OFF_MV = OFF_MK + M_WIDTH
OFF_MO = OFF_MV + M_WIDTH
OFF_MG = OFF_MO + M_WIDTH
N_MGATES = 4 * M_HEADS
OFF_BG = OFF_MG + N_MGATES
IN_TOTAL = OFF_BG + 2 * D_MODEL
N_GROUPS = 4
EXPERTS_PER_GROUP = 4
N_EXPERTS = N_GROUPS * EXPERTS_PER_GROUP
D_EXPERT = 512
EPS = 1e-6
NEG_BIG = -1e30
F32 = jnp.float32
BF16 = jnp.bfloat16

LANES = 128
SUBLANES = 8
VMEM_LIMIT_BYTES = 56 * 1024 * 1024

MAIN_AQ = 0
MAIN_KV = A_WIDTH
MAIN_MQ = MAIN_KV + 2 * A_KV_WIDTH
MAIN_MK = MAIN_MQ + M_WIDTH
MAIN_MO = MAIN_MK + M_WIDTH
MAIN_BG = MAIN_MO + M_WIDTH
MAIN_TOTAL = MAIN_BG + 2 * D_MODEL

ROUTER_G_LANE = N_EXPERTS


def _sigmoid(z):
    return 1.0 / (1.0 + jnp.exp(-z))


def _log_sigmoid(z):
    return jnp.minimum(z, 0.0) - jnp.log(1.0 + jnp.exp(-jnp.abs(z)))


def _is_forget(gate_idx):
    return jnp.bitwise_and(jnp.right_shift(gate_idx, 2), 1) == 1


def _dot(a, b):
    return jnp.dot(a, b, preferred_element_type=F32)


def _dot_nt(a, b):
    return lax.dot_general(a, b, (((1,), (1,)), ((), ())), preferred_element_type=F32)


def _dot_tn(a, b):
    return lax.dot_general(a, b, (((0,), (0,)), ((), ())), preferred_element_type=F32)


def _dot_f32(a, b):
    return jnp.dot(a, b, preferred_element_type=F32, precision=lax.Precision.HIGHEST)


def _compiler_params(semantics):
    return pltpu.CompilerParams(dimension_semantics=semantics, vmem_limit_bytes=VMEM_LIMIT_BYTES)


def _ada_kernel(c_ref, w_ref, b_ref, o_ref):
    c = c_ref[...]
    o_ref[...] = _dot_f32(c * _sigmoid(c), w_ref[...]) + b_ref[...]


def _ada_mod(c, ada_w, ada_b):
    B, D = c.shape
    n = ada_w.shape[1] // D
    return pl.pallas_call(
        _ada_kernel,
        grid=(n,),
        in_specs=[
            pl.BlockSpec((B, D), lambda j: (0, 0)),
            pl.BlockSpec((D, D), lambda j: (0, j)),
            pl.BlockSpec((1, D), lambda j: (0, j)),
        ],
        out_specs=pl.BlockSpec((B, D), lambda j: (0, j)),
        out_shape=jax.ShapeDtypeStruct((B, n * D), F32),
        compiler_params=_compiler_params(("arbitrary",)),
        name="ada_mod",
    )(c, ada_w, ada_b.reshape(1, -1))


def _rms_mod(x, g, scale, shift):
    ms = jnp.mean(x * x, axis=-1, keepdims=True)
    return (x * lax.rsqrt(ms + EPS) * g) * (1.0 + scale) + shift


def _rope_block(xb, cos, sin_lo, sin_hi):
    half = ROPE_DIM // 2
    return xb * cos + pltpu.roll(xb, LANES - half, axis=1) * sin_lo + pltpu.roll(xb, half, axis=1) * sin_hi


def _in_proj_kernel(x_ref, mod_ref, g_ref, w_ref, wt_ref, gb_col_ref, cos_ref, slo_ref, shi_ref,
                    q_ref, kx_ref, vx_ref, mq_ref, mk_ref, vt_ref, og_ref, bg_ref, grow_ref):
    x = x_ref[0]
    tm = x.shape[0]
    h = _rms_mod(x, g_ref[...], mod_ref[0, 1:2, :], mod_ref[0, 0:1, :])
    hb = h.astype(BF16)
    cos, slo, shi = cos_ref[...], slo_ref[...], shi_ref[...]
    lane = lax.broadcasted_iota(jnp.int32, (tm, LANES), 1)
    left = lane < A_HEAD_DIM

    pj = _dot(hb, w_ref[:, MAIN_AQ:MAIN_AQ + A_WIDTH])
    for j in range(A_WIDTH // LANES):
        blk = _rope_block(pj[:, j * LANES:(j + 1) * LANES], cos, slo, shi)
        q_ref[0, :, j * LANES:(j + 1) * LANES] = (blk * (A_HEAD_DIM ** -0.5)).astype(BF16)

    pj = _dot(hb, w_ref[:, MAIN_KV:MAIN_KV + 2 * A_KV_WIDTH])
    kk = _rope_block(pj[:, 0:LANES], cos, slo, shi)
    vv = pj[:, LANES:2 * LANES]
    for src, dst in ((kk, kx_ref), (vv, vx_ref)):
        swapped = pltpu.roll(src, A_HEAD_DIM, axis=1)
        zero = jnp.zeros_like(src)
        dst[0, :, 0 * LANES:1 * LANES] = jnp.where(left, src, zero).astype(BF16)
        dst[0, :, 1 * LANES:2 * LANES] = jnp.where(left, zero, swapped).astype(BF16)
        dst[0, :, 2 * LANES:3 * LANES] = jnp.where(left, swapped, zero).astype(BF16)
        dst[0, :, 3 * LANES:4 * LANES] = jnp.where(left, zero, src).astype(BF16)

    mq_ref[0] = _dot(hb, w_ref[:, MAIN_MQ:MAIN_MQ + M_WIDTH]).astype(BF16)
    mk_ref[0] = _dot(hb, w_ref[:, MAIN_MK:MAIN_MK + M_WIDTH]).astype(BF16)
    og_ref[0] = _sigmoid(_dot(hb, w_ref[:, MAIN_MO:MAIN_MO + M_WIDTH])).astype(BF16)
    for j in range(2 * D_MODEL // M_WIDTH):
        lo = MAIN_BG + j * M_WIDTH
        bg_ref[0, :, j * M_WIDTH:(j + 1) * M_WIDTH] = _sigmoid(_dot(hb, w_ref[:, lo:lo + M_WIDTH])).astype(BF16)

    tr = _dot_nt(wt_ref[...], hb)
    gr = tr[M_WIDTH:M_WIDTH + N_MGATES, :] + gb_col_ref[...]
    ridx = lax.broadcasted_iota(jnp.int32, gr.shape, 0)
    gr = jnp.where(_is_forget(ridx), _log_sigmoid(gr), gr)
    vt = tr[0:M_WIDTH, :].astype(BF16)
    for c in range(tm // M_CHUNK):
        grow_ref[0, c] = gr[:, c * M_CHUNK:(c + 1) * M_CHUNK]
        vt_ref[0, c] = vt[:, c * M_CHUNK:(c + 1) * M_CHUNK]


def _in_proj(x, mod, norm_g, w_main, w_t, gate_b, rope_tabs, tm):
    B, S, D = x.shape
    nt = S // tm
    cos, slo, shi = rope_tabs
    tok = lambda w: pl.BlockSpec((1, tm, w), lambda b, i: (b, i, 0))
    const2 = lambda a: pl.BlockSpec(a.shape, lambda b, i: (0, 0))
    tab = pl.BlockSpec((tm, LANES), lambda b, i: (i, 0))
    out_shapes = (
        jax.ShapeDtypeStruct((B, S, A_WIDTH), BF16),
        jax.ShapeDtypeStruct((B, S, 4 * LANES), BF16),
        jax.ShapeDtypeStruct((B, S, 4 * LANES), BF16),
        jax.ShapeDtypeStruct((B, S, M_WIDTH), BF16),
        jax.ShapeDtypeStruct((B, S, M_WIDTH), BF16),
        jax.ShapeDtypeStruct((B, S // M_CHUNK, M_WIDTH, M_CHUNK), BF16),
        jax.ShapeDtypeStruct((B, S, M_WIDTH), BF16),
        jax.ShapeDtypeStruct((B, S, 2 * D_MODEL), BF16),
        jax.ShapeDtypeStruct((B, S // M_CHUNK, N_MGATES, M_CHUNK), F32),
    )
    chunked = lambda rows: pl.BlockSpec((1, tm // M_CHUNK, rows, M_CHUNK), lambda b, i: (b, i, 0, 0))
    out_specs = (
        tok(A_WIDTH), tok(4 * LANES), tok(4 * LANES), tok(M_WIDTH), tok(M_WIDTH), chunked(M_WIDTH), tok(M_WIDTH),
        tok(2 * D_MODEL), chunked(N_MGATES),
    )
    gb_col = gate_b.reshape(N_MGATES, 1)
    g2 = norm_g.reshape(1, D)
    return pl.pallas_call(
        _in_proj_kernel,
        grid=(B, nt),
        in_specs=[
            tok(D),
            pl.BlockSpec((1, 6, D), lambda b, i: (b, 0, 0)),
            const2(g2), const2(w_main), const2(w_t), const2(gb_col),
            tab, tab, tab,
        ],
        out_specs=out_specs,
        out_shape=out_shapes,
        compiler_params=_compiler_params(("parallel", "parallel")),
        name="in_proj",
    )(x, mod, g2, w_main, w_t, gb_col, cos, slo, shi)


def _attn_kernel(sink_ref, q_ref, kx_ref, vx_ref, o_ref):
    S = q_ref.shape[1]
    nb = S // BLOCK
    kw = 3 * BLOCK
    qi = lax.broadcasted_iota(jnp.int32, (BLOCK, kw), 0)
    ki = lax.broadcasted_iota(jnp.int32, (BLOCK, kw), 1)
    rel0 = ki - qi

    def block(n, carry):
        q0 = pl.multiple_of(n * BLOCK, BLOCK)
        k0 = pl.multiple_of(jnp.clip((n - 1) * BLOCK, 0, S - kw), BLOCK)
        rel = rel0 + (k0 - q0)
        valid = jnp.abs(rel) <= WINDOW
        for hk in range(A_KV_HEADS):
            k_l = kx_ref[0, pl.ds(k0, kw), (2 * hk) * LANES:(2 * hk + 1) * LANES]
            k_r = kx_ref[0, pl.ds(k0, kw), (2 * hk + 1) * LANES:(2 * hk + 2) * LANES]
            v_l = vx_ref[0, pl.ds(k0, kw), (2 * hk) * LANES:(2 * hk + 1) * LANES]
            v_r = vx_ref[0, pl.ds(k0, kw), (2 * hk + 1) * LANES:(2 * hk + 2) * LANES]
            for j in range(A_GROUP // 2):
                col = (hk * (A_GROUP // 2) + j) * LANES
                qp = q_ref[0, pl.ds(q0, BLOCK), col:col + LANES]
                acc = None
                for side, (kk, vv) in enumerate(((k_l, v_l), (k_r, v_r))):
                    sk = sink_ref[hk * A_GROUP + 2 * j + side]
                    s = jnp.where(valid, _dot_nt(qp, kk), NEG_BIG)
                    m = jnp.maximum(jnp.max(s, axis=-1, keepdims=True), sk)
                    p = jnp.exp(s - m)
                    den = jnp.sum(p, axis=-1, keepdims=True) + jnp.exp(sk - m)
                    o = _dot((p / den).astype(BF16), vv)
                    acc = o if acc is None else acc + o
                o_ref[0, pl.ds(q0, BLOCK), col:col + LANES] = acc.astype(BF16)
        return carry

    lax.fori_loop(0, nb, block, 0)


def _attention(q, kx, vx, sink):
    B, S, _ = q.shape
    seq = lambda w: pl.BlockSpec((1, S, w), lambda b: (b, 0, 0))
    return pl.pallas_call(
        _attn_kernel,
        grid=(B,),
        in_specs=[pl.BlockSpec(memory_space=pltpu.SMEM), seq(A_WIDTH), seq(4 * LANES), seq(4 * LANES)],
        out_specs=seq(A_WIDTH),
        out_shape=jax.ShapeDtypeStruct((B, S, A_WIDTH), BF16),
        compiler_params=_compiler_params(("parallel",)),
        name="window_attn",
    )(sink, q, kx, vx)


def _conv_silu(u_ref, w_ref, pad_ref, dst_ref, scale):
    S = u_ref.shape[1]
    pad_ref[0:SUBLANES, :] = jnp.zeros((SUBLANES, LANES), F32)
    pad_ref[S + SUBLANES:S + 2 * SUBLANES, :] = jnp.zeros((SUBLANES, LANES), F32)
    pad_ref[SUBLANES:S + SUBLANES, :] = u_ref[0].astype(F32)
    w0, w1, w2 = w_ref[0:1, :], w_ref[1:2, :], w_ref[2:3, :]
    for c in range(S // M_CHUNK):
        base = SUBLANES + c * M_CHUNK
        y = (pad_ref[base - 1:base - 1 + M_CHUNK, :] * w0 + pad_ref[base:base + M_CHUNK, :] * w1
             + pad_ref[base + 1:base + 1 + M_CHUNK, :] * w2)
        y = y * _sigmoid(y)
        dst_ref[c * M_CHUNK:(c + 1) * M_CHUNK, :] = y * scale


def _split3(x):
    hi = x.astype(BF16)
    r = x - hi.astype(F32)
    mid = r.astype(BF16)
    lo = (r - mid.astype(F32)).astype(BF16)
    return [hi, mid, lo]


def _mlstm_kernel(mq_ref, mk_ref, vt_ref, og_ref, grow_ref, cwq_ref, cwk_ref, hgt_ref, o_ref,
                  pad_ref, qs_ref, ks_ref, rb_ref, ib_ref, rows_ref, cinc_ref, ninc_ref, cin_ref, sin_ref,
                  c_ref, n_ref):
    S = mq_ref.shape[1]
    nc = S // M_CHUNK
    L = M_CHUNK
    DH = M_HEAD_DIM
    _conv_silu(mq_ref, cwq_ref, pad_ref, qs_ref, 1.0)
    _conv_silu(mk_ref, cwk_ref, pad_ref, ks_ref, M_HEAD_DIM ** -0.5)

    s_i = lax.broadcasted_iota(jnp.int32, (L, L), 0)
    t_i = lax.broadcasted_iota(jnp.int32, (L, L), 1)
    tris = (s_i <= t_i, s_i >= t_i)
    eye = s_i == t_i
    row8 = lax.broadcasted_iota(jnp.int32, (SUBLANES, L), 0)
    one_if = lambda cond: jnp.where(cond, 1.0, 0.0)
    k_j = lax.broadcasted_iota(jnp.int32, (3 * L, 2 * L), 0) % L
    c_j = lax.broadcasted_iota(jnp.int32, (3 * L, 2 * L), 1)
    sum_rows = jnp.where(c_j < L, one_if(k_j <= c_j), one_if(k_j >= c_j - L)).astype(BF16)

    g_all = grow_ref[0, 0].reshape(nc * SUBLANES, L)
    rb_ref[...] = _dot(jnp.concatenate(_split3(g_all), axis=1), sum_rows).reshape(nc, SUBLANES, 2 * L)
    ones_b = jnp.ones((3 * L, L), BF16)

    def phase_a(c, carry):
        r0 = pl.multiple_of(c * L, L)
        gr = grow_ref[0, 0, c]
        rb = rb_ref[c]
        kb = ks_ref[pl.ds(r0, L), :].astype(BF16)
        vt = vt_ref[0, c].astype(F32)
        wvs, wks, diags = [], [], []
        for d in range(2):
            brow = rb[2 * d + 1:2 * d + 2, d * L:(d + 1) * L]
            blast = brow[:, L - 1:L] if d == 0 else brow[:, 0:1]
            ibr = gr[2 * d:2 * d + 1, :] - brow
            log_g = blast + ibr
            mg = jnp.max(log_g, axis=-1, keepdims=True)
            wk = jnp.exp(log_g - mg)
            wvs.append((vt * wk).astype(BF16))
            wks.append(wk)
            diags.append(jnp.concatenate(
                [jnp.where(eye, term.astype(F32), 0.0).astype(BF16) for term in _split3(ibr)], axis=1))
            rows_ref[c, 2 + d:3 + d, :] = brow
            rows_ref[c, 4 + d:5 + d, :] = jnp.broadcast_to(mg, (1, L))
            rows_ref[c, 6 + d:7 + d, :] = jnp.broadcast_to(blast, (1, L))
        ib = _dot(jnp.concatenate(diags, axis=0), ones_b)
        ib_ref[c] = ib
        for d in range(2):
            rows_ref[c, d:d + 1, :] = jnp.max(jnp.where(tris[d], ib[d * L:(d + 1) * L, :], NEG_BIG),
                                              axis=0, keepdims=True)
        cinc_ref[c] = _dot(jnp.concatenate(wvs, axis=0), kb)
        wk8 = jnp.where(row8 == 0, wks[0], jnp.where(row8 == 1, wks[1], 0.0))
        ninc_ref[c] = _dot(wk8.astype(BF16), kb)
        return carry

    def phase_b(j, ms):
        new_ms = []
        for d, cc in ((0, j), (1, nc - 1 - j)):
            m = ms[d]
            half = slice(d * DH, (d + 1) * DH)
            cst = c_ref[half, :]
            n = n_ref[d:d + 1, :]
            cin_ref[cc, half, :] = cst.astype(BF16)
            sin_ref[cc, d:d + 1, :] = n
            sin_ref[cc, 2 + d:3 + d, :] = m
            mg = rows_ref[cc, 4 + d:5 + d, :]
            blast = rows_ref[cc, 6 + d:7 + d, :]
            m_new = jnp.maximum(blast + m, mg)
            decay = jnp.exp(blast + m - m_new)
            grow = jnp.exp(mg - m_new)
            c_ref[half, :] = decay * cst + grow * cinc_ref[cc, half, :]
            n_ref[d:d + 1, :] = decay * n + grow * ninc_ref[cc, d:d + 1, :]
            new_ms.append(m_new)
        return tuple(new_ms)

    def phase_c(c, carry):
        r0 = pl.multiple_of(c * L, L)
        qb = qs_ref[pl.ds(r0, L), :].astype(BF16)
        sin = sin_ref[c]
        rows = rows_ref[c]
        qk_t = _dot_nt(ks_ref[pl.ds(r0, L), :].astype(BF16), qb)
        qc_t = _dot_nt(cin_ref[c], qb)
        qn = _dot_nt(sin.astype(BF16), qb)
        ats, stats = [], []
        for d in range(2):
            m_in = sin[2 + d:3 + d, :]
            cm = jnp.maximum(m_in, rows[d:d + 1, :])
            a_t = qk_t * jnp.where(tris[d], jnp.exp(ib_ref[c, d * L:(d + 1) * L, :] - cm), 0.0)
            w_inter = jnp.exp(m_in - cm)
            den = w_inter * qn[d:d + 1, :] + jnp.sum(a_t, axis=0, keepdims=True)
            m_t = rows[2 + d:3 + d, :] + cm
            ats.append(a_t.astype(BF16))
            stats.append((w_inter, jnp.maximum(jnp.abs(den), jnp.exp(-m_t))))
        av_t = _dot(vt_ref[0, c], jnp.concatenate(ats, axis=1))
        h_t = None
        for d in range(2):
            w_inter, den = stats[d]
            hd = (w_inter * qc_t[d * DH:(d + 1) * DH, :] + av_t[:, d * L:(d + 1) * L]) / den
            h_t = hd if h_t is None else h_t + hd
        y_t = h_t * lax.rsqrt(jnp.mean(h_t * h_t, axis=0, keepdims=True) + EPS) * hgt_ref[...]
        o_ref[0, pl.ds(r0, L), :] = (og_ref[0, pl.ds(r0, L), :].astype(F32) * y_t.T).astype(BF16)
        return carry

    lax.fori_loop(0, nc, phase_a, 0, unroll=4)
    c_ref[...] = jnp.zeros(c_ref.shape, F32)
    n_ref[...] = jnp.zeros(n_ref.shape, F32)
    m0 = jnp.zeros((1, LANES), F32)
    lax.fori_loop(0, nc, phase_b, (m0, m0))
    lax.fori_loop(0, nc, phase_c, 0, unroll=8)


def _mlstm(mq, mk, vt, og, grow_h, conv_wq, conv_wk, head_g_t):
    B, S, _ = mq.shape
    nc = S // M_CHUNK
    head = pl.BlockSpec((1, S, M_HEAD_DIM), lambda b, h: (b, 0, h))
    cw = pl.BlockSpec((CONV_W, M_HEAD_DIM), lambda b, h: (0, h))
    return pl.pallas_call(
        _mlstm_kernel,
        grid=(B, M_HEADS),
        in_specs=[
            head, head,
            pl.BlockSpec((1, nc, M_HEAD_DIM, M_CHUNK), lambda b, h: (b, 0, h, 0)),
            head,
            pl.BlockSpec((1, 1, nc, SUBLANES, M_CHUNK), lambda b, h: (b, h, 0, 0, 0)),
            cw, cw,
            pl.BlockSpec((M_HEAD_DIM, LANES), lambda b, h: (h, 0)),
        ],
        out_specs=head,
        out_shape=jax.ShapeDtypeStruct((B, S, M_WIDTH), BF16),
        scratch_shapes=[
            pltpu.VMEM((S + 2 * SUBLANES, LANES), F32),
            pltpu.VMEM((S, M_HEAD_DIM), F32),
            pltpu.VMEM((S, M_HEAD_DIM), F32),
            pltpu.VMEM((nc, SUBLANES, 2 * M_CHUNK), F32),
            pltpu.VMEM((nc, 2 * M_CHUNK, M_CHUNK), F32),
            pltpu.VMEM((nc, SUBLANES, M_CHUNK), F32),
            pltpu.VMEM((nc, 2 * M_HEAD_DIM, M_HEAD_DIM), F32),
            pltpu.VMEM((nc, SUBLANES, M_HEAD_DIM), F32),
            pltpu.VMEM((nc, 2 * M_HEAD_DIM, M_HEAD_DIM), BF16),
            pltpu.VMEM((nc, SUBLANES, M_HEAD_DIM), F32),
            pltpu.VMEM((2 * M_HEAD_DIM, M_HEAD_DIM), F32),
            pltpu.VMEM((SUBLANES, M_HEAD_DIM), F32),
        ],
        compiler_params=_compiler_params(("parallel", "parallel")),
        name="mlstm",
    )(mq, mk, vt, og, grow_h, conv_wq, conv_wk, head_g_t)


def _route(logits):
    lane = lax.broadcasted_iota(jnp.int32, logits.shape, 1)
    lane_f = lane.astype(F32)
    big = float(LANES)
    is_g = (lane >= ROUTER_G_LANE) & (lane < ROUTER_G_LANE + N_GROUPS)
    gl = jnp.where(is_g, logits, NEG_BIG)
    gmax = jnp.max(gl, axis=-1, keepdims=True)
    gsum = jnp.sum(jnp.where(is_g, jnp.exp(gl - gmax), 0.0), axis=-1, keepdims=True)
    p_grp = 1.0 / gsum
    grp = jnp.min(jnp.where(is_g & (gl == gmax), lane_f - ROUTER_G_LANE, big), axis=-1, keepdims=True)
    in_grp = (lane < N_EXPERTS) & (jnp.right_shift(lane, 2).astype(F32) == grp)
    el = jnp.where(in_grp, logits, NEG_BIG)
    v1 = jnp.max(el, axis=-1, keepdims=True)
    i1 = jnp.min(jnp.where(in_grp & (el == v1), lane_f, big), axis=-1, keepdims=True)
    rest = in_grp & (lane_f != i1)
    el2 = jnp.where(rest, logits, NEG_BIG)
    v2 = jnp.max(el2, axis=-1, keepdims=True)
    i2 = jnp.min(jnp.where(rest & (el2 == v2), lane_f, big), axis=-1, keepdims=True)
    e21 = jnp.exp(v2 - v1)
    w1 = p_grp / (1.0 + e21)
    w2 = p_grp * e21 / (1.0 + e21)
    return jnp.where(lane_f == i1, w1, 0.0) + jnp.where(lane_f == i2, w2, 0.0)


def _merge_kernel(attn_ref, mo_ref, bg_ref, x_ref, mod_ref, wua_ref, wum_ref, wo_ref, g2_ref, wr_ref, br_ref,
                  x1_ref, h2_ref, dw_ref):
    up_a = _dot(attn_ref[0], wua_ref[...])
    up_m = _dot(mo_ref[0], wum_ref[...])
    merged = (bg_ref[0, :, 0:D_MODEL].astype(F32) * up_a + bg_ref[0, :, D_MODEL:2 * D_MODEL].astype(F32) * up_m)
    x1 = x_ref[0] + mod_ref[0, 2:3, :] * _dot(merged.astype(BF16), wo_ref[...])
    x1_ref[0] = x1
    h2 = _rms_mod(x1, g2_ref[...], mod_ref[0, 4:5, :], mod_ref[0, 3:4, :])
    h2_ref[0] = h2.astype(BF16)
    logits = _dot_f32(h2, wr_ref[...]) + br_ref[...]
    dw_ref[0] = _route(logits)[:, 0:N_EXPERTS]


def _merge(attn, mo, bg, x, mod, wua, wum, wo, norm_g, w_router, b_router, tm):
    B, S, D = x.shape
    tok = lambda w: pl.BlockSpec((1, tm, w), lambda b, i: (b, i, 0))
    const2 = lambda a: pl.BlockSpec(a.shape, lambda b, i: (0, 0))
    g2 = norm_g.reshape(1, D)
    return pl.pallas_call(
        _merge_kernel,
        grid=(B, S // tm),
        in_specs=[
            tok(A_WIDTH), tok(M_WIDTH), tok(2 * D), tok(D),
            pl.BlockSpec((1, 6, D), lambda b, i: (b, 0, 0)),
            const2(wua), const2(wum), const2(wo), const2(g2), const2(w_router), const2(b_router),
        ],
        out_specs=(tok(D), tok(D), tok(N_EXPERTS)),
        out_shape=(
            jax.ShapeDtypeStruct((B, S, D), F32),
            jax.ShapeDtypeStruct((B, S, D), BF16),
            jax.ShapeDtypeStruct((B, S, N_EXPERTS), F32),
        ),
        compiler_params=_compiler_params(("parallel", "parallel")),
        name="merge_route",
    )(attn, mo, bg, x, mod, wua, wum, wo, g2, w_router, b_router)


def _moe_kernel(t_ref, dw_ref, x1_ref, mod_ref, wg_ref, wu_ref, wd_ref, fg_ref, o_ref, acc_ref):
    e = pl.program_id(2)

    @pl.when(e == 0)
    def _():
        acc_ref[...] = jnp.zeros(acc_ref.shape, F32)

    t = t_ref[0]
    g = _dot(t, wg_ref[0])
    u = _dot(t, wu_ref[0])
    he = (g * _sigmoid(g)) * u
    ye = _dot(he.astype(BF16), wd_ref[0])
    dw = dw_ref[0]
    lane = lax.broadcasted_iota(jnp.int32, dw.shape, 1)
    col = jnp.sum(jnp.where(lane == e, dw, 0.0), axis=-1, keepdims=True)
    acc_ref[...] += col * ye

    @pl.when(e == pl.num_programs(2) - 1)
    def _():
        xo = x1_ref[0] + mod_ref[0, 5:6, :] * acc_ref[...]
        ms = jnp.mean(xo * xo, axis=-1, keepdims=True)
        o_ref[0] = xo * lax.rsqrt(ms + EPS) * fg_ref[...]


def _moe(t, dw, x1, mod, wg, wu, wd, final_g, tm):
    B, S, D = x1.shape
    tok = lambda w: pl.BlockSpec((1, tm, w), lambda b, i, e: (b, i, 0))
    fg = final_g.reshape(1, D)
    return pl.pallas_call(
        _moe_kernel,
        grid=(B, S // tm, N_EXPERTS),
        in_specs=[
            tok(D), tok(N_EXPERTS), tok(D),
            pl.BlockSpec((1, 6, D), lambda b, i, e: (b, 0, 0)),
            pl.BlockSpec((1, D, D_EXPERT), lambda b, i, e: (e, 0, 0)),
            pl.BlockSpec((1, D, D_EXPERT), lambda b, i, e: (e, 0, 0)),
            pl.BlockSpec((1, D_EXPERT, D), lambda b, i, e: (e, 0, 0)),
            pl.BlockSpec((1, D), lambda b, i, e: (0, 0)),
        ],
        out_specs=tok(D),
        out_shape=jax.ShapeDtypeStruct((B, S, D), F32),
        scratch_shapes=[pltpu.VMEM((tm, D), F32)],
        compiler_params=_compiler_params(("parallel", "parallel", "arbitrary")),
        name="moe",
    )(t, dw, x1, mod, wg, wu, wd, fg)


def _rope_tables(S):
    half = ROPE_DIM // 2
    inv_freq = 1.0 / (ROPE_THETA ** (jnp.arange(half, dtype=F32) * 2.0 / ROPE_DIM))
    ang = jnp.arange(S, dtype=F32)[:, None] * inv_freq[None, :]
    cos, sin = jnp.cos(ang), jnp.sin(ang)
    zeros = jnp.zeros((S, A_HEAD_DIM - ROPE_DIM), F32)
    z8 = jnp.zeros((S, half), F32)
    cos_h = jnp.concatenate([cos, cos, jnp.ones_like(zeros)], axis=-1)
    slo_h = jnp.concatenate([-sin, z8, zeros], axis=-1)
    shi_h = jnp.concatenate([z8, sin, zeros], axis=-1)
    rep = LANES // A_HEAD_DIM
    return tuple(jnp.tile(t, (1, rep)) for t in (cos_h, slo_h, shi_h))


def _layer(x, c, p, tabs):
    B, S, D = x.shape
    mod = _ada_mod(c, p["ada_w"], p["ada_b"]).reshape(B, 6, D)
    q, kx, vx, mq, mk, vt, og, bg, grow = _in_proj(
        x, mod, p["norm1_g"], p["w_main"], p["w_t"], p["m_gate_b"], tabs, tm=512)
    attn = _attention(q, kx, vx, p["attn_sink"])
    nc = S // M_CHUNK
    grow_h = grow.reshape(B, nc, 4, M_HEADS, M_CHUNK).transpose(0, 3, 1, 2, 4)
    grow_h = jnp.pad(grow_h, ((0, 0), (0, 0), (0, 0), (0, SUBLANES - 4), (0, 0)))
    mo = _mlstm(mq, mk, vt, og, grow_h, p["conv_wq"], p["conv_wk"], p["head_norm_g_t"])
    x1, h2, dw = _merge(attn, mo, bg, x, mod, p["w_up_attn"], p["w_up_mlstm"], p["w_out"], p["norm2_g"],
                        p["w_router"], p["b_router"], tm=512)
    return _moe(h2, dw, x1, mod, p["w_gate"], p["w_up"], p["w_down"], p["final_norm_g"], tm=min(1024, S))


def kernel(x_prompt, x_sample, c_prompt, c_sample, ada_w, ada_b, norm1_g, w_in, conv_w, m_gate_b, attn_sink,
           head_norm_g, w_up_attn, w_up_mlstm, w_out, norm2_g, rg_w, rg_b, re_w, re_b, w_gate, w_up, w_down,
           final_norm_g):
    assert ada_w.shape[0] == 1, "single-layer trunk"
    w_in0 = w_in[0]
    w_g = w_in0[:, OFF_MG:OFF_BG]
    pad = LANES - N_EXPERTS - N_GROUPS
    p = dict(
        ada_w=ada_w[0], ada_b=ada_b[0], norm1_g=norm1_g[0],
        w_main=jnp.concatenate([w_in0[:, :OFF_MV], w_in0[:, OFF_MO:OFF_MG], w_in0[:, OFF_BG:]], axis=1).astype(BF16),
        w_t=jnp.concatenate([w_in0[:, OFF_MV:OFF_MO], w_g], axis=1).T.astype(BF16),
        m_gate_b=m_gate_b[0], attn_sink=attn_sink[0],
        conv_wq=conv_w[0, :, :M_WIDTH], conv_wk=conv_w[0, :, M_WIDTH:],
        head_norm_g_t=jnp.broadcast_to(head_norm_g[0][:, None], (M_WIDTH, LANES)),
        w_up_attn=w_up_attn[0].astype(BF16), w_up_mlstm=w_up_mlstm[0].astype(BF16), w_out=w_out[0].astype(BF16),
        norm2_g=norm2_g[0],
        w_router=jnp.pad(jnp.concatenate([re_w[0], rg_w[0]], axis=1), ((0, 0), (0, pad))),
        b_router=jnp.pad(jnp.concatenate([re_b[0], rg_b[0]]), (0, pad)).reshape(1, LANES),
        w_gate=w_gate[0].astype(BF16), w_up=w_up[0].astype(BF16), w_down=w_down[0].astype(BF16),
        final_norm_g=final_norm_g,
    )
    tabs = _rope_tables(x_prompt.shape[1])
    return (_layer(x_prompt, c_prompt, p, tabs), _layer(x_sample, c_sample, p, tabs))
```

```python
import functools

import jax
import jax.numpy as jnp
from jax import lax
from jax.experimental import pallas as pl
from jax.experimental.pallas import tpu as pltpu
from jax.experimental.pallas import tpu_sc as plsc

D_MODEL = 1024
A_HEADS = 8
A_KV_HEADS = 2
A_GROUP = A_HEADS // A_KV_HEADS
A_HEAD_DIM = 64
A_WIDTH = A_HEADS * A_HEAD_DIM
A_KV_WIDTH = A_KV_HEADS * A_HEAD_DIM
WINDOW = 128
BLOCK = 128
ROPE_DIM = A_HEAD_DIM // 4
ROPE_THETA = 500000.0
M_HEADS = 4
M_HEAD_DIM = 128
M_WIDTH = M_HEADS * M_HEAD_DIM
M_CHUNK = 128
CONV_W = 3
OFF_AQ = 0
OFF_AK = OFF_AQ + A_WIDTH
OFF_AV = OFF_AK + A_KV_WIDTH
OFF_MQ = OFF_AV + A_KV_WIDTH
OFF_MK = OFF_MQ + M_WIDTH
OFF_MV = OFF_MK + M_WIDTH
OFF_MO = OFF_MV + M_WIDTH
OFF_MG = OFF_MO + M_WIDTH
N_MGATES = 4 * M_HEADS
OFF_BG = OFF_MG + N_MGATES
IN_TOTAL = OFF_BG + 2 * D_MODEL
N_GROUPS = 4
EXPERTS_PER_GROUP = 4
N_EXPERTS = N_GROUPS * EXPERTS_PER_GROUP
D_EXPERT = 512
EPS = 1e-6
NEG_BIG = -1e30
F32 = jnp.float32
BF16 = jnp.bfloat16

LANES = 128
SUBLANES = 8
VMEM_LIMIT_BYTES = 56 * 1024 * 1024

MAIN_AQ = 0
MAIN_KV = A_WIDTH
MAIN_MQ = MAIN_KV + 2 * A_KV_WIDTH
MAIN_MK = MAIN_MQ + M_WIDTH
MAIN_MO = MAIN_MK + M_WIDTH
MAIN_BG = MAIN_MO + M_WIDTH
MAIN_TOTAL = MAIN_BG + 2 * D_MODEL

ROUTER_G_LANE = N_EXPERTS
ROUTE_W = 4
PACKED = jnp.uint32
PACK_W = D_MODEL // 4
SC_WINDOW = 128
EXPERT_ROWS = 256


def _sigmoid(z):
    return 1.0 / (1.0 + jnp.exp(-z))


def _log_sigmoid(z):
    return jnp.minimum(z, 0.0) - jnp.log(1.0 + jnp.exp(-jnp.abs(z)))


def _is_forget(gate_idx):
    return jnp.bitwise_and(jnp.right_shift(gate_idx, 2), 1) == 1


def _dot(a, b):
    return jnp.dot(a, b, preferred_element_type=F32)


def _dot_nt(a, b):
    return lax.dot_general(a, b, (((1,), (1,)), ((), ())), preferred_element_type=F32)


def _dot_tn(a, b):
    return lax.dot_general(a, b, (((0,), (0,)), ((), ())), preferred_element_type=F32)


def _dot_f32(a, b):
    return jnp.dot(a, b, preferred_element_type=F32, precision=lax.Precision.HIGHEST)


def _compiler_params(semantics):
    return pltpu.CompilerParams(dimension_semantics=semantics, vmem_limit_bytes=VMEM_LIMIT_BYTES)


def _ada_kernel(c_ref, w_ref, b_ref, o_ref):
    c = c_ref[...]
    o_ref[...] = _dot_f32(c * _sigmoid(c), w_ref[...]) + b_ref[...]


def _ada_mod(c, ada_w, ada_b):
    B, D = c.shape
    n = ada_w.shape[1] // D
    return pl.pallas_call(
        _ada_kernel,
        grid=(n,),
        in_specs=[
            pl.BlockSpec((B, D), lambda j: (0, 0)),
            pl.BlockSpec((D, D), lambda j: (0, j)),
            pl.BlockSpec((1, D), lambda j: (0, j)),
        ],
        out_specs=pl.BlockSpec((B, D), lambda j: (0, j)),
        out_shape=jax.ShapeDtypeStruct((B, n * D), F32),
        compiler_params=_compiler_params(("arbitrary",)),
        name="ada_mod",
    )(c, ada_w, ada_b.reshape(1, -1))


def _rms_mod(x, g, scale, shift):
    ms = jnp.mean(x * x, axis=-1, keepdims=True)
    return (x * lax.rsqrt(ms + EPS) * g) * (1.0 + scale) + shift


def _rope_block(xb, cos, sin_lo, sin_hi):
    half = ROPE_DIM // 2
    return xb * cos + pltpu.roll(xb, LANES - half, axis=1) * sin_lo + pltpu.roll(xb, half, axis=1) * sin_hi


def _in_proj_kernel(x_ref, mod_ref, g_ref, w_ref, wt_ref, gb_col_ref, cos_ref, slo_ref, shi_ref,
                    q_ref, kx_ref, vx_ref, mq_ref, mk_ref, vt_ref, og_ref, bg_ref, grow_ref):
    x = x_ref[0]
    tm = x.shape[0]
    h = _rms_mod(x, g_ref[...], mod_ref[0, 1:2, :], mod_ref[0, 0:1, :])
    hb = h.astype(BF16)
    cos, slo, shi = cos_ref[...], slo_ref[...], shi_ref[...]
    lane = lax.broadcasted_iota(jnp.int32, (tm, LANES), 1)
    left = lane < A_HEAD_DIM

    pj = _dot(hb, w_ref[:, MAIN_AQ:MAIN_AQ + A_WIDTH])
    for j in range(A_WIDTH // LANES):
        blk = _rope_block(pj[:, j * LANES:(j + 1) * LANES], cos, slo, shi)
        q_ref[0, :, j * LANES:(j + 1) * LANES] = (blk * (A_HEAD_DIM ** -0.5)).astype(BF16)

    pj = _dot(hb, w_ref[:, MAIN_KV:MAIN_KV + 2 * A_KV_WIDTH])
    kk = _rope_block(pj[:, 0:LANES], cos, slo, shi)
    vv = pj[:, LANES:2 * LANES]
    for src, dst in ((kk, kx_ref), (vv, vx_ref)):
        swapped = pltpu.roll(src, A_HEAD_DIM, axis=1)
        zero = jnp.zeros_like(src)
        dst[0, :, 0 * LANES:1 * LANES] = jnp.where(left, src, zero).astype(BF16)
        dst[0, :, 1 * LANES:2 * LANES] = jnp.where(left, zero, swapped).astype(BF16)
        dst[0, :, 2 * LANES:3 * LANES] = jnp.where(left, swapped, zero).astype(BF16)
        dst[0, :, 3 * LANES:4 * LANES] = jnp.where(left, zero, src).astype(BF16)

    mq_ref[0] = _dot(hb, w_ref[:, MAIN_MQ:MAIN_MQ + M_WIDTH]).astype(BF16)
    mk_ref[0] = _dot(hb, w_ref[:, MAIN_MK:MAIN_MK + M_WIDTH]).astype(BF16)
    og_ref[0] = _sigmoid(_dot(hb, w_ref[:, MAIN_MO:MAIN_MO + M_WIDTH])).astype(BF16)
    for j in range(2 * D_MODEL // M_WIDTH):
        lo = MAIN_BG + j * M_WIDTH
        bg_ref[0, :, j * M_WIDTH:(j + 1) * M_WIDTH] = _sigmoid(_dot(hb, w_ref[:, lo:lo + M_WIDTH])).astype(BF16)

    tr = _dot_nt(wt_ref[...], hb)
    gr = tr[M_WIDTH:M_WIDTH + N_MGATES, :] + gb_col_ref[...]
    ridx = lax.broadcasted_iota(jnp.int32, gr.shape, 0)
    gr = jnp.where(_is_forget(ridx), _log_sigmoid(gr), gr)
    vt = tr[0:M_WIDTH, :].astype(BF16)
    for c in range(tm // M_CHUNK):
        grow_ref[0, c] = gr[:, c * M_CHUNK:(c + 1) * M_CHUNK]
        vt_ref[0, c] = vt[:, c * M_CHUNK:(c + 1) * M_CHUNK]


def _in_proj(x, mod, norm_g, w_main, w_t, gate_b, rope_tabs, tm):
    B, S, D = x.shape
    nt = S // tm
    cos, slo, shi = rope_tabs
    tok = lambda w: pl.BlockSpec((1, tm, w), lambda b, i: (b, i, 0))
    const2 = lambda a: pl.BlockSpec(a.shape, lambda b, i: (0, 0))
    tab = pl.BlockSpec((tm, LANES), lambda b, i: (i, 0))
    out_shapes = (
        jax.ShapeDtypeStruct((B, S, A_WIDTH), BF16),
        jax.ShapeDtypeStruct((B, S, 4 * LANES), BF16),
        jax.ShapeDtypeStruct((B, S, 4 * LANES), BF16),
        jax.ShapeDtypeStruct((B, S, M_WIDTH), BF16),
        jax.ShapeDtypeStruct((B, S, M_WIDTH), BF16),
        jax.ShapeDtypeStruct((B, S // M_CHUNK, M_WIDTH, M_CHUNK), BF16),
        jax.ShapeDtypeStruct((B, S, M_WIDTH), BF16),
        jax.ShapeDtypeStruct((B, S, 2 * D_MODEL), BF16),
        jax.ShapeDtypeStruct((B, S // M_CHUNK, N_MGATES, M_CHUNK), F32),
    )
    chunked = lambda rows: pl.BlockSpec((1, tm // M_CHUNK, rows, M_CHUNK), lambda b, i: (b, i, 0, 0))
    out_specs = (
        tok(A_WIDTH), tok(4 * LANES), tok(4 * LANES), tok(M_WIDTH), tok(M_WIDTH), chunked(M_WIDTH), tok(M_WIDTH),
        tok(2 * D_MODEL), chunked(N_MGATES),
    )
    gb_col = gate_b.reshape(N_MGATES, 1)
    g2 = norm_g.reshape(1, D)
    return pl.pallas_call(
        _in_proj_kernel,
        grid=(B, nt),
        in_specs=[
            tok(D),
            pl.BlockSpec((1, 6, D), lambda b, i: (b, 0, 0)),
            const2(g2), const2(w_main), const2(w_t), const2(gb_col),
            tab, tab, tab,
        ],
        out_specs=out_specs,
        out_shape=out_shapes,
        compiler_params=_compiler_params(("parallel", "parallel")),
        name="in_proj",
    )(x, mod, g2, w_main, w_t, gb_col, cos, slo, shi)


def _attn_kernel(sink_ref, q_ref, kx_ref, vx_ref, o_ref):
    S = q_ref.shape[1]
    nb = S // BLOCK
    kw = 3 * BLOCK
    qi = lax.broadcasted_iota(jnp.int32, (BLOCK, kw), 0)
    ki = lax.broadcasted_iota(jnp.int32, (BLOCK, kw), 1)
    rel0 = ki - qi

    def block(n, carry):
        q0 = pl.multiple_of(n * BLOCK, BLOCK)
        k0 = pl.multiple_of(jnp.clip((n - 1) * BLOCK, 0, S - kw), BLOCK)
        rel = rel0 + (k0 - q0)
        valid = jnp.abs(rel) <= WINDOW
        for hk in range(A_KV_HEADS):
            k_l = kx_ref[0, pl.ds(k0, kw), (2 * hk) * LANES:(2 * hk + 1) * LANES]
            k_r = kx_ref[0, pl.ds(k0, kw), (2 * hk + 1) * LANES:(2 * hk + 2) * LANES]
            v_l = vx_ref[0, pl.ds(k0, kw), (2 * hk) * LANES:(2 * hk + 1) * LANES]
            v_r = vx_ref[0, pl.ds(k0, kw), (2 * hk + 1) * LANES:(2 * hk + 2) * LANES]
            for j in range(A_GROUP // 2):
                col = (hk * (A_GROUP // 2) + j) * LANES
                qp = q_ref[0, pl.ds(q0, BLOCK), col:col + LANES]
                acc = None
                for side, (kk, vv) in enumerate(((k_l, v_l), (k_r, v_r))):
                    sk = sink_ref[hk * A_GROUP + 2 * j + side]
                    s = jnp.where(valid, _dot_nt(qp, kk), NEG_BIG)
                    m = jnp.maximum(jnp.max(s, axis=-1, keepdims=True), sk)
                    p = jnp.exp(s - m)
                    den = jnp.sum(p, axis=-1, keepdims=True) + jnp.exp(sk - m)
                    o = _dot((p / den).astype(BF16), vv)
                    acc = o if acc is None else acc + o
                o_ref[0, pl.ds(q0, BLOCK), col:col + LANES] = acc.astype(BF16)
        return carry

    lax.fori_loop(0, nb, block, 0)


def _attention(q, kx, vx, sink):
    B, S, _ = q.shape
    seq = lambda w: pl.BlockSpec((1, S, w), lambda b: (b, 0, 0))
    return pl.pallas_call(
        _attn_kernel,
        grid=(B,),
        in_specs=[pl.BlockSpec(memory_space=pltpu.SMEM), seq(A_WIDTH), seq(4 * LANES), seq(4 * LANES)],
        out_specs=seq(A_WIDTH),
        out_shape=jax.ShapeDtypeStruct((B, S, A_WIDTH), BF16),
        compiler_params=_compiler_params(("parallel",)),
        name="window_attn",
    )(sink, q, kx, vx)


def _conv_silu(u_ref, w_ref, pad_ref, dst_ref, scale):
    S = u_ref.shape[1]
    pad_ref[0:SUBLANES, :] = jnp.zeros((SUBLANES, LANES), F32)
    pad_ref[S + SUBLANES:S + 2 * SUBLANES, :] = jnp.zeros((SUBLANES, LANES), F32)
    pad_ref[SUBLANES:S + SUBLANES, :] = u_ref[0].astype(F32)
    w0, w1, w2 = w_ref[0:1, :], w_ref[1:2, :], w_ref[2:3, :]
    for c in range(S // M_CHUNK):
        base = SUBLANES + c * M_CHUNK
        y = (pad_ref[base - 1:base - 1 + M_CHUNK, :] * w0 + pad_ref[base:base + M_CHUNK, :] * w1
             + pad_ref[base + 1:base + 1 + M_CHUNK, :] * w2)
        y = y * _sigmoid(y)
        dst_ref[c * M_CHUNK:(c + 1) * M_CHUNK, :] = y * scale


def _split3(x):
    hi = x.astype(BF16)
    r = x - hi.astype(F32)
    mid = r.astype(BF16)
    lo = (r - mid.astype(F32)).astype(BF16)
    return [hi, mid, lo]


def _mlstm_kernel(mq_ref, mk_ref, vt_ref, og_ref, grow_ref, cwq_ref, cwk_ref, hgt_ref, o_ref,
                  pad_ref, qs_ref, ks_ref, rb_ref, ib_ref, rows_ref, cinc_ref, ninc_ref, cin_ref, sin_ref,
                  c_ref, n_ref):
    S = mq_ref.shape[1]
    nc = S // M_CHUNK
    L = M_CHUNK
    DH = M_HEAD_DIM
    _conv_silu(mq_ref, cwq_ref, pad_ref, qs_ref, 1.0)
    _conv_silu(mk_ref, cwk_ref, pad_ref, ks_ref, M_HEAD_DIM ** -0.5)

    s_i = lax.broadcasted_iota(jnp.int32, (L, L), 0)
    t_i = lax.broadcasted_iota(jnp.int32, (L, L), 1)
    tris = (s_i <= t_i, s_i >= t_i)
    eye = s_i == t_i
    row8 = lax.broadcasted_iota(jnp.int32, (SUBLANES, L), 0)
    one_if = lambda cond: jnp.where(cond, 1.0, 0.0)
    k_j = lax.broadcasted_iota(jnp.int32, (3 * L, 2 * L), 0) % L
    c_j = lax.broadcasted_iota(jnp.int32, (3 * L, 2 * L), 1)
    sum_rows = jnp.where(c_j < L, one_if(k_j <= c_j), one_if(k_j >= c_j - L)).astype(BF16)

    g_all = grow_ref[0, 0].reshape(nc * SUBLANES, L)
    rb_ref[...] = _dot(jnp.concatenate(_split3(g_all), axis=1), sum_rows).reshape(nc, SUBLANES, 2 * L)
    ones_b = jnp.ones((3 * L, L), BF16)

    def phase_a(c, carry):
        r0 = pl.multiple_of(c * L, L)
        gr = grow_ref[0, 0, c]
        rb = rb_ref[c]
        kb = ks_ref[pl.ds(r0, L), :].astype(BF16)
        vt = vt_ref[0, c].astype(F32)
        wvs, wks, diags = [], [], []
        for d in range(2):
            brow = rb[2 * d + 1:2 * d + 2, d * L:(d + 1) * L]
            blast = brow[:, L - 1:L] if d == 0 else brow[:, 0:1]
            ibr = gr[2 * d:2 * d + 1, :] - brow
            log_g = blast + ibr
            mg = jnp.max(log_g, axis=-1, keepdims=True)
            wk = jnp.exp(log_g - mg)
            wvs.append((vt * wk).astype(BF16))
            wks.append(wk)
            diags.append(jnp.concatenate(
                [jnp.where(eye, term.astype(F32), 0.0).astype(BF16) for term in _split3(ibr)], axis=1))
            rows_ref[c, 2 + d:3 + d, :] = brow
            rows_ref[c, 4 + d:5 + d, :] = jnp.broadcast_to(mg, (1, L))
            rows_ref[c, 6 + d:7 + d, :] = jnp.broadcast_to(blast, (1, L))
        ib = _dot(jnp.concatenate(diags, axis=0), ones_b)
        ib_ref[c] = ib
        for d in range(2):
            rows_ref[c, d:d + 1, :] = jnp.max(jnp.where(tris[d], ib[d * L:(d + 1) * L, :], NEG_BIG),
                                              axis=0, keepdims=True)
        cinc_ref[c] = _dot(jnp.concatenate(wvs, axis=0), kb)
        wk8 = jnp.where(row8 == 0, wks[0], jnp.where(row8 == 1, wks[1], 0.0))
        ninc_ref[c] = _dot(wk8.astype(BF16), kb)
        return carry

    def phase_b(j, ms):
        new_ms = []
        for d, cc in ((0, j), (1, nc - 1 - j)):
            m = ms[d]
            half = slice(d * DH, (d + 1) * DH)
            cst = c_ref[half, :]
            n = n_ref[d:d + 1, :]
            cin_ref[cc, half, :] = cst.astype(BF16)
            sin_ref[cc, d:d + 1, :] = n
            sin_ref[cc, 2 + d:3 + d, :] = m
            mg = rows_ref[cc, 4 + d:5 + d, :]
            blast = rows_ref[cc, 6 + d:7 + d, :]
            m_new = jnp.maximum(blast + m, mg)
            decay = jnp.exp(blast + m - m_new)
            grow = jnp.exp(mg - m_new)
            c_ref[half, :] = decay * cst + grow * cinc_ref[cc, half, :]
            n_ref[d:d + 1, :] = decay * n + grow * ninc_ref[cc, d:d + 1, :]
            new_ms.append(m_new)
        return tuple(new_ms)

    def phase_c(c, carry):
        r0 = pl.multiple_of(c * L, L)
        qb = qs_ref[pl.ds(r0, L), :].astype(BF16)
        sin = sin_ref[c]
        rows = rows_ref[c]
        qk_t = _dot_nt(ks_ref[pl.ds(r0, L), :].astype(BF16), qb)
        qc_t = _dot_nt(cin_ref[c], qb)
        qn = _dot_nt(sin.astype(BF16), qb)
        ats, stats = [], []
        for d in range(2):
            m_in = sin[2 + d:3 + d, :]
            cm = jnp.maximum(m_in, rows[d:d + 1, :])
            a_t = qk_t * jnp.where(tris[d], jnp.exp(ib_ref[c, d * L:(d + 1) * L, :] - cm), 0.0)
            w_inter = jnp.exp(m_in - cm)
            den = w_inter * qn[d:d + 1, :] + jnp.sum(a_t, axis=0, keepdims=True)
            m_t = rows[2 + d:3 + d, :] + cm
            ats.append(a_t.astype(BF16))
            stats.append((w_inter, jnp.maximum(jnp.abs(den), jnp.exp(-m_t))))
        av_t = _dot(vt_ref[0, c], jnp.concatenate(ats, axis=1))
        h_t = None
        for d in range(2):
            w_inter, den = stats[d]
            hd = (w_inter * qc_t[d * DH:(d + 1) * DH, :] + av_t[:, d * L:(d + 1) * L]) / den
            h_t = hd if h_t is None else h_t + hd
        y_t = h_t * lax.rsqrt(jnp.mean(h_t * h_t, axis=0, keepdims=True) + EPS) * hgt_ref[...]
        o_ref[0, pl.ds(r0, L), :] = (og_ref[0, pl.ds(r0, L), :].astype(F32) * y_t.T).astype(BF16)
        return carry

    lax.fori_loop(0, nc, phase_a, 0, unroll=4)
    c_ref[...] = jnp.zeros(c_ref.shape, F32)
    n_ref[...] = jnp.zeros(n_ref.shape, F32)
    m0 = jnp.zeros((1, LANES), F32)
    lax.fori_loop(0, nc, phase_b, (m0, m0))
    lax.fori_loop(0, nc, phase_c, 0, unroll=8)


def _mlstm(mq, mk, vt, og, grow_h, conv_wq, conv_wk, head_g_t):
    B, S, _ = mq.shape
    nc = S // M_CHUNK
    head = pl.BlockSpec((1, S, M_HEAD_DIM), lambda b, h: (b, 0, h))
    cw = pl.BlockSpec((CONV_W, M_HEAD_DIM), lambda b, h: (0, h))
    return pl.pallas_call(
        _mlstm_kernel,
        grid=(B, M_HEADS),
        in_specs=[
            head, head,
            pl.BlockSpec((1, nc, M_HEAD_DIM, M_CHUNK), lambda b, h: (b, 0, h, 0)),
            head,
            pl.BlockSpec((1, 1, nc, SUBLANES, M_CHUNK), lambda b, h: (b, h, 0, 0, 0)),
            cw, cw,
            pl.BlockSpec((M_HEAD_DIM, LANES), lambda b, h: (h, 0)),
        ],
        out_specs=head,
        out_shape=jax.ShapeDtypeStruct((B, S, M_WIDTH), BF16),
        scratch_shapes=[
            pltpu.VMEM((S + 2 * SUBLANES, LANES), F32),
            pltpu.VMEM((S, M_HEAD_DIM), F32),
            pltpu.VMEM((S, M_HEAD_DIM), F32),
            pltpu.VMEM((nc, SUBLANES, 2 * M_CHUNK), F32),
            pltpu.VMEM((nc, 2 * M_CHUNK, M_CHUNK), F32),
            pltpu.VMEM((nc, SUBLANES, M_CHUNK), F32),
            pltpu.VMEM((nc, 2 * M_HEAD_DIM, M_HEAD_DIM), F32),
            pltpu.VMEM((nc, SUBLANES, M_HEAD_DIM), F32),
            pltpu.VMEM((nc, 2 * M_HEAD_DIM, M_HEAD_DIM), BF16),
            pltpu.VMEM((nc, SUBLANES, M_HEAD_DIM), F32),
            pltpu.VMEM((2 * M_HEAD_DIM, M_HEAD_DIM), F32),
            pltpu.VMEM((SUBLANES, M_HEAD_DIM), F32),
        ],
        compiler_params=_compiler_params(("parallel", "parallel")),
        name="mlstm",
    )(mq, mk, vt, og, grow_h, conv_wq, conv_wk, head_g_t)


def _route(logits):
    lane = lax.broadcasted_iota(jnp.int32, logits.shape, 1)
    lane_f = lane.astype(F32)
    big = float(LANES)
    is_g = (lane >= ROUTER_G_LANE) & (lane < ROUTER_G_LANE + N_GROUPS)
    gl = jnp.where(is_g, logits, NEG_BIG)
    gmax = jnp.max(gl, axis=-1, keepdims=True)
    gsum = jnp.sum(jnp.where(is_g, jnp.exp(gl - gmax), 0.0), axis=-1, keepdims=True)
    p_grp = 1.0 / gsum
    grp = jnp.min(jnp.where(is_g & (gl == gmax), lane_f - ROUTER_G_LANE, big), axis=-1, keepdims=True)
    in_grp = (lane < N_EXPERTS) & (jnp.right_shift(lane, 2).astype(F32) == grp)
    el = jnp.where(in_grp, logits, NEG_BIG)
    v1 = jnp.max(el, axis=-1, keepdims=True)
    i1 = jnp.min(jnp.where(in_grp & (el == v1), lane_f, big), axis=-1, keepdims=True)
    rest = in_grp & (lane_f != i1)
    el2 = jnp.where(rest, logits, NEG_BIG)
    v2 = jnp.max(el2, axis=-1, keepdims=True)
    i2 = jnp.min(jnp.where(rest & (el2 == v2), lane_f, big), axis=-1, keepdims=True)
    e21 = jnp.exp(v2 - v1)
    w1 = p_grp / (1.0 + e21)
    w2 = p_grp * e21 / (1.0 + e21)
    return jnp.where(lane == 0, i1, jnp.where(lane == 1, i2, jnp.where(lane == 2, w1, w2)))


def _pack_bf16_pairs(lo, hi):
    bits = lambda v: lax.bitcast_convert_type(v.astype(BF16).astype(F32), PACKED)
    return jnp.right_shift(bits(lo), PACKED(16)) | (bits(hi) & PACKED(0xFFFF0000))


def _unpack_bf16_pairs(w):
    return (lax.bitcast_convert_type(jnp.left_shift(w, PACKED(16)), F32),
            lax.bitcast_convert_type(w & PACKED(0xFFFF0000), F32))


def _pack_row_halves(y):
    return (_pack_bf16_pairs(y[:, 0:PACK_W], y[:, PACK_W:2 * PACK_W]),
            _pack_bf16_pairs(y[:, 2 * PACK_W:3 * PACK_W], y[:, 3 * PACK_W:4 * PACK_W]))


def _unpack_row_halves(a, b):
    return [*_unpack_bf16_pairs(a), *_unpack_bf16_pairs(b)]


def _merge_kernel(attn_ref, mo_ref, bg_ref, x_ref, mod_ref, wua_ref, wum_ref, wo_ref, g2_ref, wr_ref, br_ref,
                  x1_ref, ha_ref, hb_ref, rt_ref):
    up_a = _dot(attn_ref[0], wua_ref[...])
    up_m = _dot(mo_ref[0], wum_ref[...])
    merged = (bg_ref[0, :, 0:D_MODEL].astype(F32) * up_a + bg_ref[0, :, D_MODEL:2 * D_MODEL].astype(F32) * up_m)
    x1 = x_ref[0] + mod_ref[0, 2:3, :] * _dot(merged.astype(BF16), wo_ref[...])
    x1_ref[0] = x1
    h2 = _rms_mod(x1, g2_ref[...], mod_ref[0, 4:5, :], mod_ref[0, 3:4, :])
    ha_ref[0], hb_ref[0] = _pack_row_halves(h2)
    logits = _dot_f32(h2, wr_ref[...]) + br_ref[...]
    rt_ref[0] = _route(logits)[:, 0:ROUTE_W]


def _merge(attn, mo, bg, x, mod, wua, wum, wo, norm_g, w_router, b_router, tm):
    B, S, D = x.shape
    tok = lambda w: pl.BlockSpec((1, tm, w), lambda b, i: (b, i, 0))
    const2 = lambda a: pl.BlockSpec(a.shape, lambda b, i: (0, 0))
    g2 = norm_g.reshape(1, D)
    return pl.pallas_call(
        _merge_kernel,
        grid=(B, S // tm),
        in_specs=[
            tok(A_WIDTH), tok(M_WIDTH), tok(2 * D), tok(D),
            pl.BlockSpec((1, 6, D), lambda b, i: (b, 0, 0)),
            const2(wua), const2(wum), const2(wo), const2(g2), const2(w_router), const2(b_router),
        ],
        out_specs=(tok(D), tok(PACK_W), tok(PACK_W), tok(ROUTE_W)),
        out_shape=(
            jax.ShapeDtypeStruct((B, S, D), F32),
            jax.ShapeDtypeStruct((B, S, PACK_W), PACKED),
            jax.ShapeDtypeStruct((B, S, PACK_W), PACKED),
            jax.ShapeDtypeStruct((B, S, ROUTE_W), F32),
        ),
        compiler_params=_compiler_params(("parallel", "parallel")),
        name="merge_route",
    )(attn, mo, bg, x, mod, wua, wum, wo, g2, w_router, b_router)


def _route_tables(eid, tmb):
    T = eid.shape[0]
    nblk = 2 * T // tmb + N_EXPERTS
    ea = eid.T.reshape(-1)
    onehot = (ea[:, None] == jnp.arange(N_EXPERTS, dtype=jnp.int32)[None, :]).astype(jnp.int32)
    csum = jnp.cumsum(onehot, axis=0)
    counts = csum[-1]
    rank = jnp.sum(onehot * csum, axis=1) - 1
    padded = ((counts + tmb - 1) // tmb) * tmb
    ends = jnp.cumsum(padded)
    starts = ends - padded
    pos = (jnp.sum(onehot * starts[None, :], axis=1) + rank).astype(jnp.int32)
    blk0 = jnp.arange(nblk, dtype=jnp.int32) * tmb
    bexp = jnp.minimum(jnp.sum((blk0[:, None] >= ends[None, :]).astype(jnp.int32), axis=1), N_EXPERTS - 1)
    nval = jnp.clip(starts[bexp] + counts[bexp] - blk0, 0, tmb)
    nval = jnp.where(blk0 < ends[-1], nval, 0).astype(jnp.int32)
    return pos, bexp.astype(jnp.int32), nval


def _sc_mesh():
    return plsc.VectorSubcoreMesh(core_axis_name="c", subcore_axis_name="s")


def _sc_dispatch(x, pos, n_rows):
    T = x.shape[0]
    nb = T // SC_WINDOW
    idx = pos.reshape(1, 2 * T)

    @pl.kernel(out_type=jax.ShapeDtypeStruct((n_rows, PACK_W), x.dtype), mesh=_sc_mesh(), scratch_types=[])
    def dispatch(x_hbm, i_hbm, o_hbm):
        def body(x_vmem, i_vmem):
            pltpu.sync_copy(x_vmem, o_hbm.at[i_vmem.at[0]])

        pltpu.emit_pipeline(
            body,
            grid=(2 * nb,),
            in_specs=[pl.BlockSpec((SC_WINDOW, PACK_W), index_map=lambda i: (i % nb, 0)),
                      pl.BlockSpec((1, SC_WINDOW), index_map=lambda i: (0, i))],
            out_specs=[],
            core_axis_name=("c", "s"),
            dimension_semantics=(pltpu.PARALLEL,),
        )(x_hbm, i_hbm)

    return dispatch(x, idx)


def _sc_combine(y, pos):
    n = pos.shape[0]
    idx = pos.reshape(1, n)

    @pl.kernel(out_type=jax.ShapeDtypeStruct((n, PACK_W), y.dtype), mesh=_sc_mesh(), scratch_types=[])
    def combine(y_hbm, i_hbm, o_hbm):
        def body(i_vmem, o_vmem):
            pltpu.sync_copy(y_hbm.at[i_vmem.at[0]], o_vmem)

        pltpu.emit_pipeline(
            body,
            grid=(n // SC_WINDOW,),
            in_specs=[pl.BlockSpec((1, SC_WINDOW), index_map=lambda i: (0, i))],
            out_specs=[pl.BlockSpec((SC_WINDOW, PACK_W), index_map=lambda i: (i, 0))],
            core_axis_name=("c", "s"),
            dimension_semantics=(pltpu.PARALLEL,),
        )(i_hbm, o_hbm)

    return combine(y, idx)


def _experts_kernel(bexp_ref, nval_ref, xa_ref, xb_ref, wg_ref, wu_ref, wd_ref, ya_ref, yb_ref):
    nv = nval_ref[pl.program_id(0)]

    @pl.when(nv > 0)
    def _():
        keep = lax.broadcasted_iota(jnp.int32, xa_ref.shape, 0) < nv
        pieces = [jnp.where(keep, piece, 0.0).astype(BF16) for piece in _unpack_row_halves(xa_ref[...], xb_ref[...])]
        x = jnp.concatenate(pieces, axis=1)
        g = _dot(x, wg_ref[0])
        u = _dot(x, wu_ref[0])
        he = (g * _sigmoid(g)) * u
        ya_ref[...], yb_ref[...] = _pack_row_halves(_dot(he.astype(BF16), wd_ref[0]))


def _experts(xa, xb, bexp, nval, wg, wu, wd, tmb):
    n = xa.shape[0]
    row = lambda: pl.BlockSpec((tmb, PACK_W), lambda i, be, nv: (i, 0))
    grid_spec = pltpu.PrefetchScalarGridSpec(
        num_scalar_prefetch=2,
        grid=(n // tmb,),
        in_specs=[row(), row(),
                  pl.BlockSpec((1, D_MODEL, D_EXPERT), lambda i, be, nv: (be[i], 0, 0)),
                  pl.BlockSpec((1, D_MODEL, D_EXPERT), lambda i, be, nv: (be[i], 0, 0)),
                  pl.BlockSpec((1, D_EXPERT, D_MODEL), lambda i, be, nv: (be[i], 0, 0))],
        out_specs=[row(), row()],
    )
    out = jax.ShapeDtypeStruct((n, PACK_W), xa.dtype)
    return pl.pallas_call(
        _experts_kernel, grid_spec=grid_spec, out_shape=(out, out),
        compiler_params=_compiler_params(("arbitrary",)),
        name="experts",
    )(bexp, nval, xa, xb, wg, wu, wd)


def _finish_kernel(a0_ref, b0_ref, a1_ref, b1_ref, rt_ref, x1_ref, mod_ref, fg_ref, o_ref):
    y0 = jnp.concatenate(_unpack_row_halves(a0_ref[...], b0_ref[...]), axis=1)
    y1 = jnp.concatenate(_unpack_row_halves(a1_ref[...], b1_ref[...]), axis=1)
    rt = rt_ref[0]
    moe = rt[:, 2:3] * y0 + rt[:, 3:4] * y1
    xo = x1_ref[0] + mod_ref[0, 5:6, :] * moe
    ms = jnp.mean(xo * xo, axis=-1, keepdims=True)
    o_ref[0] = xo * lax.rsqrt(ms + EPS) * fg_ref[...]


def _finish(ca, cb, rt, x1, mod, final_g, tm):
    B, S, D = x1.shape
    nt = S // tm
    second = B * nt
    half = lambda k: pl.BlockSpec((tm, PACK_W), lambda b, i: (k * second + b * nt + i, 0))
    tok = lambda w: pl.BlockSpec((1, tm, w), lambda b, i: (b, i, 0))
    fg = final_g.reshape(1, D)
    return pl.pallas_call(
        _finish_kernel,
        grid=(B, nt),
        in_specs=[half(0), half(0), half(1), half(1), tok(ROUTE_W), tok(D),
                  pl.BlockSpec((1, 6, D), lambda b, i: (b, 0, 0)),
                  pl.BlockSpec((1, D), lambda b, i: (0, 0))],
        out_specs=tok(D),
        out_shape=jax.ShapeDtypeStruct((B, S, D), F32),
        compiler_params=_compiler_params(("parallel", "parallel")),
        name="finish",
    )(ca, cb, ca, cb, rt, x1, mod, fg)


def _moe(ha, hb, rt, x1, mod, wg, wu, wd, final_g):
    B, S, D = x1.shape
    T = B * S
    tmb = EXPERT_ROWS
    eid = rt.reshape(T, ROUTE_W)[:, 0:2].astype(jnp.int32)
    pos, bexp, nval = _route_tables(eid, tmb)
    n_rows = 2 * T + N_EXPERTS * tmb
    xa = _sc_dispatch(ha.reshape(T, PACK_W), pos, n_rows)
    xb = _sc_dispatch(hb.reshape(T, PACK_W), pos, n_rows)
    ya, yb = _experts(xa, xb, bexp, nval, wg, wu, wd, tmb)
    return _finish(_sc_combine(ya, pos), _sc_combine(yb, pos), rt, x1, mod, final_g, tm=512)


def _rope_tables(S):
    half = ROPE_DIM // 2
    inv_freq = 1.0 / (ROPE_THETA ** (jnp.arange(half, dtype=F32) * 2.0 / ROPE_DIM))
    ang = jnp.arange(S, dtype=F32)[:, None] * inv_freq[None, :]
    cos, sin = jnp.cos(ang), jnp.sin(ang)
    zeros = jnp.zeros((S, A_HEAD_DIM - ROPE_DIM), F32)
    z8 = jnp.zeros((S, half), F32)
    cos_h = jnp.concatenate([cos, cos, jnp.ones_like(zeros)], axis=-1)
    slo_h = jnp.concatenate([-sin, z8, zeros], axis=-1)
    shi_h = jnp.concatenate([z8, sin, zeros], axis=-1)
    rep = LANES // A_HEAD_DIM
    return tuple(jnp.tile(t, (1, rep)) for t in (cos_h, slo_h, shi_h))


def _layer(x, c, p, tabs):
    B, S, D = x.shape
    mod = _ada_mod(c, p["ada_w"], p["ada_b"]).reshape(B, 6, D)
    q, kx, vx, mq, mk, vt, og, bg, grow = _in_proj(
        x, mod, p["norm1_g"], p["w_main"], p["w_t"], p["m_gate_b"], tabs, tm=512)
    attn = _attention(q, kx, vx, p["attn_sink"])
    nc = S // M_CHUNK
    grow_h = grow.reshape(B, nc, 4, M_HEADS, M_CHUNK).transpose(0, 3, 1, 2, 4)
    grow_h = jnp.pad(grow_h, ((0, 0), (0, 0), (0, 0), (0, SUBLANES - 4), (0, 0)))
    mo = _mlstm(mq, mk, vt, og, grow_h, p["conv_wq"], p["conv_wk"], p["head_norm_g_t"])
    x1, ha, hb, rt = _merge(attn, mo, bg, x, mod, p["w_up_attn"], p["w_up_mlstm"], p["w_out"], p["norm2_g"],
                            p["w_router"], p["b_router"], tm=512)
    return _moe(ha, hb, rt, x1, mod, p["w_gate"], p["w_up"], p["w_down"], p["final_norm_g"])


def kernel(x_prompt, x_sample, c_prompt, c_sample, ada_w, ada_b, norm1_g, w_in, conv_w, m_gate_b, attn_sink,
           head_norm_g, w_up_attn, w_up_mlstm, w_out, norm2_g, rg_w, rg_b, re_w, re_b, w_gate, w_up, w_down,
           final_norm_g):
    assert ada_w.shape[0] == 1, "single-layer trunk"
    w_in0 = w_in[0]
    w_g = w_in0[:, OFF_MG:OFF_BG]
    pad = LANES - N_EXPERTS - N_GROUPS
    p = dict(
        ada_w=ada_w[0], ada_b=ada_b[0], norm1_g=norm1_g[0],
        w_main=jnp.concatenate([w_in0[:, :OFF_MV], w_in0[:, OFF_MO:OFF_MG], w_in0[:, OFF_BG:]], axis=1).astype(BF16),
        w_t=jnp.concatenate([w_in0[:, OFF_MV:OFF_MO], w_g], axis=1).T.astype(BF16),
        m_gate_b=m_gate_b[0], attn_sink=attn_sink[0],
        conv_wq=conv_w[0, :, :M_WIDTH], conv_wk=conv_w[0, :, M_WIDTH:],
        head_norm_g_t=jnp.broadcast_to(head_norm_g[0][:, None], (M_WIDTH, LANES)),
        w_up_attn=w_up_attn[0].astype(BF16), w_up_mlstm=w_up_mlstm[0].astype(BF16), w_out=w_out[0].astype(BF16),
        norm2_g=norm2_g[0],
        w_router=jnp.pad(jnp.concatenate([re_w[0], rg_w[0]], axis=1), ((0, 0), (0, pad))),
        b_router=jnp.pad(jnp.concatenate([re_b[0], rg_b[0]]), (0, pad)).reshape(1, LANES),
        w_gate=w_gate[0].astype(BF16), w_up=w_up[0].astype(BF16), w_down=w_down[0].astype(BF16),
        final_norm_g=final_norm_g,
    )
    tabs = _rope_tables(x_prompt.shape[1])
    return (_layer(x_prompt, c_prompt, p, tabs), _layer(x_sample, c_sample, p, tabs))
```

```python
import functools

import jax
import jax.numpy as jnp
from jax import lax
from jax.experimental import pallas as pl
from jax.experimental.pallas import tpu as pltpu
from jax.experimental.pallas import tpu_sc as plsc

D_MODEL = 1024
A_HEADS = 8
A_KV_HEADS = 2
A_GROUP = A_HEADS // A_KV_HEADS
A_HEAD_DIM = 64
A_WIDTH = A_HEADS * A_HEAD_DIM
A_KV_WIDTH = A_KV_HEADS * A_HEAD_DIM
WINDOW = 128
BLOCK = 128
ROPE_DIM = A_HEAD_DIM // 4
ROPE_THETA = 500000.0
M_HEADS = 4
M_HEAD_DIM = 128
M_WIDTH = M_HEADS * M_HEAD_DIM
M_CHUNK = 128
CONV_W = 3
OFF_AQ = 0
OFF_AK = OFF_AQ + A_WIDTH
OFF_AV = OFF_AK + A_KV_WIDTH
OFF_MQ = OFF_AV + A_KV_WIDTH
OFF_MK = OFF_MQ + M_WIDTH
OFF_MV = OFF_MK + M_WIDTH
OFF_MO = OFF_MV + M_WIDTH
OFF_MG = OFF_MO + M_WIDTH
N_MGATES = 4 * M_HEADS
OFF_BG = OFF_MG + N_MGATES
IN_TOTAL = OFF_BG + 2 * D_MODEL
N_GROUPS = 4
EXPERTS_PER_GROUP = 4
N_EXPERTS = N_GROUPS * EXPERTS_PER_GROUP
D_EXPERT = 512
EPS = 1e-6
NEG_BIG = -1e30
F32 = jnp.float32
BF16 = jnp.bfloat16

LANES = 128
SUBLANES = 8
VMEM_LIMIT_BYTES = 56 * 1024 * 1024

MAIN_AQ = 0
MAIN_KV = A_WIDTH
MAIN_MQ = MAIN_KV + 2 * A_KV_WIDTH
MAIN_MK = MAIN_MQ + M_WIDTH
MAIN_MO = MAIN_MK + M_WIDTH
MAIN_BG = MAIN_MO + M_WIDTH
MAIN_TOTAL = MAIN_BG + 2 * D_MODEL

LOG2E = 1.4426950408889634
Q_SCALE = A_HEAD_DIM ** -0.5 * LOG2E
ROUTER_G_LANE = N_EXPERTS
ROUTE_W = 4
PACKED = jnp.uint32
PACK_W = D_MODEL // 4
SC_WINDOW = 128
EXPERT_ROWS = 512


def _sigmoid(z):
    return 1.0 / (1.0 + jnp.exp(-z))


def _log_sigmoid(z):
    return jnp.minimum(z, 0.0) - jnp.log(1.0 + jnp.exp(-jnp.abs(z)))


def _is_forget(gate_idx):
    return jnp.bitwise_and(jnp.right_shift(gate_idx, 2), 1) == 1


def _dot(a, b):
    return jnp.dot(a, b, preferred_element_type=F32)


def _dot_nt(a, b):
    return lax.dot_general(a, b, (((1,), (1,)), ((), ())), preferred_element_type=F32)


def _dot_tn(a, b):
    return lax.dot_general(a, b, (((0,), (0,)), ((), ())), preferred_element_type=F32)


def _dot_f32(a, b):
    return jnp.dot(a, b, preferred_element_type=F32, precision=lax.Precision.HIGHEST)


def _compiler_params(semantics):
    return pltpu.CompilerParams(dimension_semantics=semantics, vmem_limit_bytes=VMEM_LIMIT_BYTES)


def _ada_kernel(c_ref, w_ref, b_ref, o_ref):
    c = c_ref[...]
    o_ref[...] = _dot_f32(c * _sigmoid(c), w_ref[...]) + b_ref[...]


def _ada_mod(c, ada_w, ada_b):
    B, D = c.shape
    n = ada_w.shape[1] // D
    return pl.pallas_call(
        _ada_kernel,
        grid=(n,),
        in_specs=[
            pl.BlockSpec((B, D), lambda j: (0, 0)),
            pl.BlockSpec((D, D), lambda j: (0, j)),
            pl.BlockSpec((1, D), lambda j: (0, j)),
        ],
        out_specs=pl.BlockSpec((B, D), lambda j: (0, j)),
        out_shape=jax.ShapeDtypeStruct((B, n * D), F32),
        compiler_params=_compiler_params(("arbitrary",)),
        name="ada_mod",
    )(c, ada_w, ada_b.reshape(1, -1))


def _rms_mod(x, g, scale, shift):
    ms = jnp.mean(x * x, axis=-1, keepdims=True)
    return (x * lax.rsqrt(ms + EPS) * g) * (1.0 + scale) + shift


def _rope_block(xb, cos, sin_lo, sin_hi):
    half = ROPE_DIM // 2
    return xb * cos + pltpu.roll(xb, LANES - half, axis=1) * sin_lo + pltpu.roll(xb, half, axis=1) * sin_hi


def _in_proj_kernel(x_ref, mod_ref, g_ref, w_ref, wt_ref, gb_col_ref, cos_ref, slo_ref, shi_ref,
                    q_ref, kx_ref, vx_ref, mq_ref, mk_ref, vt_ref, og_ref, bg_ref, grow_ref):
    x = x_ref[0]
    tm = x.shape[0]
    h = _rms_mod(x, g_ref[...], mod_ref[0, 1:2, :], mod_ref[0, 0:1, :])
    hb = h.astype(BF16)
    cos, slo, shi = cos_ref[...], slo_ref[...], shi_ref[...]
    lane = lax.broadcasted_iota(jnp.int32, (tm, LANES), 1)
    left = lane < A_HEAD_DIM

    pj = _dot(hb, w_ref[:, MAIN_AQ:MAIN_AQ + A_WIDTH])
    for j in range(A_WIDTH // LANES):
        blk = _rope_block(pj[:, j * LANES:(j + 1) * LANES], cos, slo, shi)
        q_ref[0, :, j * LANES:(j + 1) * LANES] = (blk * Q_SCALE).astype(BF16)

    pj = _dot(hb, w_ref[:, MAIN_KV:MAIN_KV + 2 * A_KV_WIDTH])
    kk = _rope_block(pj[:, 0:LANES], cos, slo, shi)
    vv = pj[:, LANES:2 * LANES]
    for src, dst in ((kk, kx_ref), (vv, vx_ref)):
        swapped = pltpu.roll(src, A_HEAD_DIM, axis=1)
        zero = jnp.zeros_like(src)
        dst[0, :, 0 * LANES:1 * LANES] = jnp.where(left, src, zero).astype(BF16)
        dst[0, :, 1 * LANES:2 * LANES] = jnp.where(left, zero, swapped).astype(BF16)
        dst[0, :, 2 * LANES:3 * LANES] = jnp.where(left, swapped, zero).astype(BF16)
        dst[0, :, 3 * LANES:4 * LANES] = jnp.where(left, zero, src).astype(BF16)

    mq_ref[0] = _dot(hb, w_ref[:, MAIN_MQ:MAIN_MQ + M_WIDTH]).astype(BF16)
    mk_ref[0] = _dot(hb, w_ref[:, MAIN_MK:MAIN_MK + M_WIDTH]).astype(BF16)
    og_ref[0] = _sigmoid(_dot(hb, w_ref[:, MAIN_MO:MAIN_MO + M_WIDTH])).astype(BF16)
    for j in range(2 * D_MODEL // M_WIDTH):
        lo = MAIN_BG + j * M_WIDTH
        bg_ref[0, :, j * M_WIDTH:(j + 1) * M_WIDTH] = _sigmoid(_dot(hb, w_ref[:, lo:lo + M_WIDTH])).astype(BF16)

    tr = _dot_nt(wt_ref[...], hb)
    gr = tr[M_WIDTH:M_WIDTH + N_MGATES, :] + gb_col_ref[...]
    ridx = lax.broadcasted_iota(jnp.int32, gr.shape, 0)
    gr = jnp.where(_is_forget(ridx), _log_sigmoid(gr), gr)
    vt = tr[0:M_WIDTH, :].astype(BF16)
    for c in range(tm // M_CHUNK):
        grow_ref[0, c] = gr[:, c * M_CHUNK:(c + 1) * M_CHUNK]
        vt_ref[0, c] = vt[:, c * M_CHUNK:(c + 1) * M_CHUNK]


def _in_proj(x, mod, norm_g, w_main, w_t, gate_b, rope_tabs, tm):
    B, S, D = x.shape
    nt = S // tm
    cos, slo, shi = rope_tabs
    tok = lambda w: pl.BlockSpec((1, tm, w), lambda b, i: (b, i, 0))
    const2 = lambda a: pl.BlockSpec(a.shape, lambda b, i: (0, 0))
    tab = pl.BlockSpec((tm, LANES), lambda b, i: (i, 0))
    out_shapes = (
        jax.ShapeDtypeStruct((B, S, A_WIDTH), BF16),
        jax.ShapeDtypeStruct((B, S, 4 * LANES), BF16),
        jax.ShapeDtypeStruct((B, S, 4 * LANES), BF16),
        jax.ShapeDtypeStruct((B, S, M_WIDTH), BF16),
        jax.ShapeDtypeStruct((B, S, M_WIDTH), BF16),
        jax.ShapeDtypeStruct((B, S // M_CHUNK, M_WIDTH, M_CHUNK), BF16),
        jax.ShapeDtypeStruct((B, S, M_WIDTH), BF16),
        jax.ShapeDtypeStruct((B, S, 2 * D_MODEL), BF16),
        jax.ShapeDtypeStruct((B, S // M_CHUNK, N_MGATES, M_CHUNK), F32),
    )
    chunked = lambda rows: pl.BlockSpec((1, tm // M_CHUNK, rows, M_CHUNK), lambda b, i: (b, i, 0, 0))
    out_specs = (
        tok(A_WIDTH), tok(4 * LANES), tok(4 * LANES), tok(M_WIDTH), tok(M_WIDTH), chunked(M_WIDTH), tok(M_WIDTH),
        tok(2 * D_MODEL), chunked(N_MGATES),
    )
    gb_col = gate_b.reshape(N_MGATES, 1)
    g2 = norm_g.reshape(1, D)
    return pl.pallas_call(
        _in_proj_kernel,
        grid=(B, nt),
        in_specs=[
            tok(D),
            pl.BlockSpec((1, 6, D), lambda b, i: (b, 0, 0)),
            const2(g2), const2(w_main), const2(w_t), const2(gb_col),
            tab, tab, tab,
        ],
        out_specs=out_specs,
        out_shape=out_shapes,
        compiler_params=_compiler_params(("parallel", "parallel")),
        name="in_proj",
    )(x, mod, g2, w_main, w_t, gb_col, cos, slo, shi)


def _attn_kernel(sink_ref, q_ref, kx_ref, vx_ref, o_ref):
    S = q_ref.shape[1]
    nb = S // BLOCK
    kw = 3 * BLOCK
    qi = lax.broadcasted_iota(jnp.int32, (BLOCK, kw), 0)
    ki = lax.broadcasted_iota(jnp.int32, (BLOCK, kw), 1)
    rel0 = ki - qi
    ones_b = jnp.ones((kw, LANES), BF16)
    left = lax.broadcasted_iota(jnp.int32, (BLOCK, LANES), 1) < A_HEAD_DIM

    def block(n, carry):
        q0 = pl.multiple_of(n * BLOCK, BLOCK)
        k0 = pl.multiple_of(jnp.clip((n - 1) * BLOCK, 0, S - kw), BLOCK)
        rel = rel0 + (k0 - q0)
        valid = jnp.abs(rel) <= WINDOW
        scores = []
        for hk in range(A_KV_HEADS):
            for j in range(A_GROUP // 2):
                col = (hk * (A_GROUP // 2) + j) * LANES
                qp = q_ref[0, pl.ds(q0, BLOCK), col:col + LANES]
                for side in range(2):
                    kk = kx_ref[0, pl.ds(k0, kw), (2 * hk + side) * LANES:(2 * hk + side + 1) * LANES]
                    scores.append(_dot_nt(qp, kk))
        for hk in range(A_KV_HEADS):
            for j in range(A_GROUP // 2):
                col = (hk * (A_GROUP // 2) + j) * LANES
                outs, dens = [], []
                for side in range(2):
                    vv = vx_ref[0, pl.ds(k0, kw), (2 * hk + side) * LANES:(2 * hk + side + 1) * LANES]
                    sk = sink_ref[hk * A_GROUP + 2 * j + side] * LOG2E
                    s = jnp.where(valid, scores[(hk * (A_GROUP // 2) + j) * 2 + side], NEG_BIG)
                    m = jnp.maximum(jnp.max(s, axis=-1, keepdims=True), sk)
                    p = jnp.exp2(s - m).astype(BF16)
                    od = _dot(p, jnp.concatenate([vv, ones_b], axis=1))
                    outs.append(od[:, 0:LANES])
                    dens.append(od[:, LANES:2 * LANES] + jnp.exp2(sk - m))
                o = (outs[0] + outs[1]) / jnp.where(left, dens[0], dens[1])
                o_ref[0, pl.ds(q0, BLOCK), col:col + LANES] = o.astype(BF16)
        return carry

    lax.fori_loop(0, nb, block, 0, unroll=2)


def _attention(q, kx, vx, sink):
    B, S, _ = q.shape
    seq = lambda w: pl.BlockSpec((1, S, w), lambda b: (b, 0, 0))
    return pl.pallas_call(
        _attn_kernel,
        grid=(B,),
        in_specs=[pl.BlockSpec(memory_space=pltpu.SMEM), seq(A_WIDTH), seq(4 * LANES), seq(4 * LANES)],
        out_specs=seq(A_WIDTH),
        out_shape=jax.ShapeDtypeStruct((B, S, A_WIDTH), BF16),
        compiler_params=_compiler_params(("parallel",)),
        name="window_attn",
    )(sink, q, kx, vx)


def _conv_silu(u_ref, w_ref, pad_ref, dst_ref, scale):
    S = u_ref.shape[1]
    pad_ref[0:SUBLANES, :] = jnp.zeros((SUBLANES, LANES), F32)
    pad_ref[S + SUBLANES:S + 2 * SUBLANES, :] = jnp.zeros((SUBLANES, LANES), F32)
    pad_ref[SUBLANES:S + SUBLANES, :] = u_ref[0].astype(F32)
    w0, w1, w2 = w_ref[0:1, :], w_ref[1:2, :], w_ref[2:3, :]
    for c in range(S // M_CHUNK):
        base = SUBLANES + c * M_CHUNK
        y = (pad_ref[base - 1:base - 1 + M_CHUNK, :] * w0 + pad_ref[base:base + M_CHUNK, :] * w1
             + pad_ref[base + 1:base + 1 + M_CHUNK, :] * w2)
        y = y * _sigmoid(y)
        dst_ref[c * M_CHUNK:(c + 1) * M_CHUNK, :] = y * scale


def _split3(x):
    hi = x.astype(BF16)
    r = x - hi.astype(F32)
    mid = r.astype(BF16)
    lo = (r - mid.astype(F32)).astype(BF16)
    return [hi, mid, lo]


def _mlstm_kernel(mq_ref, mk_ref, vt_ref, og_ref, grow_ref, cwq_ref, cwk_ref, hgt_ref, o_ref,
                  pad_ref, qs_ref, ks_ref, rb_ref, ib_ref, rows_ref, cinc_ref, ninc_ref, cin_ref, sin_ref,
                  c_ref, n_ref):
    S = mq_ref.shape[1]
    nc = S // M_CHUNK
    L = M_CHUNK
    DH = M_HEAD_DIM
    C_GROUP = 4 if nc % 4 == 0 else 1
    _conv_silu(mq_ref, cwq_ref, pad_ref, qs_ref, 1.0)
    _conv_silu(mk_ref, cwk_ref, pad_ref, ks_ref, M_HEAD_DIM ** -0.5)

    s_i = lax.broadcasted_iota(jnp.int32, (L, L), 0)
    t_i = lax.broadcasted_iota(jnp.int32, (L, L), 1)
    tris = (s_i <= t_i, s_i >= t_i)
    eye = s_i == t_i
    row8 = lax.broadcasted_iota(jnp.int32, (SUBLANES, L), 0)
    one_if = lambda cond: jnp.where(cond, 1.0, 0.0)
    k_j = lax.broadcasted_iota(jnp.int32, (3 * L, 2 * L), 0) % L
    c_j = lax.broadcasted_iota(jnp.int32, (3 * L, 2 * L), 1)
    sum_rows = jnp.where(c_j < L, one_if(k_j <= c_j), one_if(k_j >= c_j - L)).astype(BF16)

    g_all = grow_ref[0, 0].reshape(nc * SUBLANES, L)
    rb_ref[...] = _dot(jnp.concatenate(_split3(g_all), axis=1), sum_rows).reshape(nc, SUBLANES, 2 * L)
    ones_b = jnp.ones((2 * L, L), BF16)

    def phase_a(c, carry):
        r0 = pl.multiple_of(c * L, L)
        gr = grow_ref[0, 0, c]
        rb = rb_ref[c]
        kb = ks_ref[pl.ds(r0, L), :].astype(BF16)
        vt = vt_ref[0, c].astype(F32)
        wvs, wks, diags = [], [], []
        for d in range(2):
            brow = rb[2 * d + 1:2 * d + 2, d * L:(d + 1) * L]
            blast = brow[:, L - 1:L] if d == 0 else brow[:, 0:1]
            ibr = gr[2 * d:2 * d + 1, :] - brow
            log_g = blast + ibr
            mg = jnp.max(log_g, axis=-1, keepdims=True)
            wk = jnp.exp(log_g - mg)
            wvs.append((vt * wk).astype(BF16))
            wks.append(wk)
            diags.append(jnp.concatenate(
                [jnp.where(eye, term.astype(F32), 0.0).astype(BF16) for term in _split3(ibr)[:2]], axis=1))
            rows_ref[c, 2 + d:3 + d, :] = brow
            rows_ref[c, 4 + d:5 + d, :] = jnp.broadcast_to(mg, (1, L))
            rows_ref[c, 6 + d:7 + d, :] = jnp.broadcast_to(blast, (1, L))
        ib = _dot(jnp.concatenate(diags, axis=0), ones_b)
        ib_ref[c] = ib
        for d in range(2):
            rows_ref[c, d:d + 1, :] = jnp.max(jnp.where(tris[d], ib[d * L:(d + 1) * L, :], NEG_BIG),
                                              axis=0, keepdims=True)
        cinc_ref[c] = _dot(jnp.concatenate(wvs, axis=0), kb)
        wk8 = jnp.where(row8 == 0, wks[0], jnp.where(row8 == 1, wks[1], 0.0))
        ninc_ref[c] = _dot(wk8.astype(BF16), kb)
        return carry

    def phase_b(j, ms):
        new_ms = []
        for d, cc in ((0, j), (1, nc - 1 - j)):
            m = ms[d]
            half = slice(d * DH, (d + 1) * DH)
            cst = c_ref[half, :]
            n = n_ref[d:d + 1, :]
            cin_ref[cc, half, :] = cst.astype(BF16)
            sin_ref[cc, d:d + 1, :] = n
            sin_ref[cc, 2 + d:3 + d, :] = m
            mg = rows_ref[cc, 4 + d:5 + d, :]
            blast = rows_ref[cc, 6 + d:7 + d, :]
            m_new = jnp.maximum(blast + m, mg)
            decay = jnp.exp(blast + m - m_new)
            grow = jnp.exp(mg - m_new)
            c_ref[half, :] = decay * cst + grow * cinc_ref[cc, half, :]
            n_ref[d:d + 1, :] = decay * n + grow * ninc_ref[cc, d:d + 1, :]
            new_ms.append(m_new)
        return tuple(new_ms)

    def phase_c_products(c):
        r0 = pl.multiple_of(c * L, L)
        qb = qs_ref[pl.ds(r0, L), :].astype(BF16)
        qk_t = _dot_nt(ks_ref[pl.ds(r0, L), :].astype(BF16), qb)
        qc_t = _dot_nt(cin_ref[c], qb)
        qn = _dot_nt(sin_ref[c].astype(BF16), qb)
        return qk_t, qc_t, qn

    def phase_c_finish(c, qk_t, qc_t, qn):
        r0 = pl.multiple_of(c * L, L)
        sin = sin_ref[c]
        rows = rows_ref[c]
        ats, stats = [], []
        for d in range(2):
            m_in = sin[2 + d:3 + d, :]
            cm = jnp.maximum(m_in, rows[d:d + 1, :])
            a_t = qk_t * jnp.where(tris[d], jnp.exp(ib_ref[c, d * L:(d + 1) * L, :] - cm), 0.0)
            w_inter = jnp.exp(m_in - cm)
            den = w_inter * qn[d:d + 1, :] + jnp.sum(a_t, axis=0, keepdims=True)
            m_t = rows[2 + d:3 + d, :] + cm
            ats.append(a_t.astype(BF16))
            stats.append((w_inter, jnp.maximum(jnp.abs(den), jnp.exp(-m_t))))
        av_t = _dot(vt_ref[0, c], jnp.concatenate(ats, axis=1))
        h_t = None
        for d in range(2):
            w_inter, den = stats[d]
            hd = (w_inter * qc_t[d * DH:(d + 1) * DH, :] + av_t[:, d * L:(d + 1) * L]) / den
            h_t = hd if h_t is None else h_t + hd
        y_t = h_t * lax.rsqrt(jnp.mean(h_t * h_t, axis=0, keepdims=True) + EPS) * hgt_ref[...]
        o_ref[0, pl.ds(r0, L), :] = (og_ref[0, pl.ds(r0, L), :].astype(F32) * y_t.T).astype(BF16)

    def phase_c(g, carry):
        chunks = [g * C_GROUP + i for i in range(C_GROUP)]
        products = [phase_c_products(c) for c in chunks]
        for c, prod in zip(chunks, products):
            phase_c_finish(c, *prod)
        return carry

    lax.fori_loop(0, nc, phase_a, 0, unroll=4)
    c_ref[...] = jnp.zeros(c_ref.shape, F32)
    n_ref[...] = jnp.zeros(n_ref.shape, F32)
    m0 = jnp.zeros((1, LANES), F32)
    lax.fori_loop(0, nc, phase_b, (m0, m0))
    lax.fori_loop(0, nc // C_GROUP, phase_c, 0, unroll=2)


def _mlstm(mq, mk, vt, og, grow_h, conv_wq, conv_wk, head_g_t):
    B, S, _ = mq.shape
    nc = S // M_CHUNK
    head = pl.BlockSpec((1, S, M_HEAD_DIM), lambda b, h: (b, 0, h))
    cw = pl.BlockSpec((CONV_W, M_HEAD_DIM), lambda b, h: (0, h))
    return pl.pallas_call(
        _mlstm_kernel,
        grid=(B, M_HEADS),
        in_specs=[
            head, head,
            pl.BlockSpec((1, nc, M_HEAD_DIM, M_CHUNK), lambda b, h: (b, 0, h, 0)),
            head,
            pl.BlockSpec((1, 1, nc, SUBLANES, M_CHUNK), lambda b, h: (b, h, 0, 0, 0)),
            cw, cw,
            pl.BlockSpec((M_HEAD_DIM, LANES), lambda b, h: (h, 0)),
        ],
        out_specs=head,
        out_shape=jax.ShapeDtypeStruct((B, S, M_WIDTH), BF16),
        scratch_shapes=[
            pltpu.VMEM((S + 2 * SUBLANES, LANES), F32),
            pltpu.VMEM((S, M_HEAD_DIM), F32),
            pltpu.VMEM((S, M_HEAD_DIM), F32),
            pltpu.VMEM((nc, SUBLANES, 2 * M_CHUNK), F32),
            pltpu.VMEM((nc, 2 * M_CHUNK, M_CHUNK), F32),
            pltpu.VMEM((nc, SUBLANES, M_CHUNK), F32),
            pltpu.VMEM((nc, 2 * M_HEAD_DIM, M_HEAD_DIM), F32),
            pltpu.VMEM((nc, SUBLANES, M_HEAD_DIM), F32),
            pltpu.VMEM((nc, 2 * M_HEAD_DIM, M_HEAD_DIM), BF16),
            pltpu.VMEM((nc, SUBLANES, M_HEAD_DIM), F32),
            pltpu.VMEM((2 * M_HEAD_DIM, M_HEAD_DIM), F32),
            pltpu.VMEM((SUBLANES, M_HEAD_DIM), F32),
        ],
        compiler_params=_compiler_params(("parallel", "parallel")),
        name="mlstm",
    )(mq, mk, vt, og, grow_h, conv_wq, conv_wk, head_g_t)


def _route(logits):
    lane = lax.broadcasted_iota(jnp.int32, logits.shape, 1)
    lane_f = lane.astype(F32)
    big = float(LANES)
    is_g = (lane >= ROUTER_G_LANE) & (lane < ROUTER_G_LANE + N_GROUPS)
    gl = jnp.where(is_g, logits, NEG_BIG)
    gmax = jnp.max(gl, axis=-1, keepdims=True)
    gsum = jnp.sum(jnp.where(is_g, jnp.exp(gl - gmax), 0.0), axis=-1, keepdims=True)
    p_grp = 1.0 / gsum
    grp = jnp.min(jnp.where(is_g & (gl == gmax), lane_f - ROUTER_G_LANE, big), axis=-1, keepdims=True)
    in_grp = (lane < N_EXPERTS) & (jnp.right_shift(lane, 2).astype(F32) == grp)
    el = jnp.where(in_grp, logits, NEG_BIG)
    v1 = jnp.max(el, axis=-1, keepdims=True)
    i1 = jnp.min(jnp.where(in_grp & (el == v1), lane_f, big), axis=-1, keepdims=True)
    rest = in_grp & (lane_f != i1)
    el2 = jnp.where(rest, logits, NEG_BIG)
    v2 = jnp.max(el2, axis=-1, keepdims=True)
    i2 = jnp.min(jnp.where(rest & (el2 == v2), lane_f, big), axis=-1, keepdims=True)
    e21 = jnp.exp(v2 - v1)
    w1 = p_grp / (1.0 + e21)
    w2 = p_grp * e21 / (1.0 + e21)
    return jnp.where(lane == 0, i1, jnp.where(lane == 1, i2, jnp.where(lane == 2, w1, w2)))


def _pack_bf16_pairs(lo, hi):
    bits = lambda v: lax.bitcast_convert_type(v.astype(BF16).astype(F32), PACKED)
    return jnp.right_shift(bits(lo), PACKED(16)) | (bits(hi) & PACKED(0xFFFF0000))


def _unpack_bf16_pairs(w):
    return (lax.bitcast_convert_type(jnp.left_shift(w, PACKED(16)), F32),
            lax.bitcast_convert_type(w & PACKED(0xFFFF0000), F32))


def _pack_row_halves(y):
    return (_pack_bf16_pairs(y[:, 0:PACK_W], y[:, PACK_W:2 * PACK_W]),
            _pack_bf16_pairs(y[:, 2 * PACK_W:3 * PACK_W], y[:, 3 * PACK_W:4 * PACK_W]))


def _unpack_row_halves(a, b):
    return [*_unpack_bf16_pairs(a), *_unpack_bf16_pairs(b)]


def _merge_kernel(attn_ref, mo_ref, bg_ref, x_ref, mod_ref, wua_ref, wum_ref, wo_ref, g2_ref, wr_ref, br_ref,
                  x1_ref, ha_ref, hb_ref, rt_ref):
    up_a = _dot(attn_ref[0], wua_ref[...])
    up_m = _dot(mo_ref[0], wum_ref[...])
    merged = (bg_ref[0, :, 0:D_MODEL].astype(F32) * up_a + bg_ref[0, :, D_MODEL:2 * D_MODEL].astype(F32) * up_m)
    x1 = x_ref[0] + mod_ref[0, 2:3, :] * _dot(merged.astype(BF16), wo_ref[...])
    x1_ref[0] = x1
    h2 = _rms_mod(x1, g2_ref[...], mod_ref[0, 4:5, :], mod_ref[0, 3:4, :])
    ha_ref[0], hb_ref[0] = _pack_row_halves(h2)
    logits = _dot(h2.astype(BF16), wr_ref[...].astype(BF16)) + br_ref[...]
    rt_ref[0] = _route(logits)[:, 0:ROUTE_W]


def _merge(attn, mo, bg, x, mod, wua, wum, wo, norm_g, w_router, b_router, tm):
    B, S, D = x.shape
    tok = lambda w: pl.BlockSpec((1, tm, w), lambda b, i: (b, i, 0))
    const2 = lambda a: pl.BlockSpec(a.shape, lambda b, i: (0, 0))
    g2 = norm_g.reshape(1, D)
    return pl.pallas_call(
        _merge_kernel,
        grid=(B, S // tm),
        in_specs=[
            tok(A_WIDTH), tok(M_WIDTH), tok(2 * D), tok(D),
            pl.BlockSpec((1, 6, D), lambda b, i: (b, 0, 0)),
            const2(wua), const2(wum), const2(wo), const2(g2), const2(w_router), const2(b_router),
        ],
        out_specs=(tok(D), tok(PACK_W), tok(PACK_W), tok(ROUTE_W)),
        out_shape=(
            jax.ShapeDtypeStruct((B, S, D), F32),
            jax.ShapeDtypeStruct((B, S, PACK_W), PACKED),
            jax.ShapeDtypeStruct((B, S, PACK_W), PACKED),
            jax.ShapeDtypeStruct((B, S, ROUTE_W), F32),
        ),
        compiler_params=_compiler_params(("parallel", "parallel")),
        name="merge_route",
    )(attn, mo, bg, x, mod, wua, wum, wo, g2, w_router, b_router)


def _route_tables(eid, tmb):
    T = eid.shape[0]
    nblk = 2 * T // tmb + N_EXPERTS
    ea = eid.T.reshape(-1)
    onehot = (ea[:, None] == jnp.arange(N_EXPERTS, dtype=jnp.int32)[None, :]).astype(jnp.int32)
    csum = jnp.cumsum(onehot, axis=0)
    counts = csum[-1]
    rank = jnp.sum(onehot * csum, axis=1) - 1
    padded = ((counts + tmb - 1) // tmb) * tmb
    ends = jnp.cumsum(padded)
    starts = ends - padded
    pos = (jnp.sum(onehot * starts[None, :], axis=1) + rank).astype(jnp.int32)
    blk0 = jnp.arange(nblk, dtype=jnp.int32) * tmb
    bexp = jnp.minimum(jnp.sum((blk0[:, None] >= ends[None, :]).astype(jnp.int32), axis=1), N_EXPERTS - 1)
    nval = jnp.clip(starts[bexp] + counts[bexp] - blk0, 0, tmb)
    nval = jnp.where(blk0 < ends[-1], nval, 0).astype(jnp.int32)
    return pos, bexp.astype(jnp.int32), nval


def _sc_mesh():
    return plsc.VectorSubcoreMesh(core_axis_name="c", subcore_axis_name="s")


def _sc_dispatch(x, pos, n_rows):
    T = x.shape[0]
    nb = T // SC_WINDOW
    idx = pos.reshape(1, 2 * T)

    @pl.kernel(out_type=jax.ShapeDtypeStruct((n_rows, PACK_W), x.dtype), mesh=_sc_mesh(), scratch_types=[])
    def dispatch(x_hbm, i_hbm, o_hbm):
        def body(x_vmem, i_vmem):
            pltpu.sync_copy(x_vmem, o_hbm.at[i_vmem.at[0]])

        pltpu.emit_pipeline(
            body,
            grid=(2 * nb,),
            in_specs=[pl.BlockSpec((SC_WINDOW, PACK_W), index_map=lambda i: (i % nb, 0)),
                      pl.BlockSpec((1, SC_WINDOW), index_map=lambda i: (0, i))],
            out_specs=[],
            core_axis_name=("c", "s"),
            dimension_semantics=(pltpu.PARALLEL,),
        )(x_hbm, i_hbm)

    return dispatch(x, idx)


def _sc_combine(y, pos):
    n = pos.shape[0]
    idx = pos.reshape(1, n)

    @pl.kernel(out_type=jax.ShapeDtypeStruct((n, PACK_W), y.dtype), mesh=_sc_mesh(), scratch_types=[])
    def combine(y_hbm, i_hbm, o_hbm):
        def body(i_vmem, o_vmem):
            pltpu.sync_copy(y_hbm.at[i_vmem.at[0]], o_vmem)

        pltpu.emit_pipeline(
            body,
            grid=(n // SC_WINDOW,),
            in_specs=[pl.BlockSpec((1, SC_WINDOW), index_map=lambda i: (0, i))],
            out_specs=[pl.BlockSpec((SC_WINDOW, PACK_W), index_map=lambda i: (i, 0))],
            core_axis_name=("c", "s"),
            dimension_semantics=(pltpu.PARALLEL,),
        )(i_hbm, o_hbm)

    return combine(y, idx)


def _experts_kernel(bexp_ref, nval_ref, xa_ref, xb_ref, wg_ref, wu_ref, wd_ref, ya_ref, yb_ref):
    nv = nval_ref[pl.program_id(0)]

    @pl.when(nv > 0)
    def _():
        keep = lax.broadcasted_iota(jnp.int32, xa_ref.shape, 0) < nv
        pieces = [jnp.where(keep, piece, 0.0).astype(BF16) for piece in _unpack_row_halves(xa_ref[...], xb_ref[...])]
        x = jnp.concatenate(pieces, axis=1)
        g = _dot(x, wg_ref[0])
        u = _dot(x, wu_ref[0])
        he = (g * _sigmoid(g)) * u
        ya_ref[...], yb_ref[...] = _pack_row_halves(_dot(he.astype(BF16), wd_ref[0]))


def _experts(xa, xb, bexp, nval, wg, wu, wd, tmb):
    n = xa.shape[0]
    row = lambda: pl.BlockSpec((tmb, PACK_W), lambda i, be, nv: (i, 0))
    grid_spec = pltpu.PrefetchScalarGridSpec(
        num_scalar_prefetch=2,
        grid=(n // tmb,),
        in_specs=[row(), row(),
                  pl.BlockSpec((1, D_MODEL, D_EXPERT), lambda i, be, nv: (be[i], 0, 0)),
                  pl.BlockSpec((1, D_MODEL, D_EXPERT), lambda i, be, nv: (be[i], 0, 0)),
                  pl.BlockSpec((1, D_EXPERT, D_MODEL), lambda i, be, nv: (be[i], 0, 0))],
        out_specs=[row(), row()],
    )
    out = jax.ShapeDtypeStruct((n, PACK_W), xa.dtype)
    return pl.pallas_call(
        _experts_kernel, grid_spec=grid_spec, out_shape=(out, out),
        compiler_params=_compiler_params(("arbitrary",)),
        name="experts",
    )(bexp, nval, xa, xb, wg, wu, wd)


def _finish_kernel(a0_ref, b0_ref, a1_ref, b1_ref, rt_ref, x1_ref, mod_ref, fg_ref, o_ref):
    y0 = jnp.concatenate(_unpack_row_halves(a0_ref[...], b0_ref[...]), axis=1)
    y1 = jnp.concatenate(_unpack_row_halves(a1_ref[...], b1_ref[...]), axis=1)
    rt = rt_ref[0]
    moe = rt[:, 2:3] * y0 + rt[:, 3:4] * y1
    xo = x1_ref[0] + mod_ref[0, 5:6, :] * moe
    ms = jnp.mean(xo * xo, axis=-1, keepdims=True)
    o_ref[0] = xo * lax.rsqrt(ms + EPS) * fg_ref[...]


def _finish(ca, cb, rt, x1, mod, final_g, tm):
    B, S, D = x1.shape
    nt = S // tm
    second = B * nt
    half = lambda k: pl.BlockSpec((tm, PACK_W), lambda b, i: (k * second + b * nt + i, 0))
    tok = lambda w: pl.BlockSpec((1, tm, w), lambda b, i: (b, i, 0))
    fg = final_g.reshape(1, D)
    return pl.pallas_call(
        _finish_kernel,
        grid=(B, nt),
        in_specs=[half(0), half(0), half(1), half(1), tok(ROUTE_W), tok(D),
                  pl.BlockSpec((1, 6, D), lambda b, i: (b, 0, 0)),
                  pl.BlockSpec((1, D), lambda b, i: (0, 0))],
        out_specs=tok(D),
        out_shape=jax.ShapeDtypeStruct((B, S, D), F32),
        compiler_params=_compiler_params(("parallel", "parallel")),
        name="finish",
    )(ca, cb, ca, cb, rt, x1, mod, fg)


def _moe(ha, hb, rt, x1, mod, wg, wu, wd, final_g):
    B, S, D = x1.shape
    T = B * S
    tmb = EXPERT_ROWS
    eid = rt.reshape(T, ROUTE_W)[:, 0:2].astype(jnp.int32)
    pos, bexp, nval = _route_tables(eid, tmb)
    n_rows = 2 * T + N_EXPERTS * tmb
    xa = _sc_dispatch(ha.reshape(T, PACK_W), pos, n_rows)
    xb = _sc_dispatch(hb.reshape(T, PACK_W), pos, n_rows)
    ya, yb = _experts(xa, xb, bexp, nval, wg, wu, wd, tmb)
    return _finish(_sc_combine(ya, pos), _sc_combine(yb, pos), rt, x1, mod, final_g, tm=512)


def _rope_tables(S):
    half = ROPE_DIM // 2
    inv_freq = 1.0 / (ROPE_THETA ** (jnp.arange(half, dtype=F32) * 2.0 / ROPE_DIM))
    ang = jnp.arange(S, dtype=F32)[:, None] * inv_freq[None, :]
    cos, sin = jnp.cos(ang), jnp.sin(ang)
    zeros = jnp.zeros((S, A_HEAD_DIM - ROPE_DIM), F32)
    z8 = jnp.zeros((S, half), F32)
    cos_h = jnp.concatenate([cos, cos, jnp.ones_like(zeros)], axis=-1)
    slo_h = jnp.concatenate([-sin, z8, zeros], axis=-1)
    shi_h = jnp.concatenate([z8, sin, zeros], axis=-1)
    rep = LANES // A_HEAD_DIM
    return tuple(jnp.tile(t, (1, rep)) for t in (cos_h, slo_h, shi_h))


def _layer(x, c, p, tabs):
    B, S, D = x.shape
    mod = _ada_mod(c, p["ada_w"], p["ada_b"]).reshape(B, 6, D)
    q, kx, vx, mq, mk, vt, og, bg, grow = _in_proj(
        x, mod, p["norm1_g"], p["w_main"], p["w_t"], p["m_gate_b"], tabs, tm=512)
    attn = _attention(q, kx, vx, p["attn_sink"])
    nc = S // M_CHUNK
    grow_h = grow.reshape(B, nc, 4, M_HEADS, M_CHUNK).transpose(0, 3, 1, 2, 4)
    grow_h = jnp.pad(grow_h, ((0, 0), (0, 0), (0, 0), (0, SUBLANES - 4), (0, 0)))
    mo = _mlstm(mq, mk, vt, og, grow_h, p["conv_wq"], p["conv_wk"], p["head_norm_g_t"])
    x1, ha, hb, rt = _merge(attn, mo, bg, x, mod, p["w_up_attn"], p["w_up_mlstm"], p["w_out"], p["norm2_g"],
                            p["w_router"], p["b_router"], tm=512)
    return _moe(ha, hb, rt, x1, mod, p["w_gate"], p["w_up"], p["w_down"], p["final_norm_g"])


def kernel(x_prompt, x_sample, c_prompt, c_sample, ada_w, ada_b, norm1_g, w_in, conv_w, m_gate_b, attn_sink,
           head_norm_g, w_up_attn, w_up_mlstm, w_out, norm2_g, rg_w, rg_b, re_w, re_b, w_gate, w_up, w_down,
           final_norm_g):
    assert ada_w.shape[0] == 1, "single-layer trunk"
    w_in0 = w_in[0]
    w_g = w_in0[:, OFF_MG:OFF_BG]
    pad = LANES - N_EXPERTS - N_GROUPS
    p = dict(
        ada_w=ada_w[0], ada_b=ada_b[0], norm1_g=norm1_g[0],
        w_main=jnp.concatenate([w_in0[:, :OFF_MV], w_in0[:, OFF_MO:OFF_MG], w_in0[:, OFF_BG:]], axis=1).astype(BF16),
        w_t=jnp.concatenate([w_in0[:, OFF_MV:OFF_MO], w_g], axis=1).T.astype(BF16),
        m_gate_b=m_gate_b[0], attn_sink=attn_sink[0],
        conv_wq=conv_w[0, :, :M_WIDTH], conv_wk=conv_w[0, :, M_WIDTH:],
        head_norm_g_t=jnp.broadcast_to(head_norm_g[0][:, None], (M_WIDTH, LANES)),
        w_up_attn=w_up_attn[0].astype(BF16), w_up_mlstm=w_up_mlstm[0].astype(BF16), w_out=w_out[0].astype(BF16),
        norm2_g=norm2_g[0],
        w_router=jnp.pad(jnp.concatenate([re_w[0], rg_w[0]], axis=1), ((0, 0), (0, pad))),
        b_router=jnp.pad(jnp.concatenate([re_b[0], rg_b[0]]), (0, pad)).reshape(1, LANES),
        w_gate=w_gate[0].astype(BF16), w_up=w_up[0].astype(BF16), w_down=w_down[0].astype(BF16),
        final_norm_g=final_norm_g,
    )
    tabs = _rope_tables(x_prompt.shape[1])
    return (_layer(x_prompt, c_prompt, p, tabs), _layer(x_sample, c_sample, p, tabs))
```

```python
import functools

import jax
import jax.numpy as jnp
from jax import lax
from jax.experimental import pallas as pl
from jax.experimental.pallas import tpu as pltpu
from jax.experimental.pallas import tpu_sc as plsc

D_MODEL = 1024
A_HEADS = 8
A_KV_HEADS = 2
A_GROUP = A_HEADS // A_KV_HEADS
A_HEAD_DIM = 64
A_WIDTH = A_HEADS * A_HEAD_DIM
A_KV_WIDTH = A_KV_HEADS * A_HEAD_DIM
WINDOW = 128
BLOCK = 128
ROPE_DIM = A_HEAD_DIM // 4
ROPE_THETA = 500000.0
M_HEADS = 4
M_HEAD_DIM = 128
M_WIDTH = M_HEADS * M_HEAD_DIM
M_CHUNK = 128
CONV_W = 3
OFF_AQ = 0
OFF_AK = OFF_AQ + A_WIDTH
OFF_AV = OFF_AK + A_KV_WIDTH
OFF_MQ = OFF_AV + A_KV_WIDTH
OFF_MK = OFF_MQ + M_WIDTH
OFF_MV = OFF_MK + M_WIDTH
OFF_MO = OFF_MV + M_WIDTH
OFF_MG = OFF_MO + M_WIDTH
N_MGATES = 4 * M_HEADS
OFF_BG = OFF_MG + N_MGATES
IN_TOTAL = OFF_BG + 2 * D_MODEL
N_GROUPS = 4
EXPERTS_PER_GROUP = 4
N_EXPERTS = N_GROUPS * EXPERTS_PER_GROUP
D_EXPERT = 512
EPS = 1e-6
NEG_BIG = -1e30
F32 = jnp.float32
BF16 = jnp.bfloat16

LANES = 128
SUBLANES = 8
VMEM_LIMIT_BYTES = 56 * 1024 * 1024

MAIN_AQ = 0
MAIN_KV = A_WIDTH
MAIN_MQ = MAIN_KV + 2 * A_KV_WIDTH
MAIN_MK = MAIN_MQ + M_WIDTH
MAIN_MO = MAIN_MK + M_WIDTH
MAIN_BG = MAIN_MO + M_WIDTH
MAIN_TOTAL = MAIN_BG + 2 * D_MODEL

LOG2E = 1.4426950408889634
Q_SCALE = A_HEAD_DIM ** -0.5 * LOG2E
ROUTER_G_LANE = N_EXPERTS
ROUTE_W = 4
PACKED = jnp.uint32
PACK_W = D_MODEL // 4
SC_WINDOW = 128
EXPERT_ROWS = 512
MERGE_ROWS = 256


def _sigmoid(z):
    return 1.0 / (1.0 + jnp.exp(-z))


def _log_sigmoid(z):
    return jnp.minimum(z, 0.0) - jnp.log(1.0 + jnp.exp(-jnp.abs(z)))


def _is_forget(gate_idx):
    return jnp.bitwise_and(jnp.right_shift(gate_idx, 2), 1) == 1


def _dot(a, b):
    return jnp.dot(a, b, preferred_element_type=F32)


def _dot_nt(a, b):
    return lax.dot_general(a, b, (((1,), (1,)), ((), ())), preferred_element_type=F32)


def _dot_tn(a, b):
    return lax.dot_general(a, b, (((0,), (0,)), ((), ())), preferred_element_type=F32)


def _dot_f32(a, b):
    return jnp.dot(a, b, preferred_element_type=F32, precision=lax.Precision.HIGHEST)


def _compiler_params(semantics):
    return pltpu.CompilerParams(dimension_semantics=semantics, vmem_limit_bytes=VMEM_LIMIT_BYTES)


def _ada_kernel(c_ref, w_ref, b_ref, o_ref):
    c = c_ref[...]
    o_ref[...] = _dot_f32(c * _sigmoid(c), w_ref[...]) + b_ref[...]


def _ada_mod(c, ada_w, ada_b):
    B, D = c.shape
    n = ada_w.shape[1] // D
    return pl.pallas_call(
        _ada_kernel,
        grid=(n,),
        in_specs=[
            pl.BlockSpec((B, D), lambda j: (0, 0)),
            pl.BlockSpec((D, D), lambda j: (0, j)),
            pl.BlockSpec((1, D), lambda j: (0, j)),
        ],
        out_specs=pl.BlockSpec((B, D), lambda j: (0, j)),
        out_shape=jax.ShapeDtypeStruct((B, n * D), F32),
        compiler_params=_compiler_params(("arbitrary",)),
        name="ada_mod",
    )(c, ada_w, ada_b.reshape(1, -1))


def _rms_mod(x, g, scale, shift):
    ms = jnp.mean(x * x, axis=-1, keepdims=True)
    return (x * lax.rsqrt(ms + EPS) * g) * (1.0 + scale) + shift


def _rope_block(xb, cos, sin_lo, sin_hi):
    half = ROPE_DIM // 2
    return xb * cos + pltpu.roll(xb, LANES - half, axis=1) * sin_lo + pltpu.roll(xb, half, axis=1) * sin_hi


def _in_proj_kernel(x_ref, mod_ref, g_ref, w_ref, wt_ref, gb_col_ref, cos_ref, slo_ref, shi_ref,
                    q_ref, kx_ref, vx_ref, mq_ref, mk_ref, vt_ref, og_ref, bg_ref, grow_ref):
    x = x_ref[0]
    tm = x.shape[0]
    h = _rms_mod(x, g_ref[...], mod_ref[0, 1:2, :], mod_ref[0, 0:1, :])
    hb = h.astype(BF16)
    cos, slo, shi = cos_ref[...], slo_ref[...], shi_ref[...]
    lane = lax.broadcasted_iota(jnp.int32, (tm, LANES), 1)
    left = lane < A_HEAD_DIM

    pj = _dot(hb, w_ref[:, MAIN_AQ:MAIN_AQ + A_WIDTH])
    for j in range(A_WIDTH // LANES):
        blk = _rope_block(pj[:, j * LANES:(j + 1) * LANES], cos, slo, shi)
        q_ref[0, :, j * LANES:(j + 1) * LANES] = (blk * Q_SCALE).astype(BF16)

    pj = _dot(hb, w_ref[:, MAIN_KV:MAIN_KV + 2 * A_KV_WIDTH])
    kk = _rope_block(pj[:, 0:LANES], cos, slo, shi)
    vv = pj[:, LANES:2 * LANES]
    for src, dst in ((kk, kx_ref), (vv, vx_ref)):
        swapped = pltpu.roll(src, A_HEAD_DIM, axis=1)
        zero = jnp.zeros_like(src)
        dst[0, :, 0 * LANES:1 * LANES] = jnp.where(left, src, zero).astype(BF16)
        dst[0, :, 1 * LANES:2 * LANES] = jnp.where(left, zero, swapped).astype(BF16)
        dst[0, :, 2 * LANES:3 * LANES] = jnp.where(left, swapped, zero).astype(BF16)
        dst[0, :, 3 * LANES:4 * LANES] = jnp.where(left, zero, src).astype(BF16)

    mq_ref[0] = _dot(hb, w_ref[:, MAIN_MQ:MAIN_MQ + M_WIDTH]).astype(BF16)
    mk_ref[0] = _dot(hb, w_ref[:, MAIN_MK:MAIN_MK + M_WIDTH]).astype(BF16)
    og_ref[0] = _sigmoid(_dot(hb, w_ref[:, MAIN_MO:MAIN_MO + M_WIDTH])).astype(BF16)
    for j in range(2 * D_MODEL // M_WIDTH):
        lo = MAIN_BG + j * M_WIDTH
        bg_ref[0, :, j * M_WIDTH:(j + 1) * M_WIDTH] = _sigmoid(_dot(hb, w_ref[:, lo:lo + M_WIDTH])).astype(BF16)

    tr = _dot_nt(wt_ref[...], hb)
    gr = tr[M_WIDTH:M_WIDTH + N_MGATES, :] + gb_col_ref[...]
    ridx = lax.broadcasted_iota(jnp.int32, gr.shape, 0)
    gr = jnp.where(_is_forget(ridx), _log_sigmoid(gr), gr)
    vt = tr[0:M_WIDTH, :].astype(BF16)
    for c in range(tm // M_CHUNK):
        grow_ref[0, c] = gr[:, c * M_CHUNK:(c + 1) * M_CHUNK]
        vt_ref[0, c] = vt[:, c * M_CHUNK:(c + 1) * M_CHUNK]


def _in_proj(x, mod, norm_g, w_main, w_t, gate_b, rope_tabs, tm):
    B, S, D = x.shape
    nt = S // tm
    cos, slo, shi = rope_tabs
    tok = lambda w: pl.BlockSpec((1, tm, w), lambda b, i: (b, i, 0))
    const2 = lambda a: pl.BlockSpec(a.shape, lambda b, i: (0, 0))
    tab = pl.BlockSpec((tm, LANES), lambda b, i: (i, 0))
    out_shapes = (
        jax.ShapeDtypeStruct((B, S, A_WIDTH), BF16),
        jax.ShapeDtypeStruct((B, S, 4 * LANES), BF16),
        jax.ShapeDtypeStruct((B, S, 4 * LANES), BF16),
        jax.ShapeDtypeStruct((B, S, M_WIDTH), BF16),
        jax.ShapeDtypeStruct((B, S, M_WIDTH), BF16),
        jax.ShapeDtypeStruct((B, S // M_CHUNK, M_WIDTH, M_CHUNK), BF16),
        jax.ShapeDtypeStruct((B, S, M_WIDTH), BF16),
        jax.ShapeDtypeStruct((B, S, 2 * D_MODEL), BF16),
        jax.ShapeDtypeStruct((B, S // M_CHUNK, N_MGATES, M_CHUNK), F32),
    )
    chunked = lambda rows: pl.BlockSpec((1, tm // M_CHUNK, rows, M_CHUNK), lambda b, i: (b, i, 0, 0))
    out_specs = (
        tok(A_WIDTH), tok(4 * LANES), tok(4 * LANES), tok(M_WIDTH), tok(M_WIDTH), chunked(M_WIDTH), tok(M_WIDTH),
        tok(2 * D_MODEL), chunked(N_MGATES),
    )
    gb_col = gate_b.reshape(N_MGATES, 1)
    g2 = norm_g.reshape(1, D)
    return pl.pallas_call(
        _in_proj_kernel,
        grid=(B, nt),
        in_specs=[
            tok(D),
            pl.BlockSpec((1, 6, D), lambda b, i: (b, 0, 0)),
            const2(g2), const2(w_main), const2(w_t), const2(gb_col),
            tab, tab, tab,
        ],
        out_specs=out_specs,
        out_shape=out_shapes,
        compiler_params=_compiler_params(("parallel", "parallel")),
        name="in_proj",
    )(x, mod, g2, w_main, w_t, gb_col, cos, slo, shi)


def _attn_kernel(sink_ref, q_ref, kx_ref, vx_ref, o_ref):
    S = q_ref.shape[1]
    nb = S // BLOCK
    kw = 3 * BLOCK
    qi = lax.broadcasted_iota(jnp.int32, (BLOCK, kw), 0)
    ki = lax.broadcasted_iota(jnp.int32, (BLOCK, kw), 1)
    rel0 = ki - qi
    ones_b = jnp.ones((kw, LANES), BF16)
    left = lax.broadcasted_iota(jnp.int32, (BLOCK, LANES), 1) < A_HEAD_DIM

    def block(n, carry):
        q0 = pl.multiple_of(n * BLOCK, BLOCK)
        k0 = pl.multiple_of(jnp.clip((n - 1) * BLOCK, 0, S - kw), BLOCK)
        rel = rel0 + (k0 - q0)
        valid = jnp.abs(rel) <= WINDOW
        scores = []
        for hk in range(A_KV_HEADS):
            for j in range(A_GROUP // 2):
                col = (hk * (A_GROUP // 2) + j) * LANES
                qp = q_ref[0, pl.ds(q0, BLOCK), col:col + LANES]
                for side in range(2):
                    kk = kx_ref[0, pl.ds(k0, kw), (2 * hk + side) * LANES:(2 * hk + side + 1) * LANES]
                    scores.append(_dot_nt(qp, kk))
        for hk in range(A_KV_HEADS):
            for j in range(A_GROUP // 2):
                col = (hk * (A_GROUP // 2) + j) * LANES
                outs, dens = [], []
                for side in range(2):
                    vv = vx_ref[0, pl.ds(k0, kw), (2 * hk + side) * LANES:(2 * hk + side + 1) * LANES]
                    sk = sink_ref[hk * A_GROUP + 2 * j + side] * LOG2E
                    s = jnp.where(valid, scores[(hk * (A_GROUP // 2) + j) * 2 + side], NEG_BIG)
                    m = jnp.maximum(jnp.max(s, axis=-1, keepdims=True), sk)
                    p = jnp.exp2(s - m).astype(BF16)
                    od = _dot(p, jnp.concatenate([vv, ones_b], axis=1))
                    outs.append(od[:, 0:LANES])
                    dens.append(od[:, LANES:2 * LANES] + jnp.exp2(sk - m))
                o = (outs[0] + outs[1]) / jnp.where(left, dens[0], dens[1])
                o_ref[0, pl.ds(q0, BLOCK), col:col + LANES] = o.astype(BF16)
        return carry

    lax.fori_loop(0, nb, block, 0, unroll=2)


def _attention(q, kx, vx, sink):
    B, S, _ = q.shape
    seq = lambda w: pl.BlockSpec((1, S, w), lambda b: (b, 0, 0))
    return pl.pallas_call(
        _attn_kernel,
        grid=(B,),
        in_specs=[pl.BlockSpec(memory_space=pltpu.SMEM), seq(A_WIDTH), seq(4 * LANES), seq(4 * LANES)],
        out_specs=seq(A_WIDTH),
        out_shape=jax.ShapeDtypeStruct((B, S, A_WIDTH), BF16),
        compiler_params=_compiler_params(("parallel",)),
        name="window_attn",
    )(sink, q, kx, vx)


def _conv_silu(u_ref, w_ref, pad_ref, dst_ref, scale):
    S = u_ref.shape[1]
    pad_ref[0:SUBLANES, :] = jnp.zeros((SUBLANES, LANES), F32)
    pad_ref[S + SUBLANES:S + 2 * SUBLANES, :] = jnp.zeros((SUBLANES, LANES), F32)
    pad_ref[SUBLANES:S + SUBLANES, :] = u_ref[0].astype(F32)
    w0, w1, w2 = w_ref[0:1, :], w_ref[1:2, :], w_ref[2:3, :]
    for c in range(S // M_CHUNK):
        base = SUBLANES + c * M_CHUNK
        y = (pad_ref[base - 1:base - 1 + M_CHUNK, :] * w0 + pad_ref[base:base + M_CHUNK, :] * w1
             + pad_ref[base + 1:base + 1 + M_CHUNK, :] * w2)
        y = y * _sigmoid(y)
        dst_ref[c * M_CHUNK:(c + 1) * M_CHUNK, :] = y if scale == 1.0 else y * scale


def _split3(x):
    hi = x.astype(BF16)
    r = x - hi.astype(F32)
    mid = r.astype(BF16)
    lo = (r - mid.astype(F32)).astype(BF16)
    return [hi, mid, lo]


def _mlstm_kernel(mq_ref, mk_ref, vt_ref, og_ref, grow_ref, cwq_ref, cwk_ref, hgt_ref, o_ref,
                  pad_ref, qs_ref, ks_ref, rb_ref, ib_ref, rows_ref, cinc_ref, ninc_ref, cin_ref, sin_ref,
                  c_ref, n_ref):
    S = mq_ref.shape[1]
    nc = S // M_CHUNK
    L = M_CHUNK
    DH = M_HEAD_DIM
    C_GROUP = 4 if nc % 4 == 0 else 1
    _conv_silu(mq_ref, cwq_ref, pad_ref, qs_ref, 1.0)
    _conv_silu(mk_ref, cwk_ref, pad_ref, ks_ref, M_HEAD_DIM ** -0.5)

    s_i = lax.broadcasted_iota(jnp.int32, (L, L), 0)
    t_i = lax.broadcasted_iota(jnp.int32, (L, L), 1)
    tris = (s_i <= t_i, s_i >= t_i)
    eye = s_i == t_i
    row8 = lax.broadcasted_iota(jnp.int32, (SUBLANES, L), 0)
    one_if = lambda cond: jnp.where(cond, 1.0, 0.0)
    k_j = lax.broadcasted_iota(jnp.int32, (3 * L, 2 * L), 0) % L
    c_j = lax.broadcasted_iota(jnp.int32, (3 * L, 2 * L), 1)
    sum_rows = jnp.where(c_j < L, one_if(k_j <= c_j), one_if(k_j >= c_j - L)).astype(BF16)

    g_all = grow_ref[0, 0].reshape(nc * SUBLANES, L)
    rb_ref[...] = _dot(jnp.concatenate(_split3(g_all), axis=1), sum_rows).reshape(nc, SUBLANES, 2 * L)
    ones_b = jnp.ones((2 * L, L), BF16)

    def phase_a(c, carry):
        r0 = pl.multiple_of(c * L, L)
        gr = grow_ref[0, 0, c]
        rb = rb_ref[c]
        kb = ks_ref[pl.ds(r0, L), :].astype(BF16)
        vt = vt_ref[0, c].astype(F32)
        wvs, wks, diags = [], [], []
        for d in range(2):
            brow = rb[2 * d + 1:2 * d + 2, d * L:(d + 1) * L]
            blast = brow[:, L - 1:L] if d == 0 else brow[:, 0:1]
            ibr = gr[2 * d:2 * d + 1, :] - brow
            log_g = blast + ibr
            mg = jnp.max(log_g, axis=-1, keepdims=True)
            wk = jnp.exp(log_g - mg)
            wvs.append((vt * wk).astype(BF16))
            wks.append(wk)
            diags.append(jnp.concatenate(
                [jnp.where(eye, term.astype(F32), 0.0).astype(BF16) for term in _split3(ibr)[:2]], axis=1))
            rows_ref[c, 2 + d:3 + d, :] = brow
            rows_ref[c, 4 + d:5 + d, :] = jnp.broadcast_to(mg, (1, L))
            rows_ref[c, 6 + d:7 + d, :] = jnp.broadcast_to(blast, (1, L))
        ib = _dot(jnp.concatenate(diags, axis=0), ones_b)
        ib_ref[c] = ib
        for d in range(2):
            rows_ref[c, d:d + 1, :] = jnp.max(jnp.where(tris[d], ib[d * L:(d + 1) * L, :], NEG_BIG),
                                              axis=0, keepdims=True)
        cinc_ref[c] = _dot(jnp.concatenate(wvs, axis=0), kb)
        wk8 = jnp.where(row8 == 0, wks[0], jnp.where(row8 == 1, wks[1], 0.0))
        ninc_ref[c] = _dot(wk8.astype(BF16), kb)
        return carry

    def phase_b(j, ms):
        new_ms = []
        for d, cc in ((0, j), (1, nc - 1 - j)):
            m = ms[d]
            half = slice(d * DH, (d + 1) * DH)
            cst = c_ref[half, :]
            n = n_ref[d:d + 1, :]
            cin_ref[cc, half, :] = cst.astype(BF16)
            sin_ref[cc, d:d + 1, :] = n
            sin_ref[cc, 2 + d:3 + d, :] = m
            mg = rows_ref[cc, 4 + d:5 + d, :]
            blast = rows_ref[cc, 6 + d:7 + d, :]
            m_new = jnp.maximum(blast + m, mg)
            decay = jnp.exp(blast + m - m_new)
            grow = jnp.exp(mg - m_new)
            c_ref[half, :] = decay * cst + grow * cinc_ref[cc, half, :]
            n_ref[d:d + 1, :] = decay * n + grow * ninc_ref[cc, d:d + 1, :]
            new_ms.append(m_new)
        return tuple(new_ms)

    def phase_c_products(c):
        r0 = pl.multiple_of(c * L, L)
        qb = qs_ref[pl.ds(r0, L), :].astype(BF16)
        qk_t = _dot_nt(ks_ref[pl.ds(r0, L), :].astype(BF16), qb)
        qc_t = _dot_nt(cin_ref[c], qb)
        qn = _dot_nt(sin_ref[c].astype(BF16), qb)
        return qk_t, qc_t, qn

    def phase_c_finish(c, qk_t, qc_t, qn):
        r0 = pl.multiple_of(c * L, L)
        sin = sin_ref[c]
        rows = rows_ref[c]
        ats, stats = [], []
        for d in range(2):
            m_in = sin[2 + d:3 + d, :]
            cm = jnp.maximum(m_in, rows[d:d + 1, :])
            a_t = qk_t * jnp.where(tris[d], jnp.exp(ib_ref[c, d * L:(d + 1) * L, :] - cm), 0.0)
            w_inter = jnp.exp(m_in - cm)
            den = w_inter * qn[d:d + 1, :] + jnp.sum(a_t, axis=0, keepdims=True)
            m_t = rows[2 + d:3 + d, :] + cm
            ats.append(a_t.astype(BF16))
            stats.append((w_inter, jnp.maximum(jnp.abs(den), jnp.exp(-m_t))))
        av_t = _dot(vt_ref[0, c], jnp.concatenate(ats, axis=1))
        h_t = None
        for d in range(2):
            w_inter, den = stats[d]
            hd = (w_inter * qc_t[d * DH:(d + 1) * DH, :] + av_t[:, d * L:(d + 1) * L]) / den
            h_t = hd if h_t is None else h_t + hd
        y_t = h_t * lax.rsqrt(jnp.mean(h_t * h_t, axis=0, keepdims=True) + EPS) * hgt_ref[...]
        o_ref[0, pl.ds(r0, L), :] = (og_ref[0, pl.ds(r0, L), :].astype(F32) * y_t.T).astype(BF16)

    def phase_c(g, carry):
        chunks = [g * C_GROUP + i for i in range(C_GROUP)]
        products = [phase_c_products(c) for c in chunks]
        for c, prod in zip(chunks, products):
            phase_c_finish(c, *prod)
        return carry

    lax.fori_loop(0, nc, phase_a, 0, unroll=4)
    c_ref[...] = jnp.zeros(c_ref.shape, F32)
    n_ref[...] = jnp.zeros(n_ref.shape, F32)
    m0 = jnp.zeros((1, LANES), F32)
    lax.fori_loop(0, nc, phase_b, (m0, m0))
    lax.fori_loop(0, nc // C_GROUP, phase_c, 0, unroll=2)


def _mlstm(mq, mk, vt, og, grow_h, conv_wq, conv_wk, head_g_t):
    B, S, _ = mq.shape
    nc = S // M_CHUNK
    head = pl.BlockSpec((1, S, M_HEAD_DIM), lambda b, h: (b, 0, h))
    cw = pl.BlockSpec((CONV_W, M_HEAD_DIM), lambda b, h: (0, h))
    return pl.pallas_call(
        _mlstm_kernel,
        grid=(B, M_HEADS),
        in_specs=[
            head, head,
            pl.BlockSpec((1, nc, M_HEAD_DIM, M_CHUNK), lambda b, h: (b, 0, h, 0)),
            head,
            pl.BlockSpec((1, 1, nc, SUBLANES, M_CHUNK), lambda b, h: (b, h, 0, 0, 0)),
            cw, cw,
            pl.BlockSpec((M_HEAD_DIM, LANES), lambda b, h: (h, 0)),
        ],
        out_specs=head,
        out_shape=jax.ShapeDtypeStruct((B, S, M_WIDTH), BF16),
        scratch_shapes=[
            pltpu.VMEM((S + 2 * SUBLANES, LANES), F32),
            pltpu.VMEM((S, M_HEAD_DIM), F32),
            pltpu.VMEM((S, M_HEAD_DIM), F32),
            pltpu.VMEM((nc, SUBLANES, 2 * M_CHUNK), F32),
            pltpu.VMEM((nc, 2 * M_CHUNK, M_CHUNK), F32),
            pltpu.VMEM((nc, SUBLANES, M_CHUNK), F32),
            pltpu.VMEM((nc, 2 * M_HEAD_DIM, M_HEAD_DIM), F32),
            pltpu.VMEM((nc, SUBLANES, M_HEAD_DIM), F32),
            pltpu.VMEM((nc, 2 * M_HEAD_DIM, M_HEAD_DIM), BF16),
            pltpu.VMEM((nc, SUBLANES, M_HEAD_DIM), F32),
            pltpu.VMEM((2 * M_HEAD_DIM, M_HEAD_DIM), F32),
            pltpu.VMEM((SUBLANES, M_HEAD_DIM), F32),
        ],
        compiler_params=_compiler_params(("parallel", "parallel")),
        name="mlstm",
    )(mq, mk, vt, og, grow_h, conv_wq, conv_wk, head_g_t)


def _route(logits):
    lane = lax.broadcasted_iota(jnp.int32, logits.shape, 1)
    lane_f = lane.astype(F32)
    big = float(LANES)
    is_g = (lane >= ROUTER_G_LANE) & (lane < ROUTER_G_LANE + N_GROUPS)
    gl = jnp.where(is_g, logits, NEG_BIG)
    gmax = jnp.max(gl, axis=-1, keepdims=True)
    gsum = jnp.sum(jnp.where(is_g, jnp.exp(gl - gmax), 0.0), axis=-1, keepdims=True)
    p_grp = 1.0 / gsum
    grp = jnp.min(jnp.where(is_g & (gl == gmax), lane_f - ROUTER_G_LANE, big), axis=-1, keepdims=True)
    in_grp = (lane < N_EXPERTS) & (jnp.right_shift(lane, 2).astype(F32) == grp)
    el = jnp.where(in_grp, logits, NEG_BIG)
    v1 = jnp.max(el, axis=-1, keepdims=True)
    i1 = jnp.min(jnp.where(in_grp & (el == v1), lane_f, big), axis=-1, keepdims=True)
    rest = in_grp & (lane_f != i1)
    el2 = jnp.where(rest, logits, NEG_BIG)
    v2 = jnp.max(el2, axis=-1, keepdims=True)
    i2 = jnp.min(jnp.where(rest & (el2 == v2), lane_f, big), axis=-1, keepdims=True)
    e21 = jnp.exp(v2 - v1)
    w1 = p_grp / (1.0 + e21)
    w2 = p_grp * e21 / (1.0 + e21)
    return jnp.where(lane == 0, i1, jnp.where(lane == 1, i2, jnp.where(lane == 2, w1, w2)))


def _pack_bf16_pairs(lo, hi):
    bits = lambda v: lax.bitcast_convert_type(v.astype(BF16).astype(F32), PACKED)
    return jnp.right_shift(bits(lo), PACKED(16)) | (bits(hi) & PACKED(0xFFFF0000))


def _unpack_bf16_pairs(w):
    return (lax.bitcast_convert_type(jnp.left_shift(w, PACKED(16)), F32),
            lax.bitcast_convert_type(w & PACKED(0xFFFF0000), F32))


def _pack_row_halves(y):
    return (_pack_bf16_pairs(y[:, 0:PACK_W], y[:, PACK_W:2 * PACK_W]),
            _pack_bf16_pairs(y[:, 2 * PACK_W:3 * PACK_W], y[:, 3 * PACK_W:4 * PACK_W]))


def _unpack_row_halves(a, b):
    return [*_unpack_bf16_pairs(a), *_unpack_bf16_pairs(b)]


def _merge_kernel(attn_ref, mo_ref, bg_ref, x_ref, mod_ref, wua_ref, wum_ref, wo_ref, g2_ref, wr_ref, br_ref,
                  x1_ref, ha_ref, hb_ref, rt_ref):
    tm = x_ref.shape[1]
    nparts = tm // MERGE_ROWS
    parts = [pl.ds(i * MERGE_ROWS, MERGE_ROWS) for i in range(nparts)]
    wr = wr_ref[...].astype(BF16)
    merged, h2s = {}, {}
    for step in range(nparts + 2):
        if step < nparts:
            h = parts[step]
            up_a = _dot(attn_ref[0, h, :], wua_ref[...])
            up_m = _dot(mo_ref[0, h, :], wum_ref[...])
            merged[step] = (bg_ref[0, h, 0:D_MODEL].astype(F32) * up_a
                            + bg_ref[0, h, D_MODEL:2 * D_MODEL].astype(F32) * up_m).astype(BF16)
        if 0 <= step - 1 < nparts:
            h = parts[step - 1]
            x1 = x_ref[0, h, :] + mod_ref[0, 2:3, :] * _dot(merged.pop(step - 1), wo_ref[...])
            x1_ref[0, h, :] = x1
            h2s[step - 1] = _rms_mod(x1, g2_ref[...], mod_ref[0, 4:5, :], mod_ref[0, 3:4, :])
        if 0 <= step - 2 < nparts:
            h = parts[step - 2]
            h2 = h2s.pop(step - 2)
            logits = _dot(h2.astype(BF16), wr) + br_ref[...]
            ha_ref[0, h, :], hb_ref[0, h, :] = _pack_row_halves(h2)
            rt_ref[0, h, :] = _route(logits)[:, 0:ROUTE_W]


def _merge(attn, mo, bg, x, mod, wua, wum, wo, norm_g, w_router, b_router, tm):
    B, S, D = x.shape
    tok = lambda w: pl.BlockSpec((1, tm, w), lambda b, i: (b, i, 0))
    const2 = lambda a: pl.BlockSpec(a.shape, lambda b, i: (0, 0))
    g2 = norm_g.reshape(1, D)
    return pl.pallas_call(
        _merge_kernel,
        grid=(B, S // tm),
        in_specs=[
            tok(A_WIDTH), tok(M_WIDTH), tok(2 * D), tok(D),
            pl.BlockSpec((1, 6, D), lambda b, i: (b, 0, 0)),
            const2(wua), const2(wum), const2(wo), const2(g2), const2(w_router), const2(b_router),
        ],
        out_specs=(tok(D), tok(PACK_W), tok(PACK_W), tok(ROUTE_W)),
        out_shape=(
            jax.ShapeDtypeStruct((B, S, D), F32),
            jax.ShapeDtypeStruct((B, S, PACK_W), PACKED),
            jax.ShapeDtypeStruct((B, S, PACK_W), PACKED),
            jax.ShapeDtypeStruct((B, S, ROUTE_W), F32),
        ),
        compiler_params=_compiler_params(("parallel", "parallel")),
        name="merge_route",
    )(attn, mo, bg, x, mod, wua, wum, wo, g2, w_router, b_router)


def _route_tables(eid, tmb):
    T = eid.shape[0]
    nblk = 2 * T // tmb + N_EXPERTS
    ea = eid.T.reshape(-1)
    onehot = (ea[:, None] == jnp.arange(N_EXPERTS, dtype=jnp.int32)[None, :]).astype(jnp.int32)
    csum = jnp.cumsum(onehot, axis=0)
    counts = csum[-1]
    rank = jnp.sum(onehot * csum, axis=1) - 1
    padded = ((counts + tmb - 1) // tmb) * tmb
    ends = jnp.cumsum(padded)
    starts = ends - padded
    pos = (jnp.sum(onehot * starts[None, :], axis=1) + rank).astype(jnp.int32)
    blk0 = jnp.arange(nblk, dtype=jnp.int32) * tmb
    bexp = jnp.minimum(jnp.sum((blk0[:, None] >= ends[None, :]).astype(jnp.int32), axis=1), N_EXPERTS - 1)
    nval = jnp.clip(starts[bexp] + counts[bexp] - blk0, 0, tmb)
    nval = jnp.where(blk0 < ends[-1], nval, 0).astype(jnp.int32)
    return pos, bexp.astype(jnp.int32), nval


def _sc_mesh():
    return plsc.VectorSubcoreMesh(core_axis_name="c", subcore_axis_name="s")


def _sc_dispatch(x, pos, n_rows):
    T = x.shape[0]
    nb = T // SC_WINDOW
    idx = pos.reshape(1, 2 * T)

    @pl.kernel(out_type=jax.ShapeDtypeStruct((n_rows, PACK_W), x.dtype), mesh=_sc_mesh(), scratch_types=[])
    def dispatch(x_hbm, i_hbm, o_hbm):
        def body(x_vmem, i_vmem):
            pltpu.sync_copy(x_vmem, o_hbm.at[i_vmem.at[0]])

        pltpu.emit_pipeline(
            body,
            grid=(2 * nb,),
            in_specs=[pl.BlockSpec((SC_WINDOW, PACK_W), index_map=lambda i: (i % nb, 0)),
                      pl.BlockSpec((1, SC_WINDOW), index_map=lambda i: (0, i))],
            out_specs=[],
            core_axis_name=("c", "s"),
            dimension_semantics=(pltpu.PARALLEL,),
        )(x_hbm, i_hbm)

    return dispatch(x, idx)


def _sc_combine(y, pos):
    n = pos.shape[0]
    idx = pos.reshape(1, n)

    @pl.kernel(out_type=jax.ShapeDtypeStruct((n, PACK_W), y.dtype), mesh=_sc_mesh(), scratch_types=[])
    def combine(y_hbm, i_hbm, o_hbm):
        def body(i_vmem, o_vmem):
            pltpu.sync_copy(y_hbm.at[i_vmem.at[0]], o_vmem)

        pltpu.emit_pipeline(
            body,
            grid=(n // SC_WINDOW,),
            in_specs=[pl.BlockSpec((1, SC_WINDOW), index_map=lambda i: (0, i))],
            out_specs=[pl.BlockSpec((SC_WINDOW, PACK_W), index_map=lambda i: (i, 0))],
            core_axis_name=("c", "s"),
            dimension_semantics=(pltpu.PARALLEL,),
        )(i_hbm, o_hbm)

    return combine(y, idx)


def _experts_kernel(bexp_ref, nval_ref, xa_ref, xb_ref, wg_ref, wu_ref, wd_ref, ya_ref, yb_ref):
    nv = nval_ref[pl.program_id(0)]

    @pl.when(nv > 0)
    def _():
        keep = lax.broadcasted_iota(jnp.int32, xa_ref.shape, 0) < nv
        pieces = [jnp.where(keep, piece, 0.0).astype(BF16) for piece in _unpack_row_halves(xa_ref[...], xb_ref[...])]
        x = jnp.concatenate(pieces, axis=1)
        half = x.shape[0] // 2
        gus = [(_dot(x[r:r + half], wg_ref[0]), _dot(x[r:r + half], wu_ref[0])) for r in (0, half)]
        hes = [((g * _sigmoid(g)) * u).astype(BF16) for g, u in gus]
        for r, he in zip((0, half), hes):
            ya_ref[r:r + half, :], yb_ref[r:r + half, :] = _pack_row_halves(_dot(he, wd_ref[0]))


def _experts(xa, xb, bexp, nval, wg, wu, wd, tmb):
    n = xa.shape[0]
    row = lambda: pl.BlockSpec((tmb, PACK_W), lambda i, be, nv: (i, 0))
    grid_spec = pltpu.PrefetchScalarGridSpec(
        num_scalar_prefetch=2,
        grid=(n // tmb,),
        in_specs=[row(), row(),
                  pl.BlockSpec((1, D_MODEL, D_EXPERT), lambda i, be, nv: (be[i], 0, 0)),
                  pl.BlockSpec((1, D_MODEL, D_EXPERT), lambda i, be, nv: (be[i], 0, 0)),
                  pl.BlockSpec((1, D_EXPERT, D_MODEL), lambda i, be, nv: (be[i], 0, 0))],
        out_specs=[row(), row()],
    )
    out = jax.ShapeDtypeStruct((n, PACK_W), xa.dtype)
    return pl.pallas_call(
        _experts_kernel, grid_spec=grid_spec, out_shape=(out, out),
        compiler_params=_compiler_params(("arbitrary",)),
        name="experts",
    )(bexp, nval, xa, xb, wg, wu, wd)


def _finish_kernel(a0_ref, b0_ref, a1_ref, b1_ref, rt_ref, x1_ref, mod_ref, fg_ref, o_ref):
    y0 = jnp.concatenate(_unpack_row_halves(a0_ref[...], b0_ref[...]), axis=1)
    y1 = jnp.concatenate(_unpack_row_halves(a1_ref[...], b1_ref[...]), axis=1)
    rt = rt_ref[0]
    moe = rt[:, 2:3] * y0 + rt[:, 3:4] * y1
    xo = x1_ref[0] + mod_ref[0, 5:6, :] * moe
    ms = jnp.mean(xo * xo, axis=-1, keepdims=True)
    o_ref[0] = xo * lax.rsqrt(ms + EPS) * fg_ref[...]


def _finish(ca, cb, rt, x1, mod, final_g, tm):
    B, S, D = x1.shape
    nt = S // tm
    second = B * nt
    half = lambda k: pl.BlockSpec((tm, PACK_W), lambda b, i: (k * second + b * nt + i, 0))
    tok = lambda w: pl.BlockSpec((1, tm, w), lambda b, i: (b, i, 0))
    fg = final_g.reshape(1, D)
    return pl.pallas_call(
        _finish_kernel,
        grid=(B, nt),
        in_specs=[half(0), half(0), half(1), half(1), tok(ROUTE_W), tok(D),
                  pl.BlockSpec((1, 6, D), lambda b, i: (b, 0, 0)),
                  pl.BlockSpec((1, D), lambda b, i: (0, 0))],
        out_specs=tok(D),
        out_shape=jax.ShapeDtypeStruct((B, S, D), F32),
        compiler_params=_compiler_params(("parallel", "parallel")),
        name="finish",
    )(ca, cb, ca, cb, rt, x1, mod, fg)


def _moe(ha, hb, rt, x1, mod, wg, wu, wd, final_g):
    B, S, D = x1.shape
    T = B * S
    tmb = EXPERT_ROWS
    eid = rt.reshape(T, ROUTE_W)[:, 0:2].astype(jnp.int32)
    pos, bexp, nval = _route_tables(eid, tmb)
    n_rows = 2 * T + N_EXPERTS * tmb
    xa = _sc_dispatch(ha.reshape(T, PACK_W), pos, n_rows)
    xb = _sc_dispatch(hb.reshape(T, PACK_W), pos, n_rows)
    ya, yb = _experts(xa, xb, bexp, nval, wg, wu, wd, tmb)
    return _finish(_sc_combine(ya, pos), _sc_combine(yb, pos), rt, x1, mod, final_g, tm=min(1024, S))


def _rope_tables(S):
    half = ROPE_DIM // 2
    inv_freq = 1.0 / (ROPE_THETA ** (jnp.arange(half, dtype=F32) * 2.0 / ROPE_DIM))
    ang = jnp.arange(S, dtype=F32)[:, None] * inv_freq[None, :]
    cos, sin = jnp.cos(ang), jnp.sin(ang)
    zeros = jnp.zeros((S, A_HEAD_DIM - ROPE_DIM), F32)
    z8 = jnp.zeros((S, half), F32)
    cos_h = jnp.concatenate([cos, cos, jnp.ones_like(zeros)], axis=-1)
    slo_h = jnp.concatenate([-sin, z8, zeros], axis=-1)
    shi_h = jnp.concatenate([z8, sin, zeros], axis=-1)
    rep = LANES // A_HEAD_DIM
    return tuple(jnp.tile(t, (1, rep)) for t in (cos_h, slo_h, shi_h))


def _layer(x, mod, p, tabs):
    B, S, D = x.shape
    mod = mod.reshape(B, 6, D)
    q, kx, vx, mq, mk, vt, og, bg, grow = _in_proj(
        x, mod, p["norm1_g"], p["w_main"], p["w_t"], p["m_gate_b"], tabs, tm=512)
    attn = _attention(q, kx, vx, p["attn_sink"])
    nc = S // M_CHUNK
    grow_h = grow.reshape(B, nc, 4, M_HEADS, M_CHUNK).transpose(0, 3, 1, 2, 4)
    grow_h = jnp.pad(grow_h, ((0, 0), (0, 0), (0, 0), (0, SUBLANES - 4), (0, 0)))
    mo = _mlstm(mq, mk, vt, og, grow_h, p["conv_wq"], p["conv_wk"], p["head_norm_g_t"])
    x1, ha, hb, rt = _merge(attn, mo, bg, x, mod, p["w_up_attn"], p["w_up_mlstm"], p["w_out"], p["norm2_g"],
                            p["w_router"], p["b_router"], tm=min(1024, S))
    return _moe(ha, hb, rt, x1, mod, p["w_gate"], p["w_up"], p["w_down"], p["final_norm_g"])


def kernel(x_prompt, x_sample, c_prompt, c_sample, ada_w, ada_b, norm1_g, w_in, conv_w, m_gate_b, attn_sink,
           head_norm_g, w_up_attn, w_up_mlstm, w_out, norm2_g, rg_w, rg_b, re_w, re_b, w_gate, w_up, w_down,
           final_norm_g):
    assert ada_w.shape[0] == 1, "single-layer trunk"
    w_in0 = w_in[0]
    w_g = w_in0[:, OFF_MG:OFF_BG]
    pad = LANES - N_EXPERTS - N_GROUPS
    p = dict(
        ada_w=ada_w[0], ada_b=ada_b[0], norm1_g=norm1_g[0],
        w_main=jnp.concatenate([w_in0[:, :OFF_MV], w_in0[:, OFF_MO:OFF_MG], w_in0[:, OFF_BG:]], axis=1).astype(BF16),
        w_t=jnp.concatenate([w_in0[:, OFF_MV:OFF_MO], w_g], axis=1).T.astype(BF16),
        m_gate_b=m_gate_b[0], attn_sink=attn_sink[0],
        conv_wq=conv_w[0, :, :M_WIDTH], conv_wk=conv_w[0, :, M_WIDTH:],
        head_norm_g_t=jnp.broadcast_to(head_norm_g[0][:, None], (M_WIDTH, LANES)),
        w_up_attn=w_up_attn[0].astype(BF16), w_up_mlstm=w_up_mlstm[0].astype(BF16), w_out=w_out[0].astype(BF16),
        norm2_g=norm2_g[0],
        w_router=jnp.pad(jnp.concatenate([re_w[0], rg_w[0]], axis=1), ((0, 0), (0, pad))),
        b_router=jnp.pad(jnp.concatenate([re_b[0], rg_b[0]]), (0, pad)).reshape(1, LANES),
        w_gate=w_gate[0].astype(BF16), w_up=w_up[0].astype(BF16), w_down=w_down[0].astype(BF16),
        final_norm_g=final_norm_g,
    )
    tabs = _rope_tables(x_prompt.shape[1])
    nbp = x_prompt.shape[0]
    mod = _ada_mod(jnp.concatenate([c_prompt, c_sample], axis=0), p["ada_w"], p["ada_b"])
    return (_layer(x_prompt, mod[:nbp], p, tabs), _layer(x_sample, mod[nbp:], p, tabs))
```

```python
import functools

import jax
import jax.numpy as jnp
from jax import lax
from jax.experimental import pallas as pl
from jax.experimental.pallas import tpu as pltpu
from jax.experimental.pallas import tpu_sc as plsc

D_MODEL = 1024
A_HEADS = 8
A_KV_HEADS = 2
A_GROUP = A_HEADS // A_KV_HEADS
A_HEAD_DIM = 64
A_WIDTH = A_HEADS * A_HEAD_DIM
A_KV_WIDTH = A_KV_HEADS * A_HEAD_DIM
WINDOW = 128
BLOCK = 128
ROPE_DIM = A_HEAD_DIM // 4
ROPE_THETA = 500000.0
M_HEADS = 4
M_HEAD_DIM = 128
M_WIDTH = M_HEADS * M_HEAD_DIM
M_CHUNK = 128
CONV_W = 3
OFF_AQ = 0
OFF_AK = OFF_AQ + A_WIDTH
OFF_AV = OFF_AK + A_KV_WIDTH
OFF_MQ = OFF_AV + A_KV_WIDTH
OFF_MK = OFF_MQ + M_WIDTH
OFF_MV = OFF_MK + M_WIDTH
OFF_MO = OFF_MV + M_WIDTH
OFF_MG = OFF_MO + M_WIDTH
N_MGATES = 4 * M_HEADS
OFF_BG = OFF_MG + N_MGATES
IN_TOTAL = OFF_BG + 2 * D_MODEL
N_GROUPS = 4
EXPERTS_PER_GROUP = 4
N_EXPERTS = N_GROUPS * EXPERTS_PER_GROUP
D_EXPERT = 512
EPS = 1e-6
NEG_BIG = -1e30
F32 = jnp.float32
BF16 = jnp.bfloat16

LANES = 128
SUBLANES = 8
VMEM_LIMIT_BYTES = 56 * 1024 * 1024

MAIN_AQ = 0
MAIN_KV = A_WIDTH
MAIN_MQ = MAIN_KV + 2 * A_KV_WIDTH
MAIN_MK = MAIN_MQ + M_WIDTH
MAIN_MO = MAIN_MK + M_WIDTH
MAIN_BG = MAIN_MO + M_WIDTH
MAIN_TOTAL = MAIN_BG + 2 * D_MODEL

LOG2E = 1.4426950408889634
Q_SCALE = A_HEAD_DIM ** -0.5 * LOG2E
ROUTER_G_LANE = N_EXPERTS
ROUTE_W = 4
PACKED = jnp.uint32
PACK_W = D_MODEL // 4
SC_WINDOW = 128
EXPERT_ROWS = 512
MERGE_ROWS = 256
GATE_ROWS = M_HEADS * SUBLANES


def _sigmoid(z):
    return 1.0 / (1.0 + jnp.exp(-z))


def _log_sigmoid(z):
    return jnp.minimum(z, 0.0) - jnp.log(1.0 + jnp.exp(-jnp.abs(z)))


def _gates_head_major(a):
    lead = a.shape[:-1]
    a = jnp.swapaxes(a.reshape(*lead, 4, M_HEADS), -1, -2)
    a = jnp.pad(a, [(0, 0)] * (a.ndim - 1) + [(0, SUBLANES - 4)])
    return a.reshape(*lead, GATE_ROWS)


def _dot(a, b):
    return jnp.dot(a, b, preferred_element_type=F32)


def _dot_nt(a, b):
    return lax.dot_general(a, b, (((1,), (1,)), ((), ())), preferred_element_type=F32)


def _dot_tn(a, b):
    return lax.dot_general(a, b, (((0,), (0,)), ((), ())), preferred_element_type=F32)


def _dot_f32(a, b):
    return jnp.dot(a, b, preferred_element_type=F32, precision=lax.Precision.HIGHEST)


def _compiler_params(semantics):
    return pltpu.CompilerParams(dimension_semantics=semantics, vmem_limit_bytes=VMEM_LIMIT_BYTES)


def _ada_kernel(c_ref, w_ref, b_ref, o_ref):
    c = c_ref[...]
    o_ref[...] = _dot_f32(c * _sigmoid(c), w_ref[...]) + b_ref[...]


def _ada_mod(c, ada_w, ada_b):
    B, D = c.shape
    n = ada_w.shape[1] // D
    return pl.pallas_call(
        _ada_kernel,
        grid=(n,),
        in_specs=[
            pl.BlockSpec((B, D), lambda j: (0, 0)),
            pl.BlockSpec((D, D), lambda j: (0, j)),
            pl.BlockSpec((1, D), lambda j: (0, j)),
        ],
        out_specs=pl.BlockSpec((B, D), lambda j: (0, j)),
        out_shape=jax.ShapeDtypeStruct((B, n * D), F32),
        compiler_params=_compiler_params(("arbitrary",)),
        name="ada_mod",
    )(c, ada_w, ada_b.reshape(1, -1))


def _rms_mod(x, g, scale, shift):
    ms = jnp.mean(x * x, axis=-1, keepdims=True)
    return (x * lax.rsqrt(ms + EPS) * g) * (1.0 + scale) + shift


def _rope_block(xb, cos, sin_lo, sin_hi):
    half = ROPE_DIM // 2
    return xb * cos + pltpu.roll(xb, LANES - half, axis=1) * sin_lo + pltpu.roll(xb, half, axis=1) * sin_hi


def _in_proj_kernel(x_ref, mod_ref, g_ref, w_ref, wt_ref, gb_col_ref, cos_ref, slo_ref, shi_ref,
                    q_ref, kx_ref, vx_ref, mq_ref, mk_ref, vt_ref, og_ref, bg_ref, grow_ref):
    x = x_ref[0]
    tm = x.shape[0]
    h = _rms_mod(x, g_ref[...], mod_ref[0, 1:2, :], mod_ref[0, 0:1, :])
    hb = h.astype(BF16)
    cos, slo, shi = cos_ref[...], slo_ref[...], shi_ref[...]
    lane = lax.broadcasted_iota(jnp.int32, (tm, LANES), 1)
    left = lane < A_HEAD_DIM

    pj = _dot(hb, w_ref[:, MAIN_AQ:MAIN_AQ + A_WIDTH])
    for j in range(A_WIDTH // LANES):
        blk = _rope_block(pj[:, j * LANES:(j + 1) * LANES], cos, slo, shi)
        q_ref[0, :, j * LANES:(j + 1) * LANES] = (blk * Q_SCALE).astype(BF16)

    pj = _dot(hb, w_ref[:, MAIN_KV:MAIN_KV + 2 * A_KV_WIDTH])
    kk = _rope_block(pj[:, 0:LANES], cos, slo, shi)
    vv = pj[:, LANES:2 * LANES]
    for src, dst in ((kk, kx_ref), (vv, vx_ref)):
        swapped = pltpu.roll(src, A_HEAD_DIM, axis=1)
        zero = jnp.zeros_like(src)
        dst[0, :, 0 * LANES:1 * LANES] = jnp.where(left, src, zero).astype(BF16)
        dst[0, :, 1 * LANES:2 * LANES] = jnp.where(left, zero, swapped).astype(BF16)
        dst[0, :, 2 * LANES:3 * LANES] = jnp.where(left, swapped, zero).astype(BF16)
        dst[0, :, 3 * LANES:4 * LANES] = jnp.where(left, zero, src).astype(BF16)

    mq_ref[0] = _dot(hb, w_ref[:, MAIN_MQ:MAIN_MQ + M_WIDTH]).astype(BF16)
    mk_ref[0] = _dot(hb, w_ref[:, MAIN_MK:MAIN_MK + M_WIDTH]).astype(BF16)
    og_ref[0] = _sigmoid(_dot(hb, w_ref[:, MAIN_MO:MAIN_MO + M_WIDTH])).astype(BF16)
    for j in range(2 * D_MODEL // M_WIDTH):
        lo = MAIN_BG + j * M_WIDTH
        bg_ref[0, :, j * M_WIDTH:(j + 1) * M_WIDTH] = _sigmoid(_dot(hb, w_ref[:, lo:lo + M_WIDTH])).astype(BF16)

    tr = _dot_nt(wt_ref[...], hb)
    gr = tr[M_WIDTH:M_WIDTH + GATE_ROWS, :] + gb_col_ref[...]
    gtype = jnp.bitwise_and(lax.broadcasted_iota(jnp.int32, gr.shape, 0), SUBLANES - 1)
    gr = jnp.where(gtype == 1, _log_sigmoid(gr), jnp.where(gtype == 3, _log_sigmoid(gr), gr))
    vt = tr[0:M_WIDTH, :].astype(BF16)
    for c in range(tm // M_CHUNK):
        grow_ref[0, c] = gr[:, c * M_CHUNK:(c + 1) * M_CHUNK]
        vt_ref[0, c] = vt[:, c * M_CHUNK:(c + 1) * M_CHUNK]


def _in_proj(x, mod, norm_g, w_main, w_t, gate_b, rope_tabs, tm):
    B, S, D = x.shape
    nt = S // tm
    cos, slo, shi = rope_tabs
    tok = lambda w: pl.BlockSpec((1, tm, w), lambda b, i: (b, i, 0))
    const2 = lambda a: pl.BlockSpec(a.shape, lambda b, i: (0, 0))
    tab = pl.BlockSpec((tm, LANES), lambda b, i: (i, 0))
    out_shapes = (
        jax.ShapeDtypeStruct((B, S, A_WIDTH), BF16),
        jax.ShapeDtypeStruct((B, S, 4 * LANES), BF16),
        jax.ShapeDtypeStruct((B, S, 4 * LANES), BF16),
        jax.ShapeDtypeStruct((B, S, M_WIDTH), BF16),
        jax.ShapeDtypeStruct((B, S, M_WIDTH), BF16),
        jax.ShapeDtypeStruct((B, S // M_CHUNK, M_WIDTH, M_CHUNK), BF16),
        jax.ShapeDtypeStruct((B, S, M_WIDTH), BF16),
        jax.ShapeDtypeStruct((B, S, 2 * D_MODEL), BF16),
        jax.ShapeDtypeStruct((B, S // M_CHUNK, GATE_ROWS, M_CHUNK), F32),
    )
    chunked = lambda rows: pl.BlockSpec((1, tm // M_CHUNK, rows, M_CHUNK), lambda b, i: (b, i, 0, 0))
    out_specs = (
        tok(A_WIDTH), tok(4 * LANES), tok(4 * LANES), tok(M_WIDTH), tok(M_WIDTH), chunked(M_WIDTH), tok(M_WIDTH),
        tok(2 * D_MODEL), chunked(GATE_ROWS),
    )
    gb_col = gate_b.reshape(GATE_ROWS, 1)
    g2 = norm_g.reshape(1, D)
    return pl.pallas_call(
        _in_proj_kernel,
        grid=(B, nt),
        in_specs=[
            tok(D),
            pl.BlockSpec((1, 6, D), lambda b, i: (b, 0, 0)),
            const2(g2), const2(w_main), const2(w_t), const2(gb_col),
            tab, tab, tab,
        ],
        out_specs=out_specs,
        out_shape=out_shapes,
        compiler_params=_compiler_params(("parallel", "parallel")),
        name="in_proj",
    )(x, mod, g2, w_main, w_t, gb_col, cos, slo, shi)


def _attn_kernel(sink_ref, q_ref, kx_ref, vx_ref, o_ref):
    S = q_ref.shape[1]
    nb = S // BLOCK
    kw = 3 * BLOCK
    qi = lax.broadcasted_iota(jnp.int32, (BLOCK, kw), 0)
    ki = lax.broadcasted_iota(jnp.int32, (BLOCK, kw), 1)
    rel0 = ki - qi
    ones_b = jnp.ones((kw, LANES), BF16)
    left = lax.broadcasted_iota(jnp.int32, (BLOCK, LANES), 1) < A_HEAD_DIM

    def block(n, carry):
        q0 = pl.multiple_of(n * BLOCK, BLOCK)
        k0 = pl.multiple_of(jnp.clip((n - 1) * BLOCK, 0, S - kw), BLOCK)
        rel = rel0 + (k0 - q0)
        valid = jnp.abs(rel) <= WINDOW
        scores = []
        for hk in range(A_KV_HEADS):
            for j in range(A_GROUP // 2):
                col = (hk * (A_GROUP // 2) + j) * LANES
                qp = q_ref[0, pl.ds(q0, BLOCK), col:col + LANES]
                for side in range(2):
                    kk = kx_ref[0, pl.ds(k0, kw), (2 * hk + side) * LANES:(2 * hk + side + 1) * LANES]
                    scores.append(_dot_nt(qp, kk))
        for hk in range(A_KV_HEADS):
            for j in range(A_GROUP // 2):
                col = (hk * (A_GROUP // 2) + j) * LANES
                outs, dens = [], []
                for side in range(2):
                    vv = vx_ref[0, pl.ds(k0, kw), (2 * hk + side) * LANES:(2 * hk + side + 1) * LANES]
                    sk = sink_ref[hk * A_GROUP + 2 * j + side] * LOG2E
                    s = jnp.where(valid, scores[(hk * (A_GROUP // 2) + j) * 2 + side], NEG_BIG)
                    m = jnp.maximum(jnp.max(s, axis=-1, keepdims=True), sk)
                    p = jnp.exp2(s - m).astype(BF16)
                    od = _dot(p, jnp.concatenate([vv, ones_b], axis=1))
                    outs.append(od[:, 0:LANES])
                    dens.append(od[:, LANES:2 * LANES] + jnp.exp2(sk - m))
                o = (outs[0] + outs[1]) / jnp.where(left, dens[0], dens[1])
                o_ref[0, pl.ds(q0, BLOCK), col:col + LANES] = o.astype(BF16)
        return carry

    lax.fori_loop(0, nb, block, 0, unroll=2)


def _attention(q, kx, vx, sink):
    B, S, _ = q.shape
    seq = lambda w: pl.BlockSpec((1, S, w), lambda b: (b, 0, 0))
    return pl.pallas_call(
        _attn_kernel,
        grid=(B,),
        in_specs=[pl.BlockSpec(memory_space=pltpu.SMEM), seq(A_WIDTH), seq(4 * LANES), seq(4 * LANES)],
        out_specs=seq(A_WIDTH),
        out_shape=jax.ShapeDtypeStruct((B, S, A_WIDTH), BF16),
        compiler_params=_compiler_params(("parallel",)),
        name="window_attn",
    )(sink, q, kx, vx)


def _conv_silu(u_ref, w_ref, pad_ref, dst_ref, scale):
    S = u_ref.shape[1]
    pad_ref[0:SUBLANES, :] = jnp.zeros((SUBLANES, LANES), F32)
    pad_ref[S + SUBLANES:S + 2 * SUBLANES, :] = jnp.zeros((SUBLANES, LANES), F32)
    pad_ref[SUBLANES:S + SUBLANES, :] = u_ref[0].astype(F32)
    w0, w1, w2 = w_ref[0:1, :], w_ref[1:2, :], w_ref[2:3, :]
    for c in range(S // M_CHUNK):
        base = SUBLANES + c * M_CHUNK
        y = (pad_ref[base - 1:base - 1 + M_CHUNK, :] * w0 + pad_ref[base:base + M_CHUNK, :] * w1
             + pad_ref[base + 1:base + 1 + M_CHUNK, :] * w2)
        y = y * _sigmoid(y)
        dst_ref[c * M_CHUNK:(c + 1) * M_CHUNK, :] = y if scale == 1.0 else y * scale


def _split3(x):
    hi = x.astype(BF16)
    r = x - hi.astype(F32)
    mid = r.astype(BF16)
    lo = (r - mid.astype(F32)).astype(BF16)
    return [hi, mid, lo]


def _mlstm_kernel(mq_ref, mk_ref, vt_ref, og_ref, grow_ref, cwq_ref, cwk_ref, hgt_ref, o_ref,
                  pad_ref, qs_ref, ks_ref, rb_ref, ib_ref, rows_ref, cinc_ref, ninc_ref, cin_ref, sin_ref,
                  c_ref, n_ref):
    S = mq_ref.shape[1]
    nc = S // M_CHUNK
    L = M_CHUNK
    DH = M_HEAD_DIM
    C_GROUP = 4 if nc % 4 == 0 else 1
    _conv_silu(mq_ref, cwq_ref, pad_ref, qs_ref, 1.0)
    _conv_silu(mk_ref, cwk_ref, pad_ref, ks_ref, M_HEAD_DIM ** -0.5)

    s_i = lax.broadcasted_iota(jnp.int32, (L, L), 0)
    t_i = lax.broadcasted_iota(jnp.int32, (L, L), 1)
    tris = (s_i <= t_i, s_i >= t_i)
    eye = s_i == t_i
    row8 = lax.broadcasted_iota(jnp.int32, (SUBLANES, L), 0)
    one_if = lambda cond: jnp.where(cond, 1.0, 0.0)
    k_j = lax.broadcasted_iota(jnp.int32, (3 * L, 2 * L), 0) % L
    c_j = lax.broadcasted_iota(jnp.int32, (3 * L, 2 * L), 1)
    sum_rows = jnp.where(c_j < L, one_if(k_j <= c_j), one_if(k_j >= c_j - L)).astype(BF16)

    g_all = grow_ref[0].reshape(nc * SUBLANES, L)
    rb_ref[...] = _dot(jnp.concatenate(_split3(g_all), axis=1), sum_rows).reshape(nc, SUBLANES, 2 * L)
    ones_b = jnp.ones((2 * L, L), BF16)

    def phase_a(c, carry):
        r0 = pl.multiple_of(c * L, L)
        gr = grow_ref[0, c]
        rb = rb_ref[c]
        kb = ks_ref[pl.ds(r0, L), :].astype(BF16)
        vt = vt_ref[0, c].astype(F32)
        wvs, wks, diags = [], [], []
        for d in range(2):
            brow = rb[2 * d + 1:2 * d + 2, d * L:(d + 1) * L]
            blast = brow[:, L - 1:L] if d == 0 else brow[:, 0:1]
            ibr = gr[2 * d:2 * d + 1, :] - brow
            log_g = blast + ibr
            mg = jnp.max(log_g, axis=-1, keepdims=True)
            wk = jnp.exp(log_g - mg)
            wvs.append((vt * wk).astype(BF16))
            wks.append(wk)
            diags.append(jnp.concatenate(
                [jnp.where(eye, term.astype(F32), 0.0).astype(BF16) for term in _split3(ibr)[:2]], axis=1))
            rows_ref[c, 2 + d:3 + d, :] = brow
            rows_ref[c, 4 + d:5 + d, :] = jnp.broadcast_to(mg, (1, L))
            rows_ref[c, 6 + d:7 + d, :] = jnp.broadcast_to(blast, (1, L))
        ib = _dot(jnp.concatenate(diags, axis=0), ones_b)
        ib_ref[c] = ib
        for d in range(2):
            rows_ref[c, d:d + 1, :] = jnp.max(jnp.where(tris[d], ib[d * L:(d + 1) * L, :], NEG_BIG),
                                              axis=0, keepdims=True)
        cinc_ref[c] = _dot(jnp.concatenate(wvs, axis=0), kb)
        wk8 = jnp.where(row8 == 0, wks[0], jnp.where(row8 == 1, wks[1], 0.0))
        ninc_ref[c] = _dot(wk8.astype(BF16), kb)
        return carry

    def phase_b(j, ms):
        new_ms = []
        for d, cc in ((0, j), (1, nc - 1 - j)):
            m = ms[d]
            half = slice(d * DH, (d + 1) * DH)
            cst = c_ref[half, :]
            n = n_ref[d:d + 1, :]
            cin_ref[cc, half, :] = cst.astype(BF16)
            sin_ref[cc, d:d + 1, :] = n
            sin_ref[cc, 2 + d:3 + d, :] = m
            mg = rows_ref[cc, 4 + d:5 + d, :]
            blast = rows_ref[cc, 6 + d:7 + d, :]
            m_new = jnp.maximum(blast + m, mg)
            decay = jnp.exp(blast + m - m_new)
            grow = jnp.exp(mg - m_new)
            c_ref[half, :] = decay * cst + grow * cinc_ref[cc, half, :]
            n_ref[d:d + 1, :] = decay * n + grow * ninc_ref[cc, d:d + 1, :]
            new_ms.append(m_new)
        return tuple(new_ms)

    def phase_c_products(c):
        r0 = pl.multiple_of(c * L, L)
        qb = qs_ref[pl.ds(r0, L), :].astype(BF16)
        qk_t = _dot_nt(ks_ref[pl.ds(r0, L), :].astype(BF16), qb)
        qc_t = _dot_nt(cin_ref[c], qb)
        qn = _dot_nt(sin_ref[c].astype(BF16), qb)
        return qk_t, qc_t, qn

    def phase_c_finish(c, qk_t, qc_t, qn):
        r0 = pl.multiple_of(c * L, L)
        sin = sin_ref[c]
        rows = rows_ref[c]
        ats, stats = [], []
        for d in range(2):
            m_in = sin[2 + d:3 + d, :]
            cm = jnp.maximum(m_in, rows[d:d + 1, :])
            a_t = qk_t * jnp.where(tris[d], jnp.exp(ib_ref[c, d * L:(d + 1) * L, :] - cm), 0.0)
            w_inter = jnp.exp(m_in - cm)
            den = w_inter * qn[d:d + 1, :] + jnp.sum(a_t, axis=0, keepdims=True)
            m_t = rows[2 + d:3 + d, :] + cm
            ats.append(a_t.astype(BF16))
            stats.append((w_inter, jnp.maximum(jnp.abs(den), jnp.exp(-m_t))))
        av_t = _dot(vt_ref[0, c], jnp.concatenate(ats, axis=1))
        h_t = None
        for d in range(2):
            w_inter, den = stats[d]
            hd = (w_inter * qc_t[d * DH:(d + 1) * DH, :] + av_t[:, d * L:(d + 1) * L]) / den
            h_t = hd if h_t is None else h_t + hd
        y_t = h_t * lax.rsqrt(jnp.mean(h_t * h_t, axis=0, keepdims=True) + EPS) * hgt_ref[...]
        o_ref[0, pl.ds(r0, L), :] = (og_ref[0, pl.ds(r0, L), :].astype(F32) * y_t.T).astype(BF16)

    def phase_c(g, carry):
        chunks = [g * C_GROUP + i for i in range(C_GROUP)]
        products = [phase_c_products(c) for c in chunks]
        for c, prod in zip(chunks, products):
            phase_c_finish(c, *prod)
        return carry

    lax.fori_loop(0, nc, phase_a, 0, unroll=4)
    c_ref[...] = jnp.zeros(c_ref.shape, F32)
    n_ref[...] = jnp.zeros(n_ref.shape, F32)
    m0 = jnp.zeros((1, LANES), F32)
    lax.fori_loop(0, nc, phase_b, (m0, m0))
    lax.fori_loop(0, nc // C_GROUP, phase_c, 0, unroll=2)


def _mlstm(mq, mk, vt, og, grow_h, conv_wq, conv_wk, head_g_t):
    B, S, _ = mq.shape
    nc = S // M_CHUNK
    head = pl.BlockSpec((1, S, M_HEAD_DIM), lambda b, h: (b, 0, h))
    cw = pl.BlockSpec((CONV_W, M_HEAD_DIM), lambda b, h: (0, h))
    return pl.pallas_call(
        _mlstm_kernel,
        grid=(B, M_HEADS),
        in_specs=[
            head, head,
            pl.BlockSpec((1, nc, M_HEAD_DIM, M_CHUNK), lambda b, h: (b, 0, h, 0)),
            head,
            pl.BlockSpec((1, nc, SUBLANES, M_CHUNK), lambda b, h: (b, 0, h, 0)),
            cw, cw,
            pl.BlockSpec((M_HEAD_DIM, LANES), lambda b, h: (h, 0)),
        ],
        out_specs=head,
        out_shape=jax.ShapeDtypeStruct((B, S, M_WIDTH), BF16),
        scratch_shapes=[
            pltpu.VMEM((S + 2 * SUBLANES, LANES), F32),
            pltpu.VMEM((S, M_HEAD_DIM), F32),
            pltpu.VMEM((S, M_HEAD_DIM), F32),
            pltpu.VMEM((nc, SUBLANES, 2 * M_CHUNK), F32),
            pltpu.VMEM((nc, 2 * M_CHUNK, M_CHUNK), F32),
            pltpu.VMEM((nc, SUBLANES, M_CHUNK), F32),
            pltpu.VMEM((nc, 2 * M_HEAD_DIM, M_HEAD_DIM), F32),
            pltpu.VMEM((nc, SUBLANES, M_HEAD_DIM), F32),
            pltpu.VMEM((nc, 2 * M_HEAD_DIM, M_HEAD_DIM), BF16),
            pltpu.VMEM((nc, SUBLANES, M_HEAD_DIM), F32),
            pltpu.VMEM((2 * M_HEAD_DIM, M_HEAD_DIM), F32),
            pltpu.VMEM((SUBLANES, M_HEAD_DIM), F32),
        ],
        compiler_params=_compiler_params(("parallel", "parallel")),
        name="mlstm",
    )(mq, mk, vt, og, grow_h, conv_wq, conv_wk, head_g_t)


def _route(logits):
    lane = lax.broadcasted_iota(jnp.int32, logits.shape, 1)
    lane_f = lane.astype(F32)
    big = float(LANES)
    is_g = (lane >= ROUTER_G_LANE) & (lane < ROUTER_G_LANE + N_GROUPS)
    gl = jnp.where(is_g, logits, NEG_BIG)
    gmax = jnp.max(gl, axis=-1, keepdims=True)
    gsum = jnp.sum(jnp.where(is_g, jnp.exp(gl - gmax), 0.0), axis=-1, keepdims=True)
    p_grp = 1.0 / gsum
    grp = jnp.min(jnp.where(is_g & (gl == gmax), lane_f - ROUTER_G_LANE, big), axis=-1, keepdims=True)
    in_grp = (lane < N_EXPERTS) & (jnp.right_shift(lane, 2).astype(F32) == grp)
    el = jnp.where(in_grp, logits, NEG_BIG)
    v1 = jnp.max(el, axis=-1, keepdims=True)
    i1 = jnp.min(jnp.where(in_grp & (el == v1), lane_f, big), axis=-1, keepdims=True)
    rest = in_grp & (lane_f != i1)
    el2 = jnp.where(rest, logits, NEG_BIG)
    v2 = jnp.max(el2, axis=-1, keepdims=True)
    i2 = jnp.min(jnp.where(rest & (el2 == v2), lane_f, big), axis=-1, keepdims=True)
    e21 = jnp.exp(v2 - v1)
    w1 = p_grp / (1.0 + e21)
    w2 = p_grp * e21 / (1.0 + e21)
    return jnp.where(lane == 0, i1, jnp.where(lane == 1, i2, jnp.where(lane == 2, w1, w2)))


def _pack_bf16_pairs(lo, hi):
    bits = lambda v: lax.bitcast_convert_type(v.astype(BF16).astype(F32), PACKED)
    return jnp.right_shift(bits(lo), PACKED(16)) | (bits(hi) & PACKED(0xFFFF0000))


def _unpack_bf16_pairs(w):
    return (lax.bitcast_convert_type(jnp.left_shift(w, PACKED(16)), F32),
            lax.bitcast_convert_type(w & PACKED(0xFFFF0000), F32))


def _pack_row_halves(y):
    return (_pack_bf16_pairs(y[:, 0:PACK_W], y[:, PACK_W:2 * PACK_W]),
            _pack_bf16_pairs(y[:, 2 * PACK_W:3 * PACK_W], y[:, 3 * PACK_W:4 * PACK_W]))


def _unpack_row_halves(a, b):
    return [*_unpack_bf16_pairs(a), *_unpack_bf16_pairs(b)]


def _merge_kernel(attn_ref, mo_ref, bg_ref, x_ref, mod_ref, wua_ref, wum_ref, wo_ref, g2_ref, wr_ref, br_ref,
                  x1_ref, ha_ref, hb_ref, rt_ref):
    tm = x_ref.shape[1]
    nparts = tm // MERGE_ROWS
    parts = [pl.ds(i * MERGE_ROWS, MERGE_ROWS) for i in range(nparts)]
    wr = wr_ref[...].astype(BF16)
    merged, h2s = {}, {}
    for step in range(nparts + 2):
        if step < nparts:
            h = parts[step]
            up_a = _dot(attn_ref[0, h, :], wua_ref[...])
            up_m = _dot(mo_ref[0, h, :], wum_ref[...])
            merged[step] = (bg_ref[0, h, 0:D_MODEL].astype(F32) * up_a
                            + bg_ref[0, h, D_MODEL:2 * D_MODEL].astype(F32) * up_m).astype(BF16)
        if 0 <= step - 1 < nparts:
            h = parts[step - 1]
            x1 = x_ref[0, h, :] + mod_ref[0, 2:3, :] * _dot(merged.pop(step - 1), wo_ref[...])
            x1_ref[0, h, :] = x1
            h2s[step - 1] = _rms_mod(x1, g2_ref[...], mod_ref[0, 4:5, :], mod_ref[0, 3:4, :])
        if 0 <= step - 2 < nparts:
            h = parts[step - 2]
            h2 = h2s.pop(step - 2)
            logits = _dot(h2.astype(BF16), wr) + br_ref[...]
            ha_ref[0, h, :], hb_ref[0, h, :] = _pack_row_halves(h2)
            rt_ref[0, h, :] = _route(logits)[:, 0:ROUTE_W]


def _merge(attn, mo, bg, x, mod, wua, wum, wo, norm_g, w_router, b_router, tm):
    B, S, D = x.shape
    tok = lambda w: pl.BlockSpec((1, tm, w), lambda b, i: (b, i, 0))
    const2 = lambda a: pl.BlockSpec(a.shape, lambda b, i: (0, 0))
    g2 = norm_g.reshape(1, D)
    return pl.pallas_call(
        _merge_kernel,
        grid=(B, S // tm),
        in_specs=[
            tok(A_WIDTH), tok(M_WIDTH), tok(2 * D), tok(D),
            pl.BlockSpec((1, 6, D), lambda b, i: (b, 0, 0)),
            const2(wua), const2(wum), const2(wo), const2(g2), const2(w_router), const2(b_router),
        ],
        out_specs=(tok(D), tok(PACK_W), tok(PACK_W), tok(ROUTE_W)),
        out_shape=(
            jax.ShapeDtypeStruct((B, S, D), F32),
            jax.ShapeDtypeStruct((B, S, PACK_W), PACKED),
            jax.ShapeDtypeStruct((B, S, PACK_W), PACKED),
            jax.ShapeDtypeStruct((B, S, ROUTE_W), F32),
        ),
        compiler_params=_compiler_params(("parallel", "parallel")),
        name="merge_route",
    )(attn, mo, bg, x, mod, wua, wum, wo, g2, w_router, b_router)


def _route_tables(eid, tmb):
    T = eid.shape[0]
    nblk = 2 * T // tmb + N_EXPERTS
    ea = eid.T.reshape(-1)
    onehot = (ea[:, None] == jnp.arange(N_EXPERTS, dtype=jnp.int32)[None, :]).astype(jnp.int32)
    csum = jnp.cumsum(onehot, axis=0)
    counts = csum[-1]
    rank = jnp.sum(onehot * csum, axis=1) - 1
    padded = ((counts + tmb - 1) // tmb) * tmb
    ends = jnp.cumsum(padded)
    starts = ends - padded
    pos = (jnp.sum(onehot * starts[None, :], axis=1) + rank).astype(jnp.int32)
    blk0 = jnp.arange(nblk, dtype=jnp.int32) * tmb
    bexp = jnp.minimum(jnp.sum((blk0[:, None] >= ends[None, :]).astype(jnp.int32), axis=1), N_EXPERTS - 1)
    nval = jnp.clip(starts[bexp] + counts[bexp] - blk0, 0, tmb)
    nval = jnp.where(blk0 < ends[-1], nval, 0).astype(jnp.int32)
    return pos, bexp.astype(jnp.int32), nval


def _sc_mesh():
    return plsc.VectorSubcoreMesh(core_axis_name="c", subcore_axis_name="s")


def _sc_dispatch(x, pos, n_rows):
    T = x.shape[0]
    nb = T // SC_WINDOW
    idx = pos.reshape(1, 2 * T)

    @pl.kernel(out_type=jax.ShapeDtypeStruct((n_rows, PACK_W), x.dtype), mesh=_sc_mesh(), scratch_types=[])
    def dispatch(x_hbm, i_hbm, o_hbm):
        def body(x_vmem, i_vmem):
            pltpu.sync_copy(x_vmem, o_hbm.at[i_vmem.at[0]])

        pltpu.emit_pipeline(
            body,
            grid=(2 * nb,),
            in_specs=[pl.BlockSpec((SC_WINDOW, PACK_W), index_map=lambda i: (i % nb, 0)),
                      pl.BlockSpec((1, SC_WINDOW), index_map=lambda i: (0, i))],
            out_specs=[],
            core_axis_name=("c", "s"),
            dimension_semantics=(pltpu.PARALLEL,),
        )(x_hbm, i_hbm)

    return dispatch(x, idx)


def _sc_combine(y, pos):
    n = pos.shape[0]
    idx = pos.reshape(1, n)

    @pl.kernel(out_type=jax.ShapeDtypeStruct((n, PACK_W), y.dtype), mesh=_sc_mesh(), scratch_types=[])
    def combine(y_hbm, i_hbm, o_hbm):
        def body(i_vmem, o_vmem):
            pltpu.sync_copy(y_hbm.at[i_vmem.at[0]], o_vmem)

        pltpu.emit_pipeline(
            body,
            grid=(n // SC_WINDOW,),
            in_specs=[pl.BlockSpec((1, SC_WINDOW), index_map=lambda i: (0, i))],
            out_specs=[pl.BlockSpec((SC_WINDOW, PACK_W), index_map=lambda i: (i, 0))],
            core_axis_name=("c", "s"),
            dimension_semantics=(pltpu.PARALLEL,),
        )(i_hbm, o_hbm)

    return combine(y, idx)


def _experts_kernel(bexp_ref, nval_ref, xa_ref, xb_ref, wg_ref, wu_ref, wd_ref, ya_ref, yb_ref, wg_s, wu_s, wd_s):
    i = pl.program_id(0)
    nv = nval_ref[i]
    new_expert = (i == 0) | (bexp_ref[i] != bexp_ref[jnp.maximum(i - 1, 0)])

    @pl.when((nv > 0) & new_expert)
    def _():
        wg_s[...] = wg_ref[0].astype(BF16)
        wu_s[...] = wu_ref[0].astype(BF16)
        wd_s[...] = wd_ref[0].astype(BF16)

    @pl.when(nv > 0)
    def _():
        keep = lax.broadcasted_iota(jnp.int32, xa_ref.shape, 0) < nv
        pieces = [jnp.where(keep, piece, 0.0).astype(BF16) for piece in _unpack_row_halves(xa_ref[...], xb_ref[...])]
        x = jnp.concatenate(pieces, axis=1)
        half = x.shape[0] // 2
        gus = [(_dot(x[r:r + half], wg_s[...]), _dot(x[r:r + half], wu_s[...])) for r in (0, half)]
        hes = [((g * _sigmoid(g)) * u).astype(BF16) for g, u in gus]
        for r, he in zip((0, half), hes):
            ya_ref[r:r + half, :], yb_ref[r:r + half, :] = _pack_row_halves(_dot(he, wd_s[...]))


def _experts(xa, xb, bexp, nval, wg, wu, wd, tmb):
    n = xa.shape[0]
    row = lambda: pl.BlockSpec((tmb, PACK_W), lambda i, be, nv: (i, 0))
    grid_spec = pltpu.PrefetchScalarGridSpec(
        num_scalar_prefetch=2,
        grid=(n // tmb,),
        in_specs=[row(), row(),
                  pl.BlockSpec((1, D_MODEL, D_EXPERT), lambda i, be, nv: (be[i], 0, 0)),
                  pl.BlockSpec((1, D_MODEL, D_EXPERT), lambda i, be, nv: (be[i], 0, 0)),
                  pl.BlockSpec((1, D_EXPERT, D_MODEL), lambda i, be, nv: (be[i], 0, 0))],
        out_specs=[row(), row()],
        scratch_shapes=[pltpu.VMEM((D_MODEL, D_EXPERT), BF16), pltpu.VMEM((D_MODEL, D_EXPERT), BF16),
                        pltpu.VMEM((D_EXPERT, D_MODEL), BF16)],
    )
    out = jax.ShapeDtypeStruct((n, PACK_W), xa.dtype)
    return pl.pallas_call(
        _experts_kernel, grid_spec=grid_spec, out_shape=(out, out),
        compiler_params=_compiler_params(("arbitrary",)),
        name="experts",
    )(bexp, nval, xa, xb, wg, wu, wd)


def _finish_kernel(a0_ref, b0_ref, a1_ref, b1_ref, rt_ref, x1_ref, mod_ref, fg_ref, o_ref):
    y0 = jnp.concatenate(_unpack_row_halves(a0_ref[...], b0_ref[...]), axis=1)
    y1 = jnp.concatenate(_unpack_row_halves(a1_ref[...], b1_ref[...]), axis=1)
    rt = rt_ref[0]
    moe = rt[:, 2:3] * y0 + rt[:, 3:4] * y1
    xo = x1_ref[0] + mod_ref[0, 5:6, :] * moe
    ms = jnp.mean(xo * xo, axis=-1, keepdims=True)
    o_ref[0] = xo * lax.rsqrt(ms + EPS) * fg_ref[...]


def _finish(ca, cb, rt, x1, mod, final_g, tm):
    B, S, D = x1.shape
    nt = S // tm
    second = B * nt
    half = lambda k: pl.BlockSpec((tm, PACK_W), lambda b, i: (k * second + b * nt + i, 0))
    tok = lambda w: pl.BlockSpec((1, tm, w), lambda b, i: (b, i, 0))
    fg = final_g.reshape(1, D)
    return pl.pallas_call(
        _finish_kernel,
        grid=(B, nt),
        in_specs=[half(0), half(0), half(1), half(1), tok(ROUTE_W), tok(D),
                  pl.BlockSpec((1, 6, D), lambda b, i: (b, 0, 0)),
                  pl.BlockSpec((1, D), lambda b, i: (0, 0))],
        out_specs=tok(D),
        out_shape=jax.ShapeDtypeStruct((B, S, D), F32),
        compiler_params=_compiler_params(("parallel", "parallel")),
        name="finish",
    )(ca, cb, ca, cb, rt, x1, mod, fg)


def _moe(ha, hb, rt, x1, mod, wg, wu, wd, final_g):
    B, S, D = x1.shape
    T = B * S
    tmb = EXPERT_ROWS
    eid = rt.reshape(T, ROUTE_W)[:, 0:2].astype(jnp.int32)
    pos, bexp, nval = _route_tables(eid, tmb)
    n_rows = 2 * T + N_EXPERTS * tmb
    xa = _sc_dispatch(ha.reshape(T, PACK_W), pos, n_rows)
    xb = _sc_dispatch(hb.reshape(T, PACK_W), pos, n_rows)
    ya, yb = _experts(xa, xb, bexp, nval, wg, wu, wd, tmb)
    return _finish(_sc_combine(ya, pos), _sc_combine(yb, pos), rt, x1, mod, final_g, tm=min(1024, S))


def _rope_tables(S):
    half = ROPE_DIM // 2
    inv_freq = 1.0 / (ROPE_THETA ** (jnp.arange(half, dtype=F32) * 2.0 / ROPE_DIM))
    ang = jnp.arange(S, dtype=F32)[:, None] * inv_freq[None, :]
    cos, sin = jnp.cos(ang), jnp.sin(ang)
    zeros = jnp.zeros((S, A_HEAD_DIM - ROPE_DIM), F32)
    z8 = jnp.zeros((S, half), F32)
    cos_h = jnp.concatenate([cos, cos, jnp.ones_like(zeros)], axis=-1)
    slo_h = jnp.concatenate([-sin, z8, zeros], axis=-1)
    shi_h = jnp.concatenate([z8, sin, zeros], axis=-1)
    rep = LANES // A_HEAD_DIM
    return tuple(jnp.tile(t, (1, rep)) for t in (cos_h, slo_h, shi_h))


def _layer(x, mod, p, tabs):
    B, S, D = x.shape
    mod = mod.reshape(B, 6, D)
    q, kx, vx, mq, mk, vt, og, bg, grow = _in_proj(
        x, mod, p["norm1_g"], p["w_main"], p["w_t"], p["m_gate_b"], tabs, tm=512)
    attn = _attention(q, kx, vx, p["attn_sink"])
    mo = _mlstm(mq, mk, vt, og, grow, p["conv_wq"], p["conv_wk"], p["head_norm_g_t"])
    x1, ha, hb, rt = _merge(attn, mo, bg, x, mod, p["w_up_attn"], p["w_up_mlstm"], p["w_out"], p["norm2_g"],
                            p["w_router"], p["b_router"], tm=min(1024, S))
    return _moe(ha, hb, rt, x1, mod, p["w_gate"], p["w_up"], p["w_down"], p["final_norm_g"])


def kernel(x_prompt, x_sample, c_prompt, c_sample, ada_w, ada_b, norm1_g, w_in, conv_w, m_gate_b, attn_sink,
           head_norm_g, w_up_attn, w_up_mlstm, w_out, norm2_g, rg_w, rg_b, re_w, re_b, w_gate, w_up, w_down,
           final_norm_g):
    assert ada_w.shape[0] == 1, "single-layer trunk"
    w_in0 = w_in[0]
    w_g = w_in0[:, OFF_MG:OFF_BG]
    pad = LANES - N_EXPERTS - N_GROUPS
    p = dict(
        ada_w=ada_w[0], ada_b=ada_b[0], norm1_g=norm1_g[0],
        w_main=jnp.concatenate([w_in0[:, :OFF_MV], w_in0[:, OFF_MO:OFF_MG], w_in0[:, OFF_BG:]], axis=1).astype(BF16),
        w_t=jnp.concatenate([w_in0[:, OFF_MV:OFF_MO], _gates_head_major(w_g)], axis=1).T.astype(BF16),
        m_gate_b=_gates_head_major(m_gate_b[0]), attn_sink=attn_sink[0],
        conv_wq=conv_w[0, :, :M_WIDTH], conv_wk=conv_w[0, :, M_WIDTH:],
        head_norm_g_t=jnp.broadcast_to(head_norm_g[0][:, None], (M_WIDTH, LANES)),
        w_up_attn=w_up_attn[0].astype(BF16), w_up_mlstm=w_up_mlstm[0].astype(BF16), w_out=w_out[0].astype(BF16),
        norm2_g=norm2_g[0],
        w_router=jnp.pad(jnp.concatenate([re_w[0], rg_w[0]], axis=1), ((0, 0), (0, pad))),
        b_router=jnp.pad(jnp.concatenate([re_b[0], rg_b[0]]), (0, pad)).reshape(1, LANES),
        w_gate=w_gate[0], w_up=w_up[0], w_down=w_down[0],
        final_norm_g=final_norm_g,
    )
    tabs = _rope_tables(x_prompt.shape[1])
    nbp = x_prompt.shape[0]
    mod = _ada_mod(jnp.concatenate([c_prompt, c_sample], axis=0), p["ada_w"], p["ada_b"])
    return (_layer(x_prompt, mod[:nbp], p, tabs), _layer(x_sample, mod[nbp:], p, tabs))
```

```python
import functools

import jax
import jax.numpy as jnp
from jax import lax
from jax.experimental import pallas as pl
from jax.experimental.pallas import tpu as pltpu
from jax.experimental.pallas import tpu_sc as plsc

D_MODEL = 1024
A_HEADS = 8
A_KV_HEADS = 2
A_GROUP = A_HEADS // A_KV_HEADS
A_HEAD_DIM = 64
A_WIDTH = A_HEADS * A_HEAD_DIM
A_KV_WIDTH = A_KV_HEADS * A_HEAD_DIM
WINDOW = 128
BLOCK = 128
ROPE_DIM = A_HEAD_DIM // 4
ROPE_THETA = 500000.0
M_HEADS = 4
M_HEAD_DIM = 128
M_WIDTH = M_HEADS * M_HEAD_DIM
M_CHUNK = 128
CONV_W = 3
OFF_AQ = 0
OFF_AK = OFF_AQ + A_WIDTH
OFF_AV = OFF_AK + A_KV_WIDTH
OFF_MQ = OFF_AV + A_KV_WIDTH
OFF_MK = OFF_MQ + M_WIDTH
OFF_MV = OFF_MK + M_WIDTH
OFF_MO = OFF_MV + M_WIDTH
OFF_MG = OFF_MO + M_WIDTH
N_MGATES = 4 * M_HEADS
OFF_BG = OFF_MG + N_MGATES
IN_TOTAL = OFF_BG + 2 * D_MODEL
N_GROUPS = 4
EXPERTS_PER_GROUP = 4
N_EXPERTS = N_GROUPS * EXPERTS_PER_GROUP
D_EXPERT = 512
EPS = 1e-6
NEG_BIG = -1e30
F32 = jnp.float32
BF16 = jnp.bfloat16

LANES = 128
SUBLANES = 8
VMEM_LIMIT_BYTES = 56 * 1024 * 1024

MAIN_AQ = 0
MAIN_KV = A_WIDTH
MAIN_MQ = MAIN_KV + 2 * A_KV_WIDTH
MAIN_MK = MAIN_MQ + M_WIDTH
MAIN_MO = MAIN_MK + M_WIDTH
MAIN_BG = MAIN_MO + M_WIDTH
MAIN_TOTAL = MAIN_BG + 2 * D_MODEL

LOG2E = 1.4426950408889634
Q_SCALE = A_HEAD_DIM ** -0.5 * LOG2E
ROUTER_G_LANE = N_EXPERTS
ROUTE_W = 4
PACKED = jnp.uint32
PACK_W = D_MODEL // 4
SC_WINDOW = 128
EXPERT_ROWS = 512
MERGE_ROWS = 256
GATE_ROWS = M_HEADS * SUBLANES


def _sigmoid(z):
    return 1.0 / (1.0 + jnp.exp(-z))


def _log_sigmoid(z):
    return jnp.minimum(z, 0.0) - jnp.log(1.0 + jnp.exp(-jnp.abs(z)))


def _gates_head_major(a):
    lead = a.shape[:-1]
    a = jnp.swapaxes(a.reshape(*lead, 4, M_HEADS), -1, -2)
    a = jnp.pad(a, [(0, 0)] * (a.ndim - 1) + [(0, SUBLANES - 4)])
    return a.reshape(*lead, GATE_ROWS)


def _dot(a, b):
    return jnp.dot(a, b, preferred_element_type=F32)


def _dot_nt(a, b):
    return lax.dot_general(a, b, (((1,), (1,)), ((), ())), preferred_element_type=F32)


def _dot_tn(a, b):
    return lax.dot_general(a, b, (((0,), (0,)), ((), ())), preferred_element_type=F32)


def _dot_f32(a, b):
    return jnp.dot(a, b, preferred_element_type=F32, precision=lax.Precision.HIGHEST)


def _compiler_params(semantics):
    return pltpu.CompilerParams(dimension_semantics=semantics, vmem_limit_bytes=VMEM_LIMIT_BYTES)


def _ada_kernel(c_ref, w_ref, b_ref, o_ref):
    c = c_ref[...]
    o_ref[...] = _dot_f32(c * _sigmoid(c), w_ref[...]) + b_ref[...]


def _ada_mod(c, ada_w, ada_b):
    B, D = c.shape
    n = ada_w.shape[1] // D
    return pl.pallas_call(
        _ada_kernel,
        grid=(n,),
        in_specs=[
            pl.BlockSpec((B, D), lambda j: (0, 0)),
            pl.BlockSpec((D, D), lambda j: (0, j)),
            pl.BlockSpec((1, D), lambda j: (0, j)),
        ],
        out_specs=pl.BlockSpec((B, D), lambda j: (0, j)),
        out_shape=jax.ShapeDtypeStruct((B, n * D), F32),
        compiler_params=_compiler_params(("arbitrary",)),
        name="ada_mod",
    )(c, ada_w, ada_b.reshape(1, -1))


def _rms_mod(x, g, scale, shift):
    ms = jnp.mean(x * x, axis=-1, keepdims=True)
    return (x * lax.rsqrt(ms + EPS) * g) * (1.0 + scale) + shift


def _rope_block(xb, cos, sin_lo, sin_hi):
    half = ROPE_DIM // 2
    return xb * cos + pltpu.roll(xb, LANES - half, axis=1) * sin_lo + pltpu.roll(xb, half, axis=1) * sin_hi


def _in_proj_kernel(x_ref, mod_ref, g_ref, w_ref, wt_ref, gb_col_ref, cos_ref, slo_ref, shi_ref,
                    q_ref, kx_ref, vx_ref, mq_ref, mk_ref, vt_ref, og_ref, bg_ref, grow_ref):
    x = x_ref[0]
    tm = x.shape[0]
    h = _rms_mod(x, g_ref[...], mod_ref[0, 1:2, :], mod_ref[0, 0:1, :])
    hb = h.astype(BF16)
    cos, slo, shi = cos_ref[...], slo_ref[...], shi_ref[...]
    lane = lax.broadcasted_iota(jnp.int32, (tm, LANES), 1)
    left = lane < A_HEAD_DIM

    pj = _dot(hb, w_ref[:, MAIN_AQ:MAIN_AQ + A_WIDTH])
    for j in range(A_WIDTH // LANES):
        blk = _rope_block(pj[:, j * LANES:(j + 1) * LANES], cos, slo, shi)
        q_ref[0, :, j * LANES:(j + 1) * LANES] = (blk * Q_SCALE).astype(BF16)

    pj = _dot(hb, w_ref[:, MAIN_KV:MAIN_KV + 2 * A_KV_WIDTH])
    kk = _rope_block(pj[:, 0:LANES], cos, slo, shi)
    vv = pj[:, LANES:2 * LANES]
    for src, dst in ((kk, kx_ref), (vv, vx_ref)):
        swapped = pltpu.roll(src, A_HEAD_DIM, axis=1)
        zero = jnp.zeros_like(src)
        dst[0, :, 0 * LANES:1 * LANES] = jnp.where(left, src, zero).astype(BF16)
        dst[0, :, 1 * LANES:2 * LANES] = jnp.where(left, zero, swapped).astype(BF16)
        dst[0, :, 2 * LANES:3 * LANES] = jnp.where(left, swapped, zero).astype(BF16)
        dst[0, :, 3 * LANES:4 * LANES] = jnp.where(left, zero, src).astype(BF16)

    mq_ref[0] = _dot(hb, w_ref[:, MAIN_MQ:MAIN_MQ + M_WIDTH]).astype(BF16)
    mk_ref[0] = _dot(hb, w_ref[:, MAIN_MK:MAIN_MK + M_WIDTH]).astype(BF16)
    og_ref[0] = _sigmoid(_dot(hb, w_ref[:, MAIN_MO:MAIN_MO + M_WIDTH])).astype(BF16)
    for j in range(2 * D_MODEL // M_WIDTH):
        lo = MAIN_BG + j * M_WIDTH
        bg_ref[0, :, j * M_WIDTH:(j + 1) * M_WIDTH] = _sigmoid(_dot(hb, w_ref[:, lo:lo + M_WIDTH])).astype(BF16)

    tr = _dot_nt(wt_ref[...], hb)
    gr = tr[M_WIDTH:M_WIDTH + GATE_ROWS, :] + gb_col_ref[...]
    gtype = jnp.bitwise_and(lax.broadcasted_iota(jnp.int32, gr.shape, 0), SUBLANES - 1)
    gr = jnp.where(gtype == 1, _log_sigmoid(gr), jnp.where(gtype == 3, _log_sigmoid(gr), gr))
    vt = tr[0:M_WIDTH, :].astype(BF16)
    for c in range(tm // M_CHUNK):
        grow_ref[0, c] = gr[:, c * M_CHUNK:(c + 1) * M_CHUNK]
        vt_ref[0, c] = vt[:, c * M_CHUNK:(c + 1) * M_CHUNK]


def _in_proj(x, mod, norm_g, w_main, w_t, gate_b, rope_tabs, tm):
    B, S, D = x.shape
    nt = S // tm
    cos, slo, shi = rope_tabs
    tok = lambda w: pl.BlockSpec((1, tm, w), lambda b, i: (b, i, 0))
    const2 = lambda a: pl.BlockSpec(a.shape, lambda b, i: (0, 0))
    tab = pl.BlockSpec((tm, LANES), lambda b, i: (i, 0))
    out_shapes = (
        jax.ShapeDtypeStruct((B, S, A_WIDTH), BF16),
        jax.ShapeDtypeStruct((B, S, 4 * LANES), BF16),
        jax.ShapeDtypeStruct((B, S, 4 * LANES), BF16),
        jax.ShapeDtypeStruct((B, S, M_WIDTH), BF16),
        jax.ShapeDtypeStruct((B, S, M_WIDTH), BF16),
        jax.ShapeDtypeStruct((B, S // M_CHUNK, M_WIDTH, M_CHUNK), BF16),
        jax.ShapeDtypeStruct((B, S, M_WIDTH), BF16),
        jax.ShapeDtypeStruct((B, S, 2 * D_MODEL), BF16),
        jax.ShapeDtypeStruct((B, S // M_CHUNK, GATE_ROWS, M_CHUNK), F32),
    )
    chunked = lambda rows: pl.BlockSpec((1, tm // M_CHUNK, rows, M_CHUNK), lambda b, i: (b, i, 0, 0))
    out_specs = (
        tok(A_WIDTH), tok(4 * LANES), tok(4 * LANES), tok(M_WIDTH), tok(M_WIDTH), chunked(M_WIDTH), tok(M_WIDTH),
        tok(2 * D_MODEL), chunked(GATE_ROWS),
    )
    gb_col = gate_b.reshape(GATE_ROWS, 1)
    g2 = norm_g.reshape(1, D)
    return pl.pallas_call(
        _in_proj_kernel,
        grid=(B, nt),
        in_specs=[
            tok(D),
            pl.BlockSpec((1, 6, D), lambda b, i: (b, 0, 0)),
            const2(g2), const2(w_main), const2(w_t), const2(gb_col),
            tab, tab, tab,
        ],
        out_specs=out_specs,
        out_shape=out_shapes,
        compiler_params=_compiler_params(("parallel", "parallel")),
        name="in_proj",
    )(x, mod, g2, w_main, w_t, gb_col, cos, slo, shi)


def _attn_kernel(sink_ref, q_ref, kx_ref, vx_ref, o_ref):
    S = q_ref.shape[1]
    nb = S // BLOCK
    kw = 3 * BLOCK
    qi = lax.broadcasted_iota(jnp.int32, (BLOCK, kw), 0)
    ki = lax.broadcasted_iota(jnp.int32, (BLOCK, kw), 1)
    rel0 = ki - qi
    ones_b = jnp.ones((kw, LANES), BF16)
    left = lax.broadcasted_iota(jnp.int32, (BLOCK, LANES), 1) < A_HEAD_DIM

    def block(n, carry):
        q0 = pl.multiple_of(n * BLOCK, BLOCK)
        k0 = pl.multiple_of(jnp.clip((n - 1) * BLOCK, 0, S - kw), BLOCK)
        rel = rel0 + (k0 - q0)
        valid = jnp.abs(rel) <= WINDOW
        scores = []
        for hk in range(A_KV_HEADS):
            for j in range(A_GROUP // 2):
                col = (hk * (A_GROUP // 2) + j) * LANES
                qp = q_ref[0, pl.ds(q0, BLOCK), col:col + LANES]
                for side in range(2):
                    kk = kx_ref[0, pl.ds(k0, kw), (2 * hk + side) * LANES:(2 * hk + side + 1) * LANES]
                    scores.append(_dot_nt(qp, kk))
        for hk in range(A_KV_HEADS):
            for j in range(A_GROUP // 2):
                col = (hk * (A_GROUP // 2) + j) * LANES
                outs, dens = [], []
                for side in range(2):
                    vv = vx_ref[0, pl.ds(k0, kw), (2 * hk + side) * LANES:(2 * hk + side + 1) * LANES]
                    sk = sink_ref[hk * A_GROUP + 2 * j + side] * LOG2E
                    s = jnp.where(valid, scores[(hk * (A_GROUP // 2) + j) * 2 + side], NEG_BIG)
                    m = jnp.maximum(jnp.max(s, axis=-1, keepdims=True), sk)
                    p = jnp.exp2(s - m).astype(BF16)
                    od = _dot(p, jnp.concatenate([vv, ones_b], axis=1))
                    outs.append(od[:, 0:LANES])
                    dens.append(od[:, LANES:2 * LANES] + jnp.exp2(sk - m))
                o = (outs[0] + outs[1]) / jnp.where(left, dens[0], dens[1])
                o_ref[0, pl.ds(q0, BLOCK), col:col + LANES] = o.astype(BF16)
        return carry

    lax.fori_loop(0, nb, block, 0, unroll=2)


def _attention(q, kx, vx, sink):
    B, S, _ = q.shape
    seq = lambda w: pl.BlockSpec((1, S, w), lambda b: (b, 0, 0))
    return pl.pallas_call(
        _attn_kernel,
        grid=(B,),
        in_specs=[pl.BlockSpec(memory_space=pltpu.SMEM), seq(A_WIDTH), seq(4 * LANES), seq(4 * LANES)],
        out_specs=seq(A_WIDTH),
        out_shape=jax.ShapeDtypeStruct((B, S, A_WIDTH), BF16),
        compiler_params=_compiler_params(("parallel",)),
        name="window_attn",
    )(sink, q, kx, vx)


def _conv_silu(u_ref, w_ref, pad_ref, dst_ref, scale):
    S = u_ref.shape[1]
    pad_ref[0:SUBLANES, :] = jnp.zeros((SUBLANES, LANES), F32)
    pad_ref[S + SUBLANES:S + 2 * SUBLANES, :] = jnp.zeros((SUBLANES, LANES), F32)
    pad_ref[SUBLANES:S + SUBLANES, :] = u_ref[0].astype(F32)
    w0, w1, w2 = w_ref[0:1, :], w_ref[1:2, :], w_ref[2:3, :]
    for c in range(S // M_CHUNK):
        base = SUBLANES + c * M_CHUNK
        y = (pad_ref[base - 1:base - 1 + M_CHUNK, :] * w0 + pad_ref[base:base + M_CHUNK, :] * w1
             + pad_ref[base + 1:base + 1 + M_CHUNK, :] * w2)
        y = y * _sigmoid(y)
        dst_ref[c * M_CHUNK:(c + 1) * M_CHUNK, :] = y if scale == 1.0 else y * scale


def _split3(x):
    hi = x.astype(BF16)
    r = x - hi.astype(F32)
    mid = r.astype(BF16)
    lo = (r - mid.astype(F32)).astype(BF16)
    return [hi, mid, lo]


def _mlstm_kernel(mq_ref, mk_ref, vt_ref, og_ref, grow_ref, cwq_ref, cwk_ref, hgt_ref, o_ref,
                  pad_ref, qs_ref, ks_ref, rb_ref, ib_ref, rows_ref, cinc_ref, ninc_ref, cin_ref, sin_ref,
                  c_ref, n_ref):
    S = mq_ref.shape[1]
    nc = S // M_CHUNK
    L = M_CHUNK
    DH = M_HEAD_DIM
    C_GROUP = 4 if nc % 4 == 0 else 1
    _conv_silu(mq_ref, cwq_ref, pad_ref, qs_ref, 1.0)
    _conv_silu(mk_ref, cwk_ref, pad_ref, ks_ref, M_HEAD_DIM ** -0.5)

    s_i = lax.broadcasted_iota(jnp.int32, (L, L), 0)
    t_i = lax.broadcasted_iota(jnp.int32, (L, L), 1)
    tris = (s_i <= t_i, s_i >= t_i)
    eye = s_i == t_i
    row8 = lax.broadcasted_iota(jnp.int32, (SUBLANES, L), 0)
    one_if = lambda cond: jnp.where(cond, 1.0, 0.0)
    k_j = lax.broadcasted_iota(jnp.int32, (3 * L, 2 * L), 0) % L
    c_j = lax.broadcasted_iota(jnp.int32, (3 * L, 2 * L), 1)
    sum_rows = jnp.where(c_j < L, one_if(k_j <= c_j), one_if(k_j >= c_j - L)).astype(BF16)

    g_all = grow_ref[0].reshape(nc * SUBLANES, L)
    rb_ref[...] = _dot(jnp.concatenate(_split3(g_all), axis=1), sum_rows).reshape(nc, SUBLANES, 2 * L)
    ones_b = jnp.ones((2 * L, L), BF16)

    def phase_a_operands(c):
        gr = grow_ref[0, c]
        rb = rb_ref[c]
        vt = vt_ref[0, c].astype(F32)
        wvs, wks, diags = [], [], []
        for d in range(2):
            brow = rb[2 * d + 1:2 * d + 2, d * L:(d + 1) * L]
            blast = brow[:, L - 1:L] if d == 0 else brow[:, 0:1]
            ibr = gr[2 * d:2 * d + 1, :] - brow
            log_g = blast + ibr
            mg = jnp.max(log_g, axis=-1, keepdims=True)
            wk = jnp.exp(log_g - mg)
            wvs.append((vt * wk).astype(BF16))
            wks.append(wk)
            diags.append(jnp.concatenate(
                [jnp.where(eye, term.astype(F32), 0.0).astype(BF16) for term in _split3(ibr)[:2]], axis=1))
            rows_ref[c, 2 + d:3 + d, :] = brow
            rows_ref[c, 4 + d:5 + d, :] = jnp.broadcast_to(mg, (1, L))
            rows_ref[c, 6 + d:7 + d, :] = jnp.broadcast_to(blast, (1, L))
        wk8 = jnp.where(row8 == 0, wks[0], jnp.where(row8 == 1, wks[1], 0.0))
        return jnp.concatenate(diags, axis=0), jnp.concatenate(wvs, axis=0), wk8.astype(BF16)

    def phase_a_products(c, diag, wv, wk8):
        kb = ks_ref[pl.ds(pl.multiple_of(c * L, L), L), :].astype(BF16)
        ib = _dot(diag, ones_b)
        ib_ref[c] = ib
        for d in range(2):
            rows_ref[c, d:d + 1, :] = jnp.max(jnp.where(tris[d], ib[d * L:(d + 1) * L, :], NEG_BIG),
                                              axis=0, keepdims=True)
        cinc_ref[c] = _dot(wv, kb)
        ninc_ref[c] = _dot(wk8, kb)

    def phase_a(g, carry):
        chunks = [g * C_GROUP + i for i in range(C_GROUP)]
        operands = [phase_a_operands(c) for c in chunks]
        for c, ops in zip(chunks, operands):
            phase_a_products(c, *ops)
        return carry

    def phase_b(j, ms):
        new_ms = []
        for d, cc in ((0, j), (1, nc - 1 - j)):
            m = ms[d]
            half = slice(d * DH, (d + 1) * DH)
            cst = c_ref[half, :]
            n = n_ref[d:d + 1, :]
            cin_ref[cc, half, :] = cst.astype(BF16)
            sin_ref[cc, d:d + 1, :] = n
            sin_ref[cc, 2 + d:3 + d, :] = m
            mg = rows_ref[cc, 4 + d:5 + d, :]
            blast = rows_ref[cc, 6 + d:7 + d, :]
            m_new = jnp.maximum(blast + m, mg)
            decay = jnp.exp(blast + m - m_new)
            grow = jnp.exp(mg - m_new)
            c_ref[half, :] = decay * cst + grow * cinc_ref[cc, half, :]
            n_ref[d:d + 1, :] = decay * n + grow * ninc_ref[cc, d:d + 1, :]
            new_ms.append(m_new)
        return tuple(new_ms)

    def phase_c_products(c):
        r0 = pl.multiple_of(c * L, L)
        qb = qs_ref[pl.ds(r0, L), :].astype(BF16)
        qk_t = _dot_nt(ks_ref[pl.ds(r0, L), :].astype(BF16), qb)
        qc_t = _dot_nt(cin_ref[c], qb)
        qn = _dot_nt(sin_ref[c].astype(BF16), qb)
        return qk_t, qc_t, qn

    def phase_c_finish(c, qk_t, qc_t, qn):
        r0 = pl.multiple_of(c * L, L)
        sin = sin_ref[c]
        rows = rows_ref[c]
        ats, stats = [], []
        for d in range(2):
            m_in = sin[2 + d:3 + d, :]
            cm = jnp.maximum(m_in, rows[d:d + 1, :])
            a_t = qk_t * jnp.where(tris[d], jnp.exp(ib_ref[c, d * L:(d + 1) * L, :] - cm), 0.0)
            w_inter = jnp.exp(m_in - cm)
            den = w_inter * qn[d:d + 1, :] + jnp.sum(a_t, axis=0, keepdims=True)
            m_t = rows[2 + d:3 + d, :] + cm
            ats.append(a_t.astype(BF16))
            stats.append((w_inter, jnp.maximum(jnp.abs(den), jnp.exp(-m_t))))
        av_t = _dot(vt_ref[0, c], jnp.concatenate(ats, axis=1))
        h_t = None
        for d in range(2):
            w_inter, den = stats[d]
            hd = (w_inter * qc_t[d * DH:(d + 1) * DH, :] + av_t[:, d * L:(d + 1) * L]) / den
            h_t = hd if h_t is None else h_t + hd
        y_t = h_t * lax.rsqrt(jnp.mean(h_t * h_t, axis=0, keepdims=True) + EPS) * hgt_ref[...]
        o_ref[0, pl.ds(r0, L), :] = (og_ref[0, pl.ds(r0, L), :].astype(F32) * y_t.T).astype(BF16)

    def phase_c(g, carry):
        chunks = [g * C_GROUP + i for i in range(C_GROUP)]
        products = [phase_c_products(c) for c in chunks]
        for c, prod in zip(chunks, products):
            phase_c_finish(c, *prod)
        return carry

    lax.fori_loop(0, nc // C_GROUP, phase_a, 0, unroll=2)
    c_ref[...] = jnp.zeros(c_ref.shape, F32)
    n_ref[...] = jnp.zeros(n_ref.shape, F32)
    m0 = jnp.zeros((1, LANES), F32)
    lax.fori_loop(0, nc, phase_b, (m0, m0))
    lax.fori_loop(0, nc // C_GROUP, phase_c, 0, unroll=2)


def _mlstm(mq, mk, vt, og, grow_h, conv_wq, conv_wk, head_g_t):
    B, S, _ = mq.shape
    nc = S // M_CHUNK
    head = pl.BlockSpec((1, S, M_HEAD_DIM), lambda b, h: (b, 0, h))
    cw = pl.BlockSpec((CONV_W, M_HEAD_DIM), lambda b, h: (0, h))
    return pl.pallas_call(
        _mlstm_kernel,
        grid=(B, M_HEADS),
        in_specs=[
            head, head,
            pl.BlockSpec((1, nc, M_HEAD_DIM, M_CHUNK), lambda b, h: (b, 0, h, 0)),
            head,
            pl.BlockSpec((1, nc, SUBLANES, M_CHUNK), lambda b, h: (b, 0, h, 0)),
            cw, cw,
            pl.BlockSpec((M_HEAD_DIM, LANES), lambda b, h: (h, 0)),
        ],
        out_specs=head,
        out_shape=jax.ShapeDtypeStruct((B, S, M_WIDTH), BF16),
        scratch_shapes=[
            pltpu.VMEM((S + 2 * SUBLANES, LANES), F32),
            pltpu.VMEM((S, M_HEAD_DIM), F32),
            pltpu.VMEM((S, M_HEAD_DIM), F32),
            pltpu.VMEM((nc, SUBLANES, 2 * M_CHUNK), F32),
            pltpu.VMEM((nc, 2 * M_CHUNK, M_CHUNK), F32),
            pltpu.VMEM((nc, SUBLANES, M_CHUNK), F32),
            pltpu.VMEM((nc, 2 * M_HEAD_DIM, M_HEAD_DIM), F32),
            pltpu.VMEM((nc, SUBLANES, M_HEAD_DIM), F32),
            pltpu.VMEM((nc, 2 * M_HEAD_DIM, M_HEAD_DIM), BF16),
            pltpu.VMEM((nc, SUBLANES, M_HEAD_DIM), F32),
            pltpu.VMEM((2 * M_HEAD_DIM, M_HEAD_DIM), F32),
            pltpu.VMEM((SUBLANES, M_HEAD_DIM), F32),
        ],
        compiler_params=_compiler_params(("parallel", "parallel")),
        name="mlstm",
    )(mq, mk, vt, og, grow_h, conv_wq, conv_wk, head_g_t)


def _route(logits):
    lane = lax.broadcasted_iota(jnp.int32, logits.shape, 1)
    lane_f = lane.astype(F32)
    big = float(LANES)
    is_g = (lane >= ROUTER_G_LANE) & (lane < ROUTER_G_LANE + N_GROUPS)
    gl = jnp.where(is_g, logits, NEG_BIG)
    gmax = jnp.max(gl, axis=-1, keepdims=True)
    gsum = jnp.sum(jnp.where(is_g, jnp.exp(gl - gmax), 0.0), axis=-1, keepdims=True)
    p_grp = 1.0 / gsum
    grp = jnp.min(jnp.where(is_g & (gl == gmax), lane_f - ROUTER_G_LANE, big), axis=-1, keepdims=True)
    in_grp = (lane < N_EXPERTS) & (jnp.right_shift(lane, 2).astype(F32) == grp)
    el = jnp.where(in_grp, logits, NEG_BIG)
    v1 = jnp.max(el, axis=-1, keepdims=True)
    i1 = jnp.min(jnp.where(in_grp & (el == v1), lane_f, big), axis=-1, keepdims=True)
    rest = in_grp & (lane_f != i1)
    el2 = jnp.where(rest, logits, NEG_BIG)
    v2 = jnp.max(el2, axis=-1, keepdims=True)
    i2 = jnp.min(jnp.where(rest & (el2 == v2), lane_f, big), axis=-1, keepdims=True)
    e21 = jnp.exp(v2 - v1)
    w1 = p_grp / (1.0 + e21)
    w2 = p_grp * e21 / (1.0 + e21)
    return jnp.where(lane == 0, i1, jnp.where(lane == 1, i2, jnp.where(lane == 2, w1, w2)))


def _pack_bf16_pairs(lo, hi):
    bits = lambda v: lax.bitcast_convert_type(v.astype(BF16).astype(F32), PACKED)
    return jnp.right_shift(bits(lo), PACKED(16)) | (bits(hi) & PACKED(0xFFFF0000))


def _unpack_bf16_pairs(w):
    return (lax.bitcast_convert_type(jnp.left_shift(w, PACKED(16)), F32),
            lax.bitcast_convert_type(w & PACKED(0xFFFF0000), F32))


def _pack_row_halves(y):
    return (_pack_bf16_pairs(y[:, 0:PACK_W], y[:, PACK_W:2 * PACK_W]),
            _pack_bf16_pairs(y[:, 2 * PACK_W:3 * PACK_W], y[:, 3 * PACK_W:4 * PACK_W]))


def _unpack_row_halves(a, b):
    return [*_unpack_bf16_pairs(a), *_unpack_bf16_pairs(b)]


def _merge_kernel(attn_ref, mo_ref, bg_ref, x_ref, mod_ref, wua_ref, wum_ref, wo_ref, g2_ref, wr_ref, br_ref,
                  x1_ref, ha_ref, hb_ref, rt_ref, ids_ref):
    tm = x_ref.shape[1]
    nparts = tm // MERGE_ROWS
    parts = [pl.ds(i * MERGE_ROWS, MERGE_ROWS) for i in range(nparts)]
    wr = wr_ref[...].astype(BF16)
    merged, h2s = {}, {}
    for step in range(nparts + 2):
        if step < nparts:
            h = parts[step]
            up_a = _dot(attn_ref[0, h, :], wua_ref[...])
            up_m = _dot(mo_ref[0, h, :], wum_ref[...])
            merged[step] = (bg_ref[0, h, 0:D_MODEL].astype(F32) * up_a
                            + bg_ref[0, h, D_MODEL:2 * D_MODEL].astype(F32) * up_m).astype(BF16)
        if 0 <= step - 1 < nparts:
            h = parts[step - 1]
            x1 = x_ref[0, h, :] + mod_ref[0, 2:3, :] * _dot(merged.pop(step - 1), wo_ref[...])
            x1_ref[0, h, :] = x1
            h2s[step - 1] = _rms_mod(x1, g2_ref[...], mod_ref[0, 4:5, :], mod_ref[0, 3:4, :])
        if 0 <= step - 2 < nparts:
            h = parts[step - 2]
            h2 = h2s.pop(step - 2)
            logits = _dot(h2.astype(BF16), wr) + br_ref[...]
            ha_ref[0, h, :], hb_ref[0, h, :] = _pack_row_halves(h2)
            route = _route(logits)
            rt_ref[0, h, :] = route[:, 0:ROUTE_W]
            ids_ref[0, :, h] = route.T[0:SUBLANES, :]


def _merge(attn, mo, bg, x, mod, wua, wum, wo, norm_g, w_router, b_router, tm):
    B, S, D = x.shape
    tok = lambda w: pl.BlockSpec((1, tm, w), lambda b, i: (b, i, 0))
    const2 = lambda a: pl.BlockSpec(a.shape, lambda b, i: (0, 0))
    g2 = norm_g.reshape(1, D)
    return pl.pallas_call(
        _merge_kernel,
        grid=(B, S // tm),
        in_specs=[
            tok(A_WIDTH), tok(M_WIDTH), tok(2 * D), tok(D),
            pl.BlockSpec((1, 6, D), lambda b, i: (b, 0, 0)),
            const2(wua), const2(wum), const2(wo), const2(g2), const2(w_router), const2(b_router),
        ],
        out_specs=(tok(D), tok(PACK_W), tok(PACK_W), tok(ROUTE_W),
                   pl.BlockSpec((1, SUBLANES, tm), lambda b, i: (b, 0, i))),
        out_shape=(
            jax.ShapeDtypeStruct((B, S, D), F32),
            jax.ShapeDtypeStruct((B, S, PACK_W), PACKED),
            jax.ShapeDtypeStruct((B, S, PACK_W), PACKED),
            jax.ShapeDtypeStruct((B, S, ROUTE_W), F32),
            jax.ShapeDtypeStruct((B, SUBLANES, S), F32),
        ),
        compiler_params=_compiler_params(("parallel", "parallel")),
        name="merge_route",
    )(attn, mo, bg, x, mod, wua, wum, wo, g2, w_router, b_router)


def _route_tables(ea, tmb):
    n = ea.shape[0]
    T = n // 2
    nblk = 2 * T // tmb + N_EXPERTS
    onehot = ea[None, :] == jnp.arange(N_EXPERTS, dtype=jnp.int32)[:, None]
    pieces = onehot.reshape(N_EXPERTS, n // LANES, LANES).astype(BF16)
    upto = (jnp.arange(LANES)[:, None] <= jnp.arange(LANES)[None, :]).astype(BF16)
    within = jnp.einsum("eps,st->ept", pieces, upto, preferred_element_type=F32).astype(jnp.int32)
    totals = within[:, :, -1]
    before = jnp.cumsum(totals, axis=1) - totals
    csum = (within + before[:, :, None]).reshape(N_EXPERTS, n)
    counts = before[:, -1] + totals[:, -1]
    padded = ((counts + tmb - 1) // tmb) * tmb
    ends = jnp.cumsum(padded)
    starts = ends - padded
    pos = jnp.sum(jnp.where(onehot, csum - 1 + starts[:, None], 0), axis=0).astype(jnp.int32)
    blk0 = jnp.arange(nblk, dtype=jnp.int32) * tmb
    bexp = jnp.minimum(jnp.sum((blk0[:, None] >= ends[None, :]).astype(jnp.int32), axis=1), N_EXPERTS - 1)
    nval = jnp.clip(starts[bexp] + counts[bexp] - blk0, 0, tmb)
    nval = jnp.where(blk0 < ends[-1], nval, 0).astype(jnp.int32)
    return pos, bexp.astype(jnp.int32), nval


def _sc_mesh():
    return plsc.VectorSubcoreMesh(core_axis_name="c", subcore_axis_name="s")


def _sc_dispatch(x, pos, n_rows):
    T = x.shape[0]
    nb = T // SC_WINDOW
    idx = pos.reshape(1, 2 * T)

    @pl.kernel(out_type=jax.ShapeDtypeStruct((n_rows, PACK_W), x.dtype), mesh=_sc_mesh(), scratch_types=[])
    def dispatch(x_hbm, i_hbm, o_hbm):
        def body(x_vmem, i_vmem):
            pltpu.sync_copy(x_vmem, o_hbm.at[i_vmem.at[0]])

        pltpu.emit_pipeline(
            body,
            grid=(2 * nb,),
            in_specs=[pl.BlockSpec((SC_WINDOW, PACK_W), index_map=lambda i: (i % nb, 0)),
                      pl.BlockSpec((1, SC_WINDOW), index_map=lambda i: (0, i))],
            out_specs=[],
            core_axis_name=("c", "s"),
            dimension_semantics=(pltpu.PARALLEL,),
        )(x_hbm, i_hbm)

    return dispatch(x, idx)


def _sc_combine(y, pos):
    n = pos.shape[0]
    idx = pos.reshape(1, n)

    @pl.kernel(out_type=jax.ShapeDtypeStruct((n, PACK_W), y.dtype), mesh=_sc_mesh(), scratch_types=[])
    def combine(y_hbm, i_hbm, o_hbm):
        def body(i_vmem, o_vmem):
            pltpu.sync_copy(y_hbm.at[i_vmem.at[0]], o_vmem)

        pltpu.emit_pipeline(
            body,
            grid=(n // SC_WINDOW,),
            in_specs=[pl.BlockSpec((1, SC_WINDOW), index_map=lambda i: (0, i))],
            out_specs=[pl.BlockSpec((SC_WINDOW, PACK_W), index_map=lambda i: (i, 0))],
            core_axis_name=("c", "s"),
            dimension_semantics=(pltpu.PARALLEL,),
        )(i_hbm, o_hbm)

    return combine(y, idx)


def _experts_kernel(bexp_ref, nval_ref, xa_ref, xb_ref, wg_ref, wu_ref, wd_ref, ya_ref, yb_ref, wg_s, wu_s, wd_s):
    i = pl.program_id(0)
    nv = nval_ref[i]
    new_expert = (i == 0) | (bexp_ref[i] != bexp_ref[jnp.maximum(i - 1, 0)])

    @pl.when((nv > 0) & new_expert)
    def _():
        wg_s[...] = wg_ref[0].astype(BF16)
        wu_s[...] = wu_ref[0].astype(BF16)
        wd_s[...] = wd_ref[0].astype(BF16)

    @pl.when(nv > 0)
    def _():
        keep = lax.broadcasted_iota(jnp.int32, xa_ref.shape, 0) < nv
        pieces = [jnp.where(keep, piece, 0.0).astype(BF16) for piece in _unpack_row_halves(xa_ref[...], xb_ref[...])]
        x = jnp.concatenate(pieces, axis=1)
        half = x.shape[0] // 2
        gus = [(_dot(x[r:r + half], wg_s[...]), _dot(x[r:r + half], wu_s[...])) for r in (0, half)]
        hes = [((g * _sigmoid(g)) * u).astype(BF16) for g, u in gus]
        for r, he in zip((0, half), hes):
            ya_ref[r:r + half, :], yb_ref[r:r + half, :] = _pack_row_halves(_dot(he, wd_s[...]))


def _experts(xa, xb, bexp, nval, wg, wu, wd, tmb):
    n = xa.shape[0]
    row = lambda: pl.BlockSpec((tmb, PACK_W), lambda i, be, nv: (i, 0))
    grid_spec = pltpu.PrefetchScalarGridSpec(
        num_scalar_prefetch=2,
        grid=(n // tmb,),
        in_specs=[row(), row(),
                  pl.BlockSpec((1, D_MODEL, D_EXPERT), lambda i, be, nv: (be[i], 0, 0)),
                  pl.BlockSpec((1, D_MODEL, D_EXPERT), lambda i, be, nv: (be[i], 0, 0)),
                  pl.BlockSpec((1, D_EXPERT, D_MODEL), lambda i, be, nv: (be[i], 0, 0))],
        out_specs=[row(), row()],
        scratch_shapes=[pltpu.VMEM((D_MODEL, D_EXPERT), BF16), pltpu.VMEM((D_MODEL, D_EXPERT), BF16),
                        pltpu.VMEM((D_EXPERT, D_MODEL), BF16)],
    )
    out = jax.ShapeDtypeStruct((n, PACK_W), xa.dtype)
    return pl.pallas_call(
        _experts_kernel, grid_spec=grid_spec, out_shape=(out, out),
        compiler_params=_compiler_params(("arbitrary",)),
        name="experts",
    )(bexp, nval, xa, xb, wg, wu, wd)


def _finish_kernel(a0_ref, b0_ref, a1_ref, b1_ref, rt_ref, x1_ref, mod_ref, fg_ref, o_ref):
    y0 = jnp.concatenate(_unpack_row_halves(a0_ref[...], b0_ref[...]), axis=1)
    y1 = jnp.concatenate(_unpack_row_halves(a1_ref[...], b1_ref[...]), axis=1)
    rt = rt_ref[0]
    moe = rt[:, 2:3] * y0 + rt[:, 3:4] * y1
    xo = x1_ref[0] + mod_ref[0, 5:6, :] * moe
    ms = jnp.mean(xo * xo, axis=-1, keepdims=True)
    o_ref[0] = xo * lax.rsqrt(ms + EPS) * fg_ref[...]


def _finish(ca, cb, rt, x1, mod, final_g, tm):
    B, S, D = x1.shape
    nt = S // tm
    second = B * nt
    half = lambda k: pl.BlockSpec((tm, PACK_W), lambda b, i: (k * second + b * nt + i, 0))
    tok = lambda w: pl.BlockSpec((1, tm, w), lambda b, i: (b, i, 0))
    fg = final_g.reshape(1, D)
    return pl.pallas_call(
        _finish_kernel,
        grid=(B, nt),
        in_specs=[half(0), half(0), half(1), half(1), tok(ROUTE_W), tok(D),
                  pl.BlockSpec((1, 6, D), lambda b, i: (b, 0, 0)),
                  pl.BlockSpec((1, D), lambda b, i: (0, 0))],
        out_specs=tok(D),
        out_shape=jax.ShapeDtypeStruct((B, S, D), F32),
        compiler_params=_compiler_params(("parallel", "parallel")),
        name="finish",
    )(ca, cb, ca, cb, rt, x1, mod, fg)


def _moe(ha, hb, rt, ids, x1, mod, wg, wu, wd, final_g):
    B, S, D = x1.shape
    T = B * S
    tmb = EXPERT_ROWS
    ea = ids[:, 0:2, :].astype(jnp.int32).transpose(1, 0, 2).reshape(2 * T)
    pos, bexp, nval = _route_tables(ea, tmb)
    n_rows = 2 * T + N_EXPERTS * tmb
    xa = _sc_dispatch(ha.reshape(T, PACK_W), pos, n_rows)
    xb = _sc_dispatch(hb.reshape(T, PACK_W), pos, n_rows)
    ya, yb = _experts(xa, xb, bexp, nval, wg, wu, wd, tmb)
    return _finish(_sc_combine(ya, pos), _sc_combine(yb, pos), rt, x1, mod, final_g, tm=min(1024, S))


def _rope_tables(S):
    half = ROPE_DIM // 2
    inv_freq = 1.0 / (ROPE_THETA ** (jnp.arange(half, dtype=F32) * 2.0 / ROPE_DIM))
    ang = jnp.arange(S, dtype=F32)[:, None] * inv_freq[None, :]
    cos, sin = jnp.cos(ang), jnp.sin(ang)
    zeros = jnp.zeros((S, A_HEAD_DIM - ROPE_DIM), F32)
    z8 = jnp.zeros((S, half), F32)
    cos_h = jnp.concatenate([cos, cos, jnp.ones_like(zeros)], axis=-1)
    slo_h = jnp.concatenate([-sin, z8, zeros], axis=-1)
    shi_h = jnp.concatenate([z8, sin, zeros], axis=-1)
    rep = LANES // A_HEAD_DIM
    return tuple(jnp.tile(t, (1, rep)) for t in (cos_h, slo_h, shi_h))


def _layer(x, mod, p, tabs):
    B, S, D = x.shape
    mod = mod.reshape(B, 6, D)
    q, kx, vx, mq, mk, vt, og, bg, grow = _in_proj(
        x, mod, p["norm1_g"], p["w_main"], p["w_t"], p["m_gate_b"], tabs, tm=512)
    attn = _attention(q, kx, vx, p["attn_sink"])
    mo = _mlstm(mq, mk, vt, og, grow, p["conv_wq"], p["conv_wk"], p["head_norm_g_t"])
    x1, ha, hb, rt, ids = _merge(attn, mo, bg, x, mod, p["w_up_attn"], p["w_up_mlstm"], p["w_out"], p["norm2_g"],
                            p["w_router"], p["b_router"], tm=min(1024, S))
    return _moe(ha, hb, rt, ids, x1, mod, p["w_gate"], p["w_up"], p["w_down"], p["final_norm_g"])


def kernel(x_prompt, x_sample, c_prompt, c_sample, ada_w, ada_b, norm1_g, w_in, conv_w, m_gate_b, attn_sink,
           head_norm_g, w_up_attn, w_up_mlstm, w_out, norm2_g, rg_w, rg_b, re_w, re_b, w_gate, w_up, w_down,
           final_norm_g):
    assert ada_w.shape[0] == 1, "single-layer trunk"
    w_in0 = w_in[0]
    w_g = w_in0[:, OFF_MG:OFF_BG]
    pad = LANES - N_EXPERTS - N_GROUPS
    p = dict(
        ada_w=ada_w[0], ada_b=ada_b[0], norm1_g=norm1_g[0],
        w_main=jnp.concatenate([w_in0[:, :OFF_MV], w_in0[:, OFF_MO:OFF_MG], w_in0[:, OFF_BG:]], axis=1).astype(BF16),
        w_t=jnp.concatenate([w_in0[:, OFF_MV:OFF_MO], _gates_head_major(w_g)], axis=1).T.astype(BF16),
        m_gate_b=_gates_head_major(m_gate_b[0]), attn_sink=attn_sink[0],
        conv_wq=conv_w[0, :, :M_WIDTH], conv_wk=conv_w[0, :, M_WIDTH:],
        head_norm_g_t=jnp.broadcast_to(head_norm_g[0][:, None], (M_WIDTH, LANES)),
        w_up_attn=w_up_attn[0].astype(BF16), w_up_mlstm=w_up_mlstm[0].astype(BF16), w_out=w_out[0].astype(BF16),
        norm2_g=norm2_g[0],
        w_router=jnp.pad(jnp.concatenate([re_w[0], rg_w[0]], axis=1), ((0, 0), (0, pad))),
        b_router=jnp.pad(jnp.concatenate([re_b[0], rg_b[0]]), (0, pad)).reshape(1, LANES),
        w_gate=w_gate[0], w_up=w_up[0], w_down=w_down[0],
        final_norm_g=final_norm_g,
    )
    tabs = _rope_tables(x_prompt.shape[1])
    nbp = x_prompt.shape[0]
    mod = _ada_mod(jnp.concatenate([c_prompt, c_sample], axis=0), p["ada_w"], p["ada_b"])
    return (_layer(x_prompt, mod[:nbp], p, tabs), _layer(x_sample, mod[nbp:], p, tabs))
```

```python
import functools

import jax
import jax.numpy as jnp
from jax import lax
from jax.experimental import pallas as pl
from jax.experimental.pallas import tpu as pltpu
from jax.experimental.pallas import tpu_sc as plsc

D_MODEL = 1024
A_HEADS = 8
A_KV_HEADS = 2
A_GROUP = A_HEADS // A_KV_HEADS
A_HEAD_DIM = 64
A_WIDTH = A_HEADS * A_HEAD_DIM
A_KV_WIDTH = A_KV_HEADS * A_HEAD_DIM
WINDOW = 128
BLOCK = 128
ROPE_DIM = A_HEAD_DIM // 4
ROPE_THETA = 500000.0
M_HEADS = 4
M_HEAD_DIM = 128
M_WIDTH = M_HEADS * M_HEAD_DIM
M_CHUNK = 128
CONV_W = 3
OFF_AQ = 0
OFF_AK = OFF_AQ + A_WIDTH
OFF_AV = OFF_AK + A_KV_WIDTH
OFF_MQ = OFF_AV + A_KV_WIDTH
OFF_MK = OFF_MQ + M_WIDTH
OFF_MV = OFF_MK + M_WIDTH
OFF_MO = OFF_MV + M_WIDTH
OFF_MG = OFF_MO + M_WIDTH
N_MGATES = 4 * M_HEADS
OFF_BG = OFF_MG + N_MGATES
IN_TOTAL = OFF_BG + 2 * D_MODEL
N_GROUPS = 4
EXPERTS_PER_GROUP = 4
N_EXPERTS = N_GROUPS * EXPERTS_PER_GROUP
D_EXPERT = 512
EPS = 1e-6
NEG_BIG = -1e30
F32 = jnp.float32
BF16 = jnp.bfloat16

LANES = 128
SUBLANES = 8
VMEM_LIMIT_BYTES = 56 * 1024 * 1024

MAIN_AQ = 0
MAIN_KV = A_WIDTH
MAIN_MQ = MAIN_KV + 2 * A_KV_WIDTH
MAIN_MK = MAIN_MQ + M_WIDTH
MAIN_MO = MAIN_MK + M_WIDTH
MAIN_BG = MAIN_MO + M_WIDTH
MAIN_TOTAL = MAIN_BG + 2 * D_MODEL

LOG2E = 1.4426950408889634
Q_SCALE = A_HEAD_DIM ** -0.5 * LOG2E
ROUTER_G_LANE = N_EXPERTS
ROUTE_W = 4
PACKED = jnp.uint32
PACK_W = D_MODEL // 4
SC_WINDOW = 128
EXPERT_ROWS = 512
MERGE_ROWS = 256
PROJ_ROWS = 512
GATE_ROWS = M_HEADS * SUBLANES


def _sigmoid(z):
    return 1.0 / (1.0 + jnp.exp(-z))


def _log_sigmoid(z):
    return jnp.minimum(z, 0.0) - jnp.log(1.0 + jnp.exp(-jnp.abs(z)))


def _gates_head_major(a):
    lead = a.shape[:-1]
    a = jnp.swapaxes(a.reshape(*lead, 4, M_HEADS), -1, -2)
    a = jnp.pad(a, [(0, 0)] * (a.ndim - 1) + [(0, SUBLANES - 4)])
    return a.reshape(*lead, GATE_ROWS)


def _dot(a, b):
    return jnp.dot(a, b, preferred_element_type=F32)


def _dot_nt(a, b):
    return lax.dot_general(a, b, (((1,), (1,)), ((), ())), preferred_element_type=F32)


def _dot_tn(a, b):
    return lax.dot_general(a, b, (((0,), (0,)), ((), ())), preferred_element_type=F32)


def _dot_f32(a, b):
    return jnp.dot(a, b, preferred_element_type=F32, precision=lax.Precision.HIGHEST)


def _compiler_params(semantics):
    return pltpu.CompilerParams(dimension_semantics=semantics, vmem_limit_bytes=VMEM_LIMIT_BYTES)


def _ada_kernel(c_ref, w_ref, b_ref, o_ref):
    c = c_ref[...]
    o_ref[...] = _dot_f32(c * _sigmoid(c), w_ref[...]) + b_ref[...]


def _ada_mod(c, ada_w, ada_b):
    B, D = c.shape
    n = ada_w.shape[1] // D
    return pl.pallas_call(
        _ada_kernel,
        grid=(n,),
        in_specs=[
            pl.BlockSpec((B, D), lambda j: (0, 0)),
            pl.BlockSpec((D, D), lambda j: (0, j)),
            pl.BlockSpec((1, D), lambda j: (0, j)),
        ],
        out_specs=pl.BlockSpec((B, D), lambda j: (0, j)),
        out_shape=jax.ShapeDtypeStruct((B, n * D), F32),
        compiler_params=_compiler_params(("arbitrary",)),
        name="ada_mod",
    )(c, ada_w, ada_b.reshape(1, -1))


def _rms_mod(x, g, scale, shift):
    ms = jnp.mean(x * x, axis=-1, keepdims=True)
    return (x * lax.rsqrt(ms + EPS) * g) * (1.0 + scale) + shift


def _rope_block(xb, cos, sin_lo, sin_hi):
    half = ROPE_DIM // 2
    return xb * cos + pltpu.roll(xb, LANES - half, axis=1) * sin_lo + pltpu.roll(xb, half, axis=1) * sin_hi


def _in_proj_kernel(x_ref, mod_ref, g_ref, w_ref, wt_ref, gb_col_ref, cos_ref, slo_ref, shi_ref,
                    q_ref, kx_ref, vx_ref, mq_ref, mk_ref, vt_ref, og_ref, bg_ref, grow_ref):
    tm = x_ref.shape[1]
    rows = min(PROJ_ROWS, tm)
    parts = [pl.ds(r, rows) for r in range(0, tm, rows)]
    hbs = [_rms_mod(x_ref[0, p, :], g_ref[...], mod_ref[0, 1:2, :], mod_ref[0, 0:1, :]).astype(BF16) for p in parts]
    left = lax.broadcasted_iota(jnp.int32, (rows, LANES), 1) < A_HEAD_DIM
    for ip, (p, hb) in enumerate(zip(parts, hbs)):
        cos, slo, shi = cos_ref[p, :], slo_ref[p, :], shi_ref[p, :]

        pj = _dot(hb, w_ref[:, MAIN_AQ:MAIN_AQ + A_WIDTH])
        for j in range(A_WIDTH // LANES):
            blk = _rope_block(pj[:, j * LANES:(j + 1) * LANES], cos, slo, shi)
            q_ref[0, p, j * LANES:(j + 1) * LANES] = (blk * Q_SCALE).astype(BF16)

        pj = _dot(hb, w_ref[:, MAIN_KV:MAIN_KV + 2 * A_KV_WIDTH])
        kk = _rope_block(pj[:, 0:LANES], cos, slo, shi)
        vv = pj[:, LANES:2 * LANES]
        for src, dst in ((kk, kx_ref), (vv, vx_ref)):
            swapped = pltpu.roll(src, A_HEAD_DIM, axis=1)
            zero = jnp.zeros_like(src)
            dst[0, p, 0 * LANES:1 * LANES] = jnp.where(left, src, zero).astype(BF16)
            dst[0, p, 1 * LANES:2 * LANES] = jnp.where(left, zero, swapped).astype(BF16)
            dst[0, p, 2 * LANES:3 * LANES] = jnp.where(left, swapped, zero).astype(BF16)
            dst[0, p, 3 * LANES:4 * LANES] = jnp.where(left, zero, src).astype(BF16)

        mq_ref[0, p, :] = _dot(hb, w_ref[:, MAIN_MQ:MAIN_MQ + M_WIDTH]).astype(BF16)
        mk_ref[0, p, :] = _dot(hb, w_ref[:, MAIN_MK:MAIN_MK + M_WIDTH]).astype(BF16)
        og_ref[0, p, :] = _sigmoid(_dot(hb, w_ref[:, MAIN_MO:MAIN_MO + M_WIDTH])).astype(BF16)
        for j in range(2 * D_MODEL // M_WIDTH):
            lo = MAIN_BG + j * M_WIDTH
            bg_ref[0, p, j * M_WIDTH:(j + 1) * M_WIDTH] = _sigmoid(_dot(hb, w_ref[:, lo:lo + M_WIDTH])).astype(BF16)

        tr = _dot_nt(wt_ref[...], hb)
        gr = tr[M_WIDTH:M_WIDTH + GATE_ROWS, :] + gb_col_ref[...]
        gtype = jnp.bitwise_and(lax.broadcasted_iota(jnp.int32, gr.shape, 0), SUBLANES - 1)
        gr = jnp.where(gtype == 1, _log_sigmoid(gr), jnp.where(gtype == 3, _log_sigmoid(gr), gr))
        vt = tr[0:M_WIDTH, :].astype(BF16)
        for c in range(rows // M_CHUNK):
            chunk = ip * (rows // M_CHUNK) + c
            grow_ref[0, chunk] = gr[:, c * M_CHUNK:(c + 1) * M_CHUNK]
            vt_ref[0, chunk] = vt[:, c * M_CHUNK:(c + 1) * M_CHUNK]


def _in_proj(x, mod, norm_g, w_main, w_t, gate_b, rope_tabs, tm):
    B, S, D = x.shape
    nt = S // tm
    cos, slo, shi = rope_tabs
    tok = lambda w: pl.BlockSpec((1, tm, w), lambda b, i: (b, i, 0))
    const2 = lambda a: pl.BlockSpec(a.shape, lambda b, i: (0, 0), pipeline_mode=pl.Buffered(1))
    tab = pl.BlockSpec((tm, LANES), lambda b, i: (i, 0))
    out_shapes = (
        jax.ShapeDtypeStruct((B, S, A_WIDTH), BF16),
        jax.ShapeDtypeStruct((B, S, 4 * LANES), BF16),
        jax.ShapeDtypeStruct((B, S, 4 * LANES), BF16),
        jax.ShapeDtypeStruct((B, S, M_WIDTH), BF16),
        jax.ShapeDtypeStruct((B, S, M_WIDTH), BF16),
        jax.ShapeDtypeStruct((B, S // M_CHUNK, M_WIDTH, M_CHUNK), BF16),
        jax.ShapeDtypeStruct((B, S, M_WIDTH), BF16),
        jax.ShapeDtypeStruct((B, S, 2 * D_MODEL), BF16),
        jax.ShapeDtypeStruct((B, S // M_CHUNK, GATE_ROWS, M_CHUNK), F32),
    )
    chunked = lambda rows: pl.BlockSpec((1, tm // M_CHUNK, rows, M_CHUNK), lambda b, i: (b, i, 0, 0))
    out_specs = (
        tok(A_WIDTH), tok(4 * LANES), tok(4 * LANES), tok(M_WIDTH), tok(M_WIDTH), chunked(M_WIDTH), tok(M_WIDTH),
        tok(2 * D_MODEL), chunked(GATE_ROWS),
    )
    gb_col = gate_b.reshape(GATE_ROWS, 1)
    g2 = norm_g.reshape(1, D)
    return pl.pallas_call(
        _in_proj_kernel,
        grid=(B, nt),
        in_specs=[
            tok(D),
            pl.BlockSpec((1, 6, D), lambda b, i: (b, 0, 0)),
            const2(g2), const2(w_main), const2(w_t), const2(gb_col),
            tab, tab, tab,
        ],
        out_specs=out_specs,
        out_shape=out_shapes,
        compiler_params=_compiler_params(("parallel", "parallel")),
        name="in_proj",
    )(x, mod, g2, w_main, w_t, gb_col, cos, slo, shi)


def _attn_kernel(sink_ref, q_ref, kx_ref, vx_ref, o_ref):
    S = q_ref.shape[1]
    nb = S // BLOCK
    kw = 3 * BLOCK
    qi = lax.broadcasted_iota(jnp.int32, (BLOCK, kw), 0)
    ki = lax.broadcasted_iota(jnp.int32, (BLOCK, kw), 1)
    rel0 = ki - qi
    ones_b = jnp.ones((kw, LANES), BF16)
    left = lax.broadcasted_iota(jnp.int32, (BLOCK, LANES), 1) < A_HEAD_DIM

    def block(n, carry):
        q0 = pl.multiple_of(n * BLOCK, BLOCK)
        k0 = pl.multiple_of(jnp.clip((n - 1) * BLOCK, 0, S - kw), BLOCK)
        rel = rel0 + (k0 - q0)
        valid = jnp.abs(rel) <= WINDOW
        scores = []
        for hk in range(A_KV_HEADS):
            for j in range(A_GROUP // 2):
                col = (hk * (A_GROUP // 2) + j) * LANES
                qp = q_ref[0, pl.ds(q0, BLOCK), col:col + LANES]
                for side in range(2):
                    kk = kx_ref[0, pl.ds(k0, kw), (2 * hk + side) * LANES:(2 * hk + side + 1) * LANES]
                    scores.append(_dot_nt(qp, kk))
        for hk in range(A_KV_HEADS):
            for j in range(A_GROUP // 2):
                col = (hk * (A_GROUP // 2) + j) * LANES
                outs, dens = [], []
                for side in range(2):
                    vv = vx_ref[0, pl.ds(k0, kw), (2 * hk + side) * LANES:(2 * hk + side + 1) * LANES]
                    sk = sink_ref[hk * A_GROUP + 2 * j + side] * LOG2E
                    s = jnp.where(valid, scores[(hk * (A_GROUP // 2) + j) * 2 + side], NEG_BIG)
                    m = jnp.maximum(jnp.max(s, axis=-1, keepdims=True), sk)
                    p = jnp.exp2(s - m).astype(BF16)
                    od = _dot(p, jnp.concatenate([vv, ones_b], axis=1))
                    outs.append(od[:, 0:LANES])
                    dens.append(od[:, LANES:2 * LANES] + jnp.exp2(sk - m))
                o = (outs[0] + outs[1]) / jnp.where(left, dens[0], dens[1])
                o_ref[0, pl.ds(q0, BLOCK), col:col + LANES] = o.astype(BF16)
        return carry

    lax.fori_loop(0, nb, block, 0, unroll=2)


def _attention(q, kx, vx, sink):
    B, S, _ = q.shape
    seq = lambda w: pl.BlockSpec((1, S, w), lambda b: (b, 0, 0))
    return pl.pallas_call(
        _attn_kernel,
        grid=(B,),
        in_specs=[pl.BlockSpec(memory_space=pltpu.SMEM), seq(A_WIDTH), seq(4 * LANES), seq(4 * LANES)],
        out_specs=seq(A_WIDTH),
        out_shape=jax.ShapeDtypeStruct((B, S, A_WIDTH), BF16),
        compiler_params=_compiler_params(("parallel",)),
        name="window_attn",
    )(sink, q, kx, vx)


def _conv_silu(u_ref, w_ref, pad_ref, dst_ref, scale):
    S = u_ref.shape[1]
    pad_ref[0:SUBLANES, :] = jnp.zeros((SUBLANES, LANES), F32)
    pad_ref[S + SUBLANES:S + 2 * SUBLANES, :] = jnp.zeros((SUBLANES, LANES), F32)
    pad_ref[SUBLANES:S + SUBLANES, :] = u_ref[0].astype(F32)
    w0, w1, w2 = w_ref[0:1, :], w_ref[1:2, :], w_ref[2:3, :]
    for c in range(S // M_CHUNK):
        base = SUBLANES + c * M_CHUNK
        y = (pad_ref[base - 1:base - 1 + M_CHUNK, :] * w0 + pad_ref[base:base + M_CHUNK, :] * w1
             + pad_ref[base + 1:base + 1 + M_CHUNK, :] * w2)
        y = y * _sigmoid(y)
        dst_ref[c * M_CHUNK:(c + 1) * M_CHUNK, :] = y if scale == 1.0 else y * scale


def _split3(x):
    hi = x.astype(BF16)
    r = x - hi.astype(F32)
    mid = r.astype(BF16)
    lo = (r - mid.astype(F32)).astype(BF16)
    return [hi, mid, lo]


def _mlstm_kernel(mq_ref, mk_ref, vt_ref, og_ref, grow_ref, cwq_ref, cwk_ref, hgt_ref, o_ref,
                  pad_ref, qs_ref, ks_ref, rb_ref, ib_ref, rows_ref, cinc_ref, ninc_ref, cin_ref, sin_ref,
                  c_ref, n_ref):
    S = mq_ref.shape[1]
    nc = S // M_CHUNK
    L = M_CHUNK
    DH = M_HEAD_DIM
    C_GROUP = 4 if nc % 4 == 0 else 1
    _conv_silu(mq_ref, cwq_ref, pad_ref, qs_ref, 1.0)
    _conv_silu(mk_ref, cwk_ref, pad_ref, ks_ref, M_HEAD_DIM ** -0.5)

    s_i = lax.broadcasted_iota(jnp.int32, (L, L), 0)
    t_i = lax.broadcasted_iota(jnp.int32, (L, L), 1)
    tris = (s_i <= t_i, s_i >= t_i)
    eye = s_i == t_i
    row8 = lax.broadcasted_iota(jnp.int32, (SUBLANES, L), 0)
    one_if = lambda cond: jnp.where(cond, 1.0, 0.0)
    k_j = lax.broadcasted_iota(jnp.int32, (3 * L, 2 * L), 0) % L
    c_j = lax.broadcasted_iota(jnp.int32, (3 * L, 2 * L), 1)
    sum_rows = jnp.where(c_j < L, one_if(k_j <= c_j), one_if(k_j >= c_j - L)).astype(BF16)

    g_all = grow_ref[0].reshape(nc * SUBLANES, L)
    rb_ref[...] = _dot(jnp.concatenate(_split3(g_all), axis=1), sum_rows).reshape(nc, SUBLANES, 2 * L)
    ones_b = jnp.ones((2 * L, L), BF16)

    def phase_a_operands(c):
        gr = grow_ref[0, c]
        rb = rb_ref[c]
        vt = vt_ref[0, c].astype(F32)
        wvs, wks, diags = [], [], []
        for d in range(2):
            brow = rb[2 * d + 1:2 * d + 2, d * L:(d + 1) * L]
            blast = brow[:, L - 1:L] if d == 0 else brow[:, 0:1]
            ibr = gr[2 * d:2 * d + 1, :] - brow
            log_g = blast + ibr
            mg = jnp.max(log_g, axis=-1, keepdims=True)
            wk = jnp.exp(log_g - mg)
            wvs.append((vt * wk).astype(BF16))
            wks.append(wk)
            diags.append(jnp.concatenate(
                [jnp.where(eye, term.astype(F32), 0.0).astype(BF16) for term in _split3(ibr)[:2]], axis=1))
            rows_ref[c, 2 + d:3 + d, :] = brow
            rows_ref[c, 4 + d:5 + d, :] = jnp.broadcast_to(mg, (1, L))
            rows_ref[c, 6 + d:7 + d, :] = jnp.broadcast_to(blast, (1, L))
        wk8 = jnp.where(row8 == 0, wks[0], jnp.where(row8 == 1, wks[1], 0.0))
        return jnp.concatenate(diags, axis=0), jnp.concatenate(wvs, axis=0), wk8.astype(BF16)

    def phase_a_products(c, diag, wv, wk8):
        kb = ks_ref[pl.ds(pl.multiple_of(c * L, L), L), :].astype(BF16)
        ib = _dot(diag, ones_b)
        ib_ref[c] = ib
        for d in range(2):
            rows_ref[c, d:d + 1, :] = jnp.max(jnp.where(tris[d], ib[d * L:(d + 1) * L, :], NEG_BIG),
                                              axis=0, keepdims=True)
        cinc_ref[c] = _dot(wv, kb)
        ninc_ref[c] = _dot(wk8, kb)

    def phase_a(g, carry):
        chunks = [g * C_GROUP + i for i in range(C_GROUP)]
        operands = [phase_a_operands(c) for c in chunks]
        for c, ops in zip(chunks, operands):
            phase_a_products(c, *ops)
        return carry

    def phase_b(j, ms):
        new_ms = []
        for d, cc in ((0, j), (1, nc - 1 - j)):
            m = ms[d]
            half = slice(d * DH, (d + 1) * DH)
            cst = c_ref[half, :]
            n = n_ref[d:d + 1, :]
            cin_ref[cc, half, :] = cst.astype(BF16)
            sin_ref[cc, d:d + 1, :] = n
            sin_ref[cc, 2 + d:3 + d, :] = m
            mg = rows_ref[cc, 4 + d:5 + d, :]
            blast = rows_ref[cc, 6 + d:7 + d, :]
            m_new = jnp.maximum(blast + m, mg)
            decay = jnp.exp(blast + m - m_new)
            grow = jnp.exp(mg - m_new)
            c_ref[half, :] = decay * cst + grow * cinc_ref[cc, half, :]
            n_ref[d:d + 1, :] = decay * n + grow * ninc_ref[cc, d:d + 1, :]
            new_ms.append(m_new)
        return tuple(new_ms)

    def phase_c_products(c):
        r0 = pl.multiple_of(c * L, L)
        qb = qs_ref[pl.ds(r0, L), :].astype(BF16)
        qk_t = _dot_nt(ks_ref[pl.ds(r0, L), :].astype(BF16), qb)
        qc_t = _dot_nt(cin_ref[c], qb)
        qn = _dot_nt(sin_ref[c].astype(BF16), qb)
        return qk_t, qc_t, qn

    def phase_c_finish(c, qk_t, qc_t, qn):
        r0 = pl.multiple_of(c * L, L)
        sin = sin_ref[c]
        rows = rows_ref[c]
        ats, stats = [], []
        for d in range(2):
            m_in = sin[2 + d:3 + d, :]
            cm = jnp.maximum(m_in, rows[d:d + 1, :])
            a_t = qk_t * jnp.where(tris[d], jnp.exp(ib_ref[c, d * L:(d + 1) * L, :] - cm), 0.0)
            w_inter = jnp.exp(m_in - cm)
            den = w_inter * qn[d:d + 1, :] + jnp.sum(a_t, axis=0, keepdims=True)
            m_t = rows[2 + d:3 + d, :] + cm
            ats.append(a_t.astype(BF16))
            stats.append((w_inter, jnp.maximum(jnp.abs(den), jnp.exp(-m_t))))
        av_t = _dot(vt_ref[0, c], jnp.concatenate(ats, axis=1))
        h_t = None
        for d in range(2):
            w_inter, den = stats[d]
            hd = (w_inter * qc_t[d * DH:(d + 1) * DH, :] + av_t[:, d * L:(d + 1) * L]) / den
            h_t = hd if h_t is None else h_t + hd
        y_t = h_t * lax.rsqrt(jnp.mean(h_t * h_t, axis=0, keepdims=True) + EPS) * hgt_ref[...]
        o_ref[0, pl.ds(r0, L), :] = (og_ref[0, pl.ds(r0, L), :].astype(F32) * y_t.T).astype(BF16)

    def phase_c(g, carry):
        chunks = [g * C_GROUP + i for i in range(C_GROUP)]
        products = [phase_c_products(c) for c in chunks]
        for c, prod in zip(chunks, products):
            phase_c_finish(c, *prod)
        return carry

    lax.fori_loop(0, nc // C_GROUP, phase_a, 0, unroll=2)
    c_ref[...] = jnp.zeros(c_ref.shape, F32)
    n_ref[...] = jnp.zeros(n_ref.shape, F32)
    m0 = jnp.zeros((1, LANES), F32)
    lax.fori_loop(0, nc, phase_b, (m0, m0))
    lax.fori_loop(0, nc // C_GROUP, phase_c, 0, unroll=2)


def _mlstm(mq, mk, vt, og, grow_h, conv_wq, conv_wk, head_g_t):
    B, S, _ = mq.shape
    nc = S // M_CHUNK
    head = pl.BlockSpec((1, S, M_HEAD_DIM), lambda b, h: (b, 0, h))
    cw = pl.BlockSpec((CONV_W, M_HEAD_DIM), lambda b, h: (0, h))
    return pl.pallas_call(
        _mlstm_kernel,
        grid=(B, M_HEADS),
        in_specs=[
            head, head,
            pl.BlockSpec((1, nc, M_HEAD_DIM, M_CHUNK), lambda b, h: (b, 0, h, 0)),
            head,
            pl.BlockSpec((1, nc, SUBLANES, M_CHUNK), lambda b, h: (b, 0, h, 0)),
            cw, cw,
            pl.BlockSpec((M_HEAD_DIM, LANES), lambda b, h: (h, 0)),
        ],
        out_specs=head,
        out_shape=jax.ShapeDtypeStruct((B, S, M_WIDTH), BF16),
        scratch_shapes=[
            pltpu.VMEM((S + 2 * SUBLANES, LANES), F32),
            pltpu.VMEM((S, M_HEAD_DIM), F32),
            pltpu.VMEM((S, M_HEAD_DIM), F32),
            pltpu.VMEM((nc, SUBLANES, 2 * M_CHUNK), F32),
            pltpu.VMEM((nc, 2 * M_CHUNK, M_CHUNK), F32),
            pltpu.VMEM((nc, SUBLANES, M_CHUNK), F32),
            pltpu.VMEM((nc, 2 * M_HEAD_DIM, M_HEAD_DIM), F32),
            pltpu.VMEM((nc, SUBLANES, M_HEAD_DIM), F32),
            pltpu.VMEM((nc, 2 * M_HEAD_DIM, M_HEAD_DIM), BF16),
            pltpu.VMEM((nc, SUBLANES, M_HEAD_DIM), F32),
            pltpu.VMEM((2 * M_HEAD_DIM, M_HEAD_DIM), F32),
            pltpu.VMEM((SUBLANES, M_HEAD_DIM), F32),
        ],
        compiler_params=_compiler_params(("parallel", "parallel")),
        name="mlstm",
    )(mq, mk, vt, og, grow_h, conv_wq, conv_wk, head_g_t)


def _route(logits):
    lane = lax.broadcasted_iota(jnp.int32, logits.shape, 1)
    lane_f = lane.astype(F32)
    big = float(LANES)
    is_g = (lane >= ROUTER_G_LANE) & (lane < ROUTER_G_LANE + N_GROUPS)
    gl = jnp.where(is_g, logits, NEG_BIG)
    gmax = jnp.max(gl, axis=-1, keepdims=True)
    gsum = jnp.sum(jnp.where(is_g, jnp.exp(gl - gmax), 0.0), axis=-1, keepdims=True)
    p_grp = 1.0 / gsum
    grp = jnp.min(jnp.where(is_g & (gl == gmax), lane_f - ROUTER_G_LANE, big), axis=-1, keepdims=True)
    in_grp = (lane < N_EXPERTS) & (jnp.right_shift(lane, 2).astype(F32) == grp)
    el = jnp.where(in_grp, logits, NEG_BIG)
    v1 = jnp.max(el, axis=-1, keepdims=True)
    i1 = jnp.min(jnp.where(in_grp & (el == v1), lane_f, big), axis=-1, keepdims=True)
    rest = in_grp & (lane_f != i1)
    el2 = jnp.where(rest, logits, NEG_BIG)
    v2 = jnp.max(el2, axis=-1, keepdims=True)
    i2 = jnp.min(jnp.where(rest & (el2 == v2), lane_f, big), axis=-1, keepdims=True)
    e21 = jnp.exp(v2 - v1)
    w1 = p_grp / (1.0 + e21)
    w2 = p_grp * e21 / (1.0 + e21)
    return jnp.where(lane == 0, i1, jnp.where(lane == 1, i2, jnp.where(lane == 2, w1, w2)))


def _pack_bf16_pairs(lo, hi):
    bits = lambda v: lax.bitcast_convert_type(v.astype(BF16).astype(F32), PACKED)
    return jnp.right_shift(bits(lo), PACKED(16)) | (bits(hi) & PACKED(0xFFFF0000))


def _unpack_bf16_pairs(w):
    return (lax.bitcast_convert_type(jnp.left_shift(w, PACKED(16)), F32),
            lax.bitcast_convert_type(w & PACKED(0xFFFF0000), F32))


def _pack_row_halves(y):
    return (_pack_bf16_pairs(y[:, 0:PACK_W], y[:, PACK_W:2 * PACK_W]),
            _pack_bf16_pairs(y[:, 2 * PACK_W:3 * PACK_W], y[:, 3 * PACK_W:4 * PACK_W]))


def _unpack_row_halves(a, b):
    return [*_unpack_bf16_pairs(a), *_unpack_bf16_pairs(b)]


def _merge_kernel(attn_ref, mo_ref, bg_ref, x_ref, mod_ref, wua_ref, wum_ref, wo_ref, g2_ref, wr_ref, br_ref,
                  x1_ref, ha_ref, hb_ref, rt_ref, ids_ref):
    tm = x_ref.shape[1]
    nparts = tm // MERGE_ROWS
    parts = [pl.ds(i * MERGE_ROWS, MERGE_ROWS) for i in range(nparts)]
    wr = wr_ref[...].astype(BF16)
    merged, h2s = {}, {}
    for step in range(nparts + 2):
        if step < nparts:
            h = parts[step]
            up_a = _dot(attn_ref[0, h, :], wua_ref[...])
            up_m = _dot(mo_ref[0, h, :], wum_ref[...])
            merged[step] = (bg_ref[0, h, 0:D_MODEL].astype(F32) * up_a
                            + bg_ref[0, h, D_MODEL:2 * D_MODEL].astype(F32) * up_m).astype(BF16)
        if 0 <= step - 1 < nparts:
            h = parts[step - 1]
            x1 = x_ref[0, h, :] + mod_ref[0, 2:3, :] * _dot(merged.pop(step - 1), wo_ref[...])
            x1_ref[0, h, :] = x1
            h2s[step - 1] = _rms_mod(x1, g2_ref[...], mod_ref[0, 4:5, :], mod_ref[0, 3:4, :])
        if 0 <= step - 2 < nparts:
            h = parts[step - 2]
            h2 = h2s.pop(step - 2)
            logits = _dot(h2.astype(BF16), wr) + br_ref[...]
            ha_ref[0, h, :], hb_ref[0, h, :] = _pack_row_halves(h2)
            route = _route(logits)
            rt_ref[0, h, :] = route[:, 0:ROUTE_W]
            ids_ref[0, :, h] = route.T[0:SUBLANES, :]


def _merge(attn, mo, bg, x, mod, wua, wum, wo, norm_g, w_router, b_router, tm):
    B, S, D = x.shape
    tok = lambda w: pl.BlockSpec((1, tm, w), lambda b, i: (b, i, 0))
    const2 = lambda a: pl.BlockSpec(a.shape, lambda b, i: (0, 0))
    g2 = norm_g.reshape(1, D)
    return pl.pallas_call(
        _merge_kernel,
        grid=(B, S // tm),
        in_specs=[
            tok(A_WIDTH), tok(M_WIDTH), tok(2 * D), tok(D),
            pl.BlockSpec((1, 6, D), lambda b, i: (b, 0, 0)),
            const2(wua), const2(wum), const2(wo), const2(g2), const2(w_router), const2(b_router),
        ],
        out_specs=(tok(D), tok(PACK_W), tok(PACK_W), tok(ROUTE_W),
                   pl.BlockSpec((1, SUBLANES, tm), lambda b, i: (b, 0, i))),
        out_shape=(
            jax.ShapeDtypeStruct((B, S, D), F32),
            jax.ShapeDtypeStruct((B, S, PACK_W), PACKED),
            jax.ShapeDtypeStruct((B, S, PACK_W), PACKED),
            jax.ShapeDtypeStruct((B, S, ROUTE_W), F32),
            jax.ShapeDtypeStruct((B, SUBLANES, S), F32),
        ),
        compiler_params=_compiler_params(("parallel", "parallel")),
        name="merge_route",
    )(attn, mo, bg, x, mod, wua, wum, wo, g2, w_router, b_router)


def _route_tables(ea, tmb):
    n = ea.shape[0]
    T = n // 2
    nblk = 2 * T // tmb + N_EXPERTS
    onehot = ea[None, :] == jnp.arange(N_EXPERTS, dtype=jnp.int32)[:, None]
    pieces = onehot.reshape(N_EXPERTS, n // LANES, LANES).astype(BF16)
    upto = (jnp.arange(LANES)[:, None] <= jnp.arange(LANES)[None, :]).astype(BF16)
    within = jnp.einsum("eps,st->ept", pieces, upto, preferred_element_type=F32).astype(jnp.int32)
    totals = within[:, :, -1]
    before = jnp.cumsum(totals, axis=1) - totals
    csum = (within + before[:, :, None]).reshape(N_EXPERTS, n)
    counts = before[:, -1] + totals[:, -1]
    padded = ((counts + tmb - 1) // tmb) * tmb
    ends = jnp.cumsum(padded)
    starts = ends - padded
    pos = jnp.sum(jnp.where(onehot, csum - 1 + starts[:, None], 0), axis=0).astype(jnp.int32)
    blk0 = jnp.arange(nblk, dtype=jnp.int32) * tmb
    bexp = jnp.minimum(jnp.sum((blk0[:, None] >= ends[None, :]).astype(jnp.int32), axis=1), N_EXPERTS - 1)
    nval = jnp.clip(starts[bexp] + counts[bexp] - blk0, 0, tmb)
    nval = jnp.where(blk0 < ends[-1], nval, 0).astype(jnp.int32)
    return pos, bexp.astype(jnp.int32), nval


def _sc_mesh():
    return plsc.VectorSubcoreMesh(core_axis_name="c", subcore_axis_name="s")


def _sc_dispatch(x, pos, n_rows):
    T = x.shape[0]
    nb = T // SC_WINDOW
    idx = pos.reshape(1, 2 * T)

    @pl.kernel(out_type=jax.ShapeDtypeStruct((n_rows, PACK_W), x.dtype), mesh=_sc_mesh(), scratch_types=[])
    def dispatch(x_hbm, i_hbm, o_hbm):
        def body(x_vmem, i_vmem):
            pltpu.sync_copy(x_vmem, o_hbm.at[i_vmem.at[0]])

        pltpu.emit_pipeline(
            body,
            grid=(2 * nb,),
            in_specs=[pl.BlockSpec((SC_WINDOW, PACK_W), index_map=lambda i: (i % nb, 0)),
                      pl.BlockSpec((1, SC_WINDOW), index_map=lambda i: (0, i))],
            out_specs=[],
            core_axis_name=("c", "s"),
            dimension_semantics=(pltpu.PARALLEL,),
        )(x_hbm, i_hbm)

    return dispatch(x, idx)


def _sc_combine(y, pos):
    n = pos.shape[0]
    idx = pos.reshape(1, n)

    @pl.kernel(out_type=jax.ShapeDtypeStruct((n, PACK_W), y.dtype), mesh=_sc_mesh(), scratch_types=[])
    def combine(y_hbm, i_hbm, o_hbm):
        def body(i_vmem, o_vmem):
            pltpu.sync_copy(y_hbm.at[i_vmem.at[0]], o_vmem)

        pltpu.emit_pipeline(
            body,
            grid=(n // SC_WINDOW,),
            in_specs=[pl.BlockSpec((1, SC_WINDOW), index_map=lambda i: (0, i))],
            out_specs=[pl.BlockSpec((SC_WINDOW, PACK_W), index_map=lambda i: (i, 0))],
            core_axis_name=("c", "s"),
            dimension_semantics=(pltpu.PARALLEL,),
        )(i_hbm, o_hbm)

    return combine(y, idx)


def _experts_kernel(bexp_ref, nval_ref, xa_ref, xb_ref, wg_ref, wu_ref, wd_ref, ya_ref, yb_ref, wg_s, wu_s, wd_s):
    i = pl.program_id(0)
    nv = nval_ref[i]
    new_expert = (i == 0) | (bexp_ref[i] != bexp_ref[jnp.maximum(i - 1, 0)])

    @pl.when((nv > 0) & new_expert)
    def _():
        wg_s[...] = wg_ref[0].astype(BF16)
        wu_s[...] = wu_ref[0].astype(BF16)
        wd_s[...] = wd_ref[0].astype(BF16)

    @pl.when(nv > 0)
    def _():
        half = xa_ref.shape[0] // 2

        def rows(r):
            keep = r + lax.broadcasted_iota(jnp.int32, (half, PACK_W), 0) < nv
            pieces = [jnp.where(keep, piece, 0.0).astype(BF16)
                      for piece in _unpack_row_halves(xa_ref[r:r + half, :], xb_ref[r:r + half, :])]
            x = jnp.concatenate(pieces, axis=1)
            g = _dot(x, wg_s[...])
            u = _dot(x, wu_s[...])
            he = ((g * _sigmoid(g)) * u).astype(BF16)
            ya_ref[r:r + half, :], yb_ref[r:r + half, :] = _pack_row_halves(_dot(he, wd_s[...]))

        rows(0)

        @pl.when(nv > half)
        def _():
            rows(half)


def _experts(xa, xb, bexp, nval, wg, wu, wd, tmb):
    n = xa.shape[0]
    row = lambda: pl.BlockSpec((tmb, PACK_W), lambda i, be, nv: (i, 0))
    grid_spec = pltpu.PrefetchScalarGridSpec(
        num_scalar_prefetch=2,
        grid=(n // tmb,),
        in_specs=[row(), row(),
                  pl.BlockSpec((1, D_MODEL, D_EXPERT), lambda i, be, nv: (be[i], 0, 0)),
                  pl.BlockSpec((1, D_MODEL, D_EXPERT), lambda i, be, nv: (be[i], 0, 0)),
                  pl.BlockSpec((1, D_EXPERT, D_MODEL), lambda i, be, nv: (be[i], 0, 0))],
        out_specs=[row(), row()],
        scratch_shapes=[pltpu.VMEM((D_MODEL, D_EXPERT), BF16), pltpu.VMEM((D_MODEL, D_EXPERT), BF16),
                        pltpu.VMEM((D_EXPERT, D_MODEL), BF16)],
    )
    out = jax.ShapeDtypeStruct((n, PACK_W), xa.dtype)
    return pl.pallas_call(
        _experts_kernel, grid_spec=grid_spec, out_shape=(out, out),
        compiler_params=_compiler_params(("arbitrary",)),
        name="experts",
    )(bexp, nval, xa, xb, wg, wu, wd)


def _finish_kernel(a0_ref, b0_ref, a1_ref, b1_ref, rt_ref, x1_ref, mod_ref, fg_ref, o_ref):
    y0 = jnp.concatenate(_unpack_row_halves(a0_ref[...], b0_ref[...]), axis=1)
    y1 = jnp.concatenate(_unpack_row_halves(a1_ref[...], b1_ref[...]), axis=1)
    rt = rt_ref[0]
    moe = rt[:, 2:3] * y0 + rt[:, 3:4] * y1
    xo = x1_ref[0] + mod_ref[0, 5:6, :] * moe
    ms = jnp.mean(xo * xo, axis=-1, keepdims=True)
    o_ref[0] = xo * lax.rsqrt(ms + EPS) * fg_ref[...]


def _finish(ca, cb, rt, x1, mod, final_g, tm):
    B, S, D = x1.shape
    nt = S // tm
    second = B * nt
    half = lambda k: pl.BlockSpec((tm, PACK_W), lambda b, i: (k * second + b * nt + i, 0))
    tok = lambda w: pl.BlockSpec((1, tm, w), lambda b, i: (b, i, 0))
    fg = final_g.reshape(1, D)
    return pl.pallas_call(
        _finish_kernel,
        grid=(B, nt),
        in_specs=[half(0), half(0), half(1), half(1), tok(ROUTE_W), tok(D),
                  pl.BlockSpec((1, 6, D), lambda b, i: (b, 0, 0)),
                  pl.BlockSpec((1, D), lambda b, i: (0, 0))],
        out_specs=tok(D),
        out_shape=jax.ShapeDtypeStruct((B, S, D), F32),
        compiler_params=_compiler_params(("parallel", "parallel")),
        name="finish",
    )(ca, cb, ca, cb, rt, x1, mod, fg)


def _moe(ha, hb, rt, ids, x1, mod, wg, wu, wd, final_g):
    B, S, D = x1.shape
    T = B * S
    tmb = EXPERT_ROWS
    ea = ids[:, 0:2, :].astype(jnp.int32).transpose(1, 0, 2).reshape(2 * T)
    pos, bexp, nval = _route_tables(ea, tmb)
    n_rows = 2 * T + N_EXPERTS * tmb
    xa = _sc_dispatch(ha.reshape(T, PACK_W), pos, n_rows)
    xb = _sc_dispatch(hb.reshape(T, PACK_W), pos, n_rows)
    ya, yb = _experts(xa, xb, bexp, nval, wg, wu, wd, tmb)
    return _finish(_sc_combine(ya, pos), _sc_combine(yb, pos), rt, x1, mod, final_g, tm=min(1024, S))


def _rope_tables(S):
    half = ROPE_DIM // 2
    inv_freq = 1.0 / (ROPE_THETA ** (jnp.arange(half, dtype=F32) * 2.0 / ROPE_DIM))
    ang = jnp.arange(S, dtype=F32)[:, None] * inv_freq[None, :]
    cos, sin = jnp.cos(ang), jnp.sin(ang)
    zeros = jnp.zeros((S, A_HEAD_DIM - ROPE_DIM), F32)
    z8 = jnp.zeros((S, half), F32)
    cos_h = jnp.concatenate([cos, cos, jnp.ones_like(zeros)], axis=-1)
    slo_h = jnp.concatenate([-sin, z8, zeros], axis=-1)
    shi_h = jnp.concatenate([z8, sin, zeros], axis=-1)
    rep = LANES // A_HEAD_DIM
    return tuple(jnp.tile(t, (1, rep)) for t in (cos_h, slo_h, shi_h))


def _layer(x, mod, p, tabs):
    B, S, D = x.shape
    mod = mod.reshape(B, 6, D)
    q, kx, vx, mq, mk, vt, og, bg, grow = _in_proj(
        x, mod, p["norm1_g"], p["w_main"], p["w_t"], p["m_gate_b"], tabs, tm=min(1024, S))
    attn = _attention(q, kx, vx, p["attn_sink"])
    mo = _mlstm(mq, mk, vt, og, grow, p["conv_wq"], p["conv_wk"], p["head_norm_g_t"])
    x1, ha, hb, rt, ids = _merge(attn, mo, bg, x, mod, p["w_up_attn"], p["w_up_mlstm"], p["w_out"], p["norm2_g"],
                            p["w_router"], p["b_router"], tm=min(1024, S))
    return _moe(ha, hb, rt, ids, x1, mod, p["w_gate"], p["w_up"], p["w_down"], p["final_norm_g"])


def kernel(x_prompt, x_sample, c_prompt, c_sample, ada_w, ada_b, norm1_g, w_in, conv_w, m_gate_b, attn_sink,
           head_norm_g, w_up_attn, w_up_mlstm, w_out, norm2_g, rg_w, rg_b, re_w, re_b, w_gate, w_up, w_down,
           final_norm_g):
    assert ada_w.shape[0] == 1, "single-layer trunk"
    w_in0 = w_in[0]
    w_g = w_in0[:, OFF_MG:OFF_BG]
    pad = LANES - N_EXPERTS - N_GROUPS
    p = dict(
        ada_w=ada_w[0], ada_b=ada_b[0], norm1_g=norm1_g[0],
        w_main=jnp.concatenate([w_in0[:, :OFF_MV], w_in0[:, OFF_MO:OFF_MG], w_in0[:, OFF_BG:]], axis=1).astype(BF16),
        w_t=jnp.concatenate([w_in0[:, OFF_MV:OFF_MO], _gates_head_major(w_g)], axis=1).T.astype(BF16),
        m_gate_b=_gates_head_major(m_gate_b[0]), attn_sink=attn_sink[0],
        conv_wq=conv_w[0, :, :M_WIDTH], conv_wk=conv_w[0, :, M_WIDTH:],
        head_norm_g_t=jnp.broadcast_to(head_norm_g[0][:, None], (M_WIDTH, LANES)),
        w_up_attn=w_up_attn[0].astype(BF16), w_up_mlstm=w_up_mlstm[0].astype(BF16), w_out=w_out[0].astype(BF16),
        norm2_g=norm2_g[0],
        w_router=jnp.pad(jnp.concatenate([re_w[0], rg_w[0]], axis=1), ((0, 0), (0, pad))),
        b_router=jnp.pad(jnp.concatenate([re_b[0], rg_b[0]]), (0, pad)).reshape(1, LANES),
        w_gate=w_gate[0], w_up=w_up[0], w_down=w_down[0],
        final_norm_g=final_norm_g,
    )
    tabs = _rope_tables(x_prompt.shape[1])
    nbp = x_prompt.shape[0]
    mod = _ada_mod(jnp.concatenate([c_prompt, c_sample], axis=0), p["ada_w"], p["ada_b"])
    return (_layer(x_prompt, mod[:nbp], p, tabs), _layer(x_sample, mod[nbp:], p, tabs))
```

```python
import jax
import jax.numpy as jnp
from jax import lax
from jax.experimental import pallas as pl
from jax.experimental.pallas import tpu as pltpu
from jax.experimental.pallas import tpu_sc as plsc

D_MODEL = 1024
A_HEADS = 8
A_KV_HEADS = 2
A_GROUP = A_HEADS // A_KV_HEADS
A_HEAD_DIM = 64
A_WIDTH = A_HEADS * A_HEAD_DIM
A_KV_WIDTH = A_KV_HEADS * A_HEAD_DIM
WINDOW = 128
BLOCK = 128
ROPE_DIM = A_HEAD_DIM // 4
ROPE_THETA = 500000.0
M_HEADS = 4
M_HEAD_DIM = 128
M_WIDTH = M_HEADS * M_HEAD_DIM
M_CHUNK = 128
CONV_W = 3
OFF_AQ = 0
OFF_AK = OFF_AQ + A_WIDTH
OFF_AV = OFF_AK + A_KV_WIDTH
OFF_MQ = OFF_AV + A_KV_WIDTH
OFF_MK = OFF_MQ + M_WIDTH
OFF_MV = OFF_MK + M_WIDTH
OFF_MO = OFF_MV + M_WIDTH
OFF_MG = OFF_MO + M_WIDTH
N_MGATES = 4 * M_HEADS
OFF_BG = OFF_MG + N_MGATES
IN_TOTAL = OFF_BG + 2 * D_MODEL
N_GROUPS = 4
EXPERTS_PER_GROUP = 4
N_EXPERTS = N_GROUPS * EXPERTS_PER_GROUP
D_EXPERT = 512
EPS = 1e-6
NEG_BIG = -1e30
F32 = jnp.float32
BF16 = jnp.bfloat16

LANES = 128
SUBLANES = 8
VMEM_LIMIT_BYTES = 56 * 1024 * 1024

MAIN_AQ = 0
MAIN_KV = A_WIDTH
MAIN_MQ = MAIN_KV + 2 * A_KV_WIDTH
MAIN_MK = MAIN_MQ + M_WIDTH
MAIN_MO = MAIN_MK + M_WIDTH
MAIN_BG = MAIN_MO + M_WIDTH
MAIN_TOTAL = MAIN_BG + 2 * D_MODEL

LOG2E = 1.4426950408889634
Q_SCALE = A_HEAD_DIM ** -0.5 * LOG2E
ROUTER_G_LANE = N_EXPERTS
ROUTE_W = 4
PACKED = jnp.uint32
PACK_W = D_MODEL // 4
SC_WINDOW = 128
EXPERT_ROWS = 512
IN_PROJ_TILE = 512
MERGE_TILE = 1024
MERGE_ROWS = 256
FINISH_TILE = 1024
GATE_ROWS = M_HEADS * SUBLANES


def _sigmoid(z):
    return 1.0 / (1.0 + jnp.exp(-z))


def _log_sigmoid(z):
    return jnp.minimum(z, 0.0) - jnp.log(1.0 + jnp.exp(-jnp.abs(z)))


def _gates_head_major(a):
    lead = a.shape[:-1]
    a = jnp.swapaxes(a.reshape(*lead, 4, M_HEADS), -1, -2)
    a = jnp.pad(a, [(0, 0)] * (a.ndim - 1) + [(0, SUBLANES - 4)])
    return a.reshape(*lead, GATE_ROWS)


def _dot(a, b):
    return jnp.dot(a, b, preferred_element_type=F32)


def _dot_nt(a, b):
    return lax.dot_general(a, b, (((1,), (1,)), ((), ())), preferred_element_type=F32)


def _dot_f32(a, b):
    return jnp.dot(a, b, preferred_element_type=F32, precision=lax.Precision.HIGHEST)


def _compiler_params(semantics):
    return pltpu.CompilerParams(dimension_semantics=semantics, vmem_limit_bytes=VMEM_LIMIT_BYTES)


def _ada_kernel(c_ref, w_ref, b_ref, o_ref):
    c = c_ref[...]
    o_ref[...] = _dot_f32(c * _sigmoid(c), w_ref[...]) + b_ref[...]


def _ada_mod(c, ada_w, ada_b):
    B, D = c.shape
    n = ada_w.shape[1] // D
    return pl.pallas_call(
        _ada_kernel,
        grid=(n,),
        in_specs=[
            pl.BlockSpec((B, D), lambda j: (0, 0)),
            pl.BlockSpec((D, D), lambda j: (0, j)),
            pl.BlockSpec((1, D), lambda j: (0, j)),
        ],
        out_specs=pl.BlockSpec((B, D), lambda j: (0, j)),
        out_shape=jax.ShapeDtypeStruct((B, n * D), F32),
        compiler_params=_compiler_params(("arbitrary",)),
        name="ada_mod",
    )(c, ada_w, ada_b.reshape(1, -1))


def _rms_mod(x, g, scale, shift):
    ms = jnp.mean(x * x, axis=-1, keepdims=True)
    return (x * lax.rsqrt(ms + EPS) * g) * (1.0 + scale) + shift


def _rope_block(xb, cos, sin_lo, sin_hi):
    half = ROPE_DIM // 2
    return xb * cos + pltpu.roll(xb, LANES - half, axis=1) * sin_lo + pltpu.roll(xb, half, axis=1) * sin_hi


def _in_proj_kernel(x_ref, mod_ref, g_ref, w_ref, wt_ref, gb_col_ref, cos_ref, slo_ref, shi_ref,
                    q_ref, kx_ref, vx_ref, mq_ref, mk_ref, vt_ref, og_ref, bg_ref, grow_ref):
    x = x_ref[0]
    tm = x.shape[0]
    h = _rms_mod(x, g_ref[...], mod_ref[0, 1:2, :], mod_ref[0, 0:1, :])
    hb = h.astype(BF16)
    cos, slo, shi = cos_ref[...], slo_ref[...], shi_ref[...]
    lane = lax.broadcasted_iota(jnp.int32, (tm, LANES), 1)
    left = lane < A_HEAD_DIM

    pj = _dot(hb, w_ref[:, MAIN_AQ:MAIN_AQ + A_WIDTH])
    for j in range(A_WIDTH // LANES):
        blk = _rope_block(pj[:, j * LANES:(j + 1) * LANES], cos, slo, shi)
        q_ref[0, :, j * LANES:(j + 1) * LANES] = (blk * Q_SCALE).astype(BF16)

    pj = _dot(hb, w_ref[:, MAIN_KV:MAIN_KV + 2 * A_KV_WIDTH])
    kk = _rope_block(pj[:, 0:LANES], cos, slo, shi)
    vv = pj[:, LANES:2 * LANES]
    for src, dst in ((kk, kx_ref), (vv, vx_ref)):
        swapped = pltpu.roll(src, A_HEAD_DIM, axis=1)
        zero = jnp.zeros_like(src)
        dst[0, :, 0 * LANES:1 * LANES] = jnp.where(left, src, zero).astype(BF16)
        dst[0, :, 1 * LANES:2 * LANES] = jnp.where(left, zero, swapped).astype(BF16)
        dst[0, :, 2 * LANES:3 * LANES] = jnp.where(left, swapped, zero).astype(BF16)
        dst[0, :, 3 * LANES:4 * LANES] = jnp.where(left, zero, src).astype(BF16)

    mq_ref[0] = _dot(hb, w_ref[:, MAIN_MQ:MAIN_MQ + M_WIDTH]).astype(BF16)
    mk_ref[0] = _dot(hb, w_ref[:, MAIN_MK:MAIN_MK + M_WIDTH]).astype(BF16)
    og_ref[0] = _sigmoid(_dot(hb, w_ref[:, MAIN_MO:MAIN_MO + M_WIDTH])).astype(BF16)
    for j in range(2 * D_MODEL // M_WIDTH):
        lo = MAIN_BG + j * M_WIDTH
        bg_ref[0, :, j * M_WIDTH:(j + 1) * M_WIDTH] = _sigmoid(_dot(hb, w_ref[:, lo:lo + M_WIDTH])).astype(BF16)

    tr = _dot_nt(wt_ref[...], hb)
    gr = tr[M_WIDTH:M_WIDTH + GATE_ROWS, :] + gb_col_ref[...]
    gtype = jnp.bitwise_and(lax.broadcasted_iota(jnp.int32, gr.shape, 0), SUBLANES - 1)
    gr = jnp.where(gtype == 1, _log_sigmoid(gr), jnp.where(gtype == 3, _log_sigmoid(gr), gr))
    vt = tr[0:M_WIDTH, :].astype(BF16)
    for c in range(tm // M_CHUNK):
        grow_ref[0, c] = gr[:, c * M_CHUNK:(c + 1) * M_CHUNK]
        vt_ref[0, c] = vt[:, c * M_CHUNK:(c + 1) * M_CHUNK]


def _in_proj(x, mod, norm_g, w_main, w_t, gate_b, rope_tabs, tm):
    B, S, D = x.shape
    nt = S // tm
    cos, slo, shi = rope_tabs
    tok = lambda w: pl.BlockSpec((1, tm, w), lambda b, i: (b, i, 0))
    const2 = lambda a: pl.BlockSpec(a.shape, lambda b, i: (0, 0))
    tab = pl.BlockSpec((tm, LANES), lambda b, i: (i, 0))
    out_shapes = (
        jax.ShapeDtypeStruct((B, S, A_WIDTH), BF16),
        jax.ShapeDtypeStruct((B, S, 4 * LANES), BF16),
        jax.ShapeDtypeStruct((B, S, 4 * LANES), BF16),
        jax.ShapeDtypeStruct((B, S, M_WIDTH), BF16),
        jax.ShapeDtypeStruct((B, S, M_WIDTH), BF16),
        jax.ShapeDtypeStruct((B, S // M_CHUNK, M_WIDTH, M_CHUNK), BF16),
        jax.ShapeDtypeStruct((B, S, M_WIDTH), BF16),
        jax.ShapeDtypeStruct((B, S, 2 * D_MODEL), BF16),
        jax.ShapeDtypeStruct((B, S // M_CHUNK, GATE_ROWS, M_CHUNK), F32),
    )
    chunked = lambda rows: pl.BlockSpec((1, tm // M_CHUNK, rows, M_CHUNK), lambda b, i: (b, i, 0, 0))
    out_specs = (
        tok(A_WIDTH), tok(4 * LANES), tok(4 * LANES), tok(M_WIDTH), tok(M_WIDTH), chunked(M_WIDTH), tok(M_WIDTH),
        tok(2 * D_MODEL), chunked(GATE_ROWS),
    )
    gb_col = gate_b.reshape(GATE_ROWS, 1)
    g2 = norm_g.reshape(1, D)
    return pl.pallas_call(
        _in_proj_kernel,
        grid=(B, nt),
        in_specs=[
            tok(D),
            pl.BlockSpec((1, 6, D), lambda b, i: (b, 0, 0)),
            const2(g2), const2(w_main), const2(w_t), const2(gb_col),
            tab, tab, tab,
        ],
        out_specs=out_specs,
        out_shape=out_shapes,
        compiler_params=_compiler_params(("parallel", "parallel")),
        name="in_proj",
    )(x, mod, g2, w_main, w_t, gb_col, cos, slo, shi)


def _attn_kernel(sink_ref, q_ref, kx_ref, vx_ref, o_ref):
    S = q_ref.shape[1]
    nb = S // BLOCK
    kw = 3 * BLOCK
    qi = lax.broadcasted_iota(jnp.int32, (BLOCK, kw), 0)
    ki = lax.broadcasted_iota(jnp.int32, (BLOCK, kw), 1)
    rel0 = ki - qi
    ones_b = jnp.ones((kw, LANES), BF16)
    left = lax.broadcasted_iota(jnp.int32, (BLOCK, LANES), 1) < A_HEAD_DIM

    def block(n, carry):
        q0 = pl.multiple_of(n * BLOCK, BLOCK)
        k0 = pl.multiple_of(jnp.clip((n - 1) * BLOCK, 0, S - kw), BLOCK)
        rel = rel0 + (k0 - q0)
        valid = jnp.abs(rel) <= WINDOW
        scores = []
        for hk in range(A_KV_HEADS):
            for j in range(A_GROUP // 2):
                col = (hk * (A_GROUP // 2) + j) * LANES
                qp = q_ref[0, pl.ds(q0, BLOCK), col:col + LANES]
                for side in range(2):
                    kk = kx_ref[0, pl.ds(k0, kw), (2 * hk + side) * LANES:(2 * hk + side + 1) * LANES]
                    scores.append(_dot_nt(qp, kk))
        for hk in range(A_KV_HEADS):
            for j in range(A_GROUP // 2):
                col = (hk * (A_GROUP // 2) + j) * LANES
                outs, dens = [], []
                for side in range(2):
                    vv = vx_ref[0, pl.ds(k0, kw), (2 * hk + side) * LANES:(2 * hk + side + 1) * LANES]
                    sk = sink_ref[hk * A_GROUP + 2 * j + side] * LOG2E
                    s = jnp.where(valid, scores[(hk * (A_GROUP // 2) + j) * 2 + side], NEG_BIG)
                    m = jnp.maximum(jnp.max(s, axis=-1, keepdims=True), sk)
                    p = jnp.exp2(s - m).astype(BF16)
                    od = _dot(p, jnp.concatenate([vv, ones_b], axis=1))
                    outs.append(od[:, 0:LANES])
                    dens.append(od[:, LANES:2 * LANES] + jnp.exp2(sk - m))
                o = (outs[0] + outs[1]) / jnp.where(left, dens[0], dens[1])
                o_ref[0, pl.ds(q0, BLOCK), col:col + LANES] = o.astype(BF16)
        return carry

    lax.fori_loop(0, nb, block, 0, unroll=4)


def _attention(q, kx, vx, sink):
    B, S, _ = q.shape
    seq = lambda w: pl.BlockSpec((1, S, w), lambda b: (b, 0, 0))
    return pl.pallas_call(
        _attn_kernel,
        grid=(B,),
        in_specs=[pl.BlockSpec(memory_space=pltpu.SMEM), seq(A_WIDTH), seq(4 * LANES), seq(4 * LANES)],
        out_specs=seq(A_WIDTH),
        out_shape=jax.ShapeDtypeStruct((B, S, A_WIDTH), BF16),
        compiler_params=_compiler_params(("parallel",)),
        name="window_attn",
    )(sink, q, kx, vx)


def _conv_silu(u_ref, w_ref, pad_ref, dst_ref, scale):
    S = u_ref.shape[1]
    pad_ref[0:SUBLANES, :] = jnp.zeros((SUBLANES, LANES), F32)
    pad_ref[S + SUBLANES:S + 2 * SUBLANES, :] = jnp.zeros((SUBLANES, LANES), F32)
    pad_ref[SUBLANES:S + SUBLANES, :] = u_ref[0].astype(F32)
    w0, w1, w2 = w_ref[0:1, :], w_ref[1:2, :], w_ref[2:3, :]
    for c in range(S // M_CHUNK):
        base = SUBLANES + c * M_CHUNK
        y = (pad_ref[base - 1:base - 1 + M_CHUNK, :] * w0 + pad_ref[base:base + M_CHUNK, :] * w1
             + pad_ref[base + 1:base + 1 + M_CHUNK, :] * w2)
        y = y * _sigmoid(y)
        dst_ref[c * M_CHUNK:(c + 1) * M_CHUNK, :] = y if scale == 1.0 else y * scale


def _split3(x):
    hi = x.astype(BF16)
    r = x - hi.astype(F32)
    mid = r.astype(BF16)
    lo = (r - mid.astype(F32)).astype(BF16)
    return [hi, mid, lo]


def _mlstm_kernel(mq_ref, mk_ref, vt_ref, og_ref, grow_ref, cwq_ref, cwk_ref, hgt_ref, o_ref,
                  pad_ref, qs_ref, ks_ref, rb_ref, ib_ref, rows_ref, cinc_ref, ninc_ref, cin_ref, sin_ref,
                  c_ref, n_ref):
    S = mq_ref.shape[1]
    nc = S // M_CHUNK
    L = M_CHUNK
    DH = M_HEAD_DIM
    C_GROUP = 8 if nc % 8 == 0 else 1
    _conv_silu(mq_ref, cwq_ref, pad_ref, qs_ref, 1.0)
    _conv_silu(mk_ref, cwk_ref, pad_ref, ks_ref, M_HEAD_DIM ** -0.5)

    s_i = lax.broadcasted_iota(jnp.int32, (L, L), 0)
    t_i = lax.broadcasted_iota(jnp.int32, (L, L), 1)
    tris = (s_i <= t_i, s_i >= t_i)
    eye = s_i == t_i
    row8 = lax.broadcasted_iota(jnp.int32, (SUBLANES, L), 0)
    one_if = lambda cond: jnp.where(cond, 1.0, 0.0)
    k_j = lax.broadcasted_iota(jnp.int32, (3 * L, 2 * L), 0) % L
    c_j = lax.broadcasted_iota(jnp.int32, (3 * L, 2 * L), 1)
    sum_rows = jnp.where(c_j < L, one_if(k_j <= c_j), one_if(k_j >= c_j - L)).astype(BF16)

    g_all = grow_ref[0].reshape(nc * SUBLANES, L)
    rb_ref[...] = _dot(jnp.concatenate(_split3(g_all), axis=1), sum_rows).reshape(nc, SUBLANES, 2 * L)
    ones_b = jnp.ones((2 * L, L), BF16)

    def phase_a_operands(c):
        gr = grow_ref[0, c]
        rb = rb_ref[c]
        vt = vt_ref[0, c].astype(F32)
        wvs, wks, diags = [], [], []
        for d in range(2):
            brow = rb[2 * d + 1:2 * d + 2, d * L:(d + 1) * L]
            blast = brow[:, L - 1:L] if d == 0 else brow[:, 0:1]
            ibr = gr[2 * d:2 * d + 1, :] - brow
            log_g = blast + ibr
            mg = jnp.max(log_g, axis=-1, keepdims=True)
            wk = jnp.exp(log_g - mg)
            wvs.append((vt * wk).astype(BF16))
            wks.append(wk)
            diags.append(jnp.concatenate(
                [jnp.where(eye, term.astype(F32), 0.0).astype(BF16) for term in _split3(ibr)[:2]], axis=1))
            rows_ref[c, 2 + d:3 + d, :] = brow
            rows_ref[c, 4 + d:5 + d, :] = jnp.broadcast_to(mg, (1, L))
            rows_ref[c, 6 + d:7 + d, :] = jnp.broadcast_to(blast, (1, L))
        wk8 = jnp.where(row8 == 0, wks[0], jnp.where(row8 == 1, wks[1], 0.0))
        return jnp.concatenate(diags, axis=0), jnp.concatenate(wvs, axis=0), wk8.astype(BF16)

    def phase_a_products(c, diag, wv, wk8):
        kb = ks_ref[pl.ds(pl.multiple_of(c * L, L), L), :].astype(BF16)
        ib = _dot(diag, ones_b)
        ib_ref[c] = ib
        for d in range(2):
            rows_ref[c, d:d + 1, :] = jnp.max(jnp.where(tris[d], ib[d * L:(d + 1) * L, :], NEG_BIG),
                                              axis=0, keepdims=True)
        cinc_ref[c] = _dot(wv, kb)
        ninc_ref[c] = _dot(wk8, kb)

    def phase_a(g, carry):
        chunks = [g * C_GROUP + i for i in range(C_GROUP)]
        operands = [phase_a_operands(c) for c in chunks]
        for c, ops in zip(chunks, operands):
            phase_a_products(c, *ops)
        return carry

    def phase_b(j, ms):
        new_ms = []
        for d, cc in ((0, j), (1, nc - 1 - j)):
            m = ms[d]
            half = slice(d * DH, (d + 1) * DH)
            cst = c_ref[half, :]
            n = n_ref[d:d + 1, :]
            cin_ref[cc, half, :] = cst.astype(BF16)
            sin_ref[cc, d:d + 1, :] = n
            sin_ref[cc, 2 + d:3 + d, :] = m
            mg = rows_ref[cc, 4 + d:5 + d, :]
            blast = rows_ref[cc, 6 + d:7 + d, :]
            m_new = jnp.maximum(blast + m, mg)
            decay = jnp.exp(blast + m - m_new)
            grow = jnp.exp(mg - m_new)
            c_ref[half, :] = decay * cst + grow * cinc_ref[cc, half, :]
            n_ref[d:d + 1, :] = decay * n + grow * ninc_ref[cc, d:d + 1, :]
            new_ms.append(m_new)
        return tuple(new_ms)

    def phase_c_products(c):
        r0 = pl.multiple_of(c * L, L)
        qb = qs_ref[pl.ds(r0, L), :].astype(BF16)
        qk_t = _dot_nt(ks_ref[pl.ds(r0, L), :].astype(BF16), qb)
        qc_t = _dot_nt(cin_ref[c], qb)
        qn = _dot_nt(sin_ref[c].astype(BF16), qb)
        return qk_t, qc_t, qn

    def phase_c_finish(c, qk_t, qc_t, qn):
        r0 = pl.multiple_of(c * L, L)
        sin = sin_ref[c]
        rows = rows_ref[c]
        ats, stats = [], []
        for d in range(2):
            m_in = sin[2 + d:3 + d, :]
            cm = jnp.maximum(m_in, rows[d:d + 1, :])
            a_t = qk_t * jnp.where(tris[d], jnp.exp(ib_ref[c, d * L:(d + 1) * L, :] - cm), 0.0)
            w_inter = jnp.exp(m_in - cm)
            den = w_inter * qn[d:d + 1, :] + jnp.sum(a_t, axis=0, keepdims=True)
            m_t = rows[2 + d:3 + d, :] + cm
            ats.append(a_t.astype(BF16))
            stats.append((w_inter, jnp.maximum(jnp.abs(den), jnp.exp(-m_t))))
        av_t = _dot(vt_ref[0, c], jnp.concatenate(ats, axis=1))
        h_t = None
        for d in range(2):
            w_inter, den = stats[d]
            hd = (w_inter * qc_t[d * DH:(d + 1) * DH, :] + av_t[:, d * L:(d + 1) * L]) / den
            h_t = hd if h_t is None else h_t + hd
        y_t = h_t * lax.rsqrt(jnp.mean(h_t * h_t, axis=0, keepdims=True) + EPS) * hgt_ref[...]
        o_ref[0, pl.ds(r0, L), :] = (og_ref[0, pl.ds(r0, L), :].astype(F32) * y_t.T).astype(BF16)

    def phase_c(g, carry):
        chunks = [g * C_GROUP + i for i in range(C_GROUP)]
        products = [phase_c_products(c) for c in chunks]
        for c, prod in zip(chunks, products):
            phase_c_finish(c, *prod)
        return carry

    lax.fori_loop(0, nc // C_GROUP, phase_a, 0, unroll=2)
    c_ref[...] = jnp.zeros(c_ref.shape, F32)
    n_ref[...] = jnp.zeros(n_ref.shape, F32)
    m0 = jnp.zeros((1, LANES), F32)
    lax.fori_loop(0, nc, phase_b, (m0, m0))
    lax.fori_loop(0, nc // C_GROUP, phase_c, 0, unroll=2)


def _mlstm(mq, mk, vt, og, grow_h, conv_wq, conv_wk, head_g_t):
    B, S, _ = mq.shape
    nc = S // M_CHUNK
    head = pl.BlockSpec((1, S, M_HEAD_DIM), lambda b, h: (b, 0, h))
    cw = pl.BlockSpec((CONV_W, M_HEAD_DIM), lambda b, h: (0, h))
    return pl.pallas_call(
        _mlstm_kernel,
        grid=(B, M_HEADS),
        in_specs=[
            head, head,
            pl.BlockSpec((1, nc, M_HEAD_DIM, M_CHUNK), lambda b, h: (b, 0, h, 0)),
            head,
            pl.BlockSpec((1, nc, SUBLANES, M_CHUNK), lambda b, h: (b, 0, h, 0)),
            cw, cw,
            pl.BlockSpec((M_HEAD_DIM, LANES), lambda b, h: (h, 0)),
        ],
        out_specs=head,
        out_shape=jax.ShapeDtypeStruct((B, S, M_WIDTH), BF16),
        scratch_shapes=[
            pltpu.VMEM((S + 2 * SUBLANES, LANES), F32),
            pltpu.VMEM((S, M_HEAD_DIM), F32),
            pltpu.VMEM((S, M_HEAD_DIM), F32),
            pltpu.VMEM((nc, SUBLANES, 2 * M_CHUNK), F32),
            pltpu.VMEM((nc, 2 * M_CHUNK, M_CHUNK), F32),
            pltpu.VMEM((nc, SUBLANES, M_CHUNK), F32),
            pltpu.VMEM((nc, 2 * M_HEAD_DIM, M_HEAD_DIM), F32),
            pltpu.VMEM((nc, SUBLANES, M_HEAD_DIM), F32),
            pltpu.VMEM((nc, 2 * M_HEAD_DIM, M_HEAD_DIM), BF16),
            pltpu.VMEM((nc, SUBLANES, M_HEAD_DIM), F32),
            pltpu.VMEM((2 * M_HEAD_DIM, M_HEAD_DIM), F32),
            pltpu.VMEM((SUBLANES, M_HEAD_DIM), F32),
        ],
        compiler_params=_compiler_params(("parallel", "parallel")),
        name="mlstm",
    )(mq, mk, vt, og, grow_h, conv_wq, conv_wk, head_g_t)


def _route(logits):
    lane = lax.broadcasted_iota(jnp.int32, logits.shape, 1)
    lane_f = lane.astype(F32)
    big = float(LANES)
    is_g = (lane >= ROUTER_G_LANE) & (lane < ROUTER_G_LANE + N_GROUPS)
    gl = jnp.where(is_g, logits, NEG_BIG)
    gmax = jnp.max(gl, axis=-1, keepdims=True)
    gsum = jnp.sum(jnp.where(is_g, jnp.exp(gl - gmax), 0.0), axis=-1, keepdims=True)
    p_grp = 1.0 / gsum
    grp = jnp.min(jnp.where(is_g & (gl == gmax), lane_f - ROUTER_G_LANE, big), axis=-1, keepdims=True)
    in_grp = (lane < N_EXPERTS) & (jnp.right_shift(lane, 2).astype(F32) == grp)
    el = jnp.where(in_grp, logits, NEG_BIG)
    v1 = jnp.max(el, axis=-1, keepdims=True)
    i1 = jnp.min(jnp.where(in_grp & (el == v1), lane_f, big), axis=-1, keepdims=True)
    rest = in_grp & (lane_f != i1)
    el2 = jnp.where(rest, logits, NEG_BIG)
    v2 = jnp.max(el2, axis=-1, keepdims=True)
    i2 = jnp.min(jnp.where(rest & (el2 == v2), lane_f, big), axis=-1, keepdims=True)
    e21 = jnp.exp(v2 - v1)
    w1 = p_grp / (1.0 + e21)
    w2 = p_grp * e21 / (1.0 + e21)
    return jnp.where(lane == 0, i1, jnp.where(lane == 1, i2, jnp.where(lane == 2, w1, w2)))


def _pack_bf16_pairs(lo, hi):
    bits = lambda v: lax.bitcast_convert_type(v.astype(BF16).astype(F32), PACKED)
    return jnp.right_shift(bits(lo), PACKED(16)) | (bits(hi) & PACKED(0xFFFF0000))


def _unpack_bf16_pairs(w):
    return (lax.bitcast_convert_type(jnp.left_shift(w, PACKED(16)), F32),
            lax.bitcast_convert_type(w & PACKED(0xFFFF0000), F32))


def _pack_row_halves(y):
    return (_pack_bf16_pairs(y[:, 0:PACK_W], y[:, PACK_W:2 * PACK_W]),
            _pack_bf16_pairs(y[:, 2 * PACK_W:3 * PACK_W], y[:, 3 * PACK_W:4 * PACK_W]))


def _unpack_row_halves(a, b):
    return [*_unpack_bf16_pairs(a), *_unpack_bf16_pairs(b)]


def _merge_kernel(attn_ref, mo_ref, bg_ref, x_ref, mod_ref, wua_ref, wum_ref, wo_ref, g2_ref, wr_ref, br_ref,
                  x1_ref, ha_ref, hb_ref, rt_ref, ids_ref):
    tm = x_ref.shape[1]
    nparts = tm // MERGE_ROWS
    parts = [pl.ds(i * MERGE_ROWS, MERGE_ROWS) for i in range(nparts)]
    wr = wr_ref[...].astype(BF16)
    merged, h2s = {}, {}
    for step in range(nparts + 2):
        if step < nparts:
            h = parts[step]
            up_a = _dot(attn_ref[0, h, :], wua_ref[...])
            up_m = _dot(mo_ref[0, h, :], wum_ref[...])
            merged[step] = (bg_ref[0, h, 0:D_MODEL].astype(F32) * up_a
                            + bg_ref[0, h, D_MODEL:2 * D_MODEL].astype(F32) * up_m).astype(BF16)
        if 0 <= step - 1 < nparts:
            h = parts[step - 1]
            x1 = x_ref[0, h, :] + mod_ref[0, 2:3, :] * _dot(merged.pop(step - 1), wo_ref[...])
            x1_ref[0, h, :] = x1
            h2s[step - 1] = _rms_mod(x1, g2_ref[...], mod_ref[0, 4:5, :], mod_ref[0, 3:4, :])
        if 0 <= step - 2 < nparts:
            h = parts[step - 2]
            h2 = h2s.pop(step - 2)
            logits = _dot(h2.astype(BF16), wr) + br_ref[...]
            ha_ref[0, h, :], hb_ref[0, h, :] = _pack_row_halves(h2)
            route = _route(logits)
            rt_ref[0, h, :] = route[:, 0:ROUTE_W]
            ids_ref[0, :, h] = route.T[0:SUBLANES, :]


def _merge(attn, mo, bg, x, mod, wua, wum, wo, norm_g, w_router, b_router, tm):
    B, S, D = x.shape
    tok = lambda w: pl.BlockSpec((1, tm, w), lambda b, i: (b, i, 0))
    const2 = lambda a: pl.BlockSpec(a.shape, lambda b, i: (0, 0))
    g2 = norm_g.reshape(1, D)
    return pl.pallas_call(
        _merge_kernel,
        grid=(B, S // tm),
        in_specs=[
            tok(A_WIDTH), tok(M_WIDTH), tok(2 * D), tok(D),
            pl.BlockSpec((1, 6, D), lambda b, i: (b, 0, 0)),
            const2(wua), const2(wum), const2(wo), const2(g2), const2(w_router), const2(b_router),
        ],
        out_specs=(tok(D), tok(PACK_W), tok(PACK_W), tok(ROUTE_W),
                   pl.BlockSpec((1, SUBLANES, tm), lambda b, i: (b, 0, i))),
        out_shape=(
            jax.ShapeDtypeStruct((B, S, D), F32),
            jax.ShapeDtypeStruct((B, S, PACK_W), PACKED),
            jax.ShapeDtypeStruct((B, S, PACK_W), PACKED),
            jax.ShapeDtypeStruct((B, S, ROUTE_W), F32),
            jax.ShapeDtypeStruct((B, SUBLANES, S), F32),
        ),
        compiler_params=_compiler_params(("parallel", "parallel")),
        name="merge_route",
    )(attn, mo, bg, x, mod, wua, wum, wo, g2, w_router, b_router)


def _route_tables(ea, tmb):
    n = ea.shape[0]
    T = n // 2
    nblk = 2 * T // tmb + N_EXPERTS
    onehot = ea[None, :] == jnp.arange(N_EXPERTS, dtype=jnp.int32)[:, None]
    pieces = onehot.reshape(N_EXPERTS, n // LANES, LANES).astype(BF16)
    upto = (jnp.arange(LANES)[:, None] <= jnp.arange(LANES)[None, :]).astype(BF16)
    within = jnp.einsum("eps,st->ept", pieces, upto, preferred_element_type=F32).astype(jnp.int32)
    totals = within[:, :, -1]
    before = jnp.cumsum(totals, axis=1) - totals
    csum = (within + before[:, :, None]).reshape(N_EXPERTS, n)
    counts = before[:, -1] + totals[:, -1]
    padded = ((counts + tmb - 1) // tmb) * tmb
    ends = jnp.cumsum(padded)
    starts = ends - padded
    pos = jnp.sum(jnp.where(onehot, csum - 1 + starts[:, None], 0), axis=0).astype(jnp.int32)
    blk0 = jnp.arange(nblk, dtype=jnp.int32) * tmb
    bexp = jnp.minimum(jnp.sum((blk0[:, None] >= ends[None, :]).astype(jnp.int32), axis=1), N_EXPERTS - 1)
    nval = jnp.clip(starts[bexp] + counts[bexp] - blk0, 0, tmb)
    nval = jnp.where(blk0 < ends[-1], nval, 0).astype(jnp.int32)
    return pos, bexp.astype(jnp.int32), nval


def _sc_mesh():
    return plsc.VectorSubcoreMesh(core_axis_name="c", subcore_axis_name="s")


def _sc_dispatch(x, pos, n_rows):
    T = x.shape[0]
    nb = T // SC_WINDOW
    idx = pos.reshape(1, 2 * T)

    @pl.kernel(out_type=jax.ShapeDtypeStruct((n_rows, PACK_W), x.dtype), mesh=_sc_mesh(), scratch_types=[])
    def dispatch(x_hbm, i_hbm, o_hbm):
        def body(x_vmem, i_vmem):
            pltpu.sync_copy(x_vmem, o_hbm.at[i_vmem.at[0]])

        pltpu.emit_pipeline(
            body,
            grid=(2 * nb,),
            in_specs=[pl.BlockSpec((SC_WINDOW, PACK_W), index_map=lambda i: (i % nb, 0)),
                      pl.BlockSpec((1, SC_WINDOW), index_map=lambda i: (0, i))],
            out_specs=[],
            core_axis_name=("c", "s"),
            dimension_semantics=(pltpu.PARALLEL,),
        )(x_hbm, i_hbm)

    return dispatch(x, idx)


def _sc_combine(y, pos):
    n = pos.shape[0]
    idx = pos.reshape(1, n)

    @pl.kernel(out_type=jax.ShapeDtypeStruct((n, PACK_W), y.dtype), mesh=_sc_mesh(), scratch_types=[])
    def combine(y_hbm, i_hbm, o_hbm):
        def body(i_vmem, o_vmem):
            pltpu.sync_copy(y_hbm.at[i_vmem.at[0]], o_vmem)

        pltpu.emit_pipeline(
            body,
            grid=(n // SC_WINDOW,),
            in_specs=[pl.BlockSpec((1, SC_WINDOW), index_map=lambda i: (0, i))],
            out_specs=[pl.BlockSpec((SC_WINDOW, PACK_W), index_map=lambda i: (i, 0))],
            core_axis_name=("c", "s"),
            dimension_semantics=(pltpu.PARALLEL,),
        )(i_hbm, o_hbm)

    return combine(y, idx)


def _experts_kernel(bexp_ref, nval_ref, xa_ref, xb_ref, wg_ref, wu_ref, wd_ref, ya_ref, yb_ref, wg_s, wu_s, wd_s):
    i = pl.program_id(0)
    nv = nval_ref[i]
    new_expert = (i == 0) | (bexp_ref[i] != bexp_ref[jnp.maximum(i - 1, 0)])

    @pl.when((nv > 0) & new_expert)
    def _():
        wg_s[...] = wg_ref[0].astype(BF16)
        wu_s[...] = wu_ref[0].astype(BF16)
        wd_s[...] = wd_ref[0].astype(BF16)

    @pl.when(nv > 0)
    def _():
        keep = lax.broadcasted_iota(jnp.int32, xa_ref.shape, 0) < nv
        pieces = [jnp.where(keep, piece, 0.0).astype(BF16) for piece in _unpack_row_halves(xa_ref[...], xb_ref[...])]
        x = jnp.concatenate(pieces, axis=1)
        half = x.shape[0] // 2
        gus = [(_dot(x[r:r + half], wg_s[...]), _dot(x[r:r + half], wu_s[...])) for r in (0, half)]
        hes = [((g * _sigmoid(g)) * u).astype(BF16) for g, u in gus]
        for r, he in zip((0, half), hes):
            ya_ref[r:r + half, :], yb_ref[r:r + half, :] = _pack_row_halves(_dot(he, wd_s[...]))


def _experts(xa, xb, bexp, nval, wg, wu, wd, tmb):
    n = xa.shape[0]
    row = lambda: pl.BlockSpec((tmb, PACK_W), lambda i, be, nv: (i, 0))
    grid_spec = pltpu.PrefetchScalarGridSpec(
        num_scalar_prefetch=2,
        grid=(n // tmb,),
        in_specs=[row(), row(),
                  pl.BlockSpec((1, D_MODEL, D_EXPERT), lambda i, be, nv: (be[i], 0, 0)),
                  pl.BlockSpec((1, D_MODEL, D_EXPERT), lambda i, be, nv: (be[i], 0, 0)),
                  pl.BlockSpec((1, D_EXPERT, D_MODEL), lambda i, be, nv: (be[i], 0, 0))],
        out_specs=[row(), row()],
        scratch_shapes=[pltpu.VMEM((D_MODEL, D_EXPERT), BF16), pltpu.VMEM((D_MODEL, D_EXPERT), BF16),
                        pltpu.VMEM((D_EXPERT, D_MODEL), BF16)],
    )
    out = jax.ShapeDtypeStruct((n, PACK_W), xa.dtype)
    return pl.pallas_call(
        _experts_kernel, grid_spec=grid_spec, out_shape=(out, out),
        compiler_params=_compiler_params(("arbitrary",)),
        name="experts",
    )(bexp, nval, xa, xb, wg, wu, wd)


def _finish_kernel(a0_ref, b0_ref, a1_ref, b1_ref, rt_ref, x1_ref, mod_ref, fg_ref, o_ref):
    y0 = jnp.concatenate(_unpack_row_halves(a0_ref[...], b0_ref[...]), axis=1)
    y1 = jnp.concatenate(_unpack_row_halves(a1_ref[...], b1_ref[...]), axis=1)
    rt = rt_ref[0]
    moe = rt[:, 2:3] * y0 + rt[:, 3:4] * y1
    xo = x1_ref[0] + mod_ref[0, 5:6, :] * moe
    ms = jnp.mean(xo * xo, axis=-1, keepdims=True)
    o_ref[0] = xo * lax.rsqrt(ms + EPS) * fg_ref[...]


def _finish(ca, cb, rt, x1, mod, final_g, tm):
    B, S, D = x1.shape
    nt = S // tm
    second = B * nt
    half = lambda k: pl.BlockSpec((tm, PACK_W), lambda b, i: (k * second + b * nt + i, 0))
    tok = lambda w: pl.BlockSpec((1, tm, w), lambda b, i: (b, i, 0))
    fg = final_g.reshape(1, D)
    return pl.pallas_call(
        _finish_kernel,
        grid=(B, nt),
        in_specs=[half(0), half(0), half(1), half(1), tok(ROUTE_W), tok(D),
                  pl.BlockSpec((1, 6, D), lambda b, i: (b, 0, 0)),
                  pl.BlockSpec((1, D), lambda b, i: (0, 0))],
        out_specs=tok(D),
        out_shape=jax.ShapeDtypeStruct((B, S, D), F32),
        compiler_params=_compiler_params(("parallel", "parallel")),
        name="finish",
    )(ca, cb, ca, cb, rt, x1, mod, fg)


def _moe(ha, hb, rt, ids, x1, mod, wg, wu, wd, final_g):
    B, S, D = x1.shape
    T = B * S
    tmb = EXPERT_ROWS
    ea = ids[:, 0:2, :].astype(jnp.int32).transpose(1, 0, 2).reshape(2 * T)
    pos, bexp, nval = _route_tables(ea, tmb)
    n_rows = 2 * T + N_EXPERTS * tmb
    xa = _sc_dispatch(ha.reshape(T, PACK_W), pos, n_rows)
    xb = _sc_dispatch(hb.reshape(T, PACK_W), pos, n_rows)
    ya, yb = _experts(xa, xb, bexp, nval, wg, wu, wd, tmb)
    return _finish(_sc_combine(ya, pos), _sc_combine(yb, pos), rt, x1, mod, final_g, tm=min(FINISH_TILE, S))


def _rope_tables(S):
    half = ROPE_DIM // 2
    inv_freq = 1.0 / (ROPE_THETA ** (jnp.arange(half, dtype=F32) * 2.0 / ROPE_DIM))
    ang = jnp.arange(S, dtype=F32)[:, None] * inv_freq[None, :]
    cos, sin = jnp.cos(ang), jnp.sin(ang)
    zeros = jnp.zeros((S, A_HEAD_DIM - ROPE_DIM), F32)
    z8 = jnp.zeros((S, half), F32)
    cos_h = jnp.concatenate([cos, cos, jnp.ones_like(zeros)], axis=-1)
    slo_h = jnp.concatenate([-sin, z8, zeros], axis=-1)
    shi_h = jnp.concatenate([z8, sin, zeros], axis=-1)
    rep = LANES // A_HEAD_DIM
    return tuple(jnp.tile(t, (1, rep)) for t in (cos_h, slo_h, shi_h))


def _layer(x, mod, p, tabs):
    B, S, D = x.shape
    mod = mod.reshape(B, 6, D)
    q, kx, vx, mq, mk, vt, og, bg, grow = _in_proj(
        x, mod, p["norm1_g"], p["w_main"], p["w_t"], p["m_gate_b"], tabs, tm=min(IN_PROJ_TILE, S))
    attn = _attention(q, kx, vx, p["attn_sink"])
    mo = _mlstm(mq, mk, vt, og, grow, p["conv_wq"], p["conv_wk"], p["head_norm_g_t"])
    x1, ha, hb, rt, ids = _merge(attn, mo, bg, x, mod, p["w_up_attn"], p["w_up_mlstm"], p["w_out"], p["norm2_g"],
                                 p["w_router"], p["b_router"], tm=min(MERGE_TILE, S))
    return _moe(ha, hb, rt, ids, x1, mod, p["w_gate"], p["w_up"], p["w_down"], p["final_norm_g"])


def kernel(x_prompt, x_sample, c_prompt, c_sample, ada_w, ada_b, norm1_g, w_in, conv_w, m_gate_b, attn_sink,
           head_norm_g, w_up_attn, w_up_mlstm, w_out, norm2_g, rg_w, rg_b, re_w, re_b, w_gate, w_up, w_down,
           final_norm_g):
    assert ada_w.shape[0] == 1, "single-layer trunk"
    w_in0 = w_in[0]
    w_g = w_in0[:, OFF_MG:OFF_BG]
    pad = LANES - N_EXPERTS - N_GROUPS
    p = dict(
        ada_w=ada_w[0], ada_b=ada_b[0], norm1_g=norm1_g[0],
        w_main=jnp.concatenate([w_in0[:, :OFF_MV], w_in0[:, OFF_MO:OFF_MG], w_in0[:, OFF_BG:]], axis=1).astype(BF16),
        w_t=jnp.concatenate([w_in0[:, OFF_MV:OFF_MO], _gates_head_major(w_g)], axis=1).T.astype(BF16),
        m_gate_b=_gates_head_major(m_gate_b[0]), attn_sink=attn_sink[0],
        conv_wq=conv_w[0, :, :M_WIDTH], conv_wk=conv_w[0, :, M_WIDTH:],
        head_norm_g_t=jnp.broadcast_to(head_norm_g[0][:, None], (M_WIDTH, LANES)),
        w_up_attn=w_up_attn[0].astype(BF16), w_up_mlstm=w_up_mlstm[0].astype(BF16), w_out=w_out[0].astype(BF16),
        norm2_g=norm2_g[0],
        w_router=jnp.pad(jnp.concatenate([re_w[0], rg_w[0]], axis=1), ((0, 0), (0, pad))),
        b_router=jnp.pad(jnp.concatenate([re_b[0], rg_b[0]]), (0, pad)).reshape(1, LANES),
        w_gate=w_gate[0], w_up=w_up[0], w_down=w_down[0],
        final_norm_g=final_norm_g,
    )
    tabs = _rope_tables(x_prompt.shape[1])
    nbp = x_prompt.shape[0]
    mod = _ada_mod(jnp.concatenate([c_prompt, c_sample], axis=0), p["ada_w"], p["ada_b"])
    return (_layer(x_prompt, mod[:nbp], p, tabs), _layer(x_sample, mod[nbp:], p, tabs))
```

```python
import jax
import jax.numpy as jnp
from jax import lax
from jax.experimental import pallas as pl
from jax.experimental.pallas import tpu as pltpu
from jax.experimental.pallas import tpu_sc as plsc

D_MODEL = 1024
A_HEADS = 8
A_KV_HEADS = 2
A_GROUP = A_HEADS // A_KV_HEADS
A_HEAD_DIM = 64
A_WIDTH = A_HEADS * A_HEAD_DIM
A_KV_WIDTH = A_KV_HEADS * A_HEAD_DIM
WINDOW = 128
BLOCK = 128
ROPE_DIM = A_HEAD_DIM // 4
ROPE_THETA = 500000.0
M_HEADS = 4
M_HEAD_DIM = 128
M_WIDTH = M_HEADS * M_HEAD_DIM
M_CHUNK = 128
CONV_W = 3
OFF_AQ = 0
OFF_AK = OFF_AQ + A_WIDTH
OFF_AV = OFF_AK + A_KV_WIDTH
OFF_MQ = OFF_AV + A_KV_WIDTH
OFF_MK = OFF_MQ + M_WIDTH
OFF_MV = OFF_MK + M_WIDTH
OFF_MO = OFF_MV + M_WIDTH
OFF_MG = OFF_MO + M_WIDTH
N_MGATES = 4 * M_HEADS
OFF_BG = OFF_MG + N_MGATES
IN_TOTAL = OFF_BG + 2 * D_MODEL
N_GROUPS = 4
EXPERTS_PER_GROUP = 4
N_EXPERTS = N_GROUPS * EXPERTS_PER_GROUP
D_EXPERT = 512
EPS = 1e-6
NEG_BIG = -1e30
F32 = jnp.float32
BF16 = jnp.bfloat16

LANES = 128
SUBLANES = 8
VMEM_LIMIT_BYTES = 56 * 1024 * 1024

MAIN_AQ = 0
MAIN_KV = A_WIDTH
MAIN_MQ = MAIN_KV + 2 * A_KV_WIDTH
MAIN_MK = MAIN_MQ + M_WIDTH
MAIN_MO = MAIN_MK + M_WIDTH
MAIN_BG = MAIN_MO + M_WIDTH
MAIN_TOTAL = MAIN_BG + 2 * D_MODEL

LOG2E = 1.4426950408889634
Q_SCALE = A_HEAD_DIM ** -0.5 * LOG2E
ROUTER_G_LANE = N_EXPERTS
ROUTE_W = 4
PACKED = jnp.uint32
PACK_W = D_MODEL // 4
SC_WINDOW = 128
EXPERT_ROWS = 512
IN_PROJ_TILE = 512
MERGE_TILE = 1024
MERGE_ROWS = 256
FINISH_TILE = 1024
GATE_ROWS = M_HEADS * SUBLANES


def _sigmoid(z):
    return 1.0 / (1.0 + jnp.exp2(z * (-LOG2E)))


def _log_sigmoid(z):
    return jnp.minimum(z, 0.0) - jnp.log(1.0 + jnp.exp(-jnp.abs(z)))


def _gates_head_major(a):
    lead = a.shape[:-1]
    a = jnp.swapaxes(a.reshape(*lead, 4, M_HEADS), -1, -2)
    a = jnp.pad(a, [(0, 0)] * (a.ndim - 1) + [(0, SUBLANES - 4)])
    return a.reshape(*lead, GATE_ROWS)


def _dot(a, b):
    return jnp.dot(a, b, preferred_element_type=F32)


def _dot_nt(a, b):
    return lax.dot_general(a, b, (((1,), (1,)), ((), ())), preferred_element_type=F32)


def _dot_f32(a, b):
    return jnp.dot(a, b, preferred_element_type=F32, precision=lax.Precision.HIGHEST)


def _compiler_params(semantics):
    return pltpu.CompilerParams(dimension_semantics=semantics, vmem_limit_bytes=VMEM_LIMIT_BYTES)


def _ada_kernel(c_ref, w_ref, b_ref, o_ref):
    c = c_ref[...]
    o_ref[...] = _dot_f32(c * _sigmoid(c), w_ref[...]) + b_ref[...]


def _ada_mod(c, ada_w, ada_b):
    B, D = c.shape
    n = ada_w.shape[1] // D
    return pl.pallas_call(
        _ada_kernel,
        grid=(n,),
        in_specs=[
            pl.BlockSpec((B, D), lambda j: (0, 0)),
            pl.BlockSpec((D, D), lambda j: (0, j)),
            pl.BlockSpec((1, D), lambda j: (0, j)),
        ],
        out_specs=pl.BlockSpec((B, D), lambda j: (0, j)),
        out_shape=jax.ShapeDtypeStruct((B, n * D), F32),
        compiler_params=_compiler_params(("arbitrary",)),
        name="ada_mod",
    )(c, ada_w, ada_b.reshape(1, -1))


def _rms_mod(x, g, scale, shift):
    ms = jnp.mean(x * x, axis=-1, keepdims=True)
    return (x * lax.rsqrt(ms + EPS) * g) * (1.0 + scale) + shift


def _rope_block(xb, cos, sin_lo, sin_hi):
    half = ROPE_DIM // 2
    return xb * cos + pltpu.roll(xb, LANES - half, axis=1) * sin_lo + pltpu.roll(xb, half, axis=1) * sin_hi


def _in_proj_kernel(x_ref, mod_ref, g_ref, w_ref, wt_ref, gb_col_ref, cos_ref, slo_ref, shi_ref,
                    q_ref, kx_ref, vx_ref, mq_ref, mk_ref, vt_ref, og_ref, bg_ref, grow_ref):
    x = x_ref[0]
    tm = x.shape[0]
    h = _rms_mod(x, g_ref[...], mod_ref[0, 1:2, :], mod_ref[0, 0:1, :])
    hb = h.astype(BF16)
    cos, slo, shi = cos_ref[...], slo_ref[...], shi_ref[...]
    lane = lax.broadcasted_iota(jnp.int32, (tm, LANES), 1)
    left = lane < A_HEAD_DIM

    pj = _dot(hb, w_ref[:, MAIN_AQ:MAIN_AQ + A_WIDTH])
    for j in range(A_WIDTH // LANES):
        blk = _rope_block(pj[:, j * LANES:(j + 1) * LANES], cos, slo, shi)
        q_ref[0, :, j * LANES:(j + 1) * LANES] = (blk * Q_SCALE).astype(BF16)

    pj = _dot(hb, w_ref[:, MAIN_KV:MAIN_KV + 2 * A_KV_WIDTH])
    kk = _rope_block(pj[:, 0:LANES], cos, slo, shi)
    vv = pj[:, LANES:2 * LANES]
    for src, dst in ((kk, kx_ref), (vv, vx_ref)):
        swapped = pltpu.roll(src, A_HEAD_DIM, axis=1)
        zero = jnp.zeros_like(src)
        dst[0, :, 0 * LANES:1 * LANES] = jnp.where(left, src, zero).astype(BF16)
        dst[0, :, 1 * LANES:2 * LANES] = jnp.where(left, zero, swapped).astype(BF16)
        dst[0, :, 2 * LANES:3 * LANES] = jnp.where(left, swapped, zero).astype(BF16)
        dst[0, :, 3 * LANES:4 * LANES] = jnp.where(left, zero, src).astype(BF16)

    mq_ref[0] = _dot(hb, w_ref[:, MAIN_MQ:MAIN_MQ + M_WIDTH]).astype(BF16)
    mk_ref[0] = _dot(hb, w_ref[:, MAIN_MK:MAIN_MK + M_WIDTH]).astype(BF16)
    og_ref[0] = _sigmoid(_dot(hb, w_ref[:, MAIN_MO:MAIN_MO + M_WIDTH])).astype(BF16)
    for j in range(2 * D_MODEL // M_WIDTH):
        lo = MAIN_BG + j * M_WIDTH
        bg_ref[0, :, j * M_WIDTH:(j + 1) * M_WIDTH] = _sigmoid(_dot(hb, w_ref[:, lo:lo + M_WIDTH])).astype(BF16)

    tr = _dot_nt(wt_ref[...], hb)
    gr = tr[M_WIDTH:M_WIDTH + GATE_ROWS, :] + gb_col_ref[...]
    gtype = jnp.bitwise_and(lax.broadcasted_iota(jnp.int32, gr.shape, 0), SUBLANES - 1)
    gr = jnp.where(gtype == 1, _log_sigmoid(gr), jnp.where(gtype == 3, _log_sigmoid(gr), gr))
    vt = tr[0:M_WIDTH, :].astype(BF16)
    for c in range(tm // M_CHUNK):
        grow_ref[0, c] = gr[:, c * M_CHUNK:(c + 1) * M_CHUNK]
        vt_ref[0, c] = vt[:, c * M_CHUNK:(c + 1) * M_CHUNK]


def _in_proj(x, mod, norm_g, w_main, w_t, gate_b, rope_tabs, tm):
    B, S, D = x.shape
    nt = S // tm
    cos, slo, shi = rope_tabs
    tok = lambda w: pl.BlockSpec((1, tm, w), lambda b, i: (b, i, 0))
    const2 = lambda a: pl.BlockSpec(a.shape, lambda b, i: (0, 0))
    tab = pl.BlockSpec((tm, LANES), lambda b, i: (i, 0))
    out_shapes = (
        jax.ShapeDtypeStruct((B, S, A_WIDTH), BF16),
        jax.ShapeDtypeStruct((B, S, 4 * LANES), BF16),
        jax.ShapeDtypeStruct((B, S, 4 * LANES), BF16),
        jax.ShapeDtypeStruct((B, S, M_WIDTH), BF16),
        jax.ShapeDtypeStruct((B, S, M_WIDTH), BF16),
        jax.ShapeDtypeStruct((B, S // M_CHUNK, M_WIDTH, M_CHUNK), BF16),
        jax.ShapeDtypeStruct((B, S, M_WIDTH), BF16),
        jax.ShapeDtypeStruct((B, S, 2 * D_MODEL), BF16),
        jax.ShapeDtypeStruct((B, S // M_CHUNK, GATE_ROWS, M_CHUNK), F32),
    )
    chunked = lambda rows: pl.BlockSpec((1, tm // M_CHUNK, rows, M_CHUNK), lambda b, i: (b, i, 0, 0))
    out_specs = (
        tok(A_WIDTH), tok(4 * LANES), tok(4 * LANES), tok(M_WIDTH), tok(M_WIDTH), chunked(M_WIDTH), tok(M_WIDTH),
        tok(2 * D_MODEL), chunked(GATE_ROWS),
    )
    gb_col = gate_b.reshape(GATE_ROWS, 1)
    g2 = norm_g.reshape(1, D)
    return pl.pallas_call(
        _in_proj_kernel,
        grid=(B, nt),
        in_specs=[
            tok(D),
            pl.BlockSpec((1, 6, D), lambda b, i: (b, 0, 0)),
            const2(g2), const2(w_main), const2(w_t), const2(gb_col),
            tab, tab, tab,
        ],
        out_specs=out_specs,
        out_shape=out_shapes,
        compiler_params=_compiler_params(("parallel", "parallel")),
        name="in_proj",
    )(x, mod, g2, w_main, w_t, gb_col, cos, slo, shi)


def _attn_kernel(sink_ref, q_ref, kx_ref, vx_ref, o_ref):
    S = q_ref.shape[1]
    nb = S // BLOCK
    kw = 3 * BLOCK
    qi = lax.broadcasted_iota(jnp.int32, (BLOCK, kw), 0)
    ki = lax.broadcasted_iota(jnp.int32, (BLOCK, kw), 1)
    rel0 = ki - qi
    ones_b = jnp.ones((kw, LANES), BF16)
    left = lax.broadcasted_iota(jnp.int32, (BLOCK, LANES), 1) < A_HEAD_DIM

    def block(n, carry):
        q0 = pl.multiple_of(n * BLOCK, BLOCK)
        k0 = pl.multiple_of(jnp.clip((n - 1) * BLOCK, 0, S - kw), BLOCK)
        rel = rel0 + (k0 - q0)
        valid = jnp.abs(rel) <= WINDOW
        scores = []
        for hk in range(A_KV_HEADS):
            for j in range(A_GROUP // 2):
                col = (hk * (A_GROUP // 2) + j) * LANES
                qp = q_ref[0, pl.ds(q0, BLOCK), col:col + LANES]
                for side in range(2):
                    kk = kx_ref[0, pl.ds(k0, kw), (2 * hk + side) * LANES:(2 * hk + side + 1) * LANES]
                    scores.append(_dot_nt(qp, kk))
        for hk in range(A_KV_HEADS):
            for j in range(A_GROUP // 2):
                col = (hk * (A_GROUP // 2) + j) * LANES
                outs, dens = [], []
                for side in range(2):
                    vv = vx_ref[0, pl.ds(k0, kw), (2 * hk + side) * LANES:(2 * hk + side + 1) * LANES]
                    sk = sink_ref[hk * A_GROUP + 2 * j + side] * LOG2E
                    s = jnp.where(valid, scores[(hk * (A_GROUP // 2) + j) * 2 + side], NEG_BIG)
                    m = jnp.maximum(jnp.max(s, axis=-1, keepdims=True), sk)
                    p = jnp.exp2(s - m).astype(BF16)
                    od = _dot(p, jnp.concatenate([vv, ones_b], axis=1))
                    outs.append(od[:, 0:LANES])
                    dens.append(od[:, LANES:2 * LANES] + jnp.exp2(sk - m))
                o = (outs[0] + outs[1]) / jnp.where(left, dens[0], dens[1])
                o_ref[0, pl.ds(q0, BLOCK), col:col + LANES] = o.astype(BF16)
        return carry

    lax.fori_loop(0, nb, block, 0, unroll=8)


def _attention(q, kx, vx, sink):
    B, S, _ = q.shape
    seq = lambda w: pl.BlockSpec((1, S, w), lambda b: (b, 0, 0))
    return pl.pallas_call(
        _attn_kernel,
        grid=(B,),
        in_specs=[pl.BlockSpec(memory_space=pltpu.SMEM), seq(A_WIDTH), seq(4 * LANES), seq(4 * LANES)],
        out_specs=seq(A_WIDTH),
        out_shape=jax.ShapeDtypeStruct((B, S, A_WIDTH), BF16),
        compiler_params=_compiler_params(("parallel",)),
        name="window_attn",
    )(sink, q, kx, vx)


def _conv_silu(u_ref, w_ref, pad_ref, dst_ref, scale):
    S = u_ref.shape[1]
    pad_ref[0:SUBLANES, :] = jnp.zeros((SUBLANES, LANES), F32)
    pad_ref[S + SUBLANES:S + 2 * SUBLANES, :] = jnp.zeros((SUBLANES, LANES), F32)
    pad_ref[SUBLANES:S + SUBLANES, :] = u_ref[0].astype(F32)
    w0, w1, w2 = w_ref[0:1, :], w_ref[1:2, :], w_ref[2:3, :]
    for c in range(S // M_CHUNK):
        base = SUBLANES + c * M_CHUNK
        y = (pad_ref[base - 1:base - 1 + M_CHUNK, :] * w0 + pad_ref[base:base + M_CHUNK, :] * w1
             + pad_ref[base + 1:base + 1 + M_CHUNK, :] * w2)
        y = y * _sigmoid(y)
        dst_ref[c * M_CHUNK:(c + 1) * M_CHUNK, :] = y if scale == 1.0 else y * scale


def _split3(x):
    hi = x.astype(BF16)
    r = x - hi.astype(F32)
    mid = r.astype(BF16)
    lo = (r - mid.astype(F32)).astype(BF16)
    return [hi, mid, lo]


def _mlstm_kernel(mq_ref, mk_ref, vt_ref, og_ref, grow_ref, cwq_ref, cwk_ref, hgt_ref, o_ref,
                  pad_ref, qs_ref, ks_ref, rb_ref, ib_ref, rows_ref, cinc_ref, ninc_ref, cin_ref, sin_ref,
                  c_ref, n_ref):
    S = mq_ref.shape[1]
    nc = S // M_CHUNK
    L = M_CHUNK
    DH = M_HEAD_DIM
    C_GROUP = 8 if nc % 8 == 0 else 1
    _conv_silu(mq_ref, cwq_ref, pad_ref, qs_ref, 1.0)
    _conv_silu(mk_ref, cwk_ref, pad_ref, ks_ref, M_HEAD_DIM ** -0.5)

    s_i = lax.broadcasted_iota(jnp.int32, (L, L), 0)
    t_i = lax.broadcasted_iota(jnp.int32, (L, L), 1)
    tris = (s_i <= t_i, s_i >= t_i)
    eye = s_i == t_i
    row8 = lax.broadcasted_iota(jnp.int32, (SUBLANES, L), 0)
    one_if = lambda cond: jnp.where(cond, 1.0, 0.0)
    k_j = lax.broadcasted_iota(jnp.int32, (3 * L, 2 * L), 0) % L
    c_j = lax.broadcasted_iota(jnp.int32, (3 * L, 2 * L), 1)
    sum_rows = jnp.where(c_j < L, one_if(k_j <= c_j), one_if(k_j >= c_j - L)).astype(BF16)

    g_all = grow_ref[0].reshape(nc * SUBLANES, L)
    rb_ref[...] = _dot(jnp.concatenate(_split3(g_all), axis=1), sum_rows).reshape(nc, SUBLANES, 2 * L)
    ones_b = jnp.ones((2 * L, L), BF16)

    def phase_a_operands(c):
        gr = grow_ref[0, c]
        rb = rb_ref[c]
        vt = vt_ref[0, c].astype(F32)
        wvs, wks, diags = [], [], []
        for d in range(2):
            brow = rb[2 * d + 1:2 * d + 2, d * L:(d + 1) * L]
            blast = brow[:, L - 1:L] if d == 0 else brow[:, 0:1]
            ibr = gr[2 * d:2 * d + 1, :] - brow
            log_g = blast + ibr
            mg = jnp.max(log_g, axis=-1, keepdims=True)
            wk = jnp.exp(log_g - mg)
            wvs.append((vt * wk).astype(BF16))
            wks.append(wk)
            diags.append(jnp.concatenate(
                [jnp.where(eye, term.astype(F32), 0.0).astype(BF16) for term in _split3(ibr * LOG2E)[:2]], axis=1))
            rows_ref[c, 2 + d:3 + d, :] = brow
            rows_ref[c, 4 + d:5 + d, :] = jnp.broadcast_to(mg, (1, L))
            rows_ref[c, 6 + d:7 + d, :] = jnp.broadcast_to(blast, (1, L))
        wk8 = jnp.where(row8 == 0, wks[0], jnp.where(row8 == 1, wks[1], 0.0))
        return jnp.concatenate(diags, axis=0), jnp.concatenate(wvs, axis=0), wk8.astype(BF16)

    def phase_a_products(c, diag, wv, wk8):
        kb = ks_ref[pl.ds(pl.multiple_of(c * L, L), L), :].astype(BF16)
        ib = _dot(diag, ones_b)
        ib_ref[c] = ib
        for d in range(2):
            rows_ref[c, d:d + 1, :] = jnp.max(jnp.where(tris[d], ib[d * L:(d + 1) * L, :], NEG_BIG),
                                              axis=0, keepdims=True)
        cinc_ref[c] = _dot(wv, kb)
        ninc_ref[c] = _dot(wk8, kb)

    def phase_a(g, carry):
        chunks = [g * C_GROUP + i for i in range(C_GROUP)]
        operands = [phase_a_operands(c) for c in chunks]
        for c, ops in zip(chunks, operands):
            phase_a_products(c, *ops)
        return carry

    def phase_b(j, ms):
        new_ms = []
        for d, cc in ((0, j), (1, nc - 1 - j)):
            m = ms[d]
            half = slice(d * DH, (d + 1) * DH)
            cst = c_ref[half, :]
            n = n_ref[d:d + 1, :]
            cin_ref[cc, half, :] = cst.astype(BF16)
            sin_ref[cc, d:d + 1, :] = n
            sin_ref[cc, 2 + d:3 + d, :] = m
            mg = rows_ref[cc, 4 + d:5 + d, :]
            blast = rows_ref[cc, 6 + d:7 + d, :]
            m_new = jnp.maximum(blast + m, mg)
            decay = jnp.exp(blast + m - m_new)
            grow = jnp.exp(mg - m_new)
            c_ref[half, :] = decay * cst + grow * cinc_ref[cc, half, :]
            n_ref[d:d + 1, :] = decay * n + grow * ninc_ref[cc, d:d + 1, :]
            new_ms.append(m_new)
        return tuple(new_ms)

    def phase_c_products(c):
        r0 = pl.multiple_of(c * L, L)
        qb = qs_ref[pl.ds(r0, L), :].astype(BF16)
        qk_t = _dot_nt(ks_ref[pl.ds(r0, L), :].astype(BF16), qb)
        qc_t = _dot_nt(cin_ref[c], qb)
        qn = _dot_nt(sin_ref[c].astype(BF16), qb)
        return qk_t, qc_t, qn

    def phase_c_finish(c, qk_t, qc_t, qn):
        r0 = pl.multiple_of(c * L, L)
        sin = sin_ref[c]
        rows = rows_ref[c]
        ats, stats = [], []
        for d in range(2):
            m_in = sin[2 + d:3 + d, :] * LOG2E
            cm = jnp.maximum(m_in, rows[d:d + 1, :])
            a_t = qk_t * jnp.where(tris[d], jnp.exp2(ib_ref[c, d * L:(d + 1) * L, :] - cm), 0.0)
            w_inter = jnp.exp2(m_in - cm)
            den = w_inter * qn[d:d + 1, :] + jnp.sum(a_t, axis=0, keepdims=True)
            m_t = rows[2 + d:3 + d, :] * LOG2E + cm
            ats.append(a_t.astype(BF16))
            stats.append((w_inter, jnp.maximum(jnp.abs(den), jnp.exp2(-m_t))))
        av_t = _dot(vt_ref[0, c], jnp.concatenate(ats, axis=1))
        h_t = None
        for d in range(2):
            w_inter, den = stats[d]
            hd = (w_inter * qc_t[d * DH:(d + 1) * DH, :] + av_t[:, d * L:(d + 1) * L]) / den
            h_t = hd if h_t is None else h_t + hd
        y_t = h_t * lax.rsqrt(jnp.mean(h_t * h_t, axis=0, keepdims=True) + EPS) * hgt_ref[...]
        o_ref[0, pl.ds(r0, L), :] = (og_ref[0, pl.ds(r0, L), :].astype(F32) * y_t.T).astype(BF16)

    def phase_c(g, carry):
        chunks = [g * C_GROUP + i for i in range(C_GROUP)]
        products = [phase_c_products(c) for c in chunks]
        for c, prod in zip(chunks, products):
            phase_c_finish(c, *prod)
        return carry

    lax.fori_loop(0, nc // C_GROUP, phase_a, 0, unroll=2)
    c_ref[...] = jnp.zeros(c_ref.shape, F32)
    n_ref[...] = jnp.zeros(n_ref.shape, F32)
    m0 = jnp.zeros((1, LANES), F32)
    lax.fori_loop(0, nc, phase_b, (m0, m0), unroll=True)
    lax.fori_loop(0, nc // C_GROUP, phase_c, 0, unroll=2)


def _mlstm(mq, mk, vt, og, grow_h, conv_wq, conv_wk, head_g_t):
    B, S, _ = mq.shape
    nc = S // M_CHUNK
    head = pl.BlockSpec((1, S, M_HEAD_DIM), lambda b, h: (b, 0, h))
    cw = pl.BlockSpec((CONV_W, M_HEAD_DIM), lambda b, h: (0, h))
    return pl.pallas_call(
        _mlstm_kernel,
        grid=(B, M_HEADS),
        in_specs=[
            head, head,
            pl.BlockSpec((1, nc, M_HEAD_DIM, M_CHUNK), lambda b, h: (b, 0, h, 0)),
            head,
            pl.BlockSpec((1, nc, SUBLANES, M_CHUNK), lambda b, h: (b, 0, h, 0)),
            cw, cw,
            pl.BlockSpec((M_HEAD_DIM, LANES), lambda b, h: (h, 0)),
        ],
        out_specs=head,
        out_shape=jax.ShapeDtypeStruct((B, S, M_WIDTH), BF16),
        scratch_shapes=[
            pltpu.VMEM((S + 2 * SUBLANES, LANES), F32),
            pltpu.VMEM((S, M_HEAD_DIM), F32),
            pltpu.VMEM((S, M_HEAD_DIM), F32),
            pltpu.VMEM((nc, SUBLANES, 2 * M_CHUNK), F32),
            pltpu.VMEM((nc, 2 * M_CHUNK, M_CHUNK), F32),
            pltpu.VMEM((nc, SUBLANES, M_CHUNK), F32),
            pltpu.VMEM((nc, 2 * M_HEAD_DIM, M_HEAD_DIM), F32),
            pltpu.VMEM((nc, SUBLANES, M_HEAD_DIM), F32),
            pltpu.VMEM((nc, 2 * M_HEAD_DIM, M_HEAD_DIM), BF16),
            pltpu.VMEM((nc, SUBLANES, M_HEAD_DIM), F32),
            pltpu.VMEM((2 * M_HEAD_DIM, M_HEAD_DIM), F32),
            pltpu.VMEM((SUBLANES, M_HEAD_DIM), F32),
        ],
        compiler_params=_compiler_params(("parallel", "parallel")),
        name="mlstm",
    )(mq, mk, vt, og, grow_h, conv_wq, conv_wk, head_g_t)


def _route(logits):
    lane = lax.broadcasted_iota(jnp.int32, logits.shape, 1)
    lane_f = lane.astype(F32)
    big = float(LANES)
    is_g = (lane >= ROUTER_G_LANE) & (lane < ROUTER_G_LANE + N_GROUPS)
    gl = jnp.where(is_g, logits, NEG_BIG)
    gmax = jnp.max(gl, axis=-1, keepdims=True)
    gsum = jnp.sum(jnp.where(is_g, jnp.exp(gl - gmax), 0.0), axis=-1, keepdims=True)
    p_grp = 1.0 / gsum
    grp = jnp.min(jnp.where(is_g & (gl == gmax), lane_f - ROUTER_G_LANE, big), axis=-1, keepdims=True)
    in_grp = (lane < N_EXPERTS) & (jnp.right_shift(lane, 2).astype(F32) == grp)
    el = jnp.where(in_grp, logits, NEG_BIG)
    v1 = jnp.max(el, axis=-1, keepdims=True)
    i1 = jnp.min(jnp.where(in_grp & (el == v1), lane_f, big), axis=-1, keepdims=True)
    rest = in_grp & (lane_f != i1)
    el2 = jnp.where(rest, logits, NEG_BIG)
    v2 = jnp.max(el2, axis=-1, keepdims=True)
    i2 = jnp.min(jnp.where(rest & (el2 == v2), lane_f, big), axis=-1, keepdims=True)
    e21 = jnp.exp(v2 - v1)
    w1 = p_grp / (1.0 + e21)
    w2 = p_grp * e21 / (1.0 + e21)
    return jnp.where(lane == 0, i1, jnp.where(lane == 1, i2, jnp.where(lane == 2, w1, w2)))


def _pack_bf16_pairs(lo, hi):
    bits = lambda v: lax.bitcast_convert_type(v.astype(BF16).astype(F32), PACKED)
    return jnp.right_shift(bits(lo), PACKED(16)) | (bits(hi) & PACKED(0xFFFF0000))


def _unpack_bf16_pairs(w):
    return (lax.bitcast_convert_type(jnp.left_shift(w, PACKED(16)), F32),
            lax.bitcast_convert_type(w & PACKED(0xFFFF0000), F32))


def _pack_row_halves(y):
    return (_pack_bf16_pairs(y[:, 0:PACK_W], y[:, PACK_W:2 * PACK_W]),
            _pack_bf16_pairs(y[:, 2 * PACK_W:3 * PACK_W], y[:, 3 * PACK_W:4 * PACK_W]))


def _unpack_row_halves(a, b):
    return [*_unpack_bf16_pairs(a), *_unpack_bf16_pairs(b)]


def _merge_kernel(attn_ref, mo_ref, bg_ref, x_ref, mod_ref, wua_ref, wum_ref, wo_ref, g2_ref, wr_ref, br_ref,
                  x1_ref, ha_ref, hb_ref, rt_ref, ids_ref):
    tm = x_ref.shape[1]
    nparts = tm // MERGE_ROWS
    parts = [pl.ds(i * MERGE_ROWS, MERGE_ROWS) for i in range(nparts)]
    wr = wr_ref[...].astype(BF16)
    merged, h2s = {}, {}
    for step in range(nparts + 2):
        if step < nparts:
            h = parts[step]
            up_a = _dot(attn_ref[0, h, :], wua_ref[...])
            up_m = _dot(mo_ref[0, h, :], wum_ref[...])
            merged[step] = (bg_ref[0, h, 0:D_MODEL].astype(F32) * up_a
                            + bg_ref[0, h, D_MODEL:2 * D_MODEL].astype(F32) * up_m).astype(BF16)
        if 0 <= step - 1 < nparts:
            h = parts[step - 1]
            x1 = x_ref[0, h, :] + mod_ref[0, 2:3, :] * _dot(merged.pop(step - 1), wo_ref[...])
            x1_ref[0, h, :] = x1
            h2s[step - 1] = _rms_mod(x1, g2_ref[...], mod_ref[0, 4:5, :], mod_ref[0, 3:4, :])
        if 0 <= step - 2 < nparts:
            h = parts[step - 2]
            h2 = h2s.pop(step - 2)
            logits = _dot(h2.astype(BF16), wr) + br_ref[...]
            ha_ref[0, h, :], hb_ref[0, h, :] = _pack_row_halves(h2)
            route = _route(logits)
            rt_ref[0, h, :] = route[:, 0:ROUTE_W]
            ids_ref[0, :, h] = route.T[0:SUBLANES, :]


def _merge(attn, mo, bg, x, mod, wua, wum, wo, norm_g, w_router, b_router, tm):
    B, S, D = x.shape
    tok = lambda w: pl.BlockSpec((1, tm, w), lambda b, i: (b, i, 0))
    const2 = lambda a: pl.BlockSpec(a.shape, lambda b, i: (0, 0))
    g2 = norm_g.reshape(1, D)
    return pl.pallas_call(
        _merge_kernel,
        grid=(B, S // tm),
        in_specs=[
            tok(A_WIDTH), tok(M_WIDTH), tok(2 * D), tok(D),
            pl.BlockSpec((1, 6, D), lambda b, i: (b, 0, 0)),
            const2(wua), const2(wum), const2(wo), const2(g2), const2(w_router), const2(b_router),
        ],
        out_specs=(tok(D), tok(PACK_W), tok(PACK_W), tok(ROUTE_W),
                   pl.BlockSpec((1, SUBLANES, tm), lambda b, i: (b, 0, i))),
        out_shape=(
            jax.ShapeDtypeStruct((B, S, D), F32),
            jax.ShapeDtypeStruct((B, S, PACK_W), PACKED),
            jax.ShapeDtypeStruct((B, S, PACK_W), PACKED),
            jax.ShapeDtypeStruct((B, S, ROUTE_W), F32),
            jax.ShapeDtypeStruct((B, SUBLANES, S), F32),
        ),
        compiler_params=_compiler_params(("parallel", "parallel")),
        name="merge_route",
    )(attn, mo, bg, x, mod, wua, wum, wo, g2, w_router, b_router)


def _route_tables(ea, tmb):
    n = ea.shape[0]
    T = n // 2
    nblk = 2 * T // tmb + N_EXPERTS
    onehot = ea[None, :] == jnp.arange(N_EXPERTS, dtype=jnp.int32)[:, None]
    pieces = onehot.reshape(N_EXPERTS, n // LANES, LANES).astype(BF16)
    upto = (jnp.arange(LANES)[:, None] <= jnp.arange(LANES)[None, :]).astype(BF16)
    within = jnp.einsum("eps,st->ept", pieces, upto, preferred_element_type=F32).astype(jnp.int32)
    totals = within[:, :, -1]
    before = jnp.cumsum(totals, axis=1) - totals
    csum = (within + before[:, :, None]).reshape(N_EXPERTS, n)
    counts = before[:, -1] + totals[:, -1]
    padded = ((counts + tmb - 1) // tmb) * tmb
    ends = jnp.cumsum(padded)
    starts = ends - padded
    pos = jnp.sum(jnp.where(onehot, csum - 1 + starts[:, None], 0), axis=0).astype(jnp.int32)
    blk0 = jnp.arange(nblk, dtype=jnp.int32) * tmb
    bexp = jnp.minimum(jnp.sum((blk0[:, None] >= ends[None, :]).astype(jnp.int32), axis=1), N_EXPERTS - 1)
    nval = jnp.clip(starts[bexp] + counts[bexp] - blk0, 0, tmb)
    nval = jnp.where(blk0 < ends[-1], nval, 0).astype(jnp.int32)
    return pos, bexp.astype(jnp.int32), nval


def _sc_mesh():
    return plsc.VectorSubcoreMesh(core_axis_name="c", subcore_axis_name="s")


def _sc_dispatch(x, pos, n_rows):
    T = x.shape[0]
    nb = T // SC_WINDOW
    idx = pos.reshape(1, 2 * T)

    @pl.kernel(out_type=jax.ShapeDtypeStruct((n_rows, PACK_W), x.dtype), mesh=_sc_mesh(), scratch_types=[])
    def dispatch(x_hbm, i_hbm, o_hbm):
        def body(x_vmem, i_vmem):
            pltpu.sync_copy(x_vmem, o_hbm.at[i_vmem.at[0]])

        pltpu.emit_pipeline(
            body,
            grid=(2 * nb,),
            in_specs=[pl.BlockSpec((SC_WINDOW, PACK_W), index_map=lambda i: (i % nb, 0)),
                      pl.BlockSpec((1, SC_WINDOW), index_map=lambda i: (0, i))],
            out_specs=[],
            core_axis_name=("c", "s"),
            dimension_semantics=(pltpu.PARALLEL,),
        )(x_hbm, i_hbm)

    return dispatch(x, idx)


def _sc_combine(y, pos):
    n = pos.shape[0]
    idx = pos.reshape(1, n)

    @pl.kernel(out_type=jax.ShapeDtypeStruct((n, PACK_W), y.dtype), mesh=_sc_mesh(), scratch_types=[])
    def combine(y_hbm, i_hbm, o_hbm):
        def body(i_vmem, o_vmem):
            pltpu.sync_copy(y_hbm.at[i_vmem.at[0]], o_vmem)

        pltpu.emit_pipeline(
            body,
            grid=(n // SC_WINDOW,),
            in_specs=[pl.BlockSpec((1, SC_WINDOW), index_map=lambda i: (0, i))],
            out_specs=[pl.BlockSpec((SC_WINDOW, PACK_W), index_map=lambda i: (i, 0))],
            core_axis_name=("c", "s"),
            dimension_semantics=(pltpu.PARALLEL,),
        )(i_hbm, o_hbm)

    return combine(y, idx)


def _experts_kernel(bexp_ref, nval_ref, xa_ref, xb_ref, wg_ref, wu_ref, wd_ref, ya_ref, yb_ref, wg_s, wu_s, wd_s):
    i = pl.program_id(0)
    nv = nval_ref[i]
    new_expert = (i == 0) | (bexp_ref[i] != bexp_ref[jnp.maximum(i - 1, 0)])

    @pl.when((nv > 0) & new_expert)
    def _():
        wg_s[...] = wg_ref[0].astype(BF16)
        wu_s[...] = wu_ref[0].astype(BF16)
        wd_s[...] = wd_ref[0].astype(BF16)

    @pl.when(nv > 0)
    def _():
        keep = lax.broadcasted_iota(jnp.int32, xa_ref.shape, 0) < nv
        pieces = [jnp.where(keep, piece, 0.0).astype(BF16) for piece in _unpack_row_halves(xa_ref[...], xb_ref[...])]
        x = jnp.concatenate(pieces, axis=1)
        half = x.shape[0] // 2
        gus = [(_dot(x[r:r + half], wg_s[...]), _dot(x[r:r + half], wu_s[...])) for r in (0, half)]
        hes = [((g * _sigmoid(g)) * u).astype(BF16) for g, u in gus]
        for r, he in zip((0, half), hes):
            ya_ref[r:r + half, :], yb_ref[r:r + half, :] = _pack_row_halves(_dot(he, wd_s[...]))


def _experts(xa, xb, bexp, nval, wg, wu, wd, tmb):
    n = xa.shape[0]
    row = lambda: pl.BlockSpec((tmb, PACK_W), lambda i, be, nv: (i, 0))
    grid_spec = pltpu.PrefetchScalarGridSpec(
        num_scalar_prefetch=2,
        grid=(n // tmb,),
        in_specs=[row(), row(),
                  pl.BlockSpec((1, D_MODEL, D_EXPERT), lambda i, be, nv: (be[i], 0, 0)),
                  pl.BlockSpec((1, D_MODEL, D_EXPERT), lambda i, be, nv: (be[i], 0, 0)),
                  pl.BlockSpec((1, D_EXPERT, D_MODEL), lambda i, be, nv: (be[i], 0, 0))],
        out_specs=[row(), row()],
        scratch_shapes=[pltpu.VMEM((D_MODEL, D_EXPERT), BF16), pltpu.VMEM((D_MODEL, D_EXPERT), BF16),
                        pltpu.VMEM((D_EXPERT, D_MODEL), BF16)],
    )
    out = jax.ShapeDtypeStruct((n, PACK_W), xa.dtype)
    return pl.pallas_call(
        _experts_kernel, grid_spec=grid_spec, out_shape=(out, out),
        compiler_params=_compiler_params(("arbitrary",)),
        name="experts",
    )(bexp, nval, xa, xb, wg, wu, wd)


def _finish_kernel(a0_ref, b0_ref, a1_ref, b1_ref, rt_ref, x1_ref, mod_ref, fg_ref, o_ref):
    y0 = jnp.concatenate(_unpack_row_halves(a0_ref[...], b0_ref[...]), axis=1)
    y1 = jnp.concatenate(_unpack_row_halves(a1_ref[...], b1_ref[...]), axis=1)
    rt = rt_ref[0]
    moe = rt[:, 2:3] * y0 + rt[:, 3:4] * y1
    xo = x1_ref[0] + mod_ref[0, 5:6, :] * moe
    ms = jnp.mean(xo * xo, axis=-1, keepdims=True)
    o_ref[0] = xo * lax.rsqrt(ms + EPS) * fg_ref[...]


def _finish(ca, cb, rt, x1, mod, final_g, tm):
    B, S, D = x1.shape
    nt = S // tm
    second = B * nt
    half = lambda k: pl.BlockSpec((tm, PACK_W), lambda b, i: (k * second + b * nt + i, 0))
    tok = lambda w: pl.BlockSpec((1, tm, w), lambda b, i: (b, i, 0))
    fg = final_g.reshape(1, D)
    return pl.pallas_call(
        _finish_kernel,
        grid=(B, nt),
        in_specs=[half(0), half(0), half(1), half(1), tok(ROUTE_W), tok(D),
                  pl.BlockSpec((1, 6, D), lambda b, i: (b, 0, 0)),
                  pl.BlockSpec((1, D), lambda b, i: (0, 0))],
        out_specs=tok(D),
        out_shape=jax.ShapeDtypeStruct((B, S, D), F32),
        compiler_params=_compiler_params(("parallel", "parallel")),
        name="finish",
    )(ca, cb, ca, cb, rt, x1, mod, fg)


def _moe(ha, hb, rt, ids, x1, mod, wg, wu, wd, final_g):
    B, S, D = x1.shape
    T = B * S
    tmb = EXPERT_ROWS
    ea = ids[:, 0:2, :].astype(jnp.int32).transpose(1, 0, 2).reshape(2 * T)
    pos, bexp, nval = _route_tables(ea, tmb)
    n_rows = 2 * T + N_EXPERTS * tmb
    xa = _sc_dispatch(ha.reshape(T, PACK_W), pos, n_rows)
    xb = _sc_dispatch(hb.reshape(T, PACK_W), pos, n_rows)
    ya, yb = _experts(xa, xb, bexp, nval, wg, wu, wd, tmb)
    return _finish(_sc_combine(ya, pos), _sc_combine(yb, pos), rt, x1, mod, final_g, tm=min(FINISH_TILE, S))


def _rope_tables(S):
    half = ROPE_DIM // 2
    inv_freq = 1.0 / (ROPE_THETA ** (jnp.arange(half, dtype=F32) * 2.0 / ROPE_DIM))
    ang = jnp.arange(S, dtype=F32)[:, None] * inv_freq[None, :]
    cos, sin = jnp.cos(ang), jnp.sin(ang)
    zeros = jnp.zeros((S, A_HEAD_DIM - ROPE_DIM), F32)
    z8 = jnp.zeros((S, half), F32)
    cos_h = jnp.concatenate([cos, cos, jnp.ones_like(zeros)], axis=-1)
    slo_h = jnp.concatenate([-sin, z8, zeros], axis=-1)
    shi_h = jnp.concatenate([z8, sin, zeros], axis=-1)
    rep = LANES // A_HEAD_DIM
    return tuple(jnp.tile(t, (1, rep)) for t in (cos_h, slo_h, shi_h))


def _layer(x, mod, p, tabs):
    B, S, D = x.shape
    mod = mod.reshape(B, 6, D)
    q, kx, vx, mq, mk, vt, og, bg, grow = _in_proj(
        x, mod, p["norm1_g"], p["w_main"], p["w_t"], p["m_gate_b"], tabs, tm=min(IN_PROJ_TILE, S))
    attn = _attention(q, kx, vx, p["attn_sink"])
    mo = _mlstm(mq, mk, vt, og, grow, p["conv_wq"], p["conv_wk"], p["head_norm_g_t"])
    x1, ha, hb, rt, ids = _merge(attn, mo, bg, x, mod, p["w_up_attn"], p["w_up_mlstm"], p["w_out"], p["norm2_g"],
                                 p["w_router"], p["b_router"], tm=min(MERGE_TILE, S))
    return _moe(ha, hb, rt, ids, x1, mod, p["w_gate"], p["w_up"], p["w_down"], p["final_norm_g"])


def kernel(x_prompt, x_sample, c_prompt, c_sample, ada_w, ada_b, norm1_g, w_in, conv_w, m_gate_b, attn_sink,
           head_norm_g, w_up_attn, w_up_mlstm, w_out, norm2_g, rg_w, rg_b, re_w, re_b, w_gate, w_up, w_down,
           final_norm_g):
    assert ada_w.shape[0] == 1, "single-layer trunk"
    w_in0 = w_in[0]
    w_g = w_in0[:, OFF_MG:OFF_BG]
    pad = LANES - N_EXPERTS - N_GROUPS
    p = dict(
        ada_w=ada_w[0], ada_b=ada_b[0], norm1_g=norm1_g[0],
        w_main=jnp.concatenate([w_in0[:, :OFF_MV], w_in0[:, OFF_MO:OFF_MG], w_in0[:, OFF_BG:]], axis=1).astype(BF16),
        w_t=jnp.concatenate([w_in0[:, OFF_MV:OFF_MO], _gates_head_major(w_g)], axis=1).T.astype(BF16),
        m_gate_b=_gates_head_major(m_gate_b[0]), attn_sink=attn_sink[0],
        conv_wq=conv_w[0, :, :M_WIDTH], conv_wk=conv_w[0, :, M_WIDTH:],
        head_norm_g_t=jnp.broadcast_to(head_norm_g[0][:, None], (M_WIDTH, LANES)),
        w_up_attn=w_up_attn[0].astype(BF16), w_up_mlstm=w_up_mlstm[0].astype(BF16), w_out=w_out[0].astype(BF16),
        norm2_g=norm2_g[0],
        w_router=jnp.pad(jnp.concatenate([re_w[0], rg_w[0]], axis=1), ((0, 0), (0, pad))),
        b_router=jnp.pad(jnp.concatenate([re_b[0], rg_b[0]]), (0, pad)).reshape(1, LANES),
        w_gate=w_gate[0], w_up=w_up[0], w_down=w_down[0],
        final_norm_g=final_norm_g,
    )
    tabs = _rope_tables(x_prompt.shape[1])
    nbp = x_prompt.shape[0]
    mod = _ada_mod(jnp.concatenate([c_prompt, c_sample], axis=0), p["ada_w"], p["ada_b"])
    return (_layer(x_prompt, mod[:nbp], p, tabs), _layer(x_sample, mod[nbp:], p, tabs))
```

```python
import jax
import jax.numpy as jnp
from jax import lax
from jax.experimental import pallas as pl
from jax.experimental.pallas import tpu as pltpu
from jax.experimental.pallas import tpu_sc as plsc

D_MODEL = 1024
A_HEADS = 8
A_KV_HEADS = 2
A_GROUP = A_HEADS // A_KV_HEADS
A_HEAD_DIM = 64
A_WIDTH = A_HEADS * A_HEAD_DIM
A_KV_WIDTH = A_KV_HEADS * A_HEAD_DIM
WINDOW = 128
BLOCK = 128
ROPE_DIM = A_HEAD_DIM // 4
ROPE_THETA = 500000.0
M_HEADS = 4
M_HEAD_DIM = 128
M_WIDTH = M_HEADS * M_HEAD_DIM
M_CHUNK = 128
CONV_W = 3
OFF_AQ = 0
OFF_AK = OFF_AQ + A_WIDTH
OFF_AV = OFF_AK + A_KV_WIDTH
OFF_MQ = OFF_AV + A_KV_WIDTH
OFF_MK = OFF_MQ + M_WIDTH
OFF_MV = OFF_MK + M_WIDTH
OFF_MO = OFF_MV + M_WIDTH
OFF_MG = OFF_MO + M_WIDTH
N_MGATES = 4 * M_HEADS
OFF_BG = OFF_MG + N_MGATES
IN_TOTAL = OFF_BG + 2 * D_MODEL
N_GROUPS = 4
EXPERTS_PER_GROUP = 4
N_EXPERTS = N_GROUPS * EXPERTS_PER_GROUP
D_EXPERT = 512
EPS = 1e-6
NEG_BIG = -1e30
F32 = jnp.float32
BF16 = jnp.bfloat16

LANES = 128
SUBLANES = 8
VMEM_LIMIT_BYTES = 56 * 1024 * 1024

MAIN_AQ = 0
MAIN_KV = A_WIDTH
MAIN_MQ = MAIN_KV + 2 * A_KV_WIDTH
MAIN_MK = MAIN_MQ + M_WIDTH
MAIN_MO = MAIN_MK + M_WIDTH
MAIN_BG = MAIN_MO + M_WIDTH
MAIN_TOTAL = MAIN_BG + 2 * D_MODEL

LOG2E = 1.4426950408889634
Q_SCALE = A_HEAD_DIM ** -0.5 * LOG2E
ROUTER_G_LANE = N_EXPERTS
ROUTE_W = 4
PACKED = jnp.uint32
PACK_W = D_MODEL // 4
SC_WINDOW = 128
EXPERT_ROWS = 512
IN_PROJ_TILE = 512
MERGE_TILE = 1024
MERGE_ROWS = 256
FINISH_TILE = 1024
GATE_ROWS = M_HEADS * SUBLANES


def _sigmoid(z):
    return 1.0 / (1.0 + jnp.exp2(z * (-LOG2E)))


def _log_sigmoid(z):
    return jnp.minimum(z, 0.0) - jnp.log(1.0 + jnp.exp(-jnp.abs(z)))


def _gates_head_major(a):
    lead = a.shape[:-1]
    a = jnp.swapaxes(a.reshape(*lead, 4, M_HEADS), -1, -2)
    a = jnp.pad(a, [(0, 0)] * (a.ndim - 1) + [(0, SUBLANES - 4)])
    return a.reshape(*lead, GATE_ROWS)


def _dot(a, b):
    return jnp.dot(a, b, preferred_element_type=F32)


def _dot_nt(a, b):
    return lax.dot_general(a, b, (((1,), (1,)), ((), ())), preferred_element_type=F32)


def _dot_f32(a, b):
    return jnp.dot(a, b, preferred_element_type=F32, precision=lax.Precision.HIGHEST)


def _compiler_params(semantics):
    return pltpu.CompilerParams(dimension_semantics=semantics, vmem_limit_bytes=VMEM_LIMIT_BYTES)


def _ada_kernel(c_ref, w_ref, b_ref, o_ref):
    c = c_ref[...]
    o_ref[...] = _dot_f32(c * _sigmoid(c), w_ref[...]) + b_ref[...]


def _ada_mod(c, ada_w, ada_b):
    B, D = c.shape
    n = ada_w.shape[1] // D
    return pl.pallas_call(
        _ada_kernel,
        grid=(n,),
        in_specs=[
            pl.BlockSpec((B, D), lambda j: (0, 0)),
            pl.BlockSpec((D, D), lambda j: (0, j)),
            pl.BlockSpec((1, D), lambda j: (0, j)),
        ],
        out_specs=pl.BlockSpec((B, D), lambda j: (0, j)),
        out_shape=jax.ShapeDtypeStruct((B, n * D), F32),
        compiler_params=_compiler_params(("arbitrary",)),
        name="ada_mod",
    )(c, ada_w, ada_b.reshape(1, -1))


def _rms_mod(x, g, scale, shift):
    ms = jnp.mean(x * x, axis=-1, keepdims=True)
    return (x * lax.rsqrt(ms + EPS) * g) * (1.0 + scale) + shift


def _rope_block(xb, cos, sin_lo, sin_hi):
    half = ROPE_DIM // 2
    return xb * cos + pltpu.roll(xb, LANES - half, axis=1) * sin_lo + pltpu.roll(xb, half, axis=1) * sin_hi


def _in_proj_kernel(x_ref, mod_ref, g_ref, w_ref, wt_ref, gb_col_ref, cos_ref, slo_ref, shi_ref,
                    q_ref, kx_ref, vx_ref, mq_ref, mk_ref, vt_ref, og_ref, bg_ref, grow_ref):
    x = x_ref[0]
    tm = x.shape[0]
    h = _rms_mod(x, g_ref[...], mod_ref[0, 1:2, :], mod_ref[0, 0:1, :])
    hb = h.astype(BF16)
    cos, slo, shi = cos_ref[...], slo_ref[...], shi_ref[...]
    lane = lax.broadcasted_iota(jnp.int32, (tm, LANES), 1)
    left = lane < A_HEAD_DIM

    pj = _dot(hb, w_ref[:, MAIN_AQ:MAIN_AQ + A_WIDTH])
    for j in range(A_WIDTH // LANES):
        blk = _rope_block(pj[:, j * LANES:(j + 1) * LANES], cos, slo, shi)
        q_ref[0, :, j * LANES:(j + 1) * LANES] = (blk * Q_SCALE).astype(BF16)

    pj = _dot(hb, w_ref[:, MAIN_KV:MAIN_KV + 2 * A_KV_WIDTH])
    kk = _rope_block(pj[:, 0:LANES], cos, slo, shi)
    vv = pj[:, LANES:2 * LANES]
    for src, dst in ((kk, kx_ref), (vv, vx_ref)):
        swapped = pltpu.roll(src, A_HEAD_DIM, axis=1)
        zero = jnp.zeros_like(src)
        dst[0, :, 0 * LANES:1 * LANES] = jnp.where(left, src, zero).astype(BF16)
        dst[0, :, 1 * LANES:2 * LANES] = jnp.where(left, zero, swapped).astype(BF16)
        dst[0, :, 2 * LANES:3 * LANES] = jnp.where(left, swapped, zero).astype(BF16)
        dst[0, :, 3 * LANES:4 * LANES] = jnp.where(left, zero, src).astype(BF16)

    mq_ref[0] = _dot(hb, w_ref[:, MAIN_MQ:MAIN_MQ + M_WIDTH]).astype(BF16)
    mk_ref[0] = _dot(hb, w_ref[:, MAIN_MK:MAIN_MK + M_WIDTH]).astype(BF16)
    og_ref[0] = _sigmoid(_dot(hb, w_ref[:, MAIN_MO:MAIN_MO + M_WIDTH])).astype(BF16)
    for j in range(2 * D_MODEL // M_WIDTH):
        lo = MAIN_BG + j * M_WIDTH
        bg_ref[0, :, j * M_WIDTH:(j + 1) * M_WIDTH] = _sigmoid(_dot(hb, w_ref[:, lo:lo + M_WIDTH])).astype(BF16)

    tr = _dot_nt(wt_ref[...], hb)
    gr = tr[M_WIDTH:M_WIDTH + GATE_ROWS, :] + gb_col_ref[...]
    gtype = jnp.bitwise_and(lax.broadcasted_iota(jnp.int32, gr.shape, 0), SUBLANES - 1)
    gr = jnp.where(gtype == 1, _log_sigmoid(gr), jnp.where(gtype == 3, _log_sigmoid(gr), gr))
    vt = tr[0:M_WIDTH, :].astype(BF16)
    for c in range(tm // M_CHUNK):
        grow_ref[0, c] = gr[:, c * M_CHUNK:(c + 1) * M_CHUNK]
        vt_ref[0, c] = vt[:, c * M_CHUNK:(c + 1) * M_CHUNK]


def _in_proj(x, mod, norm_g, w_main, w_t, gate_b, rope_tabs, tm):
    B, S, D = x.shape
    nt = S // tm
    cos, slo, shi = rope_tabs
    tok = lambda w: pl.BlockSpec((1, tm, w), lambda b, i: (b, i, 0))
    const2 = lambda a: pl.BlockSpec(a.shape, lambda b, i: (0, 0))
    tab = pl.BlockSpec((tm, LANES), lambda b, i: (i, 0))
    out_shapes = (
        jax.ShapeDtypeStruct((B, S, A_WIDTH), BF16),
        jax.ShapeDtypeStruct((B, S, 4 * LANES), BF16),
        jax.ShapeDtypeStruct((B, S, 4 * LANES), BF16),
        jax.ShapeDtypeStruct((B, S, M_WIDTH), BF16),
        jax.ShapeDtypeStruct((B, S, M_WIDTH), BF16),
        jax.ShapeDtypeStruct((B, S // M_CHUNK, M_WIDTH, M_CHUNK), BF16),
        jax.ShapeDtypeStruct((B, S, M_WIDTH), BF16),
        jax.ShapeDtypeStruct((B, S, 2 * D_MODEL), BF16),
        jax.ShapeDtypeStruct((B, S // M_CHUNK, GATE_ROWS, M_CHUNK), F32),
    )
    chunked = lambda rows: pl.BlockSpec((1, tm // M_CHUNK, rows, M_CHUNK), lambda b, i: (b, i, 0, 0))
    out_specs = (
        tok(A_WIDTH), tok(4 * LANES), tok(4 * LANES), tok(M_WIDTH), tok(M_WIDTH), chunked(M_WIDTH), tok(M_WIDTH),
        tok(2 * D_MODEL), chunked(GATE_ROWS),
    )
    gb_col = gate_b.reshape(GATE_ROWS, 1)
    g2 = norm_g.reshape(1, D)
    return pl.pallas_call(
        _in_proj_kernel,
        grid=(B, nt),
        in_specs=[
            tok(D),
            pl.BlockSpec((1, 6, D), lambda b, i: (b, 0, 0)),
            const2(g2), const2(w_main), const2(w_t), const2(gb_col),
            tab, tab, tab,
        ],
        out_specs=out_specs,
        out_shape=out_shapes,
        compiler_params=_compiler_params(("parallel", "parallel")),
        name="in_proj",
    )(x, mod, g2, w_main, w_t, gb_col, cos, slo, shi)


def _attn_kernel(sink_ref, q_ref, kx_ref, vx_ref, o_ref):
    S = q_ref.shape[1]
    nb = S // BLOCK
    kw = 3 * BLOCK
    qi = lax.broadcasted_iota(jnp.int32, (BLOCK, kw), 0)
    ki = lax.broadcasted_iota(jnp.int32, (BLOCK, kw), 1)
    rel0 = ki - qi
    ones_b = jnp.ones((kw, LANES), BF16)
    left = lax.broadcasted_iota(jnp.int32, (BLOCK, LANES), 1) < A_HEAD_DIM

    def block(n, carry):
        q0 = pl.multiple_of(n * BLOCK, BLOCK)
        k0 = pl.multiple_of(jnp.clip((n - 1) * BLOCK, 0, S - kw), BLOCK)
        rel = rel0 + (k0 - q0)
        valid = jnp.abs(rel) <= WINDOW
        scores = []
        for hk in range(A_KV_HEADS):
            for j in range(A_GROUP // 2):
                col = (hk * (A_GROUP // 2) + j) * LANES
                qp = q_ref[0, pl.ds(q0, BLOCK), col:col + LANES]
                for side in range(2):
                    kk = kx_ref[0, pl.ds(k0, kw), (2 * hk + side) * LANES:(2 * hk + side + 1) * LANES]
                    scores.append(_dot_nt(qp, kk))
        for hk in range(A_KV_HEADS):
            for j in range(A_GROUP // 2):
                col = (hk * (A_GROUP // 2) + j) * LANES
                outs, dens = [], []
                for side in range(2):
                    vv = vx_ref[0, pl.ds(k0, kw), (2 * hk + side) * LANES:(2 * hk + side + 1) * LANES]
                    sk = sink_ref[hk * A_GROUP + 2 * j + side] * LOG2E
                    s = jnp.where(valid, scores[(hk * (A_GROUP // 2) + j) * 2 + side], NEG_BIG)
                    m = jnp.maximum(jnp.max(s, axis=-1, keepdims=True), sk)
                    p = jnp.exp2(s - m).astype(BF16)
                    od = _dot(p, jnp.concatenate([vv, ones_b], axis=1))
                    outs.append(od[:, 0:LANES])
                    dens.append(od[:, LANES:2 * LANES] + jnp.exp2(sk - m))
                o = (outs[0] + outs[1]) / jnp.where(left, dens[0], dens[1])
                o_ref[0, pl.ds(q0, BLOCK), col:col + LANES] = o.astype(BF16)
        return carry

    lax.fori_loop(0, nb, block, 0, unroll=8)


def _attention(q, kx, vx, sink):
    B, S, _ = q.shape
    seq = lambda w: pl.BlockSpec((1, S, w), lambda b: (b, 0, 0))
    return pl.pallas_call(
        _attn_kernel,
        grid=(B,),
        in_specs=[pl.BlockSpec(memory_space=pltpu.SMEM), seq(A_WIDTH), seq(4 * LANES), seq(4 * LANES)],
        out_specs=seq(A_WIDTH),
        out_shape=jax.ShapeDtypeStruct((B, S, A_WIDTH), BF16),
        compiler_params=_compiler_params(("parallel",)),
        name="window_attn",
    )(sink, q, kx, vx)


def _conv_silu(u_ref, w_ref, pad_ref, dst_ref, scale):
    S = u_ref.shape[1]
    pad_ref[0:SUBLANES, :] = jnp.zeros((SUBLANES, LANES), F32)
    pad_ref[S + SUBLANES:S + 2 * SUBLANES, :] = jnp.zeros((SUBLANES, LANES), F32)
    pad_ref[SUBLANES:S + SUBLANES, :] = u_ref[0].astype(F32)
    w0, w1, w2 = w_ref[0:1, :], w_ref[1:2, :], w_ref[2:3, :]
    for c in range(S // M_CHUNK):
        base = SUBLANES + c * M_CHUNK
        y = (pad_ref[base - 1:base - 1 + M_CHUNK, :] * w0 + pad_ref[base:base + M_CHUNK, :] * w1
             + pad_ref[base + 1:base + 1 + M_CHUNK, :] * w2)
        y = y * _sigmoid(y)
        dst_ref[c * M_CHUNK:(c + 1) * M_CHUNK, :] = y if scale == 1.0 else y * scale


def _split3(x):
    hi = x.astype(BF16)
    r = x - hi.astype(F32)
    mid = r.astype(BF16)
    lo = (r - mid.astype(F32)).astype(BF16)
    return [hi, mid, lo]


def _mlstm_kernel(mq_ref, mk_ref, vt_ref, og_ref, grow_ref, cwq_ref, cwk_ref, hgt_ref, o_ref,
                  pad_ref, qs_ref, ks_ref, rb_ref, ib_ref, rows_ref, cinc_ref, ninc_ref, cin_ref, sin_ref,
                  c_ref, n_ref):
    S = mq_ref.shape[1]
    nc = S // M_CHUNK
    L = M_CHUNK
    DH = M_HEAD_DIM
    C_GROUP = 8 if nc % 8 == 0 else 1
    _conv_silu(mq_ref, cwq_ref, pad_ref, qs_ref, 1.0)
    _conv_silu(mk_ref, cwk_ref, pad_ref, ks_ref, M_HEAD_DIM ** -0.5)

    s_i = lax.broadcasted_iota(jnp.int32, (L, L), 0)
    t_i = lax.broadcasted_iota(jnp.int32, (L, L), 1)
    tris = (s_i <= t_i, s_i >= t_i)
    eye = s_i == t_i
    row8 = lax.broadcasted_iota(jnp.int32, (SUBLANES, L), 0)
    one_if = lambda cond: jnp.where(cond, 1.0, 0.0)
    k_j = lax.broadcasted_iota(jnp.int32, (3 * L, 2 * L), 0) % L
    c_j = lax.broadcasted_iota(jnp.int32, (3 * L, 2 * L), 1)
    sum_rows = jnp.where(c_j < L, one_if(k_j <= c_j), one_if(k_j >= c_j - L)).astype(BF16)

    g_all = grow_ref[0].reshape(nc * SUBLANES, L)
    rb_ref[...] = _dot(jnp.concatenate(_split3(g_all), axis=1), sum_rows).reshape(nc, SUBLANES, 2 * L)
    ones_b = jnp.ones((2 * L, L), BF16)

    def phase_a_operands(c):
        gr = grow_ref[0, c]
        rb = rb_ref[c]
        vt = vt_ref[0, c].astype(F32)
        wvs, wks, diags = [], [], []
        for d in range(2):
            brow = rb[2 * d + 1:2 * d + 2, d * L:(d + 1) * L]
            blast = brow[:, L - 1:L] if d == 0 else brow[:, 0:1]
            ibr = gr[2 * d:2 * d + 1, :] - brow
            log_g = blast + ibr
            mg = jnp.max(log_g, axis=-1, keepdims=True)
            wk = jnp.exp(log_g - mg)
            wvs.append((vt * wk).astype(BF16))
            wks.append(wk)
            diags.append(jnp.concatenate(
                [jnp.where(eye, term.astype(F32), 0.0).astype(BF16) for term in _split3(ibr * LOG2E)[:2]], axis=1))
            rows_ref[c, 2 + d:3 + d, :] = brow
            rows_ref[c, 4 + d:5 + d, :] = jnp.broadcast_to(mg, (1, L))
            rows_ref[c, 6 + d:7 + d, :] = jnp.broadcast_to(blast, (1, L))
        wk8 = jnp.where(row8 == 0, wks[0], jnp.where(row8 == 1, wks[1], 0.0))
        return jnp.concatenate(diags, axis=0), jnp.concatenate(wvs, axis=0), wk8.astype(BF16)

    def phase_a_products(c, diag, wv, wk8):
        kb = ks_ref[pl.ds(pl.multiple_of(c * L, L), L), :].astype(BF16)
        ib = _dot(diag, ones_b)
        ib_ref[c] = ib
        for d in range(2):
            rows_ref[c, d:d + 1, :] = jnp.max(jnp.where(tris[d], ib[d * L:(d + 1) * L, :], NEG_BIG),
                                              axis=0, keepdims=True)
        cinc_ref[c] = _dot(wv, kb)
        ninc_ref[c] = _dot(wk8, kb)

    def phase_a(g, carry):
        chunks = [g * C_GROUP + i for i in range(C_GROUP)]
        operands = [phase_a_operands(c) for c in chunks]
        for c, ops in zip(chunks, operands):
            phase_a_products(c, *ops)
        return carry

    def phase_b(j, ms):
        new_ms = []
        for d, cc in ((0, j), (1, nc - 1 - j)):
            m = ms[d]
            half = slice(d * DH, (d + 1) * DH)
            cst = c_ref[half, :]
            n = n_ref[d:d + 1, :]
            cin_ref[cc, half, :] = cst.astype(BF16)
            sin_ref[cc, d:d + 1, :] = n
            sin_ref[cc, 2 + d:3 + d, :] = m
            mg = rows_ref[cc, 4 + d:5 + d, :]
            blast = rows_ref[cc, 6 + d:7 + d, :]
            m_new = jnp.maximum(blast + m, mg)
            decay = jnp.exp(blast + m - m_new)
            grow = jnp.exp(mg - m_new)
            c_ref[half, :] = decay * cst + grow * cinc_ref[cc, half, :]
            n_ref[d:d + 1, :] = decay * n + grow * ninc_ref[cc, d:d + 1, :]
            new_ms.append(m_new)
        return tuple(new_ms)

    def phase_c_products(c):
        r0 = pl.multiple_of(c * L, L)
        qb = qs_ref[pl.ds(r0, L), :].astype(BF16)
        qk_t = _dot_nt(ks_ref[pl.ds(r0, L), :].astype(BF16), qb)
        qc_t = _dot_nt(cin_ref[c], qb)
        qn = _dot_nt(sin_ref[c].astype(BF16), qb)
        return qk_t, qc_t, qn

    def phase_c_finish(c, qk_t, qc_t, qn):
        r0 = pl.multiple_of(c * L, L)
        sin = sin_ref[c]
        rows = rows_ref[c]
        ats, stats = [], []
        for d in range(2):
            m_in = sin[2 + d:3 + d, :] * LOG2E
            cm = jnp.maximum(m_in, rows[d:d + 1, :])
            a_t = qk_t * jnp.where(tris[d], jnp.exp2(ib_ref[c, d * L:(d + 1) * L, :] - cm), 0.0)
            w_inter = jnp.exp2(m_in - cm)
            den = w_inter * qn[d:d + 1, :] + jnp.sum(a_t, axis=0, keepdims=True)
            m_t = rows[2 + d:3 + d, :] * LOG2E + cm
            ats.append(a_t.astype(BF16))
            stats.append((w_inter, jnp.maximum(jnp.abs(den), jnp.exp2(-m_t))))
        av_t = _dot(vt_ref[0, c], jnp.concatenate(ats, axis=1))
        h_t = None
        for d in range(2):
            w_inter, den = stats[d]
            hd = (w_inter * qc_t[d * DH:(d + 1) * DH, :] + av_t[:, d * L:(d + 1) * L]) / den
            h_t = hd if h_t is None else h_t + hd
        y_t = h_t * lax.rsqrt(jnp.mean(h_t * h_t, axis=0, keepdims=True) + EPS) * hgt_ref[...]
        o_ref[0, pl.ds(r0, L), :] = (og_ref[0, pl.ds(r0, L), :].astype(F32) * y_t.T).astype(BF16)

    def phase_c(g, carry):
        chunks = [g * C_GROUP + i for i in range(C_GROUP)]
        products = [phase_c_products(c) for c in chunks]
        for c, prod in zip(chunks, products):
            phase_c_finish(c, *prod)
        return carry

    lax.fori_loop(0, nc // C_GROUP, phase_a, 0, unroll=2)
    c_ref[...] = jnp.zeros(c_ref.shape, F32)
    n_ref[...] = jnp.zeros(n_ref.shape, F32)
    m0 = jnp.zeros((1, LANES), F32)
    lax.fori_loop(0, nc, phase_b, (m0, m0), unroll=True)
    lax.fori_loop(0, nc // C_GROUP, phase_c, 0, unroll=2)


def _mlstm(mq, mk, vt, og, grow_h, conv_wq, conv_wk, head_g_t):
    B, S, _ = mq.shape
    nc = S // M_CHUNK
    head = pl.BlockSpec((1, S, M_HEAD_DIM), lambda b, h: (b, 0, h))
    cw = pl.BlockSpec((CONV_W, M_HEAD_DIM), lambda b, h: (0, h))
    return pl.pallas_call(
        _mlstm_kernel,
        grid=(B, M_HEADS),
        in_specs=[
            head, head,
            pl.BlockSpec((1, nc, M_HEAD_DIM, M_CHUNK), lambda b, h: (b, 0, h, 0)),
            head,
            pl.BlockSpec((1, nc, SUBLANES, M_CHUNK), lambda b, h: (b, 0, h, 0)),
            cw, cw,
            pl.BlockSpec((M_HEAD_DIM, LANES), lambda b, h: (h, 0)),
        ],
        out_specs=head,
        out_shape=jax.ShapeDtypeStruct((B, S, M_WIDTH), BF16),
        scratch_shapes=[
            pltpu.VMEM((S + 2 * SUBLANES, LANES), F32),
            pltpu.VMEM((S, M_HEAD_DIM), F32),
            pltpu.VMEM((S, M_HEAD_DIM), F32),
            pltpu.VMEM((nc, SUBLANES, 2 * M_CHUNK), F32),
            pltpu.VMEM((nc, 2 * M_CHUNK, M_CHUNK), F32),
            pltpu.VMEM((nc, SUBLANES, M_CHUNK), F32),
            pltpu.VMEM((nc, 2 * M_HEAD_DIM, M_HEAD_DIM), F32),
            pltpu.VMEM((nc, SUBLANES, M_HEAD_DIM), F32),
            pltpu.VMEM((nc, 2 * M_HEAD_DIM, M_HEAD_DIM), BF16),
            pltpu.VMEM((nc, SUBLANES, M_HEAD_DIM), F32),
            pltpu.VMEM((2 * M_HEAD_DIM, M_HEAD_DIM), F32),
            pltpu.VMEM((SUBLANES, M_HEAD_DIM), F32),
        ],
        compiler_params=_compiler_params(("parallel", "parallel")),
        name="mlstm",
    )(mq, mk, vt, og, grow_h, conv_wq, conv_wk, head_g_t)


def _route(logits):
    lane = lax.broadcasted_iota(jnp.int32, logits.shape, 1)
    lane_f = lane.astype(F32)
    big = float(LANES)
    is_g = (lane >= ROUTER_G_LANE) & (lane < ROUTER_G_LANE + N_GROUPS)
    gl = jnp.where(is_g, logits, NEG_BIG)
    gmax = jnp.max(gl, axis=-1, keepdims=True)
    gsum = jnp.sum(jnp.where(is_g, jnp.exp(gl - gmax), 0.0), axis=-1, keepdims=True)
    p_grp = 1.0 / gsum
    grp = jnp.min(jnp.where(is_g & (gl == gmax), lane_f - ROUTER_G_LANE, big), axis=-1, keepdims=True)
    in_grp = (lane < N_EXPERTS) & (jnp.right_shift(lane, 2).astype(F32) == grp)
    el = jnp.where(in_grp, logits, NEG_BIG)
    v1 = jnp.max(el, axis=-1, keepdims=True)
    i1 = jnp.min(jnp.where(in_grp & (el == v1), lane_f, big), axis=-1, keepdims=True)
    rest = in_grp & (lane_f != i1)
    el2 = jnp.where(rest, logits, NEG_BIG)
    v2 = jnp.max(el2, axis=-1, keepdims=True)
    i2 = jnp.min(jnp.where(rest & (el2 == v2), lane_f, big), axis=-1, keepdims=True)
    e21 = jnp.exp(v2 - v1)
    w1 = p_grp / (1.0 + e21)
    w2 = p_grp * e21 / (1.0 + e21)
    return jnp.where(lane == 0, i1, jnp.where(lane == 1, i2, jnp.where(lane == 2, w1, w2)))


def _pack_bf16_pairs(lo, hi):
    bits = lambda v: lax.bitcast_convert_type(v.astype(BF16).astype(F32), PACKED)
    return jnp.right_shift(bits(lo), PACKED(16)) | (bits(hi) & PACKED(0xFFFF0000))


def _unpack_bf16_pairs(w):
    return (lax.bitcast_convert_type(jnp.left_shift(w, PACKED(16)), F32),
            lax.bitcast_convert_type(w & PACKED(0xFFFF0000), F32))


def _pack_row_halves(y):
    return (_pack_bf16_pairs(y[:, 0:PACK_W], y[:, PACK_W:2 * PACK_W]),
            _pack_bf16_pairs(y[:, 2 * PACK_W:3 * PACK_W], y[:, 3 * PACK_W:4 * PACK_W]))


def _unpack_row_halves(a, b):
    return [*_unpack_bf16_pairs(a), *_unpack_bf16_pairs(b)]


def _merge_kernel(attn_ref, mo_ref, bg_ref, x_ref, mod_ref, wua_ref, wum_ref, wo_ref, g2_ref, wr_ref, br_ref,
                  x1_ref, ha_ref, hb_ref, rt_ref, ids_ref):
    tm = x_ref.shape[1]
    nparts = tm // MERGE_ROWS
    parts = [pl.ds(i * MERGE_ROWS, MERGE_ROWS) for i in range(nparts)]
    wr = wr_ref[...].astype(BF16)
    merged, h2s = {}, {}
    for step in range(nparts + 2):
        if step < nparts:
            h = parts[step]
            up_a = _dot(attn_ref[0, h, :], wua_ref[...])
            up_m = _dot(mo_ref[0, h, :], wum_ref[...])
            merged[step] = (bg_ref[0, h, 0:D_MODEL].astype(F32) * up_a
                            + bg_ref[0, h, D_MODEL:2 * D_MODEL].astype(F32) * up_m).astype(BF16)
        if 0 <= step - 1 < nparts:
            h = parts[step - 1]
            x1 = x_ref[0, h, :] + mod_ref[0, 2:3, :] * _dot(merged.pop(step - 1), wo_ref[...])
            x1_ref[0, h, :] = x1
            h2s[step - 1] = _rms_mod(x1, g2_ref[...], mod_ref[0, 4:5, :], mod_ref[0, 3:4, :])
        if 0 <= step - 2 < nparts:
            h = parts[step - 2]
            h2 = h2s.pop(step - 2)
            logits = _dot(h2.astype(BF16), wr) + br_ref[...]
            ha_ref[0, h, :], hb_ref[0, h, :] = _pack_row_halves(h2)
            route = _route(logits)
            rt_ref[0, h, :] = route[:, 0:ROUTE_W]
            ids_ref[0, :, h] = route.T[0:SUBLANES, :]


def _merge(attn, mo, bg, x, mod, wua, wum, wo, norm_g, w_router, b_router, tm):
    B, S, D = x.shape
    tok = lambda w: pl.BlockSpec((1, tm, w), lambda b, i: (b, i, 0))
    const2 = lambda a: pl.BlockSpec(a.shape, lambda b, i: (0, 0))
    g2 = norm_g.reshape(1, D)
    return pl.pallas_call(
        _merge_kernel,
        grid=(B, S // tm),
        in_specs=[
            tok(A_WIDTH), tok(M_WIDTH), tok(2 * D), tok(D),
            pl.BlockSpec((1, 6, D), lambda b, i: (b, 0, 0)),
            const2(wua), const2(wum), const2(wo), const2(g2), const2(w_router), const2(b_router),
        ],
        out_specs=(tok(D), tok(PACK_W), tok(PACK_W), tok(ROUTE_W),
                   pl.BlockSpec((1, SUBLANES, tm), lambda b, i: (b, 0, i))),
        out_shape=(
            jax.ShapeDtypeStruct((B, S, D), F32),
            jax.ShapeDtypeStruct((B, S, PACK_W), PACKED),
            jax.ShapeDtypeStruct((B, S, PACK_W), PACKED),
            jax.ShapeDtypeStruct((B, S, ROUTE_W), F32),
            jax.ShapeDtypeStruct((B, SUBLANES, S), F32),
        ),
        compiler_params=_compiler_params(("parallel", "parallel")),
        name="merge_route",
    )(attn, mo, bg, x, mod, wua, wum, wo, g2, w_router, b_router)


def _route_tables(ea, tmb):
    n = ea.shape[0]
    T = n // 2
    nblk = 2 * T // tmb + N_EXPERTS
    onehot = ea[None, :] == jnp.arange(N_EXPERTS, dtype=jnp.int32)[:, None]
    pieces = onehot.reshape(N_EXPERTS, n // LANES, LANES).astype(BF16)
    upto = (jnp.arange(LANES)[:, None] <= jnp.arange(LANES)[None, :]).astype(BF16)
    within = jnp.einsum("eps,st->ept", pieces, upto, preferred_element_type=F32).astype(jnp.int32)
    totals = within[:, :, -1]
    before = jnp.cumsum(totals, axis=1) - totals
    csum = (within + before[:, :, None]).reshape(N_EXPERTS, n)
    counts = before[:, -1] + totals[:, -1]
    padded = ((counts + tmb - 1) // tmb) * tmb
    ends = jnp.cumsum(padded)
    starts = ends - padded
    pos = jnp.sum(jnp.where(onehot, csum - 1 + starts[:, None], 0), axis=0).astype(jnp.int32)
    blk0 = jnp.arange(nblk, dtype=jnp.int32) * tmb
    bexp = jnp.minimum(jnp.sum((blk0[:, None] >= ends[None, :]).astype(jnp.int32), axis=1), N_EXPERTS - 1)
    nval = jnp.clip(starts[bexp] + counts[bexp] - blk0, 0, tmb)
    nval = jnp.where(blk0 < ends[-1], nval, 0).astype(jnp.int32)
    return pos, bexp.astype(jnp.int32), nval


def _sc_mesh():
    return plsc.VectorSubcoreMesh(core_axis_name="c", subcore_axis_name="s")


def _sc_dispatch(x, pos, n_rows):
    T = x.shape[0]
    nb = T // SC_WINDOW
    idx = pos.reshape(1, 2 * T)

    @pl.kernel(out_type=jax.ShapeDtypeStruct((n_rows, PACK_W), x.dtype), mesh=_sc_mesh(), scratch_types=[])
    def dispatch(x_hbm, i_hbm, o_hbm):
        def body(x_vmem, i_vmem):
            pltpu.sync_copy(x_vmem, o_hbm.at[i_vmem.at[0]])

        pltpu.emit_pipeline(
            body,
            grid=(2 * nb,),
            in_specs=[pl.BlockSpec((SC_WINDOW, PACK_W), index_map=lambda i: (i % nb, 0)),
                      pl.BlockSpec((1, SC_WINDOW), index_map=lambda i: (0, i))],
            out_specs=[],
            core_axis_name=("c", "s"),
            dimension_semantics=(pltpu.PARALLEL,),
        )(x_hbm, i_hbm)

    return dispatch(x, idx)


def _sc_combine(y, pos):
    n = pos.shape[0]
    idx = pos.reshape(1, n)

    @pl.kernel(out_type=jax.ShapeDtypeStruct((n, PACK_W), y.dtype), mesh=_sc_mesh(), scratch_types=[])
    def combine(y_hbm, i_hbm, o_hbm):
        def body(i_vmem, o_vmem):
            pltpu.sync_copy(y_hbm.at[i_vmem.at[0]], o_vmem)

        pltpu.emit_pipeline(
            body,
            grid=(n // SC_WINDOW,),
            in_specs=[pl.BlockSpec((1, SC_WINDOW), index_map=lambda i: (0, i))],
            out_specs=[pl.BlockSpec((SC_WINDOW, PACK_W), index_map=lambda i: (i, 0))],
            core_axis_name=("c", "s"),
            dimension_semantics=(pltpu.PARALLEL,),
        )(i_hbm, o_hbm)

    return combine(y, idx)


def _experts_kernel(bexp_ref, nval_ref, xa_ref, xb_ref, wg_ref, wu_ref, wd_ref, ya_ref, yb_ref, wg_s, wu_s, wd_s):
    i = pl.program_id(0)
    nv = nval_ref[i]
    new_expert = (i == 0) | (bexp_ref[i] != bexp_ref[jnp.maximum(i - 1, 0)])

    @pl.when((nv > 0) & new_expert)
    def _():
        wg_s[...] = wg_ref[0].astype(BF16)
        wu_s[...] = wu_ref[0].astype(BF16)
        wd_s[...] = wd_ref[0].astype(BF16)

    @pl.when(nv > 0)
    def _():
        keep = lax.broadcasted_iota(jnp.int32, xa_ref.shape, 0) < nv
        pieces = [jnp.where(keep, piece, 0.0).astype(BF16) for piece in _unpack_row_halves(xa_ref[...], xb_ref[...])]
        x = jnp.concatenate(pieces, axis=1)
        half = x.shape[0] // 2
        gus = [(_dot(x[r:r + half], wg_s[...]), _dot(x[r:r + half], wu_s[...])) for r in (0, half)]
        hes = [((g * _sigmoid(g)) * u).astype(BF16) for g, u in gus]
        for r, he in zip((0, half), hes):
            ya_ref[r:r + half, :], yb_ref[r:r + half, :] = _pack_row_halves(_dot(he, wd_s[...]))


def _experts(xa, xb, bexp, nval, wg, wu, wd, tmb):
    n = xa.shape[0]
    row = lambda: pl.BlockSpec((tmb, PACK_W), lambda i, be, nv: (i, 0))
    grid_spec = pltpu.PrefetchScalarGridSpec(
        num_scalar_prefetch=2,
        grid=(n // tmb,),
        in_specs=[row(), row(),
                  pl.BlockSpec((1, D_MODEL, D_EXPERT), lambda i, be, nv: (be[i], 0, 0)),
                  pl.BlockSpec((1, D_MODEL, D_EXPERT), lambda i, be, nv: (be[i], 0, 0)),
                  pl.BlockSpec((1, D_EXPERT, D_MODEL), lambda i, be, nv: (be[i], 0, 0))],
        out_specs=[row(), row()],
        scratch_shapes=[pltpu.VMEM((D_MODEL, D_EXPERT), BF16), pltpu.VMEM((D_MODEL, D_EXPERT), BF16),
                        pltpu.VMEM((D_EXPERT, D_MODEL), BF16)],
    )
    out = jax.ShapeDtypeStruct((n, PACK_W), xa.dtype)
    return pl.pallas_call(
        _experts_kernel, grid_spec=grid_spec, out_shape=(out, out),
        compiler_params=_compiler_params(("arbitrary",)),
        name="experts",
    )(bexp, nval, xa, xb, wg, wu, wd)


def _finish_kernel(a0_ref, b0_ref, a1_ref, b1_ref, rt_ref, x1_ref, mod_ref, fg_ref, o_ref):
    y0 = jnp.concatenate(_unpack_row_halves(a0_ref[...], b0_ref[...]), axis=1)
    y1 = jnp.concatenate(_unpack_row_halves(a1_ref[...], b1_ref[...]), axis=1)
    rt = rt_ref[0]
    moe = rt[:, 2:3] * y0 + rt[:, 3:4] * y1
    xo = x1_ref[0] + mod_ref[0, 5:6, :] * moe
    ms = jnp.mean(xo * xo, axis=-1, keepdims=True)
    o_ref[0] = xo * lax.rsqrt(ms + EPS) * fg_ref[...]


def _finish(ca, cb, rt, x1, mod, final_g, tm):
    B, S, D = x1.shape
    nt = S // tm
    second = B * nt
    half = lambda k: pl.BlockSpec((tm, PACK_W), lambda b, i: (k * second + b * nt + i, 0))
    tok = lambda w: pl.BlockSpec((1, tm, w), lambda b, i: (b, i, 0))
    fg = final_g.reshape(1, D)
    return pl.pallas_call(
        _finish_kernel,
        grid=(B, nt),
        in_specs=[half(0), half(0), half(1), half(1), tok(ROUTE_W), tok(D),
                  pl.BlockSpec((1, 6, D), lambda b, i: (b, 0, 0)),
                  pl.BlockSpec((1, D), lambda b, i: (0, 0))],
        out_specs=tok(D),
        out_shape=jax.ShapeDtypeStruct((B, S, D), F32),
        compiler_params=_compiler_params(("parallel", "parallel")),
        name="finish",
    )(ca, cb, ca, cb, rt, x1, mod, fg)


def _moe_dispatch(ha, hb, ids):
    B, S, _ = ha.shape
    T = B * S
    ea = ids[:, 0:2, :].astype(jnp.int32).transpose(1, 0, 2).reshape(2 * T)
    pos, bexp, nval = _route_tables(ea, EXPERT_ROWS)
    n_rows = 2 * T + N_EXPERTS * EXPERT_ROWS
    xa = _sc_dispatch(ha.reshape(T, PACK_W), pos, n_rows)
    xb = _sc_dispatch(hb.reshape(T, PACK_W), pos, n_rows)
    return xa, xb, pos, bexp, nval


def _moe_finish(ya, yb, pos, rt, x1, mod, final_g):
    S = x1.shape[1]
    return _finish(_sc_combine(ya, pos), _sc_combine(yb, pos), rt, x1, mod, final_g, tm=min(FINISH_TILE, S))


def _rope_tables(S):
    half = ROPE_DIM // 2
    inv_freq = 1.0 / (ROPE_THETA ** (jnp.arange(half, dtype=F32) * 2.0 / ROPE_DIM))
    ang = jnp.arange(S, dtype=F32)[:, None] * inv_freq[None, :]
    cos, sin = jnp.cos(ang), jnp.sin(ang)
    zeros = jnp.zeros((S, A_HEAD_DIM - ROPE_DIM), F32)
    z8 = jnp.zeros((S, half), F32)
    cos_h = jnp.concatenate([cos, cos, jnp.ones_like(zeros)], axis=-1)
    slo_h = jnp.concatenate([-sin, z8, zeros], axis=-1)
    shi_h = jnp.concatenate([z8, sin, zeros], axis=-1)
    rep = LANES // A_HEAD_DIM
    return tuple(jnp.tile(t, (1, rep)) for t in (cos_h, slo_h, shi_h))


def _mixing(x, mod, p, tabs):
    B, S, D = x.shape
    mod = mod.reshape(B, 6, D)
    q, kx, vx, mq, mk, vt, og, bg, grow = _in_proj(
        x, mod, p["norm1_g"], p["w_main"], p["w_t"], p["m_gate_b"], tabs, tm=min(IN_PROJ_TILE, S))
    attn = _attention(q, kx, vx, p["attn_sink"])
    mo = _mlstm(mq, mk, vt, og, grow, p["conv_wq"], p["conv_wk"], p["head_norm_g_t"])
    x1, ha, hb, rt, ids = _merge(attn, mo, bg, x, mod, p["w_up_attn"], p["w_up_mlstm"], p["w_out"], p["norm2_g"],
                                 p["w_router"], p["b_router"], tm=min(MERGE_TILE, S))
    return dict(x1=x1, rt=rt, mod=mod, routed=_moe_dispatch(ha, hb, ids))


def kernel(x_prompt, x_sample, c_prompt, c_sample, ada_w, ada_b, norm1_g, w_in, conv_w, m_gate_b, attn_sink,
           head_norm_g, w_up_attn, w_up_mlstm, w_out, norm2_g, rg_w, rg_b, re_w, re_b, w_gate, w_up, w_down,
           final_norm_g):
    assert ada_w.shape[0] == 1, "single-layer trunk"
    w_in0 = w_in[0]
    w_g = w_in0[:, OFF_MG:OFF_BG]
    pad = LANES - N_EXPERTS - N_GROUPS
    p = dict(
        ada_w=ada_w[0], ada_b=ada_b[0], norm1_g=norm1_g[0],
        w_main=jnp.concatenate([w_in0[:, :OFF_MV], w_in0[:, OFF_MO:OFF_MG], w_in0[:, OFF_BG:]], axis=1).astype(BF16),
        w_t=jnp.concatenate([w_in0[:, OFF_MV:OFF_MO], _gates_head_major(w_g)], axis=1).T.astype(BF16),
        m_gate_b=_gates_head_major(m_gate_b[0]), attn_sink=attn_sink[0],
        conv_wq=conv_w[0, :, :M_WIDTH], conv_wk=conv_w[0, :, M_WIDTH:],
        head_norm_g_t=jnp.broadcast_to(head_norm_g[0][:, None], (M_WIDTH, LANES)),
        w_up_attn=w_up_attn[0].astype(BF16), w_up_mlstm=w_up_mlstm[0].astype(BF16), w_out=w_out[0].astype(BF16),
        norm2_g=norm2_g[0],
        w_router=jnp.pad(jnp.concatenate([re_w[0], rg_w[0]], axis=1), ((0, 0), (0, pad))),
        b_router=jnp.pad(jnp.concatenate([re_b[0], rg_b[0]]), (0, pad)).reshape(1, LANES),
        w_gate=w_gate[0], w_up=w_up[0], w_down=w_down[0],
        final_norm_g=final_norm_g,
    )
    tabs = _rope_tables(x_prompt.shape[1])
    nbp = x_prompt.shape[0]
    mod = _ada_mod(jnp.concatenate([c_prompt, c_sample], axis=0), p["ada_w"], p["ada_b"])
    groups = [_mixing(x_prompt, mod[:nbp], p, tabs), _mixing(x_sample, mod[nbp:], p, tabs)]

    def experts(g):
        xa, xb, _, bexp, nval = g["routed"]
        return _experts(xa, xb, bexp, nval, p["w_gate"], p["w_up"], p["w_down"], EXPERT_ROWS)

    order = sorted(range(2), key=lambda i: -groups[i]["x1"].shape[0])
    ys = {order[0]: experts(groups[order[0]])}
    ys[order[0]], groups[order[1]]["routed"] = lax.optimization_barrier((ys[order[0]], groups[order[1]]["routed"]))
    ys[order[1]] = experts(groups[order[1]])
    outs = [_moe_finish(*ys[i], g["routed"][2], g["rt"], g["x1"], g["mod"], p["final_norm_g"])
            for i, g in enumerate(groups)]
    return tuple(outs)
```

```python
import jax
import jax.numpy as jnp
from jax import lax
from jax.experimental import pallas as pl
from jax.experimental.pallas import tpu as pltpu
from jax.experimental.pallas import tpu_sc as plsc

D_MODEL = 1024
A_HEADS = 8
A_KV_HEADS = 2
A_GROUP = A_HEADS // A_KV_HEADS
A_HEAD_DIM = 64
A_WIDTH = A_HEADS * A_HEAD_DIM
A_KV_WIDTH = A_KV_HEADS * A_HEAD_DIM
WINDOW = 128
BLOCK = 128
ROPE_DIM = A_HEAD_DIM // 4
ROPE_THETA = 500000.0
M_HEADS = 4
M_HEAD_DIM = 128
M_WIDTH = M_HEADS * M_HEAD_DIM
M_CHUNK = 128
CONV_W = 3
OFF_AQ = 0
OFF_AK = OFF_AQ + A_WIDTH
OFF_AV = OFF_AK + A_KV_WIDTH
OFF_MQ = OFF_AV + A_KV_WIDTH
OFF_MK = OFF_MQ + M_WIDTH
OFF_MV = OFF_MK + M_WIDTH
OFF_MO = OFF_MV + M_WIDTH
OFF_MG = OFF_MO + M_WIDTH
N_MGATES = 4 * M_HEADS
OFF_BG = OFF_MG + N_MGATES
IN_TOTAL = OFF_BG + 2 * D_MODEL
N_GROUPS = 4
EXPERTS_PER_GROUP = 4
N_EXPERTS = N_GROUPS * EXPERTS_PER_GROUP
D_EXPERT = 512
EPS = 1e-6
NEG_BIG = -1e30
F32 = jnp.float32
BF16 = jnp.bfloat16

LANES = 128
SUBLANES = 8
VMEM_LIMIT_BYTES = 56 * 1024 * 1024

MAIN_AQ = 0
MAIN_KV = A_WIDTH
MAIN_MQ = MAIN_KV + 2 * A_KV_WIDTH
MAIN_MK = MAIN_MQ + M_WIDTH
MAIN_MO = MAIN_MK + M_WIDTH
MAIN_BG = MAIN_MO + M_WIDTH
MAIN_TOTAL = MAIN_BG + 2 * D_MODEL

LOG2E = 1.4426950408889634
Q_SCALE = A_HEAD_DIM ** -0.5 * LOG2E
ROUTER_G_LANE = N_EXPERTS
ROUTE_W = 4
PACKED = jnp.uint32
PACK_W = D_MODEL // 4
SC_WINDOW = 128
EXPERT_ROWS = 512
IN_PROJ_TILE = 512
MERGE_TILE = 1024
MERGE_ROWS = 256
FINISH_TILE = 1024
GATE_ROWS = M_HEADS * SUBLANES


def _sigmoid(z):
    return 1.0 / (1.0 + jnp.exp2(z * (-LOG2E)))


def _log_sigmoid(z):
    return jnp.minimum(z, 0.0) - jnp.log(1.0 + jnp.exp(-jnp.abs(z)))


def _gates_head_major(a):
    lead = a.shape[:-1]
    a = jnp.swapaxes(a.reshape(*lead, 4, M_HEADS), -1, -2)
    a = jnp.pad(a, [(0, 0)] * (a.ndim - 1) + [(0, SUBLANES - 4)])
    return a.reshape(*lead, GATE_ROWS)


def _dot(a, b):
    return jnp.dot(a, b, preferred_element_type=F32)


def _dot_nt(a, b):
    return lax.dot_general(a, b, (((1,), (1,)), ((), ())), preferred_element_type=F32)


def _dot_f32(a, b):
    return jnp.dot(a, b, preferred_element_type=F32, precision=lax.Precision.HIGHEST)


def _compiler_params(semantics):
    return pltpu.CompilerParams(dimension_semantics=semantics, vmem_limit_bytes=VMEM_LIMIT_BYTES)


def _ada_kernel(c_ref, w_ref, b_ref, o_ref):
    c = c_ref[...]
    o_ref[...] = _dot_f32(c * _sigmoid(c), w_ref[...]) + b_ref[...]


def _ada_mod(c, ada_w, ada_b):
    B, D = c.shape
    n = ada_w.shape[1] // D
    return pl.pallas_call(
        _ada_kernel,
        grid=(n,),
        in_specs=[
            pl.BlockSpec((B, D), lambda j: (0, 0)),
            pl.BlockSpec((D, D), lambda j: (0, j)),
            pl.BlockSpec((1, D), lambda j: (0, j)),
        ],
        out_specs=pl.BlockSpec((B, D), lambda j: (0, j)),
        out_shape=jax.ShapeDtypeStruct((B, n * D), F32),
        compiler_params=_compiler_params(("arbitrary",)),
        name="ada_mod",
    )(c, ada_w, ada_b.reshape(1, -1))


def _rms_mod(x, g, scale, shift):
    ms = jnp.mean(x * x, axis=-1, keepdims=True)
    return (x * lax.rsqrt(ms + EPS) * g) * (1.0 + scale) + shift


def _rope_block(xb, cos, sin_lo, sin_hi):
    half = ROPE_DIM // 2
    return xb * cos + pltpu.roll(xb, LANES - half, axis=1) * sin_lo + pltpu.roll(xb, half, axis=1) * sin_hi


def _in_proj_kernel(x_ref, mod_ref, g_ref, w_ref, wt_ref, gb_col_ref, cos_ref, slo_ref, shi_ref,
                    q_ref, kx_ref, vx_ref, mq_ref, mk_ref, vt_ref, og_ref, bg_ref, grow_ref):
    x = x_ref[0]
    tm = x.shape[0]
    h = _rms_mod(x, g_ref[...], mod_ref[0, 1:2, :], mod_ref[0, 0:1, :])
    hb = h.astype(BF16)
    cos, slo, shi = cos_ref[...], slo_ref[...], shi_ref[...]
    lane = lax.broadcasted_iota(jnp.int32, (tm, LANES), 1)
    left = lane < A_HEAD_DIM

    pj = _dot(hb, w_ref[:, MAIN_AQ:MAIN_AQ + A_WIDTH])
    for j in range(A_WIDTH // LANES):
        blk = _rope_block(pj[:, j * LANES:(j + 1) * LANES], cos, slo, shi)
        q_ref[0, :, j * LANES:(j + 1) * LANES] = (blk * Q_SCALE).astype(BF16)

    pj = _dot(hb, w_ref[:, MAIN_KV:MAIN_KV + 2 * A_KV_WIDTH])
    kk = _rope_block(pj[:, 0:LANES], cos, slo, shi)
    vv = pj[:, LANES:2 * LANES]
    for src, dst in ((kk, kx_ref), (vv, vx_ref)):
        swapped = pltpu.roll(src, A_HEAD_DIM, axis=1)
        zero = jnp.zeros_like(src)
        dst[0, :, 0 * LANES:1 * LANES] = jnp.where(left, src, zero).astype(BF16)
        dst[0, :, 1 * LANES:2 * LANES] = jnp.where(left, zero, swapped).astype(BF16)
        dst[0, :, 2 * LANES:3 * LANES] = jnp.where(left, swapped, zero).astype(BF16)
        dst[0, :, 3 * LANES:4 * LANES] = jnp.where(left, zero, src).astype(BF16)

    mq_ref[0] = _dot(hb, w_ref[:, MAIN_MQ:MAIN_MQ + M_WIDTH]).astype(BF16)
    mk_ref[0] = _dot(hb, w_ref[:, MAIN_MK:MAIN_MK + M_WIDTH]).astype(BF16)
    og_ref[0] = _sigmoid(_dot(hb, w_ref[:, MAIN_MO:MAIN_MO + M_WIDTH])).astype(BF16)
    for j in range(2 * D_MODEL // M_WIDTH):
        lo = MAIN_BG + j * M_WIDTH
        bg_ref[0, :, j * M_WIDTH:(j + 1) * M_WIDTH] = _sigmoid(_dot(hb, w_ref[:, lo:lo + M_WIDTH])).astype(BF16)

    tr = _dot_nt(wt_ref[...], hb)
    gr = tr[M_WIDTH:M_WIDTH + GATE_ROWS, :] + gb_col_ref[...]
    gtype = jnp.bitwise_and(lax.broadcasted_iota(jnp.int32, gr.shape, 0), SUBLANES - 1)
    gr = jnp.where(gtype == 1, _log_sigmoid(gr), jnp.where(gtype == 3, _log_sigmoid(gr), gr))
    vt = tr[0:M_WIDTH, :].astype(BF16)
    for c in range(tm // M_CHUNK):
        grow_ref[0, c] = gr[:, c * M_CHUNK:(c + 1) * M_CHUNK]
        vt_ref[0, c] = vt[:, c * M_CHUNK:(c + 1) * M_CHUNK]


def _in_proj(x, mod, norm_g, w_main, w_t, gate_b, rope_tabs, tm):
    B, S, D = x.shape
    nt = S // tm
    cos, slo, shi = rope_tabs
    tok = lambda w: pl.BlockSpec((1, tm, w), lambda b, i: (b, i, 0))
    const2 = lambda a: pl.BlockSpec(a.shape, lambda b, i: (0, 0))
    tab = pl.BlockSpec((tm, LANES), lambda b, i: (i, 0))
    out_shapes = (
        jax.ShapeDtypeStruct((B, S, A_WIDTH), BF16),
        jax.ShapeDtypeStruct((B, S, 4 * LANES), BF16),
        jax.ShapeDtypeStruct((B, S, 4 * LANES), BF16),
        jax.ShapeDtypeStruct((B, S, M_WIDTH), BF16),
        jax.ShapeDtypeStruct((B, S, M_WIDTH), BF16),
        jax.ShapeDtypeStruct((B, S // M_CHUNK, M_WIDTH, M_CHUNK), BF16),
        jax.ShapeDtypeStruct((B, S, M_WIDTH), BF16),
        jax.ShapeDtypeStruct((B, S, 2 * D_MODEL), BF16),
        jax.ShapeDtypeStruct((B, S // M_CHUNK, GATE_ROWS, M_CHUNK), F32),
    )
    chunked = lambda rows: pl.BlockSpec((1, tm // M_CHUNK, rows, M_CHUNK), lambda b, i: (b, i, 0, 0))
    out_specs = (
        tok(A_WIDTH), tok(4 * LANES), tok(4 * LANES), tok(M_WIDTH), tok(M_WIDTH), chunked(M_WIDTH), tok(M_WIDTH),
        tok(2 * D_MODEL), chunked(GATE_ROWS),
    )
    gb_col = gate_b.reshape(GATE_ROWS, 1)
    g2 = norm_g.reshape(1, D)
    return pl.pallas_call(
        _in_proj_kernel,
        grid=(B, nt),
        in_specs=[
            tok(D),
            pl.BlockSpec((1, 6, D), lambda b, i: (b, 0, 0)),
            const2(g2), const2(w_main), const2(w_t), const2(gb_col),
            tab, tab, tab,
        ],
        out_specs=out_specs,
        out_shape=out_shapes,
        compiler_params=_compiler_params(("parallel", "parallel")),
        name="in_proj",
    )(x, mod, g2, w_main, w_t, gb_col, cos, slo, shi)


def _attn_kernel(sink_ref, q_ref, kx_ref, vx_ref, o_ref):
    S = q_ref.shape[1]
    nb = S // BLOCK
    kw = 3 * BLOCK
    qi = lax.broadcasted_iota(jnp.int32, (BLOCK, kw), 0)
    ki = lax.broadcasted_iota(jnp.int32, (BLOCK, kw), 1)
    rel0 = ki - qi
    ones_b = jnp.ones((kw, LANES), BF16)
    left = lax.broadcasted_iota(jnp.int32, (BLOCK, LANES), 1) < A_HEAD_DIM

    def block(n, carry):
        q0 = pl.multiple_of(n * BLOCK, BLOCK)
        k0 = pl.multiple_of(jnp.clip((n - 1) * BLOCK, 0, S - kw), BLOCK)
        rel = rel0 + (k0 - q0)
        valid = jnp.abs(rel) <= WINDOW
        scores = []
        for hk in range(A_KV_HEADS):
            for j in range(A_GROUP // 2):
                col = (hk * (A_GROUP // 2) + j) * LANES
                qp = q_ref[0, pl.ds(q0, BLOCK), col:col + LANES]
                for side in range(2):
                    kk = kx_ref[0, pl.ds(k0, kw), (2 * hk + side) * LANES:(2 * hk + side + 1) * LANES]
                    scores.append(_dot_nt(qp, kk))
        for hk in range(A_KV_HEADS):
            for j in range(A_GROUP // 2):
                col = (hk * (A_GROUP // 2) + j) * LANES
                outs, dens = [], []
                for side in range(2):
                    vv = vx_ref[0, pl.ds(k0, kw), (2 * hk + side) * LANES:(2 * hk + side + 1) * LANES]
                    sk = sink_ref[hk * A_GROUP + 2 * j + side] * LOG2E
                    s = jnp.where(valid, scores[(hk * (A_GROUP // 2) + j) * 2 + side], NEG_BIG)
                    m = jnp.maximum(jnp.max(s, axis=-1, keepdims=True), sk)
                    p = jnp.exp2(s - m).astype(BF16)
                    od = _dot(p, jnp.concatenate([vv, ones_b], axis=1))
                    outs.append(od[:, 0:LANES])
                    dens.append(od[:, LANES:2 * LANES] + jnp.exp2(sk - m))
                o = (outs[0] + outs[1]) / jnp.where(left, dens[0], dens[1])
                o_ref[0, pl.ds(q0, BLOCK), col:col + LANES] = o.astype(BF16)
        return carry

    lax.fori_loop(0, nb, block, 0, unroll=8)


def _attention(q, kx, vx, sink):
    B, S, _ = q.shape
    seq = lambda w: pl.BlockSpec((1, S, w), lambda b: (b, 0, 0))
    return pl.pallas_call(
        _attn_kernel,
        grid=(B,),
        in_specs=[pl.BlockSpec(memory_space=pltpu.SMEM), seq(A_WIDTH), seq(4 * LANES), seq(4 * LANES)],
        out_specs=seq(A_WIDTH),
        out_shape=jax.ShapeDtypeStruct((B, S, A_WIDTH), BF16),
        compiler_params=_compiler_params(("parallel",)),
        name="window_attn",
    )(sink, q, kx, vx)


def _conv_silu(u_ref, w_ref, pad_ref, dst_ref, scale):
    S = u_ref.shape[1]
    pad_ref[0:SUBLANES, :] = jnp.zeros((SUBLANES, LANES), F32)
    pad_ref[S + SUBLANES:S + 2 * SUBLANES, :] = jnp.zeros((SUBLANES, LANES), F32)
    pad_ref[SUBLANES:S + SUBLANES, :] = u_ref[0].astype(F32)
    w0, w1, w2 = w_ref[0:1, :], w_ref[1:2, :], w_ref[2:3, :]
    for c in range(S // M_CHUNK):
        base = SUBLANES + c * M_CHUNK
        y = (pad_ref[base - 1:base - 1 + M_CHUNK, :] * w0 + pad_ref[base:base + M_CHUNK, :] * w1
             + pad_ref[base + 1:base + 1 + M_CHUNK, :] * w2)
        y = y * _sigmoid(y)
        dst_ref[c * M_CHUNK:(c + 1) * M_CHUNK, :] = y if scale == 1.0 else y * scale


def _split3(x):
    hi = x.astype(BF16)
    r = x - hi.astype(F32)
    mid = r.astype(BF16)
    lo = (r - mid.astype(F32)).astype(BF16)
    return [hi, mid, lo]


def _mlstm_kernel(mq_ref, mk_ref, vt_ref, og_ref, grow_ref, cwq_ref, cwk_ref, hgt_ref, o_ref,
                  pad_ref, qs_ref, ks_ref, rb_ref, ib_ref, rows_ref, cinc_ref, ninc_ref, cin_ref, sin_ref,
                  c_ref, n_ref):
    S = mq_ref.shape[1]
    nc = S // M_CHUNK
    L = M_CHUNK
    DH = M_HEAD_DIM
    C_GROUP = 8 if nc % 8 == 0 else 1
    _conv_silu(mq_ref, cwq_ref, pad_ref, qs_ref, 1.0)
    _conv_silu(mk_ref, cwk_ref, pad_ref, ks_ref, M_HEAD_DIM ** -0.5)

    s_i = lax.broadcasted_iota(jnp.int32, (L, L), 0)
    t_i = lax.broadcasted_iota(jnp.int32, (L, L), 1)
    tris = (s_i <= t_i, s_i >= t_i)
    eye = s_i == t_i
    row8 = lax.broadcasted_iota(jnp.int32, (SUBLANES, L), 0)
    one_if = lambda cond: jnp.where(cond, 1.0, 0.0)
    k_j = lax.broadcasted_iota(jnp.int32, (3 * L, 2 * L), 0) % L
    c_j = lax.broadcasted_iota(jnp.int32, (3 * L, 2 * L), 1)
    sum_rows = jnp.where(c_j < L, one_if(k_j <= c_j), one_if(k_j >= c_j - L)).astype(BF16)

    g_all = grow_ref[0].reshape(nc * SUBLANES, L)
    rb_ref[...] = _dot(jnp.concatenate(_split3(g_all), axis=1), sum_rows).reshape(nc, SUBLANES, 2 * L)
    ones_b = jnp.ones((2 * L, L), BF16)

    def phase_a_operands(c):
        gr = grow_ref[0, c]
        rb = rb_ref[c]
        vt = vt_ref[0, c].astype(F32)
        wvs, wks, diags = [], [], []
        for d in range(2):
            brow = rb[2 * d + 1:2 * d + 2, d * L:(d + 1) * L]
            blast = brow[:, L - 1:L] if d == 0 else brow[:, 0:1]
            ibr = gr[2 * d:2 * d + 1, :] - brow
            log_g = blast + ibr
            mg = jnp.max(log_g, axis=-1, keepdims=True)
            wk = jnp.exp(log_g - mg)
            wvs.append((vt * wk).astype(BF16))
            wks.append(wk)
            diags.append(jnp.concatenate(
                [jnp.where(eye, term.astype(F32), 0.0).astype(BF16) for term in _split3(ibr * LOG2E)[:2]], axis=1))
            rows_ref[c, 2 + d:3 + d, :] = brow
            rows_ref[c, 4 + d:5 + d, :] = jnp.broadcast_to(mg, (1, L))
            rows_ref[c, 6 + d:7 + d, :] = jnp.broadcast_to(blast, (1, L))
        wk8 = jnp.where(row8 == 0, wks[0], jnp.where(row8 == 1, wks[1], 0.0))
        return jnp.concatenate(diags, axis=0), jnp.concatenate(wvs, axis=0), wk8.astype(BF16)

    def phase_a_products(c, diag, wv, wk8):
        kb = ks_ref[pl.ds(pl.multiple_of(c * L, L), L), :].astype(BF16)
        ib = _dot(diag, ones_b)
        ib_ref[c] = ib
        for d in range(2):
            rows_ref[c, d:d + 1, :] = jnp.max(jnp.where(tris[d], ib[d * L:(d + 1) * L, :], NEG_BIG),
                                              axis=0, keepdims=True)
        cinc_ref[c] = _dot(wv, kb)
        ninc_ref[c] = _dot(wk8, kb)

    def phase_a(g, carry):
        chunks = [g * C_GROUP + i for i in range(C_GROUP)]
        operands = [phase_a_operands(c) for c in chunks]
        for c, ops in zip(chunks, operands):
            phase_a_products(c, *ops)
        return carry

    def phase_b(j, ms):
        new_ms = []
        for d, cc in ((0, j), (1, nc - 1 - j)):
            m = ms[d]
            half = slice(d * DH, (d + 1) * DH)
            cst = c_ref[half, :]
            n = n_ref[d:d + 1, :]
            cin_ref[cc, half, :] = cst.astype(BF16)
            sin_ref[cc, d:d + 1, :] = n
            sin_ref[cc, 2 + d:3 + d, :] = m
            mg = rows_ref[cc, 4 + d:5 + d, :]
            blast = rows_ref[cc, 6 + d:7 + d, :]
            m_new = jnp.maximum(blast + m, mg)
            decay = jnp.exp(blast + m - m_new)
            grow = jnp.exp(mg - m_new)
            c_ref[half, :] = decay * cst + grow * cinc_ref[cc, half, :]
            n_ref[d:d + 1, :] = decay * n + grow * ninc_ref[cc, d:d + 1, :]
            new_ms.append(m_new)
        return tuple(new_ms)

    def phase_c_products(c):
        r0 = pl.multiple_of(c * L, L)
        qb = qs_ref[pl.ds(r0, L), :].astype(BF16)
        qk_t = _dot_nt(ks_ref[pl.ds(r0, L), :].astype(BF16), qb)
        qc_t = _dot_nt(cin_ref[c], qb)
        qn = _dot_nt(sin_ref[c].astype(BF16), qb)
        return qk_t, qc_t, qn

    def phase_c_finish(c, qk_t, qc_t, qn):
        r0 = pl.multiple_of(c * L, L)
        sin = sin_ref[c]
        rows = rows_ref[c]
        ats, stats = [], []
        for d in range(2):
            m_in = sin[2 + d:3 + d, :] * LOG2E
            cm = jnp.maximum(m_in, rows[d:d + 1, :])
            a_t = qk_t * jnp.where(tris[d], jnp.exp2(ib_ref[c, d * L:(d + 1) * L, :] - cm), 0.0)
            w_inter = jnp.exp2(m_in - cm)
            den = w_inter * qn[d:d + 1, :] + jnp.sum(a_t, axis=0, keepdims=True)
            m_t = rows[2 + d:3 + d, :] * LOG2E + cm
            ats.append(a_t.astype(BF16))
            stats.append((w_inter, jnp.maximum(jnp.abs(den), jnp.exp2(-m_t))))
        av_t = _dot(vt_ref[0, c], jnp.concatenate(ats, axis=1))
        h_t = None
        for d in range(2):
            w_inter, den = stats[d]
            hd = (w_inter * qc_t[d * DH:(d + 1) * DH, :] + av_t[:, d * L:(d + 1) * L]) / den
            h_t = hd if h_t is None else h_t + hd
        y_t = h_t * lax.rsqrt(jnp.mean(h_t * h_t, axis=0, keepdims=True) + EPS) * hgt_ref[...]
        o_ref[0, pl.ds(r0, L), :] = (og_ref[0, pl.ds(r0, L), :].astype(F32) * y_t.T).astype(BF16)

    def phase_c(g, carry):
        chunks = [g * C_GROUP + i for i in range(C_GROUP)]
        products = [phase_c_products(c) for c in chunks]
        for c, prod in zip(chunks, products):
            phase_c_finish(c, *prod)
        return carry

    lax.fori_loop(0, nc // C_GROUP, phase_a, 0, unroll=2)
    c_ref[...] = jnp.zeros(c_ref.shape, F32)
    n_ref[...] = jnp.zeros(n_ref.shape, F32)
    m0 = jnp.zeros((1, LANES), F32)
    lax.fori_loop(0, nc, phase_b, (m0, m0), unroll=True)
    lax.fori_loop(0, nc // C_GROUP, phase_c, 0, unroll=2)


def _mlstm(mq, mk, vt, og, grow_h, conv_wq, conv_wk, head_g_t):
    B, S, _ = mq.shape
    nc = S // M_CHUNK
    head = pl.BlockSpec((1, S, M_HEAD_DIM), lambda b, h: (b, 0, h))
    cw = pl.BlockSpec((CONV_W, M_HEAD_DIM), lambda b, h: (0, h))
    return pl.pallas_call(
        _mlstm_kernel,
        grid=(B, M_HEADS),
        in_specs=[
            head, head,
            pl.BlockSpec((1, nc, M_HEAD_DIM, M_CHUNK), lambda b, h: (b, 0, h, 0)),
            head,
            pl.BlockSpec((1, nc, SUBLANES, M_CHUNK), lambda b, h: (b, 0, h, 0)),
            cw, cw,
            pl.BlockSpec((M_HEAD_DIM, LANES), lambda b, h: (h, 0)),
        ],
        out_specs=head,
        out_shape=jax.ShapeDtypeStruct((B, S, M_WIDTH), BF16),
        scratch_shapes=[
            pltpu.VMEM((S + 2 * SUBLANES, LANES), F32),
            pltpu.VMEM((S, M_HEAD_DIM), F32),
            pltpu.VMEM((S, M_HEAD_DIM), F32),
            pltpu.VMEM((nc, SUBLANES, 2 * M_CHUNK), F32),
            pltpu.VMEM((nc, 2 * M_CHUNK, M_CHUNK), F32),
            pltpu.VMEM((nc, SUBLANES, M_CHUNK), F32),
            pltpu.VMEM((nc, 2 * M_HEAD_DIM, M_HEAD_DIM), F32),
            pltpu.VMEM((nc, SUBLANES, M_HEAD_DIM), F32),
            pltpu.VMEM((nc, 2 * M_HEAD_DIM, M_HEAD_DIM), BF16),
            pltpu.VMEM((nc, SUBLANES, M_HEAD_DIM), F32),
            pltpu.VMEM((2 * M_HEAD_DIM, M_HEAD_DIM), F32),
            pltpu.VMEM((SUBLANES, M_HEAD_DIM), F32),
        ],
        compiler_params=_compiler_params(("parallel", "parallel")),
        name="mlstm",
    )(mq, mk, vt, og, grow_h, conv_wq, conv_wk, head_g_t)


def _route(logits):
    lane = lax.broadcasted_iota(jnp.int32, logits.shape, 1)
    lane_f = lane.astype(F32)
    big = float(LANES)
    is_g = (lane >= ROUTER_G_LANE) & (lane < ROUTER_G_LANE + N_GROUPS)
    gl = jnp.where(is_g, logits, NEG_BIG)
    gmax = jnp.max(gl, axis=-1, keepdims=True)
    gsum = jnp.sum(jnp.where(is_g, jnp.exp(gl - gmax), 0.0), axis=-1, keepdims=True)
    p_grp = 1.0 / gsum
    grp = jnp.min(jnp.where(is_g & (gl == gmax), lane_f - ROUTER_G_LANE, big), axis=-1, keepdims=True)
    in_grp = (lane < N_EXPERTS) & (jnp.right_shift(lane, 2).astype(F32) == grp)
    el = jnp.where(in_grp, logits, NEG_BIG)
    v1 = jnp.max(el, axis=-1, keepdims=True)
    i1 = jnp.min(jnp.where(in_grp & (el == v1), lane_f, big), axis=-1, keepdims=True)
    rest = in_grp & (lane_f != i1)
    el2 = jnp.where(rest, logits, NEG_BIG)
    v2 = jnp.max(el2, axis=-1, keepdims=True)
    i2 = jnp.min(jnp.where(rest & (el2 == v2), lane_f, big), axis=-1, keepdims=True)
    e21 = jnp.exp(v2 - v1)
    w1 = p_grp / (1.0 + e21)
    w2 = p_grp * e21 / (1.0 + e21)
    return jnp.where(lane == 0, i1, jnp.where(lane == 1, i2, jnp.where(lane == 2, w1, w2)))


def _pack_bf16_pairs(lo, hi):
    bits = lambda v: lax.bitcast_convert_type(v.astype(BF16).astype(F32), PACKED)
    return jnp.right_shift(bits(lo), PACKED(16)) | (bits(hi) & PACKED(0xFFFF0000))


def _unpack_bf16_pairs(w):
    return (lax.bitcast_convert_type(jnp.left_shift(w, PACKED(16)), F32),
            lax.bitcast_convert_type(w & PACKED(0xFFFF0000), F32))


def _pack_row_halves(y):
    return (_pack_bf16_pairs(y[:, 0:PACK_W], y[:, PACK_W:2 * PACK_W]),
            _pack_bf16_pairs(y[:, 2 * PACK_W:3 * PACK_W], y[:, 3 * PACK_W:4 * PACK_W]))


def _unpack_row_halves(a, b):
    return [*_unpack_bf16_pairs(a), *_unpack_bf16_pairs(b)]


def _merge_kernel(attn_ref, mo_ref, bg_ref, x_ref, mod_ref, wua_ref, wum_ref, wo_ref, g2_ref, wr_ref, br_ref,
                  x1_ref, ha_ref, hb_ref, rt_ref, ids_ref):
    tm = x_ref.shape[1]
    nparts = tm // MERGE_ROWS
    parts = [pl.ds(i * MERGE_ROWS, MERGE_ROWS) for i in range(nparts)]
    wr = wr_ref[...].astype(BF16)
    merged, h2s = {}, {}
    for step in range(nparts + 2):
        if step < nparts:
            h = parts[step]
            up_a = _dot(attn_ref[0, h, :], wua_ref[...])
            up_m = _dot(mo_ref[0, h, :], wum_ref[...])
            merged[step] = (bg_ref[0, h, 0:D_MODEL].astype(F32) * up_a
                            + bg_ref[0, h, D_MODEL:2 * D_MODEL].astype(F32) * up_m).astype(BF16)
        if 0 <= step - 1 < nparts:
            h = parts[step - 1]
            x1 = x_ref[0, h, :] + mod_ref[0, 2:3, :] * _dot(merged.pop(step - 1), wo_ref[...])
            x1_ref[0, h, :] = x1
            h2s[step - 1] = _rms_mod(x1, g2_ref[...], mod_ref[0, 4:5, :], mod_ref[0, 3:4, :])
        if 0 <= step - 2 < nparts:
            h = parts[step - 2]
            h2 = h2s.pop(step - 2)
            logits = _dot(h2.astype(BF16), wr) + br_ref[...]
            ha_ref[0, h, :], hb_ref[0, h, :] = _pack_row_halves(h2)
            route = _route(logits)
            rt_ref[0, h, :] = route[:, 0:ROUTE_W]
            ids_ref[0, :, h] = route.T[0:SUBLANES, :]


def _merge(attn, mo, bg, x, mod, wua, wum, wo, norm_g, w_router, b_router, tm):
    B, S, D = x.shape
    tok = lambda w: pl.BlockSpec((1, tm, w), lambda b, i: (b, i, 0))
    const2 = lambda a: pl.BlockSpec(a.shape, lambda b, i: (0, 0))
    g2 = norm_g.reshape(1, D)
    return pl.pallas_call(
        _merge_kernel,
        grid=(B, S // tm),
        in_specs=[
            tok(A_WIDTH), tok(M_WIDTH), tok(2 * D), tok(D),
            pl.BlockSpec((1, 6, D), lambda b, i: (b, 0, 0)),
            const2(wua), const2(wum), const2(wo), const2(g2), const2(w_router), const2(b_router),
        ],
        out_specs=(tok(D), tok(PACK_W), tok(PACK_W), tok(ROUTE_W),
                   pl.BlockSpec((1, SUBLANES, tm), lambda b, i: (b, 0, i))),
        out_shape=(
            jax.ShapeDtypeStruct((B, S, D), F32),
            jax.ShapeDtypeStruct((B, S, PACK_W), PACKED),
            jax.ShapeDtypeStruct((B, S, PACK_W), PACKED),
            jax.ShapeDtypeStruct((B, S, ROUTE_W), F32),
            jax.ShapeDtypeStruct((B, SUBLANES, S), F32),
        ),
        compiler_params=_compiler_params(("parallel", "parallel")),
        name="merge_route",
    )(attn, mo, bg, x, mod, wua, wum, wo, g2, w_router, b_router)


def _route_tables(ea, tmb):
    n = ea.shape[0]
    T = n // 2
    nblk = 2 * T // tmb + N_EXPERTS
    onehot = ea[None, :] == jnp.arange(N_EXPERTS, dtype=jnp.int32)[:, None]
    pieces = onehot.reshape(N_EXPERTS, n // LANES, LANES).astype(BF16)
    upto = (jnp.arange(LANES)[:, None] <= jnp.arange(LANES)[None, :]).astype(BF16)
    within = jnp.einsum("eps,st->ept", pieces, upto, preferred_element_type=F32).astype(jnp.int32)
    totals = within[:, :, -1]
    before = jnp.cumsum(totals, axis=1) - totals
    csum = (within + before[:, :, None]).reshape(N_EXPERTS, n)
    counts = before[:, -1] + totals[:, -1]
    padded = ((counts + tmb - 1) // tmb) * tmb
    ends = jnp.cumsum(padded)
    starts = ends - padded
    pos = jnp.sum(jnp.where(onehot, csum - 1 + starts[:, None], 0), axis=0).astype(jnp.int32)
    blk0 = jnp.arange(nblk, dtype=jnp.int32) * tmb
    bexp = jnp.minimum(jnp.sum((blk0[:, None] >= ends[None, :]).astype(jnp.int32), axis=1), N_EXPERTS - 1)
    nval = jnp.clip(starts[bexp] + counts[bexp] - blk0, 0, tmb)
    nval = jnp.where(blk0 < ends[-1], nval, 0).astype(jnp.int32)
    return pos, bexp.astype(jnp.int32), nval


def _sc_mesh():
    return plsc.VectorSubcoreMesh(core_axis_name="c", subcore_axis_name="s")


def _sc_dispatch(x, pos, n_rows):
    T = x.shape[0]
    nb = T // SC_WINDOW
    idx = pos.reshape(1, 2 * T)

    @pl.kernel(out_type=jax.ShapeDtypeStruct((n_rows, PACK_W), x.dtype), mesh=_sc_mesh(), scratch_types=[])
    def dispatch(x_hbm, i_hbm, o_hbm):
        def body(x_vmem, i_vmem):
            pltpu.sync_copy(x_vmem, o_hbm.at[i_vmem.at[0]])

        pltpu.emit_pipeline(
            body,
            grid=(2 * nb,),
            in_specs=[pl.BlockSpec((SC_WINDOW, PACK_W), index_map=lambda i: (i % nb, 0)),
                      pl.BlockSpec((1, SC_WINDOW), index_map=lambda i: (0, i))],
            out_specs=[],
            core_axis_name=("c", "s"),
            dimension_semantics=(pltpu.PARALLEL,),
        )(x_hbm, i_hbm)

    return dispatch(x, idx)


def _sc_combine(y, pos):
    n = pos.shape[0]
    idx = pos.reshape(1, n)

    @pl.kernel(out_type=jax.ShapeDtypeStruct((n, PACK_W), y.dtype), mesh=_sc_mesh(), scratch_types=[])
    def combine(y_hbm, i_hbm, o_hbm):
        def body(i_vmem, o_vmem):
            pltpu.sync_copy(y_hbm.at[i_vmem.at[0]], o_vmem)

        pltpu.emit_pipeline(
            body,
            grid=(n // SC_WINDOW,),
            in_specs=[pl.BlockSpec((1, SC_WINDOW), index_map=lambda i: (0, i))],
            out_specs=[pl.BlockSpec((SC_WINDOW, PACK_W), index_map=lambda i: (i, 0))],
            core_axis_name=("c", "s"),
            dimension_semantics=(pltpu.PARALLEL,),
        )(i_hbm, o_hbm)

    return combine(y, idx)


def _experts_kernel(bexp_ref, nval_ref, xa_ref, xb_ref, wg_ref, wu_ref, wd_ref, ya_ref, yb_ref, wg_s, wu_s, wd_s):
    i = pl.program_id(0)
    nv = nval_ref[i]
    new_expert = (i == 0) | (bexp_ref[i] != bexp_ref[jnp.maximum(i - 1, 0)])

    @pl.when((nv > 0) & new_expert)
    def _():
        wg_s[...] = wg_ref[0].astype(BF16)
        wu_s[...] = wu_ref[0].astype(BF16)
        wd_s[...] = wd_ref[0].astype(BF16)

    @pl.when(nv > 0)
    def _():
        keep = lax.broadcasted_iota(jnp.int32, xa_ref.shape, 0) < nv
        pieces = [jnp.where(keep, piece, 0.0).astype(BF16) for piece in _unpack_row_halves(xa_ref[...], xb_ref[...])]
        x = jnp.concatenate(pieces, axis=1)
        half = x.shape[0] // 2
        gus = [(_dot(x[r:r + half], wg_s[...]), _dot(x[r:r + half], wu_s[...])) for r in (0, half)]
        hes = [((g * _sigmoid(g)) * u).astype(BF16) for g, u in gus]
        for r, he in zip((0, half), hes):
            ya_ref[r:r + half, :], yb_ref[r:r + half, :] = _pack_row_halves(_dot(he, wd_s[...]))


def _experts(xa, xb, bexp, nval, wg, wu, wd, tmb):
    n = xa.shape[0]
    row = lambda: pl.BlockSpec((tmb, PACK_W), lambda i, be, nv: (i, 0))
    grid_spec = pltpu.PrefetchScalarGridSpec(
        num_scalar_prefetch=2,
        grid=(n // tmb,),
        in_specs=[row(), row(),
                  pl.BlockSpec((1, D_MODEL, D_EXPERT), lambda i, be, nv: (be[i], 0, 0)),
                  pl.BlockSpec((1, D_MODEL, D_EXPERT), lambda i, be, nv: (be[i], 0, 0)),
                  pl.BlockSpec((1, D_EXPERT, D_MODEL), lambda i, be, nv: (be[i], 0, 0))],
        out_specs=[row(), row()],
        scratch_shapes=[pltpu.VMEM((D_MODEL, D_EXPERT), BF16), pltpu.VMEM((D_MODEL, D_EXPERT), BF16),
                        pltpu.VMEM((D_EXPERT, D_MODEL), BF16)],
    )
    out = jax.ShapeDtypeStruct((n, PACK_W), xa.dtype)
    return pl.pallas_call(
        _experts_kernel, grid_spec=grid_spec, out_shape=(out, out),
        compiler_params=_compiler_params(("arbitrary",)),
        name="experts",
    )(bexp, nval, xa, xb, wg, wu, wd)


def _finish_kernel(a0_ref, b0_ref, a1_ref, b1_ref, rt_ref, x1_ref, mod_ref, fg_ref, o_ref):
    y0 = jnp.concatenate(_unpack_row_halves(a0_ref[...], b0_ref[...]), axis=1)
    y1 = jnp.concatenate(_unpack_row_halves(a1_ref[...], b1_ref[...]), axis=1)
    rt = rt_ref[0]
    moe = rt[:, 2:3] * y0 + rt[:, 3:4] * y1
    xo = x1_ref[0] + mod_ref[0, 5:6, :] * moe
    ms = jnp.mean(xo * xo, axis=-1, keepdims=True)
    o_ref[0] = xo * lax.rsqrt(ms + EPS) * fg_ref[...]


def _finish(ca, cb, rt, x1, mod, final_g, tm):
    B, S, D = x1.shape
    nt = S // tm
    second = B * nt
    half = lambda k: pl.BlockSpec((tm, PACK_W), lambda b, i: (k * second + b * nt + i, 0))
    tok = lambda w: pl.BlockSpec((1, tm, w), lambda b, i: (b, i, 0))
    fg = final_g.reshape(1, D)
    return pl.pallas_call(
        _finish_kernel,
        grid=(B, nt),
        in_specs=[half(0), half(0), half(1), half(1), tok(ROUTE_W), tok(D),
                  pl.BlockSpec((1, 6, D), lambda b, i: (b, 0, 0)),
                  pl.BlockSpec((1, D), lambda b, i: (0, 0))],
        out_specs=tok(D),
        out_shape=jax.ShapeDtypeStruct((B, S, D), F32),
        compiler_params=_compiler_params(("parallel", "parallel")),
        name="finish",
    )(ca, cb, ca, cb, rt, x1, mod, fg)


def _moe_dispatch(ha, hb, ids):
    B, S, _ = ha.shape
    T = B * S
    ea = ids[:, 0:2, :].astype(jnp.int32).transpose(1, 0, 2).reshape(2 * T)
    pos, bexp, nval = _route_tables(ea, EXPERT_ROWS)
    n_rows = 2 * T + N_EXPERTS * EXPERT_ROWS
    xa = _sc_dispatch(ha.reshape(T, PACK_W), pos, n_rows)
    xb = _sc_dispatch(hb.reshape(T, PACK_W), pos, n_rows)
    return xa, xb, pos, bexp, nval


def _moe_finish(ya, yb, pos, rt, x1, mod, final_g):
    S = x1.shape[1]
    return _finish(_sc_combine(ya, pos), _sc_combine(yb, pos), rt, x1, mod, final_g, tm=min(FINISH_TILE, S))


def _rope_tables(S):
    half = ROPE_DIM // 2
    inv_freq = 1.0 / (ROPE_THETA ** (jnp.arange(half, dtype=F32) * 2.0 / ROPE_DIM))
    ang = jnp.arange(S, dtype=F32)[:, None] * inv_freq[None, :]
    cos, sin = jnp.cos(ang), jnp.sin(ang)
    zeros = jnp.zeros((S, A_HEAD_DIM - ROPE_DIM), F32)
    z8 = jnp.zeros((S, half), F32)
    cos_h = jnp.concatenate([cos, cos, jnp.ones_like(zeros)], axis=-1)
    slo_h = jnp.concatenate([-sin, z8, zeros], axis=-1)
    shi_h = jnp.concatenate([z8, sin, zeros], axis=-1)
    rep = LANES // A_HEAD_DIM
    return tuple(jnp.tile(t, (1, rep)) for t in (cos_h, slo_h, shi_h))


def _mixing(x, mod, p, tabs):
    B, S, D = x.shape
    mod = mod.reshape(B, 6, D)
    q, kx, vx, mq, mk, vt, og, bg, grow = _in_proj(
        x, mod, p["norm1_g"], p["w_main"], p["w_t"], p["m_gate_b"], tabs, tm=min(IN_PROJ_TILE, S))
    attn = _attention(q, kx, vx, p["attn_sink"])
    mo = _mlstm(mq, mk, vt, og, grow, p["conv_wq"], p["conv_wk"], p["head_norm_g_t"])
    x1, ha, hb, rt, ids = _merge(attn, mo, bg, x, mod, p["w_up_attn"], p["w_up_mlstm"], p["w_out"], p["norm2_g"],
                                 p["w_router"], p["b_router"], tm=min(MERGE_TILE, S))
    return dict(x1=x1, rt=rt, mod=mod, routed=_moe_dispatch(ha, hb, ids))


def kernel(x_prompt, x_sample, c_prompt, c_sample, ada_w, ada_b, norm1_g, w_in, conv_w, m_gate_b, attn_sink,
           head_norm_g, w_up_attn, w_up_mlstm, w_out, norm2_g, rg_w, rg_b, re_w, re_b, w_gate, w_up, w_down,
           final_norm_g):
    assert ada_w.shape[0] == 1, "single-layer trunk"
    w_in0 = w_in[0]
    w_g = w_in0[:, OFF_MG:OFF_BG]
    pad = LANES - N_EXPERTS - N_GROUPS
    p = dict(
        ada_w=ada_w[0], ada_b=ada_b[0], norm1_g=norm1_g[0],
        w_main=jnp.concatenate([w_in0[:, :OFF_MV], w_in0[:, OFF_MO:OFF_MG], w_in0[:, OFF_BG:]], axis=1).astype(BF16),
        w_t=jnp.concatenate([w_in0[:, OFF_MV:OFF_MO], _gates_head_major(w_g)], axis=1).T.astype(BF16),
        m_gate_b=_gates_head_major(m_gate_b[0]), attn_sink=attn_sink[0],
        conv_wq=conv_w[0, :, :M_WIDTH], conv_wk=conv_w[0, :, M_WIDTH:],
        head_norm_g_t=jnp.broadcast_to(head_norm_g[0][:, None], (M_WIDTH, LANES)),
        w_up_attn=w_up_attn[0].astype(BF16), w_up_mlstm=w_up_mlstm[0].astype(BF16), w_out=w_out[0].astype(BF16),
        norm2_g=norm2_g[0],
        w_router=jnp.pad(jnp.concatenate([re_w[0], rg_w[0]], axis=1), ((0, 0), (0, pad))),
        b_router=jnp.pad(jnp.concatenate([re_b[0], rg_b[0]]), (0, pad)).reshape(1, LANES),
        w_gate=w_gate[0], w_up=w_up[0], w_down=w_down[0],
        final_norm_g=final_norm_g,
    )
    tabs = _rope_tables(x_prompt.shape[1])
    nbp = x_prompt.shape[0]
    mod = _ada_mod(jnp.concatenate([c_prompt, c_sample], axis=0), p["ada_w"], p["ada_b"])
    xs, mods = [x_prompt, x_sample], [mod[:nbp], mod[nbp:]]

    def experts(g):
        xa, xb, _, bexp, nval = g["routed"]
        return _experts(xa, xb, bexp, nval, p["w_gate"], p["w_up"], p["w_down"], EXPERT_ROWS)

    first, second = sorted(range(2), key=lambda i: -xs[i].shape[0])
    groups = {first: _mixing(xs[first], mods[first], p, tabs)}
    groups[first]["x1"], x_second = lax.optimization_barrier((groups[first]["x1"], xs[second]))
    groups[second] = _mixing(x_second, mods[second], p, tabs)
    ys = {first: experts(groups[first])}
    ys[first], groups[second]["routed"] = lax.optimization_barrier((ys[first], groups[second]["routed"]))
    ys[second] = experts(groups[second])
    outs = [_moe_finish(*ys[i], groups[i]["routed"][2], groups[i]["rt"], groups[i]["x1"], groups[i]["mod"],
                        p["final_norm_g"]) for i in range(2)]
    return tuple(outs)
```

```python
import jax
import jax.numpy as jnp
from jax import lax
from jax.experimental import pallas as pl
from jax.experimental.pallas import tpu as pltpu
from jax.experimental.pallas import tpu_sc as plsc

D_MODEL = 1024
A_HEADS = 8
A_KV_HEADS = 2
A_GROUP = A_HEADS // A_KV_HEADS
A_HEAD_DIM = 64
A_WIDTH = A_HEADS * A_HEAD_DIM
A_KV_WIDTH = A_KV_HEADS * A_HEAD_DIM
WINDOW = 128
BLOCK = 128
ROPE_DIM = A_HEAD_DIM // 4
ROPE_THETA = 500000.0
M_HEADS = 4
M_HEAD_DIM = 128
M_WIDTH = M_HEADS * M_HEAD_DIM
M_CHUNK = 128
CONV_W = 3
OFF_AQ = 0
OFF_AK = OFF_AQ + A_WIDTH
OFF_AV = OFF_AK + A_KV_WIDTH
OFF_MQ = OFF_AV + A_KV_WIDTH
OFF_MK = OFF_MQ + M_WIDTH
OFF_MV = OFF_MK + M_WIDTH
OFF_MO = OFF_MV + M_WIDTH
OFF_MG = OFF_MO + M_WIDTH
N_MGATES = 4 * M_HEADS
OFF_BG = OFF_MG + N_MGATES
IN_TOTAL = OFF_BG + 2 * D_MODEL
N_GROUPS = 4
EXPERTS_PER_GROUP = 4
N_EXPERTS = N_GROUPS * EXPERTS_PER_GROUP
D_EXPERT = 512
EPS = 1e-6
NEG_BIG = -1e30
F32 = jnp.float32
BF16 = jnp.bfloat16

LANES = 128
SUBLANES = 8
VMEM_LIMIT_BYTES = 56 * 1024 * 1024

MAIN_AQ = 0
MAIN_KV = A_WIDTH
MAIN_MQ = MAIN_KV + 2 * A_KV_WIDTH
MAIN_MK = MAIN_MQ + M_WIDTH
MAIN_MO = MAIN_MK + M_WIDTH
MAIN_BG = MAIN_MO + M_WIDTH
MAIN_TOTAL = MAIN_BG + 2 * D_MODEL

LOG2E = 1.4426950408889634
Q_SCALE = A_HEAD_DIM ** -0.5 * LOG2E
ROUTER_G_LANE = N_EXPERTS
ROUTE_W = 4
PACKED = jnp.uint32
PACK_W = D_MODEL // 4
SC_WINDOW = 128
EXPERT_ROWS = 512
IN_PROJ_TILE = 512
MERGE_TILE = 1024
MERGE_ROWS = 256
FINISH_TILE = 1024
GATE_ROWS = M_HEADS * SUBLANES


def _sigmoid(z):
    return 1.0 / (1.0 + jnp.exp2(z * (-LOG2E)))


def _log_sigmoid(z):
    return jnp.minimum(z, 0.0) - jnp.log(1.0 + jnp.exp(-jnp.abs(z)))


def _gates_head_major(a):
    lead = a.shape[:-1]
    a = jnp.swapaxes(a.reshape(*lead, 4, M_HEADS), -1, -2)
    a = jnp.pad(a, [(0, 0)] * (a.ndim - 1) + [(0, SUBLANES - 4)])
    return a.reshape(*lead, GATE_ROWS)


def _dot(a, b):
    return jnp.dot(a, b, preferred_element_type=F32)


def _dot_nt(a, b):
    return lax.dot_general(a, b, (((1,), (1,)), ((), ())), preferred_element_type=F32)


def _dot_f32(a, b):
    return jnp.dot(a, b, preferred_element_type=F32, precision=lax.Precision.HIGHEST)


def _compiler_params(semantics):
    return pltpu.CompilerParams(dimension_semantics=semantics, vmem_limit_bytes=VMEM_LIMIT_BYTES)


def _ada_kernel(c_ref, w_ref, b_ref, o_ref):
    c = c_ref[...]
    o_ref[...] = _dot_f32(c * _sigmoid(c), w_ref[...]) + b_ref[...]


def _ada_mod(c, ada_w, ada_b):
    B, D = c.shape
    n = ada_w.shape[1] // D
    return pl.pallas_call(
        _ada_kernel,
        grid=(n,),
        in_specs=[
            pl.BlockSpec((B, D), lambda j: (0, 0)),
            pl.BlockSpec((D, D), lambda j: (0, j)),
            pl.BlockSpec((1, D), lambda j: (0, j)),
        ],
        out_specs=pl.BlockSpec((B, D), lambda j: (0, j)),
        out_shape=jax.ShapeDtypeStruct((B, n * D), F32),
        compiler_params=_compiler_params(("arbitrary",)),
        name="ada_mod",
    )(c, ada_w, ada_b.reshape(1, -1))


def _rms_mod(x, g, scale, shift):
    ms = jnp.mean(x * x, axis=-1, keepdims=True)
    return (x * lax.rsqrt(ms + EPS) * g) * (1.0 + scale) + shift


def _rope_block(xb, cos, sin_lo, sin_hi):
    half = ROPE_DIM // 2
    return xb * cos + pltpu.roll(xb, LANES - half, axis=1) * sin_lo + pltpu.roll(xb, half, axis=1) * sin_hi


def _in_proj_kernel(x_ref, mod_ref, g_ref, w_ref, wt_ref, gb_col_ref, cos_ref, slo_ref, shi_ref,
                    q_ref, kx_ref, vx_ref, mq_ref, mk_ref, vt_ref, og_ref, bg_ref, grow_ref):
    x = x_ref[0]
    tm = x.shape[0]
    h = _rms_mod(x, g_ref[...], mod_ref[0, 1:2, :], mod_ref[0, 0:1, :])
    hb = h.astype(BF16)
    cos, slo, shi = cos_ref[...], slo_ref[...], shi_ref[...]
    lane = lax.broadcasted_iota(jnp.int32, (tm, LANES), 1)
    left = lane < A_HEAD_DIM

    pj = _dot(hb, w_ref[:, MAIN_AQ:MAIN_AQ + A_WIDTH])
    for j in range(A_WIDTH // LANES):
        blk = _rope_block(pj[:, j * LANES:(j + 1) * LANES], cos, slo, shi)
        q_ref[0, :, j * LANES:(j + 1) * LANES] = (blk * Q_SCALE).astype(BF16)

    pj = _dot(hb, w_ref[:, MAIN_KV:MAIN_KV + 2 * A_KV_WIDTH])
    kk = _rope_block(pj[:, 0:LANES], cos, slo, shi)
    vv = pj[:, LANES:2 * LANES]
    for src, dst in ((kk, kx_ref), (vv, vx_ref)):
        swapped = pltpu.roll(src, A_HEAD_DIM, axis=1)
        zero = jnp.zeros_like(src)
        dst[0, :, 0 * LANES:1 * LANES] = jnp.where(left, src, zero).astype(BF16)
        dst[0, :, 1 * LANES:2 * LANES] = jnp.where(left, zero, swapped).astype(BF16)
        dst[0, :, 2 * LANES:3 * LANES] = jnp.where(left, swapped, zero).astype(BF16)
        dst[0, :, 3 * LANES:4 * LANES] = jnp.where(left, zero, src).astype(BF16)

    mq_ref[0] = _dot(hb, w_ref[:, MAIN_MQ:MAIN_MQ + M_WIDTH]).astype(BF16)
    mk_ref[0] = _dot(hb, w_ref[:, MAIN_MK:MAIN_MK + M_WIDTH]).astype(BF16)
    og_ref[0] = _sigmoid(_dot(hb, w_ref[:, MAIN_MO:MAIN_MO + M_WIDTH])).astype(BF16)
    for j in range(2 * D_MODEL // M_WIDTH):
        lo = MAIN_BG + j * M_WIDTH
        bg_ref[0, :, j * M_WIDTH:(j + 1) * M_WIDTH] = _sigmoid(_dot(hb, w_ref[:, lo:lo + M_WIDTH])).astype(BF16)

    tr = _dot_nt(wt_ref[...], hb)
    gr = tr[M_WIDTH:M_WIDTH + GATE_ROWS, :] + gb_col_ref[...]
    gtype = jnp.bitwise_and(lax.broadcasted_iota(jnp.int32, gr.shape, 0), SUBLANES - 1)
    gr = jnp.where(gtype == 1, _log_sigmoid(gr), jnp.where(gtype == 3, _log_sigmoid(gr), gr))
    vt = tr[0:M_WIDTH, :].astype(BF16)
    for c in range(tm // M_CHUNK):
        grow_ref[0, c] = gr[:, c * M_CHUNK:(c + 1) * M_CHUNK]
        vt_ref[0, c] = vt[:, c * M_CHUNK:(c + 1) * M_CHUNK]


def _in_proj(x, mod, norm_g, w_main, w_t, gate_b, rope_tabs, tm):
    B, S, D = x.shape
    nt = S // tm
    cos, slo, shi = rope_tabs
    tok = lambda w: pl.BlockSpec((1, tm, w), lambda b, i: (b, i, 0))
    const2 = lambda a: pl.BlockSpec(a.shape, lambda b, i: (0, 0))
    tab = pl.BlockSpec((tm, LANES), lambda b, i: (i, 0))
    out_shapes = (
        jax.ShapeDtypeStruct((B, S, A_WIDTH), BF16),
        jax.ShapeDtypeStruct((B, S, 4 * LANES), BF16),
        jax.ShapeDtypeStruct((B, S, 4 * LANES), BF16),
        jax.ShapeDtypeStruct((B, S, M_WIDTH), BF16),
        jax.ShapeDtypeStruct((B, S, M_WIDTH), BF16),
        jax.ShapeDtypeStruct((B, S // M_CHUNK, M_WIDTH, M_CHUNK), BF16),
        jax.ShapeDtypeStruct((B, S, M_WIDTH), BF16),
        jax.ShapeDtypeStruct((B, S, 2 * D_MODEL), BF16),
        jax.ShapeDtypeStruct((B, S // M_CHUNK, GATE_ROWS, M_CHUNK), F32),
    )
    chunked = lambda rows: pl.BlockSpec((1, tm // M_CHUNK, rows, M_CHUNK), lambda b, i: (b, i, 0, 0))
    out_specs = (
        tok(A_WIDTH), tok(4 * LANES), tok(4 * LANES), tok(M_WIDTH), tok(M_WIDTH), chunked(M_WIDTH), tok(M_WIDTH),
        tok(2 * D_MODEL), chunked(GATE_ROWS),
    )
    gb_col = gate_b.reshape(GATE_ROWS, 1)
    g2 = norm_g.reshape(1, D)
    return pl.pallas_call(
        _in_proj_kernel,
        grid=(B, nt),
        in_specs=[
            tok(D),
            pl.BlockSpec((1, 6, D), lambda b, i: (b, 0, 0)),
            const2(g2), const2(w_main), const2(w_t), const2(gb_col),
            tab, tab, tab,
        ],
        out_specs=out_specs,
        out_shape=out_shapes,
        compiler_params=_compiler_params(("parallel", "parallel")),
        name="in_proj",
    )(x, mod, g2, w_main, w_t, gb_col, cos, slo, shi)


def _attn_kernel(sink_ref, q_ref, kx_ref, vx_ref, o_ref):
    S = q_ref.shape[1]
    nb = S // BLOCK
    kw = 3 * BLOCK
    qi = lax.broadcasted_iota(jnp.int32, (BLOCK, kw), 0)
    ki = lax.broadcasted_iota(jnp.int32, (BLOCK, kw), 1)
    rel0 = ki - qi
    ones_b = jnp.ones((kw, LANES), BF16)
    left = lax.broadcasted_iota(jnp.int32, (BLOCK, LANES), 1) < A_HEAD_DIM

    def block(n, carry):
        q0 = pl.multiple_of(n * BLOCK, BLOCK)
        k0 = pl.multiple_of(jnp.clip((n - 1) * BLOCK, 0, S - kw), BLOCK)
        rel = rel0 + (k0 - q0)
        valid = jnp.abs(rel) <= WINDOW
        scores = []
        for hk in range(A_KV_HEADS):
            for j in range(A_GROUP // 2):
                col = (hk * (A_GROUP // 2) + j) * LANES
                qp = q_ref[0, pl.ds(q0, BLOCK), col:col + LANES]
                for side in range(2):
                    kk = kx_ref[0, pl.ds(k0, kw), (2 * hk + side) * LANES:(2 * hk + side + 1) * LANES]
                    scores.append(_dot_nt(qp, kk))
        for hk in range(A_KV_HEADS):
            for j in range(A_GROUP // 2):
                col = (hk * (A_GROUP // 2) + j) * LANES
                outs, dens = [], []
                for side in range(2):
                    vv = vx_ref[0, pl.ds(k0, kw), (2 * hk + side) * LANES:(2 * hk + side + 1) * LANES]
                    sk = sink_ref[hk * A_GROUP + 2 * j + side] * LOG2E
                    s = jnp.where(valid, scores[(hk * (A_GROUP // 2) + j) * 2 + side], NEG_BIG)
                    m = jnp.maximum(jnp.max(s, axis=-1, keepdims=True), sk)
                    p = jnp.exp2(s - m).astype(BF16)
                    od = _dot(p, jnp.concatenate([vv, ones_b], axis=1))
                    outs.append(od[:, 0:LANES])
                    dens.append(od[:, LANES:2 * LANES] + jnp.exp2(sk - m))
                o = (outs[0] + outs[1]) / jnp.where(left, dens[0], dens[1])
                o_ref[0, pl.ds(q0, BLOCK), col:col + LANES] = o.astype(BF16)
        return carry

    lax.fori_loop(0, nb, block, 0, unroll=8)


def _attention(q, kx, vx, sink):
    B, S, _ = q.shape
    seq = lambda w: pl.BlockSpec((1, S, w), lambda b: (b, 0, 0))
    return pl.pallas_call(
        _attn_kernel,
        grid=(B,),
        in_specs=[pl.BlockSpec(memory_space=pltpu.SMEM), seq(A_WIDTH), seq(4 * LANES), seq(4 * LANES)],
        out_specs=seq(A_WIDTH),
        out_shape=jax.ShapeDtypeStruct((B, S, A_WIDTH), BF16),
        compiler_params=_compiler_params(("parallel",)),
        name="window_attn",
    )(sink, q, kx, vx)


def _conv_silu(u_ref, w_ref, pad_ref, dst_ref, scale):
    S = u_ref.shape[1]
    pad_ref[0:SUBLANES, :] = jnp.zeros((SUBLANES, LANES), F32)
    pad_ref[S + SUBLANES:S + 2 * SUBLANES, :] = jnp.zeros((SUBLANES, LANES), F32)
    pad_ref[SUBLANES:S + SUBLANES, :] = u_ref[0].astype(F32)
    w0, w1, w2 = w_ref[0:1, :], w_ref[1:2, :], w_ref[2:3, :]
    for c in range(S // M_CHUNK):
        base = SUBLANES + c * M_CHUNK
        y = (pad_ref[base - 1:base - 1 + M_CHUNK, :] * w0 + pad_ref[base:base + M_CHUNK, :] * w1
             + pad_ref[base + 1:base + 1 + M_CHUNK, :] * w2)
        y = y * _sigmoid(y)
        dst_ref[c * M_CHUNK:(c + 1) * M_CHUNK, :] = y if scale == 1.0 else y * scale


def _split3(x):
    hi = x.astype(BF16)
    r = x - hi.astype(F32)
    mid = r.astype(BF16)
    lo = (r - mid.astype(F32)).astype(BF16)
    return [hi, mid, lo]


def _mlstm_kernel(mq_ref, mk_ref, vt_ref, og_ref, grow_ref, cwq_ref, cwk_ref, hgt_ref, o_ref,
                  pad_ref, qs_ref, ks_ref, rb_ref, ib_ref, rows_ref, cinc_ref, ninc_ref, cin_ref, sin_ref,
                  c_ref, n_ref):
    S = mq_ref.shape[1]
    nc = S // M_CHUNK
    L = M_CHUNK
    DH = M_HEAD_DIM
    C_GROUP = 8 if nc % 8 == 0 else 1
    _conv_silu(mq_ref, cwq_ref, pad_ref, qs_ref, 1.0)
    _conv_silu(mk_ref, cwk_ref, pad_ref, ks_ref, M_HEAD_DIM ** -0.5)

    s_i = lax.broadcasted_iota(jnp.int32, (L, L), 0)
    t_i = lax.broadcasted_iota(jnp.int32, (L, L), 1)
    tris = (s_i <= t_i, s_i >= t_i)
    eye = s_i == t_i
    row8 = lax.broadcasted_iota(jnp.int32, (SUBLANES, L), 0)
    one_if = lambda cond: jnp.where(cond, 1.0, 0.0)
    k_j = lax.broadcasted_iota(jnp.int32, (3 * L, 2 * L), 0) % L
    c_j = lax.broadcasted_iota(jnp.int32, (3 * L, 2 * L), 1)
    sum_rows = jnp.where(c_j < L, one_if(k_j <= c_j), one_if(k_j >= c_j - L)).astype(BF16)

    g_all = grow_ref[0].reshape(nc * SUBLANES, L)
    rb_ref[...] = _dot(jnp.concatenate(_split3(g_all), axis=1), sum_rows).reshape(nc, SUBLANES, 2 * L)
    ones_b = jnp.ones((2 * L, L), BF16)

    def phase_a_operands(c):
        gr = grow_ref[0, c]
        rb = rb_ref[c]
        vt = vt_ref[0, c].astype(F32)
        wvs, wks, diags = [], [], []
        for d in range(2):
            brow = rb[2 * d + 1:2 * d + 2, d * L:(d + 1) * L]
            blast = brow[:, L - 1:L] if d == 0 else brow[:, 0:1]
            ibr = gr[2 * d:2 * d + 1, :] - brow
            log_g = blast + ibr
            mg = jnp.max(log_g, axis=-1, keepdims=True)
            wk = jnp.exp(log_g - mg)
            wvs.append((vt * wk).astype(BF16))
            wks.append(wk)
            diags.append(jnp.concatenate(
                [jnp.where(eye, term.astype(F32), 0.0).astype(BF16) for term in _split3(ibr * LOG2E)[:2]], axis=1))
            rows_ref[c, 2 + d:3 + d, :] = brow
            rows_ref[c, 4 + d:5 + d, :] = jnp.broadcast_to(mg, (1, L))
            rows_ref[c, 6 + d:7 + d, :] = jnp.broadcast_to(blast, (1, L))
        wk8 = jnp.where(row8 == 0, wks[0], jnp.where(row8 == 1, wks[1], 0.0))
        return jnp.concatenate(diags, axis=0), jnp.concatenate(wvs, axis=0), wk8.astype(BF16)

    def phase_a_products(c, diag, wv, wk8):
        kb = ks_ref[pl.ds(pl.multiple_of(c * L, L), L), :].astype(BF16)
        ib = _dot(diag, ones_b)
        ib_ref[c] = ib
        for d in range(2):
            rows_ref[c, d:d + 1, :] = jnp.max(jnp.where(tris[d], ib[d * L:(d + 1) * L, :], NEG_BIG),
                                              axis=0, keepdims=True)
        cinc_ref[c] = _dot(wv, kb)
        ninc_ref[c] = _dot(wk8, kb)

    def phase_a(g, carry):
        chunks = [g * C_GROUP + i for i in range(C_GROUP)]
        operands = [phase_a_operands(c) for c in chunks]
        for c, ops in zip(chunks, operands):
            phase_a_products(c, *ops)
        return carry

    def phase_b(j, ms):
        new_ms = []
        for d, cc in ((0, j), (1, nc - 1 - j)):
            m = ms[d]
            half = slice(d * DH, (d + 1) * DH)
            cst = c_ref[half, :]
            n = n_ref[d:d + 1, :]
            cin_ref[cc, half, :] = cst.astype(BF16)
            sin_ref[cc, d:d + 1, :] = n
            sin_ref[cc, 2 + d:3 + d, :] = m
            mg = rows_ref[cc, 4 + d:5 + d, :]
            blast = rows_ref[cc, 6 + d:7 + d, :]
            m_new = jnp.maximum(blast + m, mg)
            decay = jnp.exp(blast + m - m_new)
            grow = jnp.exp(mg - m_new)
            c_ref[half, :] = decay * cst + grow * cinc_ref[cc, half, :]
            n_ref[d:d + 1, :] = decay * n + grow * ninc_ref[cc, d:d + 1, :]
            new_ms.append(m_new)
        return tuple(new_ms)

    def phase_c_products(c):
        r0 = pl.multiple_of(c * L, L)
        qb = qs_ref[pl.ds(r0, L), :].astype(BF16)
        qk_t = _dot_nt(ks_ref[pl.ds(r0, L), :].astype(BF16), qb)
        qc_t = _dot_nt(cin_ref[c], qb)
        qn = _dot_nt(sin_ref[c].astype(BF16), qb)
        return qk_t, qc_t, qn

    def phase_c_finish(c, qk_t, qc_t, qn):
        r0 = pl.multiple_of(c * L, L)
        sin = sin_ref[c]
        rows = rows_ref[c]
        ats, stats = [], []
        for d in range(2):
            m_in = sin[2 + d:3 + d, :] * LOG2E
            cm = jnp.maximum(m_in, rows[d:d + 1, :])
            a_t = qk_t * jnp.where(tris[d], jnp.exp2(ib_ref[c, d * L:(d + 1) * L, :] - cm), 0.0)
            w_inter = jnp.exp2(m_in - cm)
            den = w_inter * qn[d:d + 1, :] + jnp.sum(a_t, axis=0, keepdims=True)
            m_t = rows[2 + d:3 + d, :] * LOG2E + cm
            ats.append(a_t.astype(BF16))
            stats.append((w_inter, jnp.maximum(jnp.abs(den), jnp.exp2(-m_t))))
        av_t = _dot(vt_ref[0, c], jnp.concatenate(ats, axis=1))
        h_t = None
        for d in range(2):
            w_inter, den = stats[d]
            hd = (w_inter * qc_t[d * DH:(d + 1) * DH, :] + av_t[:, d * L:(d + 1) * L]) / den
            h_t = hd if h_t is None else h_t + hd
        y_t = h_t * lax.rsqrt(jnp.mean(h_t * h_t, axis=0, keepdims=True) + EPS) * hgt_ref[...]
        o_ref[0, pl.ds(r0, L), :] = (og_ref[0, pl.ds(r0, L), :].astype(F32) * y_t.T).astype(BF16)

    def phase_c(g, carry):
        chunks = [g * C_GROUP + i for i in range(C_GROUP)]
        products = [phase_c_products(c) for c in chunks]
        for c, prod in zip(chunks, products):
            phase_c_finish(c, *prod)
        return carry

    lax.fori_loop(0, nc // C_GROUP, phase_a, 0, unroll=2)
    c_ref[...] = jnp.zeros(c_ref.shape, F32)
    n_ref[...] = jnp.zeros(n_ref.shape, F32)
    m0 = jnp.zeros((1, LANES), F32)
    lax.fori_loop(0, nc, phase_b, (m0, m0), unroll=True)
    lax.fori_loop(0, nc // C_GROUP, phase_c, 0, unroll=2)


def _mlstm(mq, mk, vt, og, grow_h, conv_wq, conv_wk, head_g_t):
    B, S, _ = mq.shape
    nc = S // M_CHUNK
    head = pl.BlockSpec((1, S, M_HEAD_DIM), lambda b, h: (b, 0, h))
    cw = pl.BlockSpec((CONV_W, M_HEAD_DIM), lambda b, h: (0, h))
    return pl.pallas_call(
        _mlstm_kernel,
        grid=(B, M_HEADS),
        in_specs=[
            head, head,
            pl.BlockSpec((1, nc, M_HEAD_DIM, M_CHUNK), lambda b, h: (b, 0, h, 0)),
            head,
            pl.BlockSpec((1, nc, SUBLANES, M_CHUNK), lambda b, h: (b, 0, h, 0)),
            cw, cw,
            pl.BlockSpec((M_HEAD_DIM, LANES), lambda b, h: (h, 0)),
        ],
        out_specs=head,
        out_shape=jax.ShapeDtypeStruct((B, S, M_WIDTH), BF16),
        scratch_shapes=[
            pltpu.VMEM((S + 2 * SUBLANES, LANES), F32),
            pltpu.VMEM((S, M_HEAD_DIM), F32),
            pltpu.VMEM((S, M_HEAD_DIM), F32),
            pltpu.VMEM((nc, SUBLANES, 2 * M_CHUNK), F32),
            pltpu.VMEM((nc, 2 * M_CHUNK, M_CHUNK), F32),
            pltpu.VMEM((nc, SUBLANES, M_CHUNK), F32),
            pltpu.VMEM((nc, 2 * M_HEAD_DIM, M_HEAD_DIM), F32),
            pltpu.VMEM((nc, SUBLANES, M_HEAD_DIM), F32),
            pltpu.VMEM((nc, 2 * M_HEAD_DIM, M_HEAD_DIM), BF16),
            pltpu.VMEM((nc, SUBLANES, M_HEAD_DIM), F32),
            pltpu.VMEM((2 * M_HEAD_DIM, M_HEAD_DIM), F32),
            pltpu.VMEM((SUBLANES, M_HEAD_DIM), F32),
        ],
        compiler_params=_compiler_params(("parallel", "parallel")),
        name="mlstm",
    )(mq, mk, vt, og, grow_h, conv_wq, conv_wk, head_g_t)


def _route(logits):
    lane = lax.broadcasted_iota(jnp.int32, logits.shape, 1)
    lane_f = lane.astype(F32)
    big = float(LANES)
    is_g = (lane >= ROUTER_G_LANE) & (lane < ROUTER_G_LANE + N_GROUPS)
    gl = jnp.where(is_g, logits, NEG_BIG)
    gmax = jnp.max(gl, axis=-1, keepdims=True)
    gsum = jnp.sum(jnp.where(is_g, jnp.exp(gl - gmax), 0.0), axis=-1, keepdims=True)
    p_grp = 1.0 / gsum
    grp = jnp.min(jnp.where(is_g & (gl == gmax), lane_f - ROUTER_G_LANE, big), axis=-1, keepdims=True)
    in_grp = (lane < N_EXPERTS) & (jnp.right_shift(lane, 2).astype(F32) == grp)
    el = jnp.where(in_grp, logits, NEG_BIG)
    v1 = jnp.max(el, axis=-1, keepdims=True)
    i1 = jnp.min(jnp.where(in_grp & (el == v1), lane_f, big), axis=-1, keepdims=True)
    rest = in_grp & (lane_f != i1)
    el2 = jnp.where(rest, logits, NEG_BIG)
    v2 = jnp.max(el2, axis=-1, keepdims=True)
    i2 = jnp.min(jnp.where(rest & (el2 == v2), lane_f, big), axis=-1, keepdims=True)
    e21 = jnp.exp(v2 - v1)
    w1 = p_grp / (1.0 + e21)
    w2 = p_grp * e21 / (1.0 + e21)
    return jnp.where(lane == 0, i1, jnp.where(lane == 1, i2, jnp.where(lane == 2, w1, w2)))


def _pack_bf16_pairs(lo, hi):
    bits = lambda v: lax.bitcast_convert_type(v.astype(BF16).astype(F32), PACKED)
    return jnp.right_shift(bits(lo), PACKED(16)) | (bits(hi) & PACKED(0xFFFF0000))


def _unpack_bf16_pairs(w):
    return (lax.bitcast_convert_type(jnp.left_shift(w, PACKED(16)), F32),
            lax.bitcast_convert_type(w & PACKED(0xFFFF0000), F32))


def _pack_row_halves(y):
    return (_pack_bf16_pairs(y[:, 0:PACK_W], y[:, PACK_W:2 * PACK_W]),
            _pack_bf16_pairs(y[:, 2 * PACK_W:3 * PACK_W], y[:, 3 * PACK_W:4 * PACK_W]))


def _unpack_row_halves(a, b):
    return [*_unpack_bf16_pairs(a), *_unpack_bf16_pairs(b)]


def _merge_kernel(attn_ref, mo_ref, bg_ref, x_ref, mod_ref, wua_ref, wum_ref, wo_ref, g2_ref, wr_ref, br_ref,
                  x1_ref, ha_ref, hb_ref, rt_ref, ids_ref):
    tm = x_ref.shape[1]
    nparts = tm // MERGE_ROWS
    parts = [pl.ds(i * MERGE_ROWS, MERGE_ROWS) for i in range(nparts)]
    wr = wr_ref[...].astype(BF16)
    merged, h2s = {}, {}
    for step in range(nparts + 2):
        if step < nparts:
            h = parts[step]
            up_a = _dot(attn_ref[0, h, :], wua_ref[...])
            up_m = _dot(mo_ref[0, h, :], wum_ref[...])
            merged[step] = (bg_ref[0, h, 0:D_MODEL].astype(F32) * up_a
                            + bg_ref[0, h, D_MODEL:2 * D_MODEL].astype(F32) * up_m).astype(BF16)
        if 0 <= step - 1 < nparts:
            h = parts[step - 1]
            x1 = x_ref[0, h, :] + mod_ref[0, 2:3, :] * _dot(merged.pop(step - 1), wo_ref[...])
            x1_ref[0, h, :] = x1
            h2s[step - 1] = _rms_mod(x1, g2_ref[...], mod_ref[0, 4:5, :], mod_ref[0, 3:4, :])
        if 0 <= step - 2 < nparts:
            h = parts[step - 2]
            h2 = h2s.pop(step - 2)
            logits = _dot(h2.astype(BF16), wr) + br_ref[...]
            ha_ref[0, h, :], hb_ref[0, h, :] = _pack_row_halves(h2)
            route = _route(logits)
            rt_ref[0, h, :] = route[:, 0:ROUTE_W]
            ids_ref[0, :, h] = route.T[0:SUBLANES, :]


def _merge(attn, mo, bg, x, mod, wua, wum, wo, norm_g, w_router, b_router, tm):
    B, S, D = x.shape
    tok = lambda w: pl.BlockSpec((1, tm, w), lambda b, i: (b, i, 0))
    const2 = lambda a: pl.BlockSpec(a.shape, lambda b, i: (0, 0))
    g2 = norm_g.reshape(1, D)
    return pl.pallas_call(
        _merge_kernel,
        grid=(B, S // tm),
        in_specs=[
            tok(A_WIDTH), tok(M_WIDTH), tok(2 * D), tok(D),
            pl.BlockSpec((1, 6, D), lambda b, i: (b, 0, 0)),
            const2(wua), const2(wum), const2(wo), const2(g2), const2(w_router), const2(b_router),
        ],
        out_specs=(tok(D), tok(PACK_W), tok(PACK_W), tok(ROUTE_W),
                   pl.BlockSpec((1, SUBLANES, tm), lambda b, i: (b, 0, i))),
        out_shape=(
            jax.ShapeDtypeStruct((B, S, D), F32),
            jax.ShapeDtypeStruct((B, S, PACK_W), PACKED),
            jax.ShapeDtypeStruct((B, S, PACK_W), PACKED),
            jax.ShapeDtypeStruct((B, S, ROUTE_W), F32),
            jax.ShapeDtypeStruct((B, SUBLANES, S), F32),
        ),
        compiler_params=_compiler_params(("parallel", "parallel")),
        name="merge_route",
    )(attn, mo, bg, x, mod, wua, wum, wo, g2, w_router, b_router)


def _route_tables(ea, tmb):
    n = ea.shape[0]
    T = n // 2
    nblk = 2 * T // tmb + N_EXPERTS
    onehot = ea[None, :] == jnp.arange(N_EXPERTS, dtype=jnp.int32)[:, None]
    pieces = onehot.reshape(N_EXPERTS, n // LANES, LANES).astype(BF16)
    upto = (jnp.arange(LANES)[:, None] <= jnp.arange(LANES)[None, :]).astype(BF16)
    within = jnp.einsum("eps,st->ept", pieces, upto, preferred_element_type=F32).astype(jnp.int32)
    totals = within[:, :, -1]
    before = jnp.cumsum(totals, axis=1) - totals
    csum = (within + before[:, :, None]).reshape(N_EXPERTS, n)
    counts = before[:, -1] + totals[:, -1]
    padded = ((counts + tmb - 1) // tmb) * tmb
    ends = jnp.cumsum(padded)
    starts = ends - padded
    pos = jnp.sum(jnp.where(onehot, csum - 1 + starts[:, None], 0), axis=0).astype(jnp.int32)
    blk0 = jnp.arange(nblk, dtype=jnp.int32) * tmb
    bexp = jnp.minimum(jnp.sum((blk0[:, None] >= ends[None, :]).astype(jnp.int32), axis=1), N_EXPERTS - 1)
    nval = jnp.clip(starts[bexp] + counts[bexp] - blk0, 0, tmb)
    nval = jnp.where(blk0 < ends[-1], nval, 0).astype(jnp.int32)
    return pos, bexp.astype(jnp.int32), nval


def _sc_mesh():
    return plsc.VectorSubcoreMesh(core_axis_name="c", subcore_axis_name="s")


def _sc_dispatch(x, pos, n_rows):
    T = x.shape[0]
    nb = T // SC_WINDOW
    idx = pos.reshape(1, 2 * T)

    @pl.kernel(out_type=jax.ShapeDtypeStruct((n_rows, PACK_W), x.dtype), mesh=_sc_mesh(), scratch_types=[])
    def dispatch(x_hbm, i_hbm, o_hbm):
        def body(x_vmem, i_vmem):
            pltpu.sync_copy(x_vmem, o_hbm.at[i_vmem.at[0]])

        pltpu.emit_pipeline(
            body,
            grid=(2 * nb,),
            in_specs=[pl.BlockSpec((SC_WINDOW, PACK_W), index_map=lambda i: (i % nb, 0)),
                      pl.BlockSpec((1, SC_WINDOW), index_map=lambda i: (0, i))],
            out_specs=[],
            core_axis_name=("c", "s"),
            dimension_semantics=(pltpu.PARALLEL,),
        )(x_hbm, i_hbm)

    return dispatch(x, idx)


def _sc_combine(y, pos):
    n = pos.shape[0]
    idx = pos.reshape(1, n)

    @pl.kernel(out_type=jax.ShapeDtypeStruct((n, PACK_W), y.dtype), mesh=_sc_mesh(), scratch_types=[])
    def combine(y_hbm, i_hbm, o_hbm):
        def body(i_vmem, o_vmem):
            pltpu.sync_copy(y_hbm.at[i_vmem.at[0]], o_vmem)

        pltpu.emit_pipeline(
            body,
            grid=(n // SC_WINDOW,),
            in_specs=[pl.BlockSpec((1, SC_WINDOW), index_map=lambda i: (0, i))],
            out_specs=[pl.BlockSpec((SC_WINDOW, PACK_W), index_map=lambda i: (i, 0))],
            core_axis_name=("c", "s"),
            dimension_semantics=(pltpu.PARALLEL,),
        )(i_hbm, o_hbm)

    return combine(y, idx)


def _experts_kernel(bexp_ref, nval_ref, xa_ref, xb_ref, wg_ref, wu_ref, wd_ref, ya_ref, yb_ref, wg_s, wu_s, wd_s):
    i = pl.program_id(0)
    nv = nval_ref[i]
    new_expert = (i == 0) | (bexp_ref[i] != bexp_ref[jnp.maximum(i - 1, 0)])

    @pl.when((nv > 0) & new_expert)
    def _():
        wg_s[...] = wg_ref[0].astype(BF16)
        wu_s[...] = wu_ref[0].astype(BF16)
        wd_s[...] = wd_ref[0].astype(BF16)

    @pl.when(nv > 0)
    def _():
        keep = lax.broadcasted_iota(jnp.int32, xa_ref.shape, 0) < nv
        pieces = [jnp.where(keep, piece, 0.0).astype(BF16) for piece in _unpack_row_halves(xa_ref[...], xb_ref[...])]
        x = jnp.concatenate(pieces, axis=1)
        half = x.shape[0] // 2
        gus = [(_dot(x[r:r + half], wg_s[...]), _dot(x[r:r + half], wu_s[...])) for r in (0, half)]
        hes = [((g * _sigmoid(g)) * u).astype(BF16) for g, u in gus]
        for r, he in zip((0, half), hes):
            ya_ref[r:r + half, :], yb_ref[r:r + half, :] = _pack_row_halves(_dot(he, wd_s[...]))


def _experts(xa, xb, bexp, nval, wg, wu, wd, tmb):
    n = xa.shape[0]
    row = lambda: pl.BlockSpec((tmb, PACK_W), lambda i, be, nv: (i, 0))
    grid_spec = pltpu.PrefetchScalarGridSpec(
        num_scalar_prefetch=2,
        grid=(n // tmb,),
        in_specs=[row(), row(),
                  pl.BlockSpec((1, D_MODEL, D_EXPERT), lambda i, be, nv: (be[i], 0, 0)),
                  pl.BlockSpec((1, D_MODEL, D_EXPERT), lambda i, be, nv: (be[i], 0, 0)),
                  pl.BlockSpec((1, D_EXPERT, D_MODEL), lambda i, be, nv: (be[i], 0, 0))],
        out_specs=[row(), row()],
        scratch_shapes=[pltpu.VMEM((D_MODEL, D_EXPERT), BF16), pltpu.VMEM((D_MODEL, D_EXPERT), BF16),
                        pltpu.VMEM((D_EXPERT, D_MODEL), BF16)],
    )
    out = jax.ShapeDtypeStruct((n, PACK_W), xa.dtype)
    return pl.pallas_call(
        _experts_kernel, grid_spec=grid_spec, out_shape=(out, out),
        compiler_params=_compiler_params(("arbitrary",)),
        name="experts",
    )(bexp, nval, xa, xb, wg, wu, wd)


def _finish_kernel(a0_ref, b0_ref, a1_ref, b1_ref, rt_ref, x1_ref, mod_ref, fg_ref, o_ref):
    y0 = jnp.concatenate(_unpack_row_halves(a0_ref[...], b0_ref[...]), axis=1)
    y1 = jnp.concatenate(_unpack_row_halves(a1_ref[...], b1_ref[...]), axis=1)
    rt = rt_ref[0]
    moe = rt[:, 2:3] * y0 + rt[:, 3:4] * y1
    xo = x1_ref[0] + mod_ref[0, 5:6, :] * moe
    ms = jnp.mean(xo * xo, axis=-1, keepdims=True)
    o_ref[0] = xo * lax.rsqrt(ms + EPS) * fg_ref[...]


def _finish(ca, cb, rt, x1, mod, final_g, tm):
    B, S, D = x1.shape
    nt = S // tm
    second = B * nt
    half = lambda k: pl.BlockSpec((tm, PACK_W), lambda b, i: (k * second + b * nt + i, 0))
    tok = lambda w: pl.BlockSpec((1, tm, w), lambda b, i: (b, i, 0))
    fg = final_g.reshape(1, D)
    return pl.pallas_call(
        _finish_kernel,
        grid=(B, nt),
        in_specs=[half(0), half(0), half(1), half(1), tok(ROUTE_W), tok(D),
                  pl.BlockSpec((1, 6, D), lambda b, i: (b, 0, 0)),
                  pl.BlockSpec((1, D), lambda b, i: (0, 0))],
        out_specs=tok(D),
        out_shape=jax.ShapeDtypeStruct((B, S, D), F32),
        compiler_params=_compiler_params(("parallel", "parallel")),
        name="finish",
    )(ca, cb, ca, cb, rt, x1, mod, fg)


def _moe_tables(ids):
    B, _, S = ids.shape
    ea = ids[:, 0:2, :].astype(jnp.int32).transpose(1, 0, 2).reshape(2 * B * S)
    return _route_tables(ea, EXPERT_ROWS)


def _moe_dispatch(ha, hb, tables):
    B, S, _ = ha.shape
    T = B * S
    pos, bexp, nval = tables
    n_rows = 2 * T + N_EXPERTS * EXPERT_ROWS
    xa = _sc_dispatch(ha.reshape(T, PACK_W), pos, n_rows)
    xb = _sc_dispatch(hb.reshape(T, PACK_W), pos, n_rows)
    return xa, xb, pos, bexp, nval


def _moe_finish(ya, yb, pos, rt, x1, mod, final_g):
    S = x1.shape[1]
    return _finish(_sc_combine(ya, pos), _sc_combine(yb, pos), rt, x1, mod, final_g, tm=min(FINISH_TILE, S))


def _rope_tables(S):
    half = ROPE_DIM // 2
    inv_freq = 1.0 / (ROPE_THETA ** (jnp.arange(half, dtype=F32) * 2.0 / ROPE_DIM))
    ang = jnp.arange(S, dtype=F32)[:, None] * inv_freq[None, :]
    cos, sin = jnp.cos(ang), jnp.sin(ang)
    zeros = jnp.zeros((S, A_HEAD_DIM - ROPE_DIM), F32)
    z8 = jnp.zeros((S, half), F32)
    cos_h = jnp.concatenate([cos, cos, jnp.ones_like(zeros)], axis=-1)
    slo_h = jnp.concatenate([-sin, z8, zeros], axis=-1)
    shi_h = jnp.concatenate([z8, sin, zeros], axis=-1)
    rep = LANES // A_HEAD_DIM
    return tuple(jnp.tile(t, (1, rep)) for t in (cos_h, slo_h, shi_h))


def _mixing(x, mod, p, tabs):
    B, S, D = x.shape
    mod = mod.reshape(B, 6, D)
    q, kx, vx, mq, mk, vt, og, bg, grow = _in_proj(
        x, mod, p["norm1_g"], p["w_main"], p["w_t"], p["m_gate_b"], tabs, tm=min(IN_PROJ_TILE, S))
    attn = _attention(q, kx, vx, p["attn_sink"])
    mo = _mlstm(mq, mk, vt, og, grow, p["conv_wq"], p["conv_wk"], p["head_norm_g_t"])
    x1, ha, hb, rt, ids = _merge(attn, mo, bg, x, mod, p["w_up_attn"], p["w_up_mlstm"], p["w_out"], p["norm2_g"],
                                 p["w_router"], p["b_router"], tm=min(MERGE_TILE, S))
    return dict(x1=x1, rt=rt, mod=mod, ha=ha, hb=hb, tables=_moe_tables(ids))


def kernel(x_prompt, x_sample, c_prompt, c_sample, ada_w, ada_b, norm1_g, w_in, conv_w, m_gate_b, attn_sink,
           head_norm_g, w_up_attn, w_up_mlstm, w_out, norm2_g, rg_w, rg_b, re_w, re_b, w_gate, w_up, w_down,
           final_norm_g):
    assert ada_w.shape[0] == 1, "single-layer trunk"
    w_in0 = w_in[0]
    w_g = w_in0[:, OFF_MG:OFF_BG]
    pad = LANES - N_EXPERTS - N_GROUPS
    p = dict(
        ada_w=ada_w[0], ada_b=ada_b[0], norm1_g=norm1_g[0],
        w_main=jnp.concatenate([w_in0[:, :OFF_MV], w_in0[:, OFF_MO:OFF_MG], w_in0[:, OFF_BG:]], axis=1).astype(BF16),
        w_t=jnp.concatenate([w_in0[:, OFF_MV:OFF_MO], _gates_head_major(w_g)], axis=1).T.astype(BF16),
        m_gate_b=_gates_head_major(m_gate_b[0]), attn_sink=attn_sink[0],
        conv_wq=conv_w[0, :, :M_WIDTH], conv_wk=conv_w[0, :, M_WIDTH:],
        head_norm_g_t=jnp.broadcast_to(head_norm_g[0][:, None], (M_WIDTH, LANES)),
        w_up_attn=w_up_attn[0].astype(BF16), w_up_mlstm=w_up_mlstm[0].astype(BF16), w_out=w_out[0].astype(BF16),
        norm2_g=norm2_g[0],
        w_router=jnp.pad(jnp.concatenate([re_w[0], rg_w[0]], axis=1), ((0, 0), (0, pad))),
        b_router=jnp.pad(jnp.concatenate([re_b[0], rg_b[0]]), (0, pad)).reshape(1, LANES),
        w_gate=w_gate[0], w_up=w_up[0], w_down=w_down[0],
        final_norm_g=final_norm_g,
    )
    tabs = _rope_tables(x_prompt.shape[1])
    nbp = x_prompt.shape[0]
    mod = _ada_mod(jnp.concatenate([c_prompt, c_sample], axis=0), p["ada_w"], p["ada_b"])
    xs, mods = [x_prompt, x_sample], [mod[:nbp], mod[nbp:]]

    def experts(g):
        xa, xb, _, bexp, nval = g["routed"]
        return _experts(xa, xb, bexp, nval, p["w_gate"], p["w_up"], p["w_down"], EXPERT_ROWS)

    first, second = sorted(range(2), key=lambda i: -xs[i].shape[0])
    groups = {first: _mixing(xs[first], mods[first], p, tabs)}
    (groups[first]["x1"], groups[first]["tables"]), x_second = lax.optimization_barrier(
        ((groups[first]["x1"], groups[first]["tables"]), xs[second]))
    groups[second] = _mixing(x_second, mods[second], p, tabs)
    for g in groups.values():
        g["routed"] = _moe_dispatch(g["ha"], g["hb"], g["tables"])
    ys = {first: experts(groups[first])}
    ys[first], groups[second]["routed"] = lax.optimization_barrier((ys[first], groups[second]["routed"]))
    ys[second] = experts(groups[second])
    outs = [_moe_finish(*ys[i], groups[i]["routed"][2], groups[i]["rt"], groups[i]["x1"], groups[i]["mod"],
                        p["final_norm_g"]) for i in range(2)]
    return tuple(outs)
```

```python
import jax
import jax.numpy as jnp
from jax import lax
from jax.experimental import pallas as pl
from jax.experimental.pallas import tpu as pltpu
from jax.experimental.pallas import tpu_sc as plsc

D_MODEL = 1024
A_HEADS = 8
A_KV_HEADS = 2
A_GROUP = A_HEADS // A_KV_HEADS
A_HEAD_DIM = 64
A_WIDTH = A_HEADS * A_HEAD_DIM
A_KV_WIDTH = A_KV_HEADS * A_HEAD_DIM
WINDOW = 128
BLOCK = 128
ROPE_DIM = A_HEAD_DIM // 4
ROPE_THETA = 500000.0
M_HEADS = 4
M_HEAD_DIM = 128
M_WIDTH = M_HEADS * M_HEAD_DIM
M_CHUNK = 128
CONV_W = 3
OFF_AQ = 0
OFF_AK = OFF_AQ + A_WIDTH
OFF_AV = OFF_AK + A_KV_WIDTH
OFF_MQ = OFF_AV + A_KV_WIDTH
OFF_MK = OFF_MQ + M_WIDTH
OFF_MV = OFF_MK + M_WIDTH
OFF_MO = OFF_MV + M_WIDTH
OFF_MG = OFF_MO + M_WIDTH
N_MGATES = 4 * M_HEADS
OFF_BG = OFF_MG + N_MGATES
IN_TOTAL = OFF_BG + 2 * D_MODEL
N_GROUPS = 4
EXPERTS_PER_GROUP = 4
N_EXPERTS = N_GROUPS * EXPERTS_PER_GROUP
D_EXPERT = 512
EPS = 1e-6
NEG_BIG = -1e30
F32 = jnp.float32
BF16 = jnp.bfloat16

LANES = 128
SUBLANES = 8
VMEM_LIMIT_BYTES = 56 * 1024 * 1024

MAIN_AQ = 0
MAIN_KV = A_WIDTH
MAIN_MQ = MAIN_KV + 2 * A_KV_WIDTH
MAIN_MK = MAIN_MQ + M_WIDTH

LOG2E = 1.4426950408889634
Q_SCALE = A_HEAD_DIM ** -0.5 * LOG2E
ROUTER_G_LANE = N_EXPERTS
ROUTE_W = 4
PACKED = jnp.uint32
PACK_W = D_MODEL // 4
SC_WINDOW = 128
EXPERT_ROWS = 512
IN_PROJ_TILE = 512
MERGE_TILE = 1024
MERGE_ROWS = 256
FINISH_TILE = 1024
GATE_ROWS = M_HEADS * SUBLANES


def _sigmoid(z):
    return 1.0 / (1.0 + jnp.exp2(z * (-LOG2E)))


def _log_sigmoid(z):
    return jnp.minimum(z, 0.0) - jnp.log(1.0 + jnp.exp(-jnp.abs(z)))


def _gates_head_major(a):
    lead = a.shape[:-1]
    a = jnp.swapaxes(a.reshape(*lead, 4, M_HEADS), -1, -2)
    a = jnp.pad(a, [(0, 0)] * (a.ndim - 1) + [(0, SUBLANES - 4)])
    return a.reshape(*lead, GATE_ROWS)


def _dot(a, b):
    return jnp.dot(a, b, preferred_element_type=F32)


def _dot_nt(a, b):
    return lax.dot_general(a, b, (((1,), (1,)), ((), ())), preferred_element_type=F32)


def _dot_f32(a, b):
    return jnp.dot(a, b, preferred_element_type=F32, precision=lax.Precision.HIGHEST)


def _compiler_params(semantics):
    return pltpu.CompilerParams(dimension_semantics=semantics, vmem_limit_bytes=VMEM_LIMIT_BYTES)


def _ada_kernel(c_ref, w_ref, b_ref, o_ref):
    c = c_ref[...]
    o_ref[...] = _dot_f32(c * _sigmoid(c), w_ref[...]) + b_ref[...]


def _ada_mod(c, ada_w, ada_b):
    B, D = c.shape
    n = ada_w.shape[1] // D
    return pl.pallas_call(
        _ada_kernel,
        grid=(n,),
        in_specs=[
            pl.BlockSpec((B, D), lambda j: (0, 0)),
            pl.BlockSpec((D, D), lambda j: (0, j)),
            pl.BlockSpec((1, D), lambda j: (0, j)),
        ],
        out_specs=pl.BlockSpec((B, D), lambda j: (0, j)),
        out_shape=jax.ShapeDtypeStruct((B, n * D), F32),
        compiler_params=_compiler_params(("arbitrary",)),
        name="ada_mod",
    )(c, ada_w, ada_b.reshape(1, -1))


def _rms_mod(x, g, scale, shift):
    ms = jnp.mean(x * x, axis=-1, keepdims=True)
    return (x * lax.rsqrt(ms + EPS) * g) * (1.0 + scale) + shift


def _rope_block(xb, cos, sin_lo, sin_hi):
    half = ROPE_DIM // 2
    return xb * cos + pltpu.roll(xb, LANES - half, axis=1) * sin_lo + pltpu.roll(xb, half, axis=1) * sin_hi


def _in_proj_kernel(x_ref, mod_ref, g_ref, w_ref, wmo_ref, wbg_ref, wt_ref, gb_col_ref, cos_ref, slo_ref, shi_ref,
                    q_ref, kx_ref, vx_ref, mq_ref, mk_ref, vt_ref, og_ref, bg_ref, grow_ref):
    x = x_ref[0]
    tm = x.shape[0]
    h = _rms_mod(x, g_ref[...], mod_ref[0, 1:2, :], mod_ref[0, 0:1, :])
    hb = h.astype(BF16)
    cos, slo, shi = cos_ref[...], slo_ref[...], shi_ref[...]
    lane = lax.broadcasted_iota(jnp.int32, (tm, LANES), 1)
    left = lane < A_HEAD_DIM

    pj = _dot(hb, w_ref[:, MAIN_AQ:MAIN_AQ + A_WIDTH])
    for j in range(A_WIDTH // LANES):
        blk = _rope_block(pj[:, j * LANES:(j + 1) * LANES], cos, slo, shi)
        q_ref[0, :, j * LANES:(j + 1) * LANES] = (blk * Q_SCALE).astype(BF16)

    pj = _dot(hb, w_ref[:, MAIN_KV:MAIN_KV + 2 * A_KV_WIDTH])
    kk = _rope_block(pj[:, 0:LANES], cos, slo, shi)
    vv = pj[:, LANES:2 * LANES]
    for src, dst in ((kk, kx_ref), (vv, vx_ref)):
        swapped = pltpu.roll(src, A_HEAD_DIM, axis=1)
        zero = jnp.zeros_like(src)
        dst[0, :, 0 * LANES:1 * LANES] = jnp.where(left, src, zero).astype(BF16)
        dst[0, :, 1 * LANES:2 * LANES] = jnp.where(left, zero, swapped).astype(BF16)
        dst[0, :, 2 * LANES:3 * LANES] = jnp.where(left, swapped, zero).astype(BF16)
        dst[0, :, 3 * LANES:4 * LANES] = jnp.where(left, zero, src).astype(BF16)

    mq_ref[0] = _dot(hb, w_ref[:, MAIN_MQ:MAIN_MQ + M_WIDTH]).astype(BF16)
    mk_ref[0] = _dot(hb, w_ref[:, MAIN_MK:MAIN_MK + M_WIDTH]).astype(BF16)
    og_ref[0] = _sigmoid(_dot(hb, wmo_ref[...])).astype(BF16)
    for j in range(2 * D_MODEL // M_WIDTH):
        cols = slice(j * M_WIDTH, (j + 1) * M_WIDTH)
        bg_ref[0, :, cols] = _sigmoid(_dot(hb, wbg_ref[:, cols])).astype(BF16)

    tr = _dot_nt(wt_ref[...], hb)
    gr = tr[M_WIDTH:M_WIDTH + GATE_ROWS, :] + gb_col_ref[...]
    gtype = jnp.bitwise_and(lax.broadcasted_iota(jnp.int32, gr.shape, 0), SUBLANES - 1)
    gr = jnp.where(gtype == 1, _log_sigmoid(gr), jnp.where(gtype == 3, _log_sigmoid(gr), gr))
    vt = tr[0:M_WIDTH, :].astype(BF16)
    for c in range(tm // M_CHUNK):
        grow_ref[0, c] = gr[:, c * M_CHUNK:(c + 1) * M_CHUNK]
        vt_ref[0, c] = vt[:, c * M_CHUNK:(c + 1) * M_CHUNK]


def _in_proj(x, mod, norm_g, w_main, w_mo, w_bg, w_t, gate_b, rope_tabs, tm):
    B, S, D = x.shape
    nt = S // tm
    cos, slo, shi = rope_tabs
    tok = lambda w: pl.BlockSpec((1, tm, w), lambda b, i: (b, i, 0))
    const2 = lambda a: pl.BlockSpec(a.shape, lambda b, i: (0, 0))
    tab = pl.BlockSpec((tm, LANES), lambda b, i: (i, 0))
    out_shapes = (
        jax.ShapeDtypeStruct((B, S, A_WIDTH), BF16),
        jax.ShapeDtypeStruct((B, S, 4 * LANES), BF16),
        jax.ShapeDtypeStruct((B, S, 4 * LANES), BF16),
        jax.ShapeDtypeStruct((B, S, M_WIDTH), BF16),
        jax.ShapeDtypeStruct((B, S, M_WIDTH), BF16),
        jax.ShapeDtypeStruct((B, S // M_CHUNK, M_WIDTH, M_CHUNK), BF16),
        jax.ShapeDtypeStruct((B, S, M_WIDTH), BF16),
        jax.ShapeDtypeStruct((B, S, 2 * D_MODEL), BF16),
        jax.ShapeDtypeStruct((B, S // M_CHUNK, GATE_ROWS, M_CHUNK), F32),
    )
    chunked = lambda rows: pl.BlockSpec((1, tm // M_CHUNK, rows, M_CHUNK), lambda b, i: (b, i, 0, 0))
    out_specs = (
        tok(A_WIDTH), tok(4 * LANES), tok(4 * LANES), tok(M_WIDTH), tok(M_WIDTH), chunked(M_WIDTH), tok(M_WIDTH),
        tok(2 * D_MODEL), chunked(GATE_ROWS),
    )
    gb_col = gate_b.reshape(GATE_ROWS, 1)
    g2 = norm_g.reshape(1, D)
    return pl.pallas_call(
        _in_proj_kernel,
        grid=(B, nt),
        in_specs=[
            tok(D),
            pl.BlockSpec((1, 6, D), lambda b, i: (b, 0, 0)),
            const2(g2), const2(w_main), const2(w_mo), const2(w_bg), const2(w_t), const2(gb_col),
            tab, tab, tab,
        ],
        out_specs=out_specs,
        out_shape=out_shapes,
        compiler_params=_compiler_params(("parallel", "parallel")),
        name="in_proj",
    )(x, mod, g2, w_main, w_mo, w_bg, w_t, gb_col, cos, slo, shi)


def _attn_kernel(sink_ref, q_ref, kx_ref, vx_ref, o_ref):
    S = q_ref.shape[1]
    nb = S // BLOCK
    kw = 3 * BLOCK
    qi = lax.broadcasted_iota(jnp.int32, (BLOCK, kw), 0)
    ki = lax.broadcasted_iota(jnp.int32, (BLOCK, kw), 1)
    rel0 = ki - qi
    ones_b = jnp.ones((kw, LANES), BF16)
    left = lax.broadcasted_iota(jnp.int32, (BLOCK, LANES), 1) < A_HEAD_DIM

    def block(n, carry):
        q0 = pl.multiple_of(n * BLOCK, BLOCK)
        k0 = pl.multiple_of(jnp.clip((n - 1) * BLOCK, 0, S - kw), BLOCK)
        rel = rel0 + (k0 - q0)
        valid = jnp.abs(rel) <= WINDOW
        scores = []
        for hk in range(A_KV_HEADS):
            for j in range(A_GROUP // 2):
                col = (hk * (A_GROUP // 2) + j) * LANES
                qp = q_ref[0, pl.ds(q0, BLOCK), col:col + LANES]
                for side in range(2):
                    kk = kx_ref[0, pl.ds(k0, kw), (2 * hk + side) * LANES:(2 * hk + side + 1) * LANES]
                    scores.append(_dot_nt(qp, kk))
        for hk in range(A_KV_HEADS):
            for j in range(A_GROUP // 2):
                col = (hk * (A_GROUP // 2) + j) * LANES
                outs, dens = [], []
                for side in range(2):
                    vv = vx_ref[0, pl.ds(k0, kw), (2 * hk + side) * LANES:(2 * hk + side + 1) * LANES]
                    sk = sink_ref[hk * A_GROUP + 2 * j + side] * LOG2E
                    s = jnp.where(valid, scores[(hk * (A_GROUP // 2) + j) * 2 + side], NEG_BIG)
                    m = jnp.maximum(jnp.max(s, axis=-1, keepdims=True), sk)
                    p = jnp.exp2(s - m).astype(BF16)
                    od = _dot(p, jnp.concatenate([vv, ones_b], axis=1))
                    outs.append(od[:, 0:LANES])
                    dens.append(od[:, LANES:2 * LANES] + jnp.exp2(sk - m))
                o = (outs[0] + outs[1]) / jnp.where(left, dens[0], dens[1])
                o_ref[0, pl.ds(q0, BLOCK), col:col + LANES] = o.astype(BF16)
        return carry

    lax.fori_loop(0, nb, block, 0, unroll=8)


def _attention(q, kx, vx, sink):
    B, S, _ = q.shape
    seq = lambda w: pl.BlockSpec((1, S, w), lambda b: (b, 0, 0))
    return pl.pallas_call(
        _attn_kernel,
        grid=(B,),
        in_specs=[pl.BlockSpec(memory_space=pltpu.SMEM), seq(A_WIDTH), seq(4 * LANES), seq(4 * LANES)],
        out_specs=seq(A_WIDTH),
        out_shape=jax.ShapeDtypeStruct((B, S, A_WIDTH), BF16),
        compiler_params=_compiler_params(("parallel",)),
        name="window_attn",
    )(sink, q, kx, vx)


def _conv_silu(u_ref, w_ref, pad_ref, dst_ref, scale):
    S = u_ref.shape[1]
    pad_ref[0:SUBLANES, :] = jnp.zeros((SUBLANES, LANES), F32)
    pad_ref[S + SUBLANES:S + 2 * SUBLANES, :] = jnp.zeros((SUBLANES, LANES), F32)
    pad_ref[SUBLANES:S + SUBLANES, :] = u_ref[0].astype(F32)
    w0, w1, w2 = w_ref[0:1, :], w_ref[1:2, :], w_ref[2:3, :]
    for c in range(S // M_CHUNK):
        base = SUBLANES + c * M_CHUNK
        y = (pad_ref[base - 1:base - 1 + M_CHUNK, :] * w0 + pad_ref[base:base + M_CHUNK, :] * w1
             + pad_ref[base + 1:base + 1 + M_CHUNK, :] * w2)
        y = y * _sigmoid(y)
        dst_ref[c * M_CHUNK:(c + 1) * M_CHUNK, :] = y if scale == 1.0 else y * scale


def _split3(x):
    hi = x.astype(BF16)
    r = x - hi.astype(F32)
    mid = r.astype(BF16)
    lo = (r - mid.astype(F32)).astype(BF16)
    return [hi, mid, lo]


def _mlstm_kernel(mq_ref, mk_ref, vt_ref, og_ref, grow_ref, cwq_ref, cwk_ref, hgt_ref, o_ref,
                  pad_ref, qs_ref, ks_ref, rb_ref, ib_ref, rows_ref, cinc_ref, ninc_ref, cin_ref, sin_ref,
                  c_ref, n_ref):
    S = mq_ref.shape[1]
    nc = S // M_CHUNK
    L = M_CHUNK
    DH = M_HEAD_DIM
    C_GROUP = 8 if nc % 8 == 0 else 1
    _conv_silu(mq_ref, cwq_ref, pad_ref, qs_ref, 1.0)
    _conv_silu(mk_ref, cwk_ref, pad_ref, ks_ref, M_HEAD_DIM ** -0.5)

    s_i = lax.broadcasted_iota(jnp.int32, (L, L), 0)
    t_i = lax.broadcasted_iota(jnp.int32, (L, L), 1)
    tris = (s_i <= t_i, s_i >= t_i)
    eye = s_i == t_i
    row8 = lax.broadcasted_iota(jnp.int32, (SUBLANES, L), 0)
    one_if = lambda cond: jnp.where(cond, 1.0, 0.0)
    k_j = lax.broadcasted_iota(jnp.int32, (3 * L, 2 * L), 0) % L
    c_j = lax.broadcasted_iota(jnp.int32, (3 * L, 2 * L), 1)
    sum_rows = jnp.where(c_j < L, one_if(k_j <= c_j), one_if(k_j >= c_j - L)).astype(BF16)

    g_all = grow_ref[0].reshape(nc * SUBLANES, L)
    rb_ref[...] = _dot(jnp.concatenate(_split3(g_all), axis=1), sum_rows).reshape(nc, SUBLANES, 2 * L)
    ones_b = jnp.ones((2 * L, L), BF16)

    def phase_a_operands(c):
        gr = grow_ref[0, c]
        rb = rb_ref[c]
        vt = vt_ref[0, c].astype(F32)
        wvs, wks, diags = [], [], []
        for d in range(2):
            brow = rb[2 * d + 1:2 * d + 2, d * L:(d + 1) * L]
            blast = brow[:, L - 1:L] if d == 0 else brow[:, 0:1]
            ibr = gr[2 * d:2 * d + 1, :] - brow
            log_g = blast + ibr
            mg = jnp.max(log_g, axis=-1, keepdims=True)
            wk = jnp.exp(log_g - mg)
            wvs.append((vt * wk).astype(BF16))
            wks.append(wk)
            diags.append(jnp.concatenate(
                [jnp.where(eye, term.astype(F32), 0.0).astype(BF16) for term in _split3(ibr * LOG2E)[:2]], axis=1))
            rows_ref[c, 2 + d:3 + d, :] = brow
            rows_ref[c, 4 + d:5 + d, :] = jnp.broadcast_to(mg, (1, L))
            rows_ref[c, 6 + d:7 + d, :] = jnp.broadcast_to(blast, (1, L))
        wk8 = jnp.where(row8 == 0, wks[0], jnp.where(row8 == 1, wks[1], 0.0))
        return jnp.concatenate(diags, axis=0), jnp.concatenate(wvs, axis=0), wk8.astype(BF16)

    def phase_a_products(c, diag, wv, wk8):
        kb = ks_ref[pl.ds(pl.multiple_of(c * L, L), L), :].astype(BF16)
        ib = _dot(diag, ones_b)
        ib_ref[c] = ib
        for d in range(2):
            rows_ref[c, d:d + 1, :] = jnp.max(jnp.where(tris[d], ib[d * L:(d + 1) * L, :], NEG_BIG),
                                              axis=0, keepdims=True)
        cinc_ref[c] = _dot(wv, kb)
        ninc_ref[c] = _dot(wk8, kb)

    def phase_a(g, carry):
        chunks = [g * C_GROUP + i for i in range(C_GROUP)]
        operands = [phase_a_operands(c) for c in chunks]
        for c, ops in zip(chunks, operands):
            phase_a_products(c, *ops)
        return carry

    def phase_b(j, ms):
        new_ms = []
        for d, cc in ((0, j), (1, nc - 1 - j)):
            m = ms[d]
            half = slice(d * DH, (d + 1) * DH)
            cst = c_ref[half, :]
            n = n_ref[d:d + 1, :]
            cin_ref[cc, half, :] = cst.astype(BF16)
            sin_ref[cc, d:d + 1, :] = n
            sin_ref[cc, 2 + d:3 + d, :] = m
            mg = rows_ref[cc, 4 + d:5 + d, :]
            blast = rows_ref[cc, 6 + d:7 + d, :]
            m_new = jnp.maximum(blast + m, mg)
            decay = jnp.exp(blast + m - m_new)
            grow = jnp.exp(mg - m_new)
            c_ref[half, :] = decay * cst + grow * cinc_ref[cc, half, :]
            n_ref[d:d + 1, :] = decay * n + grow * ninc_ref[cc, d:d + 1, :]
            new_ms.append(m_new)
        return tuple(new_ms)

    def phase_c_products(c):
        r0 = pl.multiple_of(c * L, L)
        qb = qs_ref[pl.ds(r0, L), :].astype(BF16)
        qk_t = _dot_nt(ks_ref[pl.ds(r0, L), :].astype(BF16), qb)
        qc_t = _dot_nt(cin_ref[c], qb)
        qn = _dot_nt(sin_ref[c].astype(BF16), qb)
        return qk_t, qc_t, qn

    def phase_c_finish(c, qk_t, qc_t, qn):
        r0 = pl.multiple_of(c * L, L)
        sin = sin_ref[c]
        rows = rows_ref[c]
        ats, stats = [], []
        for d in range(2):
            m_in = sin[2 + d:3 + d, :] * LOG2E
            cm = jnp.maximum(m_in, rows[d:d + 1, :])
            a_t = qk_t * jnp.where(tris[d], jnp.exp2(ib_ref[c, d * L:(d + 1) * L, :] - cm), 0.0)
            w_inter = jnp.exp2(m_in - cm)
            den = w_inter * qn[d:d + 1, :] + jnp.sum(a_t, axis=0, keepdims=True)
            m_t = rows[2 + d:3 + d, :] * LOG2E + cm
            ats.append(a_t.astype(BF16))
            stats.append((w_inter, jnp.maximum(jnp.abs(den), jnp.exp2(-m_t))))
        av_t = _dot(vt_ref[0, c], jnp.concatenate(ats, axis=1))
        h_t = None
        for d in range(2):
            w_inter, den = stats[d]
            hd = (w_inter * qc_t[d * DH:(d + 1) * DH, :] + av_t[:, d * L:(d + 1) * L]) / den
            h_t = hd if h_t is None else h_t + hd
        y_t = h_t * lax.rsqrt(jnp.mean(h_t * h_t, axis=0, keepdims=True) + EPS) * hgt_ref[...]
        o_ref[0, pl.ds(r0, L), :] = (og_ref[0, pl.ds(r0, L), :].astype(F32) * y_t.T).astype(BF16)

    def phase_c(g, carry):
        chunks = [g * C_GROUP + i for i in range(C_GROUP)]
        products = [phase_c_products(c) for c in chunks]
        for c, prod in zip(chunks, products):
            phase_c_finish(c, *prod)
        return carry

    lax.fori_loop(0, nc // C_GROUP, phase_a, 0, unroll=2)
    c_ref[...] = jnp.zeros(c_ref.shape, F32)
    n_ref[...] = jnp.zeros(n_ref.shape, F32)
    m0 = jnp.zeros((1, LANES), F32)
    lax.fori_loop(0, nc, phase_b, (m0, m0), unroll=True)
    lax.fori_loop(0, nc // C_GROUP, phase_c, 0, unroll=2)


def _mlstm(mq, mk, vt, og, grow_h, conv_wq, conv_wk, head_g_t):
    B, S, _ = mq.shape
    nc = S // M_CHUNK
    head = pl.BlockSpec((1, S, M_HEAD_DIM), lambda b, h: (b, 0, h))
    cw = pl.BlockSpec((CONV_W, M_HEAD_DIM), lambda b, h: (0, h))
    return pl.pallas_call(
        _mlstm_kernel,
        grid=(B, M_HEADS),
        in_specs=[
            head, head,
            pl.BlockSpec((1, nc, M_HEAD_DIM, M_CHUNK), lambda b, h: (b, 0, h, 0)),
            head,
            pl.BlockSpec((1, nc, SUBLANES, M_CHUNK), lambda b, h: (b, 0, h, 0)),
            cw, cw,
            pl.BlockSpec((M_HEAD_DIM, LANES), lambda b, h: (h, 0)),
        ],
        out_specs=head,
        out_shape=jax.ShapeDtypeStruct((B, S, M_WIDTH), BF16),
        scratch_shapes=[
            pltpu.VMEM((S + 2 * SUBLANES, LANES), F32),
            pltpu.VMEM((S, M_HEAD_DIM), F32),
            pltpu.VMEM((S, M_HEAD_DIM), F32),
            pltpu.VMEM((nc, SUBLANES, 2 * M_CHUNK), F32),
            pltpu.VMEM((nc, 2 * M_CHUNK, M_CHUNK), F32),
            pltpu.VMEM((nc, SUBLANES, M_CHUNK), F32),
            pltpu.VMEM((nc, 2 * M_HEAD_DIM, M_HEAD_DIM), F32),
            pltpu.VMEM((nc, SUBLANES, M_HEAD_DIM), F32),
            pltpu.VMEM((nc, 2 * M_HEAD_DIM, M_HEAD_DIM), BF16),
            pltpu.VMEM((nc, SUBLANES, M_HEAD_DIM), F32),
            pltpu.VMEM((2 * M_HEAD_DIM, M_HEAD_DIM), F32),
            pltpu.VMEM((SUBLANES, M_HEAD_DIM), F32),
        ],
        compiler_params=_compiler_params(("parallel", "parallel")),
        name="mlstm",
    )(mq, mk, vt, og, grow_h, conv_wq, conv_wk, head_g_t)


def _route(logits):
    lane = lax.broadcasted_iota(jnp.int32, logits.shape, 1)
    lane_f = lane.astype(F32)
    big = float(LANES)
    is_g = (lane >= ROUTER_G_LANE) & (lane < ROUTER_G_LANE + N_GROUPS)
    gl = jnp.where(is_g, logits, NEG_BIG)
    gmax = jnp.max(gl, axis=-1, keepdims=True)
    gsum = jnp.sum(jnp.where(is_g, jnp.exp(gl - gmax), 0.0), axis=-1, keepdims=True)
    p_grp = 1.0 / gsum
    grp = jnp.min(jnp.where(is_g & (gl == gmax), lane_f - ROUTER_G_LANE, big), axis=-1, keepdims=True)
    in_grp = (lane < N_EXPERTS) & (jnp.right_shift(lane, 2).astype(F32) == grp)
    el = jnp.where(in_grp, logits, NEG_BIG)
    v1 = jnp.max(el, axis=-1, keepdims=True)
    i1 = jnp.min(jnp.where(in_grp & (el == v1), lane_f, big), axis=-1, keepdims=True)
    rest = in_grp & (lane_f != i1)
    el2 = jnp.where(rest, logits, NEG_BIG)
    v2 = jnp.max(el2, axis=-1, keepdims=True)
    i2 = jnp.min(jnp.where(rest & (el2 == v2), lane_f, big), axis=-1, keepdims=True)
    e21 = jnp.exp(v2 - v1)
    w1 = p_grp / (1.0 + e21)
    w2 = p_grp * e21 / (1.0 + e21)
    return jnp.where(lane == 0, i1, jnp.where(lane == 1, i2, jnp.where(lane == 2, w1, w2)))


def _pack_bf16_pairs(lo, hi):
    bits = lambda v: lax.bitcast_convert_type(v.astype(BF16).astype(F32), PACKED)
    return jnp.right_shift(bits(lo), PACKED(16)) | (bits(hi) & PACKED(0xFFFF0000))


def _unpack_bf16_pairs(w):
    return (lax.bitcast_convert_type(jnp.left_shift(w, PACKED(16)), F32),
            lax.bitcast_convert_type(w & PACKED(0xFFFF0000), F32))


def _pack_row_halves(y):
    return (_pack_bf16_pairs(y[:, 0:PACK_W], y[:, PACK_W:2 * PACK_W]),
            _pack_bf16_pairs(y[:, 2 * PACK_W:3 * PACK_W], y[:, 3 * PACK_W:4 * PACK_W]))


def _unpack_row_halves(a, b):
    return [*_unpack_bf16_pairs(a), *_unpack_bf16_pairs(b)]


def _merge_kernel(attn_ref, mo_ref, bg_ref, x_ref, mod_ref, wua_ref, wum_ref, wo_ref, g2_ref, wr_ref, br_ref,
                  x1_ref, ha_ref, hb_ref, rt_ref, ids_ref):
    tm = x_ref.shape[1]
    nparts = tm // MERGE_ROWS
    parts = [pl.ds(i * MERGE_ROWS, MERGE_ROWS) for i in range(nparts)]
    wr = wr_ref[...].astype(BF16)
    merged, h2s = {}, {}
    for step in range(nparts + 2):
        if step < nparts:
            h = parts[step]
            up_a = _dot(attn_ref[0, h, :], wua_ref[...])
            up_m = _dot(mo_ref[0, h, :], wum_ref[...])
            merged[step] = (bg_ref[0, h, 0:D_MODEL].astype(F32) * up_a
                            + bg_ref[0, h, D_MODEL:2 * D_MODEL].astype(F32) * up_m).astype(BF16)
        if 0 <= step - 1 < nparts:
            h = parts[step - 1]
            x1 = x_ref[0, h, :] + mod_ref[0, 2:3, :] * _dot(merged.pop(step - 1), wo_ref[...])
            x1_ref[0, h, :] = x1
            h2s[step - 1] = _rms_mod(x1, g2_ref[...], mod_ref[0, 4:5, :], mod_ref[0, 3:4, :])
        if 0 <= step - 2 < nparts:
            h = parts[step - 2]
            h2 = h2s.pop(step - 2)
            logits = _dot(h2.astype(BF16), wr) + br_ref[...]
            ha_ref[0, h, :], hb_ref[0, h, :] = _pack_row_halves(h2)
            route = _route(logits)
            rt_ref[0, h, :] = route[:, 0:ROUTE_W]
            ids_ref[0, :, h] = route.T[0:SUBLANES, :]


def _merge(attn, mo, bg, x, mod, wua, wum, wo, norm_g, w_router, b_router, tm):
    B, S, D = x.shape
    tok = lambda w: pl.BlockSpec((1, tm, w), lambda b, i: (b, i, 0))
    const2 = lambda a: pl.BlockSpec(a.shape, lambda b, i: (0, 0))
    g2 = norm_g.reshape(1, D)
    return pl.pallas_call(
        _merge_kernel,
        grid=(B, S // tm),
        in_specs=[
            tok(A_WIDTH), tok(M_WIDTH), tok(2 * D), tok(D),
            pl.BlockSpec((1, 6, D), lambda b, i: (b, 0, 0)),
            const2(wua), const2(wum), const2(wo), const2(g2), const2(w_router), const2(b_router),
        ],
        out_specs=(tok(D), tok(PACK_W), tok(PACK_W), tok(ROUTE_W),
                   pl.BlockSpec((1, SUBLANES, tm), lambda b, i: (b, 0, i))),
        out_shape=(
            jax.ShapeDtypeStruct((B, S, D), F32),
            jax.ShapeDtypeStruct((B, S, PACK_W), PACKED),
            jax.ShapeDtypeStruct((B, S, PACK_W), PACKED),
            jax.ShapeDtypeStruct((B, S, ROUTE_W), F32),
            jax.ShapeDtypeStruct((B, SUBLANES, S), F32),
        ),
        compiler_params=_compiler_params(("parallel", "parallel")),
        name="merge_route",
    )(attn, mo, bg, x, mod, wua, wum, wo, g2, w_router, b_router)


def _route_tables(ea, tmb):
    n = ea.shape[0]
    T = n // 2
    nblk = 2 * T // tmb + N_EXPERTS
    onehot = ea[None, :] == jnp.arange(N_EXPERTS, dtype=jnp.int32)[:, None]
    pieces = onehot.reshape(N_EXPERTS, n // LANES, LANES).astype(BF16)
    upto = (jnp.arange(LANES)[:, None] <= jnp.arange(LANES)[None, :]).astype(BF16)
    within = jnp.einsum("eps,st->ept", pieces, upto, preferred_element_type=F32).astype(jnp.int32)
    totals = within[:, :, -1]
    before = jnp.cumsum(totals, axis=1) - totals
    csum = (within + before[:, :, None]).reshape(N_EXPERTS, n)
    counts = before[:, -1] + totals[:, -1]
    padded = ((counts + tmb - 1) // tmb) * tmb
    ends = jnp.cumsum(padded)
    starts = ends - padded
    pos = jnp.sum(jnp.where(onehot, csum - 1 + starts[:, None], 0), axis=0).astype(jnp.int32)
    blk0 = jnp.arange(nblk, dtype=jnp.int32) * tmb
    bexp = jnp.minimum(jnp.sum((blk0[:, None] >= ends[None, :]).astype(jnp.int32), axis=1), N_EXPERTS - 1)
    nval = jnp.clip(starts[bexp] + counts[bexp] - blk0, 0, tmb)
    nval = jnp.where(blk0 < ends[-1], nval, 0).astype(jnp.int32)
    return pos, bexp.astype(jnp.int32), nval


def _sc_mesh():
    return plsc.VectorSubcoreMesh(core_axis_name="c", subcore_axis_name="s")


def _sc_dispatch(x, pos, n_rows):
    T = x.shape[0]
    nb = T // SC_WINDOW
    idx = pos.reshape(1, 2 * T)

    @pl.kernel(out_type=jax.ShapeDtypeStruct((n_rows, PACK_W), x.dtype), mesh=_sc_mesh(), scratch_types=[])
    def dispatch(x_hbm, i_hbm, o_hbm):
        def body(x_vmem, i_vmem):
            pltpu.sync_copy(x_vmem, o_hbm.at[i_vmem.at[0]])

        pltpu.emit_pipeline(
            body,
            grid=(2 * nb,),
            in_specs=[pl.BlockSpec((SC_WINDOW, PACK_W), index_map=lambda i: (i % nb, 0)),
                      pl.BlockSpec((1, SC_WINDOW), index_map=lambda i: (0, i))],
            out_specs=[],
            core_axis_name=("c", "s"),
            dimension_semantics=(pltpu.PARALLEL,),
        )(x_hbm, i_hbm)

    return dispatch(x, idx)


def _sc_combine(y, pos):
    n = pos.shape[0]
    idx = pos.reshape(1, n)

    @pl.kernel(out_type=jax.ShapeDtypeStruct((n, PACK_W), y.dtype), mesh=_sc_mesh(), scratch_types=[])
    def combine(y_hbm, i_hbm, o_hbm):
        def body(i_vmem, o_vmem):
            pltpu.sync_copy(y_hbm.at[i_vmem.at[0]], o_vmem)

        pltpu.emit_pipeline(
            body,
            grid=(n // SC_WINDOW,),
            in_specs=[pl.BlockSpec((1, SC_WINDOW), index_map=lambda i: (0, i))],
            out_specs=[pl.BlockSpec((SC_WINDOW, PACK_W), index_map=lambda i: (i, 0))],
            core_axis_name=("c", "s"),
            dimension_semantics=(pltpu.PARALLEL,),
        )(i_hbm, o_hbm)

    return combine(y, idx)


def _experts_kernel(bexp_ref, nval_ref, xa_ref, xb_ref, wg_ref, wu_ref, wd_ref, ya_ref, yb_ref, wg_s, wu_s, wd_s):
    i = pl.program_id(0)
    nv = nval_ref[i]
    new_expert = (i == 0) | (bexp_ref[i] != bexp_ref[jnp.maximum(i - 1, 0)])

    @pl.when((nv > 0) & new_expert)
    def _():
        wg_s[...] = wg_ref[0].astype(BF16)
        wu_s[...] = wu_ref[0].astype(BF16)
        wd_s[...] = wd_ref[0].astype(BF16)

    @pl.when(nv > 0)
    def _():
        keep = lax.broadcasted_iota(jnp.int32, xa_ref.shape, 0) < nv
        pieces = [jnp.where(keep, piece, 0.0).astype(BF16) for piece in _unpack_row_halves(xa_ref[...], xb_ref[...])]
        x = jnp.concatenate(pieces, axis=1)
        half = x.shape[0] // 2
        gus = [(_dot(x[r:r + half], wg_s[...]), _dot(x[r:r + half], wu_s[...])) for r in (0, half)]
        hes = [((g * _sigmoid(g)) * u).astype(BF16) for g, u in gus]
        for r, he in zip((0, half), hes):
            ya_ref[r:r + half, :], yb_ref[r:r + half, :] = _pack_row_halves(_dot(he, wd_s[...]))


def _experts(xa, xb, bexp, nval, wg, wu, wd, tmb):
    n = xa.shape[0]
    row = lambda: pl.BlockSpec((tmb, PACK_W), lambda i, be, nv: (i, 0))
    grid_spec = pltpu.PrefetchScalarGridSpec(
        num_scalar_prefetch=2,
        grid=(n // tmb,),
        in_specs=[row(), row(),
                  pl.BlockSpec((1, D_MODEL, D_EXPERT), lambda i, be, nv: (be[i], 0, 0)),
                  pl.BlockSpec((1, D_MODEL, D_EXPERT), lambda i, be, nv: (be[i], 0, 0)),
                  pl.BlockSpec((1, D_EXPERT, D_MODEL), lambda i, be, nv: (be[i], 0, 0))],
        out_specs=[row(), row()],
        scratch_shapes=[pltpu.VMEM((D_MODEL, D_EXPERT), BF16), pltpu.VMEM((D_MODEL, D_EXPERT), BF16),
                        pltpu.VMEM((D_EXPERT, D_MODEL), BF16)],
    )
    out = jax.ShapeDtypeStruct((n, PACK_W), xa.dtype)
    return pl.pallas_call(
        _experts_kernel, grid_spec=grid_spec, out_shape=(out, out),
        compiler_params=_compiler_params(("arbitrary",)),
        name="experts",
    )(bexp, nval, xa, xb, wg, wu, wd)


def _finish_kernel(a0_ref, b0_ref, a1_ref, b1_ref, rt_ref, x1_ref, mod_ref, fg_ref, o_ref):
    y0 = jnp.concatenate(_unpack_row_halves(a0_ref[...], b0_ref[...]), axis=1)
    y1 = jnp.concatenate(_unpack_row_halves(a1_ref[...], b1_ref[...]), axis=1)
    rt = rt_ref[0]
    moe = rt[:, 2:3] * y0 + rt[:, 3:4] * y1
    xo = x1_ref[0] + mod_ref[0, 5:6, :] * moe
    ms = jnp.mean(xo * xo, axis=-1, keepdims=True)
    o_ref[0] = xo * lax.rsqrt(ms + EPS) * fg_ref[...]


def _finish(ca, cb, rt, x1, mod, final_g, tm):
    B, S, D = x1.shape
    nt = S // tm
    second = B * nt
    half = lambda k: pl.BlockSpec((tm, PACK_W), lambda b, i: (k * second + b * nt + i, 0))
    tok = lambda w: pl.BlockSpec((1, tm, w), lambda b, i: (b, i, 0))
    fg = final_g.reshape(1, D)
    return pl.pallas_call(
        _finish_kernel,
        grid=(B, nt),
        in_specs=[half(0), half(0), half(1), half(1), tok(ROUTE_W), tok(D),
                  pl.BlockSpec((1, 6, D), lambda b, i: (b, 0, 0)),
                  pl.BlockSpec((1, D), lambda b, i: (0, 0))],
        out_specs=tok(D),
        out_shape=jax.ShapeDtypeStruct((B, S, D), F32),
        compiler_params=_compiler_params(("parallel", "parallel")),
        name="finish",
    )(ca, cb, ca, cb, rt, x1, mod, fg)


def _moe_tables(ids):
    B, _, S = ids.shape
    ea = ids[:, 0:2, :].astype(jnp.int32).transpose(1, 0, 2).reshape(2 * B * S)
    return _route_tables(ea, EXPERT_ROWS)


def _moe_dispatch(ha, hb, tables):
    B, S, _ = ha.shape
    T = B * S
    pos, bexp, nval = tables
    n_rows = 2 * T + N_EXPERTS * EXPERT_ROWS
    xa = _sc_dispatch(ha.reshape(T, PACK_W), pos, n_rows)
    xb = _sc_dispatch(hb.reshape(T, PACK_W), pos, n_rows)
    return xa, xb, pos, bexp, nval


def _moe_finish(ya, yb, pos, rt, x1, mod, final_g):
    S = x1.shape[1]
    return _finish(_sc_combine(ya, pos), _sc_combine(yb, pos), rt, x1, mod, final_g, tm=min(FINISH_TILE, S))


def _rope_tables(S):
    half = ROPE_DIM // 2
    inv_freq = 1.0 / (ROPE_THETA ** (jnp.arange(half, dtype=F32) * 2.0 / ROPE_DIM))
    ang = jnp.arange(S, dtype=F32)[:, None] * inv_freq[None, :]
    cos, sin = jnp.cos(ang), jnp.sin(ang)
    zeros = jnp.zeros((S, A_HEAD_DIM - ROPE_DIM), F32)
    z8 = jnp.zeros((S, half), F32)
    cos_h = jnp.concatenate([cos, cos, jnp.ones_like(zeros)], axis=-1)
    slo_h = jnp.concatenate([-sin, z8, zeros], axis=-1)
    shi_h = jnp.concatenate([z8, sin, zeros], axis=-1)
    rep = LANES // A_HEAD_DIM
    return tuple(jnp.tile(t, (1, rep)) for t in (cos_h, slo_h, shi_h))


def _mixing(x, mod, p, tabs):
    B, S, D = x.shape
    mod = mod.reshape(B, 6, D)
    q, kx, vx, mq, mk, vt, og, bg, grow = _in_proj(
        x, mod, p["norm1_g"], p["w_main"], p["w_mo"], p["w_bg"], p["w_t"], p["m_gate_b"], tabs,
        tm=min(IN_PROJ_TILE, S))
    attn = _attention(q, kx, vx, p["attn_sink"])
    mo = _mlstm(mq, mk, vt, og, grow, p["conv_wq"], p["conv_wk"], p["head_norm_g_t"])
    x1, ha, hb, rt, ids = _merge(attn, mo, bg, x, mod, p["w_up_attn"], p["w_up_mlstm"], p["w_out"], p["norm2_g"],
                                 p["w_router"], p["b_router"], tm=min(MERGE_TILE, S))
    return dict(x1=x1, rt=rt, mod=mod, ha=ha, hb=hb, tables=_moe_tables(ids))


def kernel(x_prompt, x_sample, c_prompt, c_sample, ada_w, ada_b, norm1_g, w_in, conv_w, m_gate_b, attn_sink,
           head_norm_g, w_up_attn, w_up_mlstm, w_out, norm2_g, rg_w, rg_b, re_w, re_b, w_gate, w_up, w_down,
           final_norm_g):
    assert ada_w.shape[0] == 1, "single-layer trunk"
    w_in0 = w_in[0]
    w_g = w_in0[:, OFF_MG:OFF_BG]
    pad = LANES - N_EXPERTS - N_GROUPS
    p = dict(
        ada_w=ada_w[0], ada_b=ada_b[0], norm1_g=norm1_g[0],
        w_main=w_in0[:, :OFF_MV].astype(BF16), w_mo=w_in0[:, OFF_MO:OFF_MG].astype(BF16),
        w_bg=w_in0[:, OFF_BG:].astype(BF16),
        w_t=jnp.concatenate([w_in0[:, OFF_MV:OFF_MO], _gates_head_major(w_g)], axis=1).T.astype(BF16),
        m_gate_b=_gates_head_major(m_gate_b[0]), attn_sink=attn_sink[0],
        conv_wq=conv_w[0, :, :M_WIDTH], conv_wk=conv_w[0, :, M_WIDTH:],
        head_norm_g_t=jnp.broadcast_to(head_norm_g[0][:, None], (M_WIDTH, LANES)),
        w_up_attn=w_up_attn[0].astype(BF16), w_up_mlstm=w_up_mlstm[0].astype(BF16), w_out=w_out[0].astype(BF16),
        norm2_g=norm2_g[0],
        w_router=jnp.pad(jnp.concatenate([re_w[0], rg_w[0]], axis=1), ((0, 0), (0, pad))),
        b_router=jnp.pad(jnp.concatenate([re_b[0], rg_b[0]]), (0, pad)).reshape(1, LANES),
        w_gate=w_gate[0], w_up=w_up[0], w_down=w_down[0],
        final_norm_g=final_norm_g,
    )
    tabs = _rope_tables(x_prompt.shape[1])
    nbp = x_prompt.shape[0]
    mod = _ada_mod(jnp.concatenate([c_prompt, c_sample], axis=0), p["ada_w"], p["ada_b"])
    xs, mods = [x_prompt, x_sample], [mod[:nbp], mod[nbp:]]

    def experts(g):
        xa, xb, _, bexp, nval = g["routed"]
        return _experts(xa, xb, bexp, nval, p["w_gate"], p["w_up"], p["w_down"], EXPERT_ROWS)

    first, second = sorted(range(2), key=lambda i: -xs[i].shape[0])
    groups = {first: _mixing(xs[first], mods[first], p, tabs)}
    (groups[first]["x1"], groups[first]["tables"]), x_second = lax.optimization_barrier(
        ((groups[first]["x1"], groups[first]["tables"]), xs[second]))
    groups[second] = _mixing(x_second, mods[second], p, tabs)
    for g in groups.values():
        g["routed"] = _moe_dispatch(g["ha"], g["hb"], g["tables"])
    ys = {first: experts(groups[first])}
    ys[first], groups[second]["routed"] = lax.optimization_barrier((ys[first], groups[second]["routed"]))
    ys[second] = experts(groups[second])
    outs = [_moe_finish(*ys[i], groups[i]["routed"][2], groups[i]["rt"], groups[i]["x1"], groups[i]["mod"],
                        p["final_norm_g"]) for i in range(2)]
    return tuple(outs)
```

```python
import jax
import jax.numpy as jnp
from jax import lax
from jax.experimental import pallas as pl
from jax.experimental.pallas import tpu as pltpu
from jax.experimental.pallas import tpu_sc as plsc

D_MODEL = 1024
A_HEADS = 8
A_KV_HEADS = 2
A_GROUP = A_HEADS // A_KV_HEADS
A_HEAD_DIM = 64
A_WIDTH = A_HEADS * A_HEAD_DIM
A_KV_WIDTH = A_KV_HEADS * A_HEAD_DIM
WINDOW = 128
BLOCK = 128
ROPE_DIM = A_HEAD_DIM // 4
ROPE_THETA = 500000.0
M_HEADS = 4
M_HEAD_DIM = 128
M_WIDTH = M_HEADS * M_HEAD_DIM
M_CHUNK = 128
CONV_W = 3
OFF_AQ = 0
OFF_AK = OFF_AQ + A_WIDTH
OFF_AV = OFF_AK + A_KV_WIDTH
OFF_MQ = OFF_AV + A_KV_WIDTH
OFF_MK = OFF_MQ + M_WIDTH
OFF_MV = OFF_MK + M_WIDTH
OFF_MO = OFF_MV + M_WIDTH
OFF_MG = OFF_MO + M_WIDTH
N_MGATES = 4 * M_HEADS
OFF_BG = OFF_MG + N_MGATES
IN_TOTAL = OFF_BG + 2 * D_MODEL
N_GROUPS = 4
EXPERTS_PER_GROUP = 4
N_EXPERTS = N_GROUPS * EXPERTS_PER_GROUP
D_EXPERT = 512
EPS = 1e-6
NEG_BIG = -1e30
F32 = jnp.float32
BF16 = jnp.bfloat16

LANES = 128
SUBLANES = 8
VMEM_LIMIT_BYTES = 56 * 1024 * 1024

MAIN_AQ = 0
MAIN_KV = A_WIDTH
MAIN_MQ = MAIN_KV + 2 * A_KV_WIDTH
MAIN_MK = MAIN_MQ + M_WIDTH

LOG2E = 1.4426950408889634
Q_SCALE = A_HEAD_DIM ** -0.5 * LOG2E
ROUTER_G_LANE = N_EXPERTS
ROUTE_W = 4
PACKED = jnp.uint32
PACK_W = D_MODEL // 4
SC_WINDOW = 128
EXPERT_ROWS = 512
IN_PROJ_TILE = 512
MERGE_TILE = 1024
MERGE_ROWS = 256
FINISH_TILE = 1024
GATE_ROWS = M_HEADS * SUBLANES


def _sigmoid(z):
    return 1.0 / (1.0 + jnp.exp2(z * (-LOG2E)))


def _log_sigmoid(z):
    return jnp.minimum(z, 0.0) - jnp.log(1.0 + jnp.exp(-jnp.abs(z)))


def _gates_head_major(a):
    lead = a.shape[:-1]
    a = jnp.swapaxes(a.reshape(*lead, 4, M_HEADS), -1, -2)
    a = jnp.pad(a, [(0, 0)] * (a.ndim - 1) + [(0, SUBLANES - 4)])
    return a.reshape(*lead, GATE_ROWS)


def _dot(a, b):
    return jnp.dot(a, b, preferred_element_type=F32)


def _dot_nt(a, b):
    return lax.dot_general(a, b, (((1,), (1,)), ((), ())), preferred_element_type=F32)


def _dot_f32(a, b):
    return jnp.dot(a, b, preferred_element_type=F32, precision=lax.Precision.HIGHEST)


def _compiler_params(semantics):
    return pltpu.CompilerParams(dimension_semantics=semantics, vmem_limit_bytes=VMEM_LIMIT_BYTES)


def _ada_kernel(c_ref, w_ref, b_ref, o_ref):
    c = c_ref[...]
    o_ref[...] = _dot_f32(c * _sigmoid(c), w_ref[...]) + b_ref[...]


def _ada_mod(c, ada_w, ada_b):
    B, D = c.shape
    n = ada_w.shape[1] // D
    return pl.pallas_call(
        _ada_kernel,
        grid=(n,),
        in_specs=[
            pl.BlockSpec((B, D), lambda j: (0, 0)),
            pl.BlockSpec((D, D), lambda j: (0, j)),
            pl.BlockSpec((1, D), lambda j: (0, j)),
        ],
        out_specs=pl.BlockSpec((B, D), lambda j: (0, j)),
        out_shape=jax.ShapeDtypeStruct((B, n * D), F32),
        compiler_params=_compiler_params(("arbitrary",)),
        name="ada_mod",
    )(c, ada_w, ada_b.reshape(1, -1))


def _rms_mod(x, g, scale, shift):
    ms = jnp.mean(x * x, axis=-1, keepdims=True)
    return (x * lax.rsqrt(ms + EPS) * g) * (1.0 + scale) + shift


def _rope_block(xb, cos, sin_lo, sin_hi):
    half = ROPE_DIM // 2
    return xb * cos + pltpu.roll(xb, LANES - half, axis=1) * sin_lo + pltpu.roll(xb, half, axis=1) * sin_hi


def _in_proj_kernel(x_ref, mod_ref, g_ref, w_ref, wmo_ref, wbg_ref, wt_ref, gb_col_ref, cos_ref, slo_ref, shi_ref,
                    q_ref, kx_ref, vx_ref, mq_ref, mk_ref, vt_ref, og_ref, bg_ref, grow_ref):
    x = x_ref[0]
    tm = x.shape[0]
    h = _rms_mod(x, g_ref[...], mod_ref[0, 1:2, :], mod_ref[0, 0:1, :])
    hb = h.astype(BF16)
    cos, slo, shi = cos_ref[...], slo_ref[...], shi_ref[...]
    lane = lax.broadcasted_iota(jnp.int32, (tm, LANES), 1)
    left = lane < A_HEAD_DIM

    pj = _dot(hb, w_ref[:, MAIN_AQ:MAIN_AQ + A_WIDTH])
    for j in range(A_WIDTH // LANES):
        blk = _rope_block(pj[:, j * LANES:(j + 1) * LANES], cos, slo, shi)
        q_ref[0, :, j * LANES:(j + 1) * LANES] = (blk * Q_SCALE).astype(BF16)

    pj = _dot(hb, w_ref[:, MAIN_KV:MAIN_KV + 2 * A_KV_WIDTH])
    kk = _rope_block(pj[:, 0:LANES], cos, slo, shi)
    vv = pj[:, LANES:2 * LANES]
    for src, dst in ((kk, kx_ref), (vv, vx_ref)):
        swapped = pltpu.roll(src, A_HEAD_DIM, axis=1)
        zero = jnp.zeros_like(src)
        dst[0, :, 0 * LANES:1 * LANES] = jnp.where(left, src, zero).astype(BF16)
        dst[0, :, 1 * LANES:2 * LANES] = jnp.where(left, zero, swapped).astype(BF16)
        dst[0, :, 2 * LANES:3 * LANES] = jnp.where(left, swapped, zero).astype(BF16)
        dst[0, :, 3 * LANES:4 * LANES] = jnp.where(left, zero, src).astype(BF16)

    mq_ref[0] = _dot(hb, w_ref[:, MAIN_MQ:MAIN_MQ + M_WIDTH]).astype(BF16)
    mk_ref[0] = _dot(hb, w_ref[:, MAIN_MK:MAIN_MK + M_WIDTH]).astype(BF16)
    og_ref[0] = _sigmoid(_dot(hb, wmo_ref[...])).astype(BF16)
    for j in range(2 * D_MODEL // M_WIDTH):
        cols = slice(j * M_WIDTH, (j + 1) * M_WIDTH)
        bg_ref[0, :, cols] = _sigmoid(_dot(hb, wbg_ref[:, cols])).astype(BF16)

    tr = _dot_nt(wt_ref[...], hb)
    gr = tr[M_WIDTH:M_WIDTH + GATE_ROWS, :] + gb_col_ref[...]
    gtype = jnp.bitwise_and(lax.broadcasted_iota(jnp.int32, gr.shape, 0), SUBLANES - 1)
    gr = jnp.where(gtype == 1, _log_sigmoid(gr), jnp.where(gtype == 3, _log_sigmoid(gr), gr))
    vt = tr[0:M_WIDTH, :].astype(BF16)
    for c in range(tm // M_CHUNK):
        grow_ref[0, c] = gr[:, c * M_CHUNK:(c + 1) * M_CHUNK]
        vt_ref[0, c] = vt[:, c * M_CHUNK:(c + 1) * M_CHUNK]


def _in_proj(x, mod, norm_g, w_main, w_mo, w_bg, w_t, gate_b, rope_tabs, tm):
    B, S, D = x.shape
    nt = S // tm
    cos, slo, shi = rope_tabs
    tok = lambda w: pl.BlockSpec((1, tm, w), lambda b, i: (b, i, 0))
    const2 = lambda a: pl.BlockSpec(a.shape, lambda b, i: (0, 0))
    tab = pl.BlockSpec((tm, LANES), lambda b, i: (i, 0))
    out_shapes = (
        jax.ShapeDtypeStruct((B, S, A_WIDTH), BF16),
        jax.ShapeDtypeStruct((B, S, 4 * LANES), BF16),
        jax.ShapeDtypeStruct((B, S, 4 * LANES), BF16),
        jax.ShapeDtypeStruct((B, S, M_WIDTH), BF16),
        jax.ShapeDtypeStruct((B, S, M_WIDTH), BF16),
        jax.ShapeDtypeStruct((B, S // M_CHUNK, M_WIDTH, M_CHUNK), BF16),
        jax.ShapeDtypeStruct((B, S, M_WIDTH), BF16),
        jax.ShapeDtypeStruct((B, S, 2 * D_MODEL), BF16),
        jax.ShapeDtypeStruct((B, S // M_CHUNK, GATE_ROWS, M_CHUNK), F32),
    )
    chunked = lambda rows: pl.BlockSpec((1, tm // M_CHUNK, rows, M_CHUNK), lambda b, i: (b, i, 0, 0))
    out_specs = (
        tok(A_WIDTH), tok(4 * LANES), tok(4 * LANES), tok(M_WIDTH), tok(M_WIDTH), chunked(M_WIDTH), tok(M_WIDTH),
        tok(2 * D_MODEL), chunked(GATE_ROWS),
    )
    gb_col = gate_b.reshape(GATE_ROWS, 1)
    g2 = norm_g.reshape(1, D)
    return pl.pallas_call(
        _in_proj_kernel,
        grid=(B, nt),
        in_specs=[
            tok(D),
            pl.BlockSpec((1, 6, D), lambda b, i: (b, 0, 0)),
            const2(g2), const2(w_main), const2(w_mo), const2(w_bg), const2(w_t), const2(gb_col),
            tab, tab, tab,
        ],
        out_specs=out_specs,
        out_shape=out_shapes,
        compiler_params=_compiler_params(("parallel", "parallel")),
        name="in_proj",
    )(x, mod, g2, w_main, w_mo, w_bg, w_t, gb_col, cos, slo, shi)


def _attn_kernel(sink_ref, q_ref, kx_ref, vx_ref, o_ref):
    S = q_ref.shape[1]
    nb = S // BLOCK
    kw = 3 * BLOCK
    qi = lax.broadcasted_iota(jnp.int32, (BLOCK, kw), 0)
    ki = lax.broadcasted_iota(jnp.int32, (BLOCK, kw), 1)
    rel0 = ki - qi
    ones_b = jnp.ones((kw, LANES), BF16)
    left = lax.broadcasted_iota(jnp.int32, (BLOCK, LANES), 1) < A_HEAD_DIM

    def block(n, carry):
        q0 = pl.multiple_of(n * BLOCK, BLOCK)
        k0 = pl.multiple_of(jnp.clip((n - 1) * BLOCK, 0, S - kw), BLOCK)
        rel = rel0 + (k0 - q0)
        valid = jnp.abs(rel) <= WINDOW
        scores = []
        for hk in range(A_KV_HEADS):
            for j in range(A_GROUP // 2):
                col = (hk * (A_GROUP // 2) + j) * LANES
                qp = q_ref[0, pl.ds(q0, BLOCK), col:col + LANES]
                for side in range(2):
                    kk = kx_ref[0, pl.ds(k0, kw), (2 * hk + side) * LANES:(2 * hk + side + 1) * LANES]
                    scores.append(_dot_nt(qp, kk))
        for hk in range(A_KV_HEADS):
            for j in range(A_GROUP // 2):
                col = (hk * (A_GROUP // 2) + j) * LANES
                outs, dens = [], []
                for side in range(2):
                    vv = vx_ref[0, pl.ds(k0, kw), (2 * hk + side) * LANES:(2 * hk + side + 1) * LANES]
                    sk = sink_ref[hk * A_GROUP + 2 * j + side] * LOG2E
                    s = jnp.where(valid, scores[(hk * (A_GROUP // 2) + j) * 2 + side], NEG_BIG)
                    m = jnp.maximum(jnp.max(s, axis=-1, keepdims=True), sk)
                    p = jnp.exp2(s - m).astype(BF16)
                    od = _dot(p, jnp.concatenate([vv, ones_b], axis=1))
                    outs.append(od[:, 0:LANES])
                    dens.append(od[:, LANES:2 * LANES] + jnp.exp2(sk - m))
                o = (outs[0] + outs[1]) / jnp.where(left, dens[0], dens[1])
                o_ref[0, pl.ds(q0, BLOCK), col:col + LANES] = o.astype(BF16)
        return carry

    lax.fori_loop(0, nb, block, 0, unroll=8)


def _attention(q, kx, vx, sink):
    B, S, _ = q.shape
    seq = lambda w: pl.BlockSpec((1, S, w), lambda b: (b, 0, 0))
    return pl.pallas_call(
        _attn_kernel,
        grid=(B,),
        in_specs=[pl.BlockSpec(memory_space=pltpu.SMEM), seq(A_WIDTH), seq(4 * LANES), seq(4 * LANES)],
        out_specs=seq(A_WIDTH),
        out_shape=jax.ShapeDtypeStruct((B, S, A_WIDTH), BF16),
        compiler_params=_compiler_params(("parallel",)),
        name="window_attn",
    )(sink, q, kx, vx)


def _conv_silu(u_ref, w_ref, pad_ref, dst_ref, scale):
    S = u_ref.shape[1]
    pad_ref[0:SUBLANES, :] = jnp.zeros((SUBLANES, LANES), F32)
    pad_ref[S + SUBLANES:S + 2 * SUBLANES, :] = jnp.zeros((SUBLANES, LANES), F32)
    pad_ref[SUBLANES:S + SUBLANES, :] = u_ref[0].astype(F32)
    w0, w1, w2 = w_ref[0:1, :], w_ref[1:2, :], w_ref[2:3, :]
    for c in range(S // M_CHUNK):
        base = SUBLANES + c * M_CHUNK
        y = (pad_ref[base - 1:base - 1 + M_CHUNK, :] * w0 + pad_ref[base:base + M_CHUNK, :] * w1
             + pad_ref[base + 1:base + 1 + M_CHUNK, :] * w2)
        y = y * _sigmoid(y)
        dst_ref[c * M_CHUNK:(c + 1) * M_CHUNK, :] = y if scale == 1.0 else y * scale


def _split3(x):
    hi = x.astype(BF16)
    r = x - hi.astype(F32)
    mid = r.astype(BF16)
    lo = (r - mid.astype(F32)).astype(BF16)
    return [hi, mid, lo]


def _mlstm_kernel(mq_ref, mk_ref, vt_ref, og_ref, grow_ref, cwq_ref, cwk_ref, hgt_ref, o_ref,
                  pad_ref, qs_ref, ks_ref, rb_ref, ib_ref, rows_ref, cinc_ref, ninc_ref, cin_ref, sin_ref,
                  c_ref, n_ref):
    S = mq_ref.shape[1]
    nc = S // M_CHUNK
    L = M_CHUNK
    DH = M_HEAD_DIM
    C_GROUP = 8 if nc % 8 == 0 else 1
    _conv_silu(mq_ref, cwq_ref, pad_ref, qs_ref, 1.0)
    _conv_silu(mk_ref, cwk_ref, pad_ref, ks_ref, M_HEAD_DIM ** -0.5)

    s_i = lax.broadcasted_iota(jnp.int32, (L, L), 0)
    t_i = lax.broadcasted_iota(jnp.int32, (L, L), 1)
    tris = (s_i <= t_i, s_i >= t_i)
    eye = s_i == t_i
    row8 = lax.broadcasted_iota(jnp.int32, (SUBLANES, L), 0)
    one_if = lambda cond: jnp.where(cond, 1.0, 0.0)
    k_j = lax.broadcasted_iota(jnp.int32, (3 * L, 2 * L), 0) % L
    c_j = lax.broadcasted_iota(jnp.int32, (3 * L, 2 * L), 1)
    sum_rows = jnp.where(c_j < L, one_if(k_j <= c_j), one_if(k_j >= c_j - L)).astype(BF16)

    g_all = grow_ref[0].reshape(nc * SUBLANES, L)
    rb_ref[...] = _dot(jnp.concatenate(_split3(g_all), axis=1), sum_rows).reshape(nc, SUBLANES, 2 * L)
    ones_b = jnp.ones((2 * L, L), BF16)

    def phase_a_operands(c):
        gr = grow_ref[0, c]
        rb = rb_ref[c]
        vt = vt_ref[0, c].astype(F32)
        wvs, wks, diags = [], [], []
        for d in range(2):
            brow = rb[2 * d + 1:2 * d + 2, d * L:(d + 1) * L]
            blast = brow[:, L - 1:L] if d == 0 else brow[:, 0:1]
            ibr = gr[2 * d:2 * d + 1, :] - brow
            log_g = blast + ibr
            mg = jnp.max(log_g, axis=-1, keepdims=True)
            wk = jnp.exp(log_g - mg)
            wvs.append((vt * wk).astype(BF16))
            wks.append(wk)
            diags.append(jnp.concatenate(
                [jnp.where(eye, term.astype(F32), 0.0).astype(BF16) for term in _split3(ibr * LOG2E)[:2]], axis=1))
            rows_ref[c, 2 + d:3 + d, :] = brow
            rows_ref[c, 4 + d:5 + d, :] = jnp.broadcast_to(mg, (1, L))
            rows_ref[c, 6 + d:7 + d, :] = jnp.broadcast_to(blast, (1, L))
        wk8 = jnp.where(row8 == 0, wks[0], jnp.where(row8 == 1, wks[1], 0.0))
        return jnp.concatenate(diags, axis=0), jnp.concatenate(wvs, axis=0), wk8.astype(BF16)

    def phase_a_products(c, diag, wv, wk8):
        kb = ks_ref[pl.ds(pl.multiple_of(c * L, L), L), :].astype(BF16)
        ib = _dot(diag, ones_b)
        ib_ref[c] = ib
        for d in range(2):
            rows_ref[c, d:d + 1, :] = jnp.max(jnp.where(tris[d], ib[d * L:(d + 1) * L, :], NEG_BIG),
                                              axis=0, keepdims=True)
        cinc_ref[c] = _dot(wv, kb)
        ninc_ref[c] = _dot(wk8, kb)

    def phase_a(g, carry):
        chunks = [g * C_GROUP + i for i in range(C_GROUP)]
        operands = [phase_a_operands(c) for c in chunks]
        for c, ops in zip(chunks, operands):
            phase_a_products(c, *ops)
        return carry

    def phase_b(j, ms):
        new_ms = []
        for d, cc in ((0, j), (1, nc - 1 - j)):
            m = ms[d]
            half = slice(d * DH, (d + 1) * DH)
            cst = c_ref[half, :]
            n = n_ref[d:d + 1, :]
            cin_ref[cc, half, :] = cst.astype(BF16)
            sin_ref[cc, d:d + 1, :] = n
            sin_ref[cc, 2 + d:3 + d, :] = m
            mg = rows_ref[cc, 4 + d:5 + d, :]
            blast = rows_ref[cc, 6 + d:7 + d, :]
            m_new = jnp.maximum(blast + m, mg)
            decay = jnp.exp(blast + m - m_new)
            grow = jnp.exp(mg - m_new)
            c_ref[half, :] = decay * cst + grow * cinc_ref[cc, half, :]
            n_ref[d:d + 1, :] = decay * n + grow * ninc_ref[cc, d:d + 1, :]
            new_ms.append(m_new)
        return tuple(new_ms)

    def phase_c_products(c):
        r0 = pl.multiple_of(c * L, L)
        qb = qs_ref[pl.ds(r0, L), :].astype(BF16)
        qk_t = _dot_nt(ks_ref[pl.ds(r0, L), :].astype(BF16), qb)
        qc_t = _dot_nt(cin_ref[c], qb)
        qn = _dot_nt(sin_ref[c].astype(BF16), qb)
        return qk_t, qc_t, qn

    def phase_c_finish(c, qk_t, qc_t, qn):
        r0 = pl.multiple_of(c * L, L)
        sin = sin_ref[c]
        rows = rows_ref[c]
        ats, stats = [], []
        for d in range(2):
            m_in = sin[2 + d:3 + d, :] * LOG2E
            cm = jnp.maximum(m_in, rows[d:d + 1, :])
            a_t = qk_t * jnp.where(tris[d], jnp.exp2(ib_ref[c, d * L:(d + 1) * L, :] - cm), 0.0)
            w_inter = jnp.exp2(m_in - cm)
            den = w_inter * qn[d:d + 1, :] + jnp.sum(a_t, axis=0, keepdims=True)
            m_t = rows[2 + d:3 + d, :] * LOG2E + cm
            ats.append(a_t.astype(BF16))
            stats.append((w_inter, jnp.maximum(jnp.abs(den), jnp.exp2(-m_t))))
        av_t = _dot(vt_ref[0, c], jnp.concatenate(ats, axis=1))
        h_t = None
        for d in range(2):
            w_inter, den = stats[d]
            hd = (w_inter * qc_t[d * DH:(d + 1) * DH, :] + av_t[:, d * L:(d + 1) * L]) / den
            h_t = hd if h_t is None else h_t + hd
        y_t = h_t * lax.rsqrt(jnp.mean(h_t * h_t, axis=0, keepdims=True) + EPS) * hgt_ref[...]
        o_ref[0, pl.ds(r0, L), :] = (og_ref[0, pl.ds(r0, L), :].astype(F32) * y_t.T).astype(BF16)

    def phase_c(g, carry):
        chunks = [g * C_GROUP + i for i in range(C_GROUP)]
        products = [phase_c_products(c) for c in chunks]
        for c, prod in zip(chunks, products):
            phase_c_finish(c, *prod)
        return carry

    lax.fori_loop(0, nc // C_GROUP, phase_a, 0, unroll=2)
    c_ref[...] = jnp.zeros(c_ref.shape, F32)
    n_ref[...] = jnp.zeros(n_ref.shape, F32)
    m0 = jnp.zeros((1, LANES), F32)
    lax.fori_loop(0, nc, phase_b, (m0, m0), unroll=True)
    lax.fori_loop(0, nc // C_GROUP, phase_c, 0, unroll=2)


def _mlstm(mq, mk, vt, og, grow_h, conv_wq, conv_wk, head_g_t):
    B, S, _ = mq.shape
    nc = S // M_CHUNK
    head = pl.BlockSpec((1, S, M_HEAD_DIM), lambda b, h: (b, 0, h))
    cw = pl.BlockSpec((CONV_W, M_HEAD_DIM), lambda b, h: (0, h))
    return pl.pallas_call(
        _mlstm_kernel,
        grid=(B, M_HEADS),
        in_specs=[
            head, head,
            pl.BlockSpec((1, nc, M_HEAD_DIM, M_CHUNK), lambda b, h: (b, 0, h, 0)),
            head,
            pl.BlockSpec((1, nc, SUBLANES, M_CHUNK), lambda b, h: (b, 0, h, 0)),
            cw, cw,
            pl.BlockSpec((M_HEAD_DIM, LANES), lambda b, h: (h, 0)),
        ],
        out_specs=head,
        out_shape=jax.ShapeDtypeStruct((B, S, M_WIDTH), BF16),
        scratch_shapes=[
            pltpu.VMEM((S + 2 * SUBLANES, LANES), F32),
            pltpu.VMEM((S, M_HEAD_DIM), F32),
            pltpu.VMEM((S, M_HEAD_DIM), F32),
            pltpu.VMEM((nc, SUBLANES, 2 * M_CHUNK), F32),
            pltpu.VMEM((nc, 2 * M_CHUNK, M_CHUNK), F32),
            pltpu.VMEM((nc, SUBLANES, M_CHUNK), F32),
            pltpu.VMEM((nc, 2 * M_HEAD_DIM, M_HEAD_DIM), F32),
            pltpu.VMEM((nc, SUBLANES, M_HEAD_DIM), F32),
            pltpu.VMEM((nc, 2 * M_HEAD_DIM, M_HEAD_DIM), BF16),
            pltpu.VMEM((nc, SUBLANES, M_HEAD_DIM), F32),
            pltpu.VMEM((2 * M_HEAD_DIM, M_HEAD_DIM), F32),
            pltpu.VMEM((SUBLANES, M_HEAD_DIM), F32),
        ],
        compiler_params=_compiler_params(("parallel", "parallel")),
        name="mlstm",
    )(mq, mk, vt, og, grow_h, conv_wq, conv_wk, head_g_t)


def _route(logits):
    lane = lax.broadcasted_iota(jnp.int32, logits.shape, 1)
    lane_f = lane.astype(F32)
    big = float(LANES)
    is_g = (lane >= ROUTER_G_LANE) & (lane < ROUTER_G_LANE + N_GROUPS)
    gl = jnp.where(is_g, logits, NEG_BIG)
    gmax = jnp.max(gl, axis=-1, keepdims=True)
    gsum = jnp.sum(jnp.where(is_g, jnp.exp(gl - gmax), 0.0), axis=-1, keepdims=True)
    p_grp = 1.0 / gsum
    grp = jnp.min(jnp.where(is_g & (gl == gmax), lane_f - ROUTER_G_LANE, big), axis=-1, keepdims=True)
    in_grp = (lane < N_EXPERTS) & (jnp.right_shift(lane, 2).astype(F32) == grp)
    el = jnp.where(in_grp, logits, NEG_BIG)
    v1 = jnp.max(el, axis=-1, keepdims=True)
    i1 = jnp.min(jnp.where(in_grp & (el == v1), lane_f, big), axis=-1, keepdims=True)
    rest = in_grp & (lane_f != i1)
    el2 = jnp.where(rest, logits, NEG_BIG)
    v2 = jnp.max(el2, axis=-1, keepdims=True)
    i2 = jnp.min(jnp.where(rest & (el2 == v2), lane_f, big), axis=-1, keepdims=True)
    e21 = jnp.exp(v2 - v1)
    w1 = p_grp / (1.0 + e21)
    w2 = p_grp * e21 / (1.0 + e21)
    return jnp.where(lane == 0, i1, jnp.where(lane == 1, i2, jnp.where(lane == 2, w1, w2)))


def _pack_bf16_pairs(lo, hi):
    bits = lambda v: lax.bitcast_convert_type(v.astype(BF16).astype(F32), PACKED)
    return jnp.right_shift(bits(lo), PACKED(16)) | (bits(hi) & PACKED(0xFFFF0000))


def _unpack_bf16_pairs(w):
    return (lax.bitcast_convert_type(jnp.left_shift(w, PACKED(16)), F32),
            lax.bitcast_convert_type(w & PACKED(0xFFFF0000), F32))


def _pack_row_halves(y):
    return (_pack_bf16_pairs(y[:, 0:PACK_W], y[:, PACK_W:2 * PACK_W]),
            _pack_bf16_pairs(y[:, 2 * PACK_W:3 * PACK_W], y[:, 3 * PACK_W:4 * PACK_W]))


def _unpack_row_halves(a, b):
    return [*_unpack_bf16_pairs(a), *_unpack_bf16_pairs(b)]


def _merge_kernel(attn_ref, mo_ref, bg_ref, x_ref, mod_ref, wua_ref, wum_ref, wo_ref, g2_ref, wr_ref, br_ref,
                  x1_ref, ha_ref, hb_ref, rt_ref, ids_ref):
    tm = x_ref.shape[1]
    nparts = tm // MERGE_ROWS
    parts = [pl.ds(i * MERGE_ROWS, MERGE_ROWS) for i in range(nparts)]
    wr = wr_ref[...].astype(BF16)
    merged, h2s = {}, {}
    for step in range(nparts + 2):
        if step < nparts:
            h = parts[step]
            up_a = _dot(attn_ref[0, h, :], wua_ref[...])
            up_m = _dot(mo_ref[0, h, :], wum_ref[...])
            merged[step] = (bg_ref[0, h, 0:D_MODEL].astype(F32) * up_a
                            + bg_ref[0, h, D_MODEL:2 * D_MODEL].astype(F32) * up_m).astype(BF16)
        if 0 <= step - 1 < nparts:
            h = parts[step - 1]
            x1 = x_ref[0, h, :] + mod_ref[0, 2:3, :] * _dot(merged.pop(step - 1), wo_ref[...])
            x1_ref[0, h, :] = x1
            h2s[step - 1] = _rms_mod(x1, g2_ref[...], mod_ref[0, 4:5, :], mod_ref[0, 3:4, :])
        if 0 <= step - 2 < nparts:
            h = parts[step - 2]
            h2 = h2s.pop(step - 2)
            logits = _dot(h2.astype(BF16), wr) + br_ref[...]
            ha_ref[0, h, :], hb_ref[0, h, :] = _pack_row_halves(h2)
            route = _route(logits)
            rt_ref[0, h, :] = route[:, 0:ROUTE_W]
            ids_ref[0, :, h] = route.T[0:SUBLANES, :]


def _merge(attn, mo, bg, x, mod, wua, wum, wo, norm_g, w_router, b_router, tm):
    B, S, D = x.shape
    tok = lambda w: pl.BlockSpec((1, tm, w), lambda b, i: (b, i, 0))
    const2 = lambda a: pl.BlockSpec(a.shape, lambda b, i: (0, 0))
    g2 = norm_g.reshape(1, D)
    return pl.pallas_call(
        _merge_kernel,
        grid=(B, S // tm),
        in_specs=[
            tok(A_WIDTH), tok(M_WIDTH), tok(2 * D), tok(D),
            pl.BlockSpec((1, 6, D), lambda b, i: (b, 0, 0)),
            const2(wua), const2(wum), const2(wo), const2(g2), const2(w_router), const2(b_router),
        ],
        out_specs=(tok(D), tok(PACK_W), tok(PACK_W), tok(ROUTE_W),
                   pl.BlockSpec((1, SUBLANES, tm), lambda b, i: (b, 0, i))),
        out_shape=(
            jax.ShapeDtypeStruct((B, S, D), F32),
            jax.ShapeDtypeStruct((B, S, PACK_W), PACKED),
            jax.ShapeDtypeStruct((B, S, PACK_W), PACKED),
            jax.ShapeDtypeStruct((B, S, ROUTE_W), F32),
            jax.ShapeDtypeStruct((B, SUBLANES, S), F32),
        ),
        compiler_params=_compiler_params(("parallel", "parallel")),
        name="merge_route",
    )(attn, mo, bg, x, mod, wua, wum, wo, g2, w_router, b_router)


def _route_tables(ea, tmb):
    n = ea.shape[0]
    T = n // 2
    nblk = 2 * T // tmb + N_EXPERTS
    onehot = ea[None, :] == jnp.arange(N_EXPERTS, dtype=jnp.int32)[:, None]
    pieces = onehot.reshape(N_EXPERTS, n // LANES, LANES).astype(BF16)
    upto = (jnp.arange(LANES)[:, None] <= jnp.arange(LANES)[None, :]).astype(BF16)
    within = jnp.einsum("eps,st->ept", pieces, upto, preferred_element_type=F32).astype(jnp.int32)
    totals = within[:, :, -1]
    before = jnp.cumsum(totals, axis=1) - totals
    csum = (within + before[:, :, None]).reshape(N_EXPERTS, n)
    counts = before[:, -1] + totals[:, -1]
    padded = ((counts + tmb - 1) // tmb) * tmb
    ends = jnp.cumsum(padded)
    starts = ends - padded
    pos = jnp.sum(jnp.where(onehot, csum - 1 + starts[:, None], 0), axis=0).astype(jnp.int32)
    blk0 = jnp.arange(nblk, dtype=jnp.int32) * tmb
    bexp = jnp.minimum(jnp.sum((blk0[:, None] >= ends[None, :]).astype(jnp.int32), axis=1), N_EXPERTS - 1)
    nval = jnp.clip(starts[bexp] + counts[bexp] - blk0, 0, tmb)
    nval = jnp.where(blk0 < ends[-1], nval, 0).astype(jnp.int32)
    return pos, bexp.astype(jnp.int32), nval


def _sc_mesh():
    return plsc.VectorSubcoreMesh(core_axis_name="c", subcore_axis_name="s")


def _sc_dispatch(x, pos, n_rows):
    T = x.shape[0]
    nb = T // SC_WINDOW
    idx = pos.reshape(1, 2 * T)

    @pl.kernel(out_type=jax.ShapeDtypeStruct((n_rows, PACK_W), x.dtype), mesh=_sc_mesh(), scratch_types=[])
    def dispatch(x_hbm, i_hbm, o_hbm):
        def body(x_vmem, i_vmem):
            pltpu.sync_copy(x_vmem, o_hbm.at[i_vmem.at[0]])

        pltpu.emit_pipeline(
            body,
            grid=(2 * nb,),
            in_specs=[pl.BlockSpec((SC_WINDOW, PACK_W), index_map=lambda i: (i % nb, 0)),
                      pl.BlockSpec((1, SC_WINDOW), index_map=lambda i: (0, i))],
            out_specs=[],
            core_axis_name=("c", "s"),
            dimension_semantics=(pltpu.PARALLEL,),
        )(x_hbm, i_hbm)

    return dispatch(x, idx)


def _sc_combine(y, pos):
    n = pos.shape[0]
    idx = pos.reshape(1, n)

    @pl.kernel(out_type=jax.ShapeDtypeStruct((n, PACK_W), y.dtype), mesh=_sc_mesh(), scratch_types=[])
    def combine(y_hbm, i_hbm, o_hbm):
        def body(i_vmem, o_vmem):
            pltpu.sync_copy(y_hbm.at[i_vmem.at[0]], o_vmem)

        pltpu.emit_pipeline(
            body,
            grid=(n // SC_WINDOW,),
            in_specs=[pl.BlockSpec((1, SC_WINDOW), index_map=lambda i: (0, i))],
            out_specs=[pl.BlockSpec((SC_WINDOW, PACK_W), index_map=lambda i: (i, 0))],
            core_axis_name=("c", "s"),
            dimension_semantics=(pltpu.PARALLEL,),
        )(i_hbm, o_hbm)

    return combine(y, idx)


def _experts_kernel(bexp_ref, nval_ref, run_ref, nxt_ref, xa_ref, xb_ref, wg_hbm, wu_hbm, wd_hbm, ya_ref, yb_ref,
                    wg_f, wu_f, wd_f, wg_s, wu_s, wd_s, sems):
    i = pl.program_id(0)
    nv = nval_ref[i]
    new_expert = (i == 0) | (bexp_ref[i] != bexp_ref[jnp.maximum(i - 1, 0)])

    def weight_copies(e, slot):
        pairs = ((wg_hbm, wg_f), (wu_hbm, wu_f), (wd_hbm, wd_f))
        return [pltpu.make_async_copy(src.at[e], dst.at[slot], sems.at[slot, k]) for k, (src, dst) in enumerate(pairs)]

    @pl.when((nv > 0) & new_expert)
    def _():
        slot = lax.rem(run_ref[i], 2)

        @pl.when(i == 0)
        def _():
            for c in weight_copies(bexp_ref[i], slot):
                c.start()

        for c in weight_copies(bexp_ref[i], slot):
            c.wait()

        @pl.when(nxt_ref[i] >= 0)
        def _():
            for c in weight_copies(nxt_ref[i], 1 - slot):
                c.start()

        wg_s[...] = wg_f[slot].astype(BF16)
        wu_s[...] = wu_f[slot].astype(BF16)
        wd_s[...] = wd_f[slot].astype(BF16)

    @pl.when(nv > 0)
    def _():
        keep = lax.broadcasted_iota(jnp.int32, xa_ref.shape, 0) < nv
        pieces = [jnp.where(keep, piece, 0.0).astype(BF16) for piece in _unpack_row_halves(xa_ref[...], xb_ref[...])]
        x = jnp.concatenate(pieces, axis=1)
        half = x.shape[0] // 2
        gus = [(_dot(x[r:r + half], wg_s[...]), _dot(x[r:r + half], wu_s[...])) for r in (0, half)]
        hes = [((g * _sigmoid(g)) * u).astype(BF16) for g, u in gus]
        for r, he in zip((0, half), hes):
            ya_ref[r:r + half, :], yb_ref[r:r + half, :] = _pack_row_halves(_dot(he, wd_s[...]))


def _experts(xa, xb, bexp, nval, wg, wu, wd, tmb):
    n = xa.shape[0]
    nblk = n // tmb
    blk = jnp.arange(nblk, dtype=jnp.int32)
    starts_run = (nval > 0) & ((blk == 0) | (bexp != jnp.roll(bexp, 1)))
    run = (jnp.cumsum(starts_run.astype(jnp.int32)) - 1).astype(jnp.int32)
    first_after = lax.cummin(jnp.where(starts_run, blk, nblk), reverse=True)
    first_after = jnp.concatenate([first_after[1:], jnp.full((1,), nblk, jnp.int32)])
    nxt = jnp.where(first_after < nblk, bexp[jnp.minimum(first_after, nblk - 1)], -1).astype(jnp.int32)

    row = lambda: pl.BlockSpec((tmb, PACK_W), lambda i, *_: (i, 0))
    hbm = pl.BlockSpec(memory_space=pl.ANY)
    grid_spec = pltpu.PrefetchScalarGridSpec(
        num_scalar_prefetch=4,
        grid=(nblk,),
        in_specs=[row(), row(), hbm, hbm, hbm],
        out_specs=[row(), row()],
        scratch_shapes=[pltpu.VMEM((2, D_MODEL, D_EXPERT), F32), pltpu.VMEM((2, D_MODEL, D_EXPERT), F32),
                        pltpu.VMEM((2, D_EXPERT, D_MODEL), F32),
                        pltpu.VMEM((D_MODEL, D_EXPERT), BF16), pltpu.VMEM((D_MODEL, D_EXPERT), BF16),
                        pltpu.VMEM((D_EXPERT, D_MODEL), BF16),
                        pltpu.SemaphoreType.DMA((2, 3))],
    )
    out = jax.ShapeDtypeStruct((n, PACK_W), xa.dtype)
    return pl.pallas_call(
        _experts_kernel, grid_spec=grid_spec, out_shape=(out, out),
        compiler_params=_compiler_params(("arbitrary",)),
        name="experts",
    )(bexp, nval, run, nxt, xa, xb, wg, wu, wd)


def _finish_kernel(a0_ref, b0_ref, a1_ref, b1_ref, rt_ref, x1_ref, mod_ref, fg_ref, o_ref):
    y0 = jnp.concatenate(_unpack_row_halves(a0_ref[...], b0_ref[...]), axis=1)
    y1 = jnp.concatenate(_unpack_row_halves(a1_ref[...], b1_ref[...]), axis=1)
    rt = rt_ref[0]
    moe = rt[:, 2:3] * y0 + rt[:, 3:4] * y1
    xo = x1_ref[0] + mod_ref[0, 5:6, :] * moe
    ms = jnp.mean(xo * xo, axis=-1, keepdims=True)
    o_ref[0] = xo * lax.rsqrt(ms + EPS) * fg_ref[...]


def _finish(ca, cb, rt, x1, mod, final_g, tm):
    B, S, D = x1.shape
    nt = S // tm
    second = B * nt
    half = lambda k: pl.BlockSpec((tm, PACK_W), lambda b, i: (k * second + b * nt + i, 0))
    tok = lambda w: pl.BlockSpec((1, tm, w), lambda b, i: (b, i, 0))
    fg = final_g.reshape(1, D)
    return pl.pallas_call(
        _finish_kernel,
        grid=(B, nt),
        in_specs=[half(0), half(0), half(1), half(1), tok(ROUTE_W), tok(D),
                  pl.BlockSpec((1, 6, D), lambda b, i: (b, 0, 0)),
                  pl.BlockSpec((1, D), lambda b, i: (0, 0))],
        out_specs=tok(D),
        out_shape=jax.ShapeDtypeStruct((B, S, D), F32),
        compiler_params=_compiler_params(("parallel", "parallel")),
        name="finish",
    )(ca, cb, ca, cb, rt, x1, mod, fg)


def _moe_tables(ids):
    B, _, S = ids.shape
    ea = ids[:, 0:2, :].astype(jnp.int32).transpose(1, 0, 2).reshape(2 * B * S)
    return _route_tables(ea, EXPERT_ROWS)


def _moe_dispatch(ha, hb, tables):
    B, S, _ = ha.shape
    T = B * S
    pos, bexp, nval = tables
    n_rows = 2 * T + N_EXPERTS * EXPERT_ROWS
    xa = _sc_dispatch(ha.reshape(T, PACK_W), pos, n_rows)
    xb = _sc_dispatch(hb.reshape(T, PACK_W), pos, n_rows)
    return xa, xb, pos, bexp, nval


def _moe_finish(ya, yb, pos, rt, x1, mod, final_g):
    S = x1.shape[1]
    return _finish(_sc_combine(ya, pos), _sc_combine(yb, pos), rt, x1, mod, final_g, tm=min(FINISH_TILE, S))


def _rope_tables(S):
    half = ROPE_DIM // 2
    inv_freq = 1.0 / (ROPE_THETA ** (jnp.arange(half, dtype=F32) * 2.0 / ROPE_DIM))
    ang = jnp.arange(S, dtype=F32)[:, None] * inv_freq[None, :]
    cos, sin = jnp.cos(ang), jnp.sin(ang)
    zeros = jnp.zeros((S, A_HEAD_DIM - ROPE_DIM), F32)
    z8 = jnp.zeros((S, half), F32)
    cos_h = jnp.concatenate([cos, cos, jnp.ones_like(zeros)], axis=-1)
    slo_h = jnp.concatenate([-sin, z8, zeros], axis=-1)
    shi_h = jnp.concatenate([z8, sin, zeros], axis=-1)
    rep = LANES // A_HEAD_DIM
    return tuple(jnp.tile(t, (1, rep)) for t in (cos_h, slo_h, shi_h))


def _mixing(x, mod, p, tabs):
    B, S, D = x.shape
    mod = mod.reshape(B, 6, D)
    q, kx, vx, mq, mk, vt, og, bg, grow = _in_proj(
        x, mod, p["norm1_g"], p["w_main"], p["w_mo"], p["w_bg"], p["w_t"], p["m_gate_b"], tabs,
        tm=min(IN_PROJ_TILE, S))
    attn = _attention(q, kx, vx, p["attn_sink"])
    mo = _mlstm(mq, mk, vt, og, grow, p["conv_wq"], p["conv_wk"], p["head_norm_g_t"])
    x1, ha, hb, rt, ids = _merge(attn, mo, bg, x, mod, p["w_up_attn"], p["w_up_mlstm"], p["w_out"], p["norm2_g"],
                                 p["w_router"], p["b_router"], tm=min(MERGE_TILE, S))
    return dict(x1=x1, rt=rt, mod=mod, ha=ha, hb=hb, tables=_moe_tables(ids))


def kernel(x_prompt, x_sample, c_prompt, c_sample, ada_w, ada_b, norm1_g, w_in, conv_w, m_gate_b, attn_sink,
           head_norm_g, w_up_attn, w_up_mlstm, w_out, norm2_g, rg_w, rg_b, re_w, re_b, w_gate, w_up, w_down,
           final_norm_g):
    assert ada_w.shape[0] == 1, "single-layer trunk"
    w_in0 = w_in[0]
    w_g = w_in0[:, OFF_MG:OFF_BG]
    pad = LANES - N_EXPERTS - N_GROUPS
    p = dict(
        ada_w=ada_w[0], ada_b=ada_b[0], norm1_g=norm1_g[0],
        w_main=w_in0[:, :OFF_MV].astype(BF16), w_mo=w_in0[:, OFF_MO:OFF_MG].astype(BF16),
        w_bg=w_in0[:, OFF_BG:].astype(BF16),
        w_t=jnp.concatenate([w_in0[:, OFF_MV:OFF_MO], _gates_head_major(w_g)], axis=1).T.astype(BF16),
        m_gate_b=_gates_head_major(m_gate_b[0]), attn_sink=attn_sink[0],
        conv_wq=conv_w[0, :, :M_WIDTH], conv_wk=conv_w[0, :, M_WIDTH:],
        head_norm_g_t=jnp.broadcast_to(head_norm_g[0][:, None], (M_WIDTH, LANES)),
        w_up_attn=w_up_attn[0].astype(BF16), w_up_mlstm=w_up_mlstm[0].astype(BF16), w_out=w_out[0].astype(BF16),
        norm2_g=norm2_g[0],
        w_router=jnp.pad(jnp.concatenate([re_w[0], rg_w[0]], axis=1), ((0, 0), (0, pad))),
        b_router=jnp.pad(jnp.concatenate([re_b[0], rg_b[0]]), (0, pad)).reshape(1, LANES),
        w_gate=w_gate[0], w_up=w_up[0], w_down=w_down[0],
        final_norm_g=final_norm_g,
    )
    tabs = _rope_tables(x_prompt.shape[1])
    nbp = x_prompt.shape[0]
    mod = _ada_mod(jnp.concatenate([c_prompt, c_sample], axis=0), p["ada_w"], p["ada_b"])
    xs, mods = [x_prompt, x_sample], [mod[:nbp], mod[nbp:]]

    def experts(g):
        xa, xb, _, bexp, nval = g["routed"]
        return _experts(xa, xb, bexp, nval, p["w_gate"], p["w_up"], p["w_down"], EXPERT_ROWS)

    first, second = sorted(range(2), key=lambda i: -xs[i].shape[0])
    groups = {first: _mixing(xs[first], mods[first], p, tabs)}
    (groups[first]["x1"], groups[first]["tables"]), x_second = lax.optimization_barrier(
        ((groups[first]["x1"], groups[first]["tables"]), xs[second]))
    groups[second] = _mixing(x_second, mods[second], p, tabs)
    for g in groups.values():
        g["routed"] = _moe_dispatch(g["ha"], g["hb"], g["tables"])
    ys = {first: experts(groups[first])}
    ys[first], groups[second]["routed"] = lax.optimization_barrier((ys[first], groups[second]["routed"]))
    ys[second] = experts(groups[second])
    outs = [_moe_finish(*ys[i], groups[i]["routed"][2], groups[i]["rt"], groups[i]["x1"], groups[i]["mod"],
                        p["final_norm_g"]) for i in range(2)]
    return tuple(outs)
```

```python
import jax
import jax.numpy as jnp
from jax import lax
from jax.experimental import pallas as pl
from jax.experimental.pallas import tpu as pltpu
from jax.experimental.pallas import tpu_sc as plsc

D_MODEL = 1024
A_HEADS = 8
A_KV_HEADS = 2
A_GROUP = A_HEADS // A_KV_HEADS
A_HEAD_DIM = 64
A_WIDTH = A_HEADS * A_HEAD_DIM
A_KV_WIDTH = A_KV_HEADS * A_HEAD_DIM
WINDOW = 128
BLOCK = 128
ROPE_DIM = A_HEAD_DIM // 4
ROPE_THETA = 500000.0
M_HEADS = 4
M_HEAD_DIM = 128
M_WIDTH = M_HEADS * M_HEAD_DIM
M_CHUNK = 128
CONV_W = 3
OFF_AQ = 0
OFF_AK = OFF_AQ + A_WIDTH
OFF_AV = OFF_AK + A_KV_WIDTH
OFF_MQ = OFF_AV + A_KV_WIDTH
OFF_MK = OFF_MQ + M_WIDTH
OFF_MV = OFF_MK + M_WIDTH
OFF_MO = OFF_MV + M_WIDTH
OFF_MG = OFF_MO + M_WIDTH
N_MGATES = 4 * M_HEADS
OFF_BG = OFF_MG + N_MGATES
IN_TOTAL = OFF_BG + 2 * D_MODEL
N_GROUPS = 4
EXPERTS_PER_GROUP = 4
N_EXPERTS = N_GROUPS * EXPERTS_PER_GROUP
D_EXPERT = 512
EPS = 1e-6
NEG_BIG = -1e30
F32 = jnp.float32
BF16 = jnp.bfloat16

LANES = 128
SUBLANES = 8
VMEM_LIMIT_BYTES = 56 * 1024 * 1024

MAIN_AQ = 0
MAIN_KV = A_WIDTH
MAIN_MQ = MAIN_KV + 2 * A_KV_WIDTH
MAIN_MK = MAIN_MQ + M_WIDTH

LOG2E = 1.4426950408889634
Q_SCALE = A_HEAD_DIM ** -0.5 * LOG2E
ROUTER_G_LANE = N_EXPERTS
ROUTE_W = 4
PACKED = jnp.uint32
PACK_W = D_MODEL // 4
SC_WINDOW = 128
EXPERT_ROWS = 512
IN_PROJ_TILE = 512
MERGE_TILE = 1024
MERGE_ROWS = 256
FINISH_TILE = 1024
GATE_ROWS = M_HEADS * SUBLANES


def _sigmoid(z):
    return 1.0 / (1.0 + jnp.exp2(z * (-LOG2E)))


def _log_sigmoid(z):
    return jnp.minimum(z, 0.0) - jnp.log(1.0 + jnp.exp(-jnp.abs(z)))


def _gates_head_major(a):
    lead = a.shape[:-1]
    a = jnp.swapaxes(a.reshape(*lead, 4, M_HEADS), -1, -2)
    a = jnp.pad(a, [(0, 0)] * (a.ndim - 1) + [(0, SUBLANES - 4)])
    return a.reshape(*lead, GATE_ROWS)


def _dot(a, b):
    return jnp.dot(a, b, preferred_element_type=F32)


def _dot_nt(a, b):
    return lax.dot_general(a, b, (((1,), (1,)), ((), ())), preferred_element_type=F32)


def _dot_f32(a, b):
    return jnp.dot(a, b, preferred_element_type=F32, precision=lax.Precision.HIGHEST)


def _compiler_params(semantics):
    return pltpu.CompilerParams(dimension_semantics=semantics, vmem_limit_bytes=VMEM_LIMIT_BYTES)


def _ada_kernel(c_ref, w_ref, b_ref, o_ref):
    c = c_ref[...]
    o_ref[...] = _dot_f32(c * _sigmoid(c), w_ref[...]) + b_ref[...]


def _ada_mod(c, ada_w, ada_b):
    B, D = c.shape
    n = ada_w.shape[1] // D
    return pl.pallas_call(
        _ada_kernel,
        grid=(n,),
        in_specs=[
            pl.BlockSpec((B, D), lambda j: (0, 0)),
            pl.BlockSpec((D, D), lambda j: (0, j)),
            pl.BlockSpec((1, D), lambda j: (0, j)),
        ],
        out_specs=pl.BlockSpec((B, D), lambda j: (0, j)),
        out_shape=jax.ShapeDtypeStruct((B, n * D), F32),
        compiler_params=_compiler_params(("arbitrary",)),
        name="ada_mod",
    )(c, ada_w, ada_b.reshape(1, -1))


def _rms_mod(x, g, scale, shift):
    ms = jnp.mean(x * x, axis=-1, keepdims=True)
    return (x * lax.rsqrt(ms + EPS) * g) * (1.0 + scale) + shift


def _rope_block(xb, cos, sin_lo, sin_hi):
    half = ROPE_DIM // 2
    return xb * cos + pltpu.roll(xb, LANES - half, axis=1) * sin_lo + pltpu.roll(xb, half, axis=1) * sin_hi


def _in_proj_kernel(x_ref, mod_ref, g_ref, w_ref, wmo_ref, wbg_ref, wt_ref, gb_col_ref, cos_ref, slo_ref, shi_ref,
                    q_ref, kx_ref, vx_ref, mq_ref, mk_ref, vt_ref, og_ref, bg_ref, grow_ref):
    x = x_ref[0]
    tm = x.shape[0]
    h = _rms_mod(x, g_ref[...], mod_ref[0, 1:2, :], mod_ref[0, 0:1, :])
    hb = h.astype(BF16)
    cos, slo, shi = cos_ref[...], slo_ref[...], shi_ref[...]
    lane = lax.broadcasted_iota(jnp.int32, (tm, LANES), 1)
    left = lane < A_HEAD_DIM

    pj = _dot(hb, w_ref[:, MAIN_AQ:MAIN_AQ + A_WIDTH])
    for j in range(A_WIDTH // LANES):
        blk = _rope_block(pj[:, j * LANES:(j + 1) * LANES], cos, slo, shi)
        q_ref[0, :, j * LANES:(j + 1) * LANES] = (blk * Q_SCALE).astype(BF16)

    pj = _dot(hb, w_ref[:, MAIN_KV:MAIN_KV + 2 * A_KV_WIDTH])
    kk = _rope_block(pj[:, 0:LANES], cos, slo, shi)
    vv = pj[:, LANES:2 * LANES]
    for src, dst in ((kk, kx_ref), (vv, vx_ref)):
        swapped = pltpu.roll(src, A_HEAD_DIM, axis=1)
        zero = jnp.zeros_like(src)
        dst[0, :, 0 * LANES:1 * LANES] = jnp.where(left, src, zero).astype(BF16)
        dst[0, :, 1 * LANES:2 * LANES] = jnp.where(left, zero, swapped).astype(BF16)
        dst[0, :, 2 * LANES:3 * LANES] = jnp.where(left, swapped, zero).astype(BF16)
        dst[0, :, 3 * LANES:4 * LANES] = jnp.where(left, zero, src).astype(BF16)

    mq_ref[0] = _dot(hb, w_ref[:, MAIN_MQ:MAIN_MQ + M_WIDTH]).astype(BF16)
    mk_ref[0] = _dot(hb, w_ref[:, MAIN_MK:MAIN_MK + M_WIDTH]).astype(BF16)
    og_ref[0] = _sigmoid(_dot(hb, wmo_ref[...])).astype(BF16)
    for j in range(2 * D_MODEL // M_WIDTH):
        cols = slice(j * M_WIDTH, (j + 1) * M_WIDTH)
        bg_ref[0, :, cols] = _sigmoid(_dot(hb, wbg_ref[:, cols])).astype(BF16)

    tr = _dot_nt(wt_ref[...], hb)
    gr = tr[M_WIDTH:M_WIDTH + GATE_ROWS, :] + gb_col_ref[...]
    gtype = jnp.bitwise_and(lax.broadcasted_iota(jnp.int32, gr.shape, 0), SUBLANES - 1)
    gr = jnp.where(gtype == 1, _log_sigmoid(gr), jnp.where(gtype == 3, _log_sigmoid(gr), gr))
    vt = tr[0:M_WIDTH, :].astype(BF16)
    for c in range(tm // M_CHUNK):
        grow_ref[0, c] = gr[:, c * M_CHUNK:(c + 1) * M_CHUNK]
        vt_ref[0, c] = vt[:, c * M_CHUNK:(c + 1) * M_CHUNK]


def _in_proj(x, mod, norm_g, w_main, w_mo, w_bg, w_t, gate_b, rope_tabs, tm):
    B, S, D = x.shape
    nt = S // tm
    cos, slo, shi = rope_tabs
    tok = lambda w: pl.BlockSpec((1, tm, w), lambda b, i: (b, i, 0))
    const2 = lambda a: pl.BlockSpec(a.shape, lambda b, i: (0, 0))
    tab = pl.BlockSpec((tm, LANES), lambda b, i: (i, 0))
    out_shapes = (
        jax.ShapeDtypeStruct((B, S, A_WIDTH), BF16),
        jax.ShapeDtypeStruct((B, S, 4 * LANES), BF16),
        jax.ShapeDtypeStruct((B, S, 4 * LANES), BF16),
        jax.ShapeDtypeStruct((B, S, M_WIDTH), BF16),
        jax.ShapeDtypeStruct((B, S, M_WIDTH), BF16),
        jax.ShapeDtypeStruct((B, S // M_CHUNK, M_WIDTH, M_CHUNK), BF16),
        jax.ShapeDtypeStruct((B, S, M_WIDTH), BF16),
        jax.ShapeDtypeStruct((B, S, 2 * D_MODEL), BF16),
        jax.ShapeDtypeStruct((B, S // M_CHUNK, GATE_ROWS, M_CHUNK), F32),
    )
    chunked = lambda rows: pl.BlockSpec((1, tm // M_CHUNK, rows, M_CHUNK), lambda b, i: (b, i, 0, 0))
    out_specs = (
        tok(A_WIDTH), tok(4 * LANES), tok(4 * LANES), tok(M_WIDTH), tok(M_WIDTH), chunked(M_WIDTH), tok(M_WIDTH),
        tok(2 * D_MODEL), chunked(GATE_ROWS),
    )
    gb_col = gate_b.reshape(GATE_ROWS, 1)
    g2 = norm_g.reshape(1, D)
    return pl.pallas_call(
        _in_proj_kernel,
        grid=(B, nt),
        in_specs=[
            tok(D),
            pl.BlockSpec((1, 6, D), lambda b, i: (b, 0, 0)),
            const2(g2), const2(w_main), const2(w_mo), const2(w_bg), const2(w_t), const2(gb_col),
            tab, tab, tab,
        ],
        out_specs=out_specs,
        out_shape=out_shapes,
        compiler_params=_compiler_params(("parallel", "parallel")),
        name="in_proj",
    )(x, mod, g2, w_main, w_mo, w_bg, w_t, gb_col, cos, slo, shi)


def _attn_kernel(sink_ref, q_ref, kx_ref, vx_ref, o_ref):
    S = q_ref.shape[1]
    nb = S // BLOCK
    kw = 3 * BLOCK
    qi = lax.broadcasted_iota(jnp.int32, (BLOCK, kw), 0)
    ki = lax.broadcasted_iota(jnp.int32, (BLOCK, kw), 1)
    rel0 = ki - qi
    ones_b = jnp.ones((kw, LANES), BF16)
    left = lax.broadcasted_iota(jnp.int32, (BLOCK, LANES), 1) < A_HEAD_DIM

    def block(n, carry):
        q0 = pl.multiple_of(n * BLOCK, BLOCK)
        k0 = pl.multiple_of(jnp.clip((n - 1) * BLOCK, 0, S - kw), BLOCK)
        rel = rel0 + (k0 - q0)
        valid = jnp.abs(rel) <= WINDOW
        scores = []
        for hk in range(A_KV_HEADS):
            for j in range(A_GROUP // 2):
                col = (hk * (A_GROUP // 2) + j) * LANES
                qp = q_ref[0, pl.ds(q0, BLOCK), col:col + LANES]
                for side in range(2):
                    kk = kx_ref[0, pl.ds(k0, kw), (2 * hk + side) * LANES:(2 * hk + side + 1) * LANES]
                    scores.append(_dot_nt(qp, kk))
        for hk in range(A_KV_HEADS):
            for j in range(A_GROUP // 2):
                col = (hk * (A_GROUP // 2) + j) * LANES
                outs, dens = [], []
                for side in range(2):
                    vv = vx_ref[0, pl.ds(k0, kw), (2 * hk + side) * LANES:(2 * hk + side + 1) * LANES]
                    sk = sink_ref[hk * A_GROUP + 2 * j + side] * LOG2E
                    s = jnp.where(valid, scores[(hk * (A_GROUP // 2) + j) * 2 + side], NEG_BIG)
                    m = jnp.maximum(jnp.max(s, axis=-1, keepdims=True), sk)
                    p = jnp.exp2(s - m).astype(BF16)
                    od = _dot(p, jnp.concatenate([vv, ones_b], axis=1))
                    outs.append(od[:, 0:LANES])
                    dens.append(od[:, LANES:2 * LANES] + jnp.exp2(sk - m))
                o = (outs[0] + outs[1]) / jnp.where(left, dens[0], dens[1])
                o_ref[0, pl.ds(q0, BLOCK), col:col + LANES] = o.astype(BF16)
        return carry

    lax.fori_loop(0, nb, block, 0, unroll=8)


def _attention(q, kx, vx, sink):
    B, S, _ = q.shape
    seq = lambda w: pl.BlockSpec((1, S, w), lambda b: (b, 0, 0))
    return pl.pallas_call(
        _attn_kernel,
        grid=(B,),
        in_specs=[pl.BlockSpec(memory_space=pltpu.SMEM), seq(A_WIDTH), seq(4 * LANES), seq(4 * LANES)],
        out_specs=seq(A_WIDTH),
        out_shape=jax.ShapeDtypeStruct((B, S, A_WIDTH), BF16),
        compiler_params=_compiler_params(("parallel",)),
        name="window_attn",
    )(sink, q, kx, vx)


def _conv_silu(u_ref, w_ref, pad_ref, dst_ref, scale):
    S = u_ref.shape[1]
    pad_ref[0:SUBLANES, :] = jnp.zeros((SUBLANES, LANES), F32)
    pad_ref[S + SUBLANES:S + 2 * SUBLANES, :] = jnp.zeros((SUBLANES, LANES), F32)
    pad_ref[SUBLANES:S + SUBLANES, :] = u_ref[0].astype(F32)
    w0, w1, w2 = w_ref[0:1, :], w_ref[1:2, :], w_ref[2:3, :]
    for c in range(S // M_CHUNK):
        base = SUBLANES + c * M_CHUNK
        y = (pad_ref[base - 1:base - 1 + M_CHUNK, :] * w0 + pad_ref[base:base + M_CHUNK, :] * w1
             + pad_ref[base + 1:base + 1 + M_CHUNK, :] * w2)
        y = y * _sigmoid(y)
        dst_ref[c * M_CHUNK:(c + 1) * M_CHUNK, :] = y if scale == 1.0 else y * scale


def _split3(x):
    hi = x.astype(BF16)
    r = x - hi.astype(F32)
    mid = r.astype(BF16)
    lo = (r - mid.astype(F32)).astype(BF16)
    return [hi, mid, lo]


def _mlstm_kernel(mq_ref, mk_ref, vt_ref, og_ref, grow_ref, cwq_ref, cwk_ref, hgt_ref, o_ref,
                  pad_ref, qs_ref, ks_ref, rb_ref, ib_ref, rows_ref, cinc_ref, ninc_ref, cin_ref, sin_ref,
                  c_ref, n_ref):
    S = mq_ref.shape[1]
    nc = S // M_CHUNK
    L = M_CHUNK
    DH = M_HEAD_DIM
    C_GROUP = 8 if nc % 8 == 0 else 1
    _conv_silu(mq_ref, cwq_ref, pad_ref, qs_ref, 1.0)
    _conv_silu(mk_ref, cwk_ref, pad_ref, ks_ref, M_HEAD_DIM ** -0.5)

    s_i = lax.broadcasted_iota(jnp.int32, (L, L), 0)
    t_i = lax.broadcasted_iota(jnp.int32, (L, L), 1)
    tris = (s_i <= t_i, s_i >= t_i)
    eye = s_i == t_i
    row8 = lax.broadcasted_iota(jnp.int32, (SUBLANES, L), 0)
    one_if = lambda cond: jnp.where(cond, 1.0, 0.0)
    k_j = lax.broadcasted_iota(jnp.int32, (3 * L, 2 * L), 0) % L
    c_j = lax.broadcasted_iota(jnp.int32, (3 * L, 2 * L), 1)
    sum_rows = jnp.where(c_j < L, one_if(k_j <= c_j), one_if(k_j >= c_j - L)).astype(BF16)

    g_all = grow_ref[0].reshape(nc * SUBLANES, L)
    rb_ref[...] = _dot(jnp.concatenate(_split3(g_all), axis=1), sum_rows).reshape(nc, SUBLANES, 2 * L)
    ones_b = jnp.ones((2 * L, L), BF16)

    def phase_a_operands(c):
        gr = grow_ref[0, c]
        rb = rb_ref[c]
        vt = vt_ref[0, c].astype(F32)
        wvs, wks, diags = [], [], []
        for d in range(2):
            brow = rb[2 * d + 1:2 * d + 2, d * L:(d + 1) * L]
            blast = brow[:, L - 1:L] if d == 0 else brow[:, 0:1]
            ibr = gr[2 * d:2 * d + 1, :] - brow
            log_g = blast + ibr
            mg = jnp.max(log_g, axis=-1, keepdims=True)
            wk = jnp.exp(log_g - mg)
            wvs.append((vt * wk).astype(BF16))
            wks.append(wk)
            diags.append(jnp.concatenate(
                [jnp.where(eye, term.astype(F32), 0.0).astype(BF16) for term in _split3(ibr * LOG2E)[:2]], axis=1))
            rows_ref[c, 2 + d:3 + d, :] = brow
            rows_ref[c, 4 + d:5 + d, :] = jnp.broadcast_to(mg, (1, L))
            rows_ref[c, 6 + d:7 + d, :] = jnp.broadcast_to(blast, (1, L))
        wk8 = jnp.where(row8 == 0, wks[0], jnp.where(row8 == 1, wks[1], 0.0))
        return jnp.concatenate(diags, axis=0), jnp.concatenate(wvs, axis=0), wk8.astype(BF16)

    def phase_a_products(c, diag, wv, wk8):
        kb = ks_ref[pl.ds(pl.multiple_of(c * L, L), L), :].astype(BF16)
        ib = _dot(diag, ones_b)
        ib_ref[c] = ib
        for d in range(2):
            rows_ref[c, d:d + 1, :] = jnp.max(jnp.where(tris[d], ib[d * L:(d + 1) * L, :], NEG_BIG),
                                              axis=0, keepdims=True)
        cinc_ref[c] = _dot(wv, kb)
        ninc_ref[c] = _dot(wk8, kb)

    def phase_a(g, carry):
        chunks = [g * C_GROUP + i for i in range(C_GROUP)]
        operands = [phase_a_operands(c) for c in chunks]
        for c, ops in zip(chunks, operands):
            phase_a_products(c, *ops)
        return carry

    def phase_b(j, ms):
        new_ms = []
        for d, cc in ((0, j), (1, nc - 1 - j)):
            m = ms[d]
            half = slice(d * DH, (d + 1) * DH)
            cst = c_ref[half, :]
            n = n_ref[d:d + 1, :]
            cin_ref[cc, half, :] = cst.astype(BF16)
            sin_ref[cc, d:d + 1, :] = n
            sin_ref[cc, 2 + d:3 + d, :] = m
            mg = rows_ref[cc, 4 + d:5 + d, :]
            blast = rows_ref[cc, 6 + d:7 + d, :]
            m_new = jnp.maximum(blast + m, mg)
            decay = jnp.exp(blast + m - m_new)
            grow = jnp.exp(mg - m_new)
            c_ref[half, :] = decay * cst + grow * cinc_ref[cc, half, :]
            n_ref[d:d + 1, :] = decay * n + grow * ninc_ref[cc, d:d + 1, :]
            new_ms.append(m_new)
        return tuple(new_ms)

    def phase_c_products(c):
        r0 = pl.multiple_of(c * L, L)
        qb = qs_ref[pl.ds(r0, L), :].astype(BF16)
        qk_t = _dot_nt(ks_ref[pl.ds(r0, L), :].astype(BF16), qb)
        qc_t = _dot_nt(cin_ref[c], qb)
        qn = _dot_nt(sin_ref[c].astype(BF16), qb)
        return qk_t, qc_t, qn

    def phase_c_finish(c, qk_t, qc_t, qn):
        r0 = pl.multiple_of(c * L, L)
        sin = sin_ref[c]
        rows = rows_ref[c]
        ats, stats = [], []
        for d in range(2):
            m_in = sin[2 + d:3 + d, :] * LOG2E
            cm = jnp.maximum(m_in, rows[d:d + 1, :])
            a_t = qk_t * jnp.where(tris[d], jnp.exp2(ib_ref[c, d * L:(d + 1) * L, :] - cm), 0.0)
            w_inter = jnp.exp2(m_in - cm)
            den = w_inter * qn[d:d + 1, :] + jnp.sum(a_t, axis=0, keepdims=True)
            m_t = rows[2 + d:3 + d, :] * LOG2E + cm
            ats.append(a_t.astype(BF16))
            stats.append((w_inter, jnp.maximum(jnp.abs(den), jnp.exp2(-m_t))))
        av_t = _dot(vt_ref[0, c], jnp.concatenate(ats, axis=1))
        h_t = None
        for d in range(2):
            w_inter, den = stats[d]
            hd = (w_inter * qc_t[d * DH:(d + 1) * DH, :] + av_t[:, d * L:(d + 1) * L]) / den
            h_t = hd if h_t is None else h_t + hd
        y_t = h_t * lax.rsqrt(jnp.mean(h_t * h_t, axis=0, keepdims=True) + EPS) * hgt_ref[...]
        o_ref[0, pl.ds(r0, L), :] = (og_ref[0, pl.ds(r0, L), :].astype(F32) * y_t.T).astype(BF16)

    def phase_c(g, carry):
        chunks = [g * C_GROUP + i for i in range(C_GROUP)]
        products = [phase_c_products(c) for c in chunks]
        for c, prod in zip(chunks, products):
            phase_c_finish(c, *prod)
        return carry

    lax.fori_loop(0, nc // C_GROUP, phase_a, 0, unroll=2)
    c_ref[...] = jnp.zeros(c_ref.shape, F32)
    n_ref[...] = jnp.zeros(n_ref.shape, F32)
    m0 = jnp.zeros((1, LANES), F32)
    lax.fori_loop(0, nc, phase_b, (m0, m0), unroll=True)
    lax.fori_loop(0, nc // C_GROUP, phase_c, 0, unroll=2)


def _mlstm(mq, mk, vt, og, grow_h, conv_wq, conv_wk, head_g_t):
    B, S, _ = mq.shape
    nc = S // M_CHUNK
    head = pl.BlockSpec((1, S, M_HEAD_DIM), lambda b, h: (b, 0, h))
    cw = pl.BlockSpec((CONV_W, M_HEAD_DIM), lambda b, h: (0, h))
    return pl.pallas_call(
        _mlstm_kernel,
        grid=(B, M_HEADS),
        in_specs=[
            head, head,
            pl.BlockSpec((1, nc, M_HEAD_DIM, M_CHUNK), lambda b, h: (b, 0, h, 0)),
            head,
            pl.BlockSpec((1, nc, SUBLANES, M_CHUNK), lambda b, h: (b, 0, h, 0)),
            cw, cw,
            pl.BlockSpec((M_HEAD_DIM, LANES), lambda b, h: (h, 0)),
        ],
        out_specs=head,
        out_shape=jax.ShapeDtypeStruct((B, S, M_WIDTH), BF16),
        scratch_shapes=[
            pltpu.VMEM((S + 2 * SUBLANES, LANES), F32),
            pltpu.VMEM((S, M_HEAD_DIM), F32),
            pltpu.VMEM((S, M_HEAD_DIM), F32),
            pltpu.VMEM((nc, SUBLANES, 2 * M_CHUNK), F32),
            pltpu.VMEM((nc, 2 * M_CHUNK, M_CHUNK), F32),
            pltpu.VMEM((nc, SUBLANES, M_CHUNK), F32),
            pltpu.VMEM((nc, 2 * M_HEAD_DIM, M_HEAD_DIM), F32),
            pltpu.VMEM((nc, SUBLANES, M_HEAD_DIM), F32),
            pltpu.VMEM((nc, 2 * M_HEAD_DIM, M_HEAD_DIM), BF16),
            pltpu.VMEM((nc, SUBLANES, M_HEAD_DIM), F32),
            pltpu.VMEM((2 * M_HEAD_DIM, M_HEAD_DIM), F32),
            pltpu.VMEM((SUBLANES, M_HEAD_DIM), F32),
        ],
        compiler_params=_compiler_params(("parallel", "parallel")),
        name="mlstm",
    )(mq, mk, vt, og, grow_h, conv_wq, conv_wk, head_g_t)


def _route(logits):
    lane = lax.broadcasted_iota(jnp.int32, logits.shape, 1)
    lane_f = lane.astype(F32)
    big = float(LANES)
    is_g = (lane >= ROUTER_G_LANE) & (lane < ROUTER_G_LANE + N_GROUPS)
    gl = jnp.where(is_g, logits, NEG_BIG)
    gmax = jnp.max(gl, axis=-1, keepdims=True)
    gsum = jnp.sum(jnp.where(is_g, jnp.exp(gl - gmax), 0.0), axis=-1, keepdims=True)
    p_grp = 1.0 / gsum
    grp = jnp.min(jnp.where(is_g & (gl == gmax), lane_f - ROUTER_G_LANE, big), axis=-1, keepdims=True)
    in_grp = (lane < N_EXPERTS) & (jnp.right_shift(lane, 2).astype(F32) == grp)
    el = jnp.where(in_grp, logits, NEG_BIG)
    v1 = jnp.max(el, axis=-1, keepdims=True)
    i1 = jnp.min(jnp.where(in_grp & (el == v1), lane_f, big), axis=-1, keepdims=True)
    rest = in_grp & (lane_f != i1)
    el2 = jnp.where(rest, logits, NEG_BIG)
    v2 = jnp.max(el2, axis=-1, keepdims=True)
    i2 = jnp.min(jnp.where(rest & (el2 == v2), lane_f, big), axis=-1, keepdims=True)
    e21 = jnp.exp(v2 - v1)
    w1 = p_grp / (1.0 + e21)
    w2 = p_grp * e21 / (1.0 + e21)
    return jnp.where(lane == 0, i1, jnp.where(lane == 1, i2, jnp.where(lane == 2, w1, w2)))


def _pack_bf16_pairs(lo, hi):
    bits = lambda v: lax.bitcast_convert_type(v.astype(BF16).astype(F32), PACKED)
    return jnp.right_shift(bits(lo), PACKED(16)) | (bits(hi) & PACKED(0xFFFF0000))


def _unpack_bf16_pairs(w):
    return (lax.bitcast_convert_type(jnp.left_shift(w, PACKED(16)), F32),
            lax.bitcast_convert_type(w & PACKED(0xFFFF0000), F32))


def _pack_row_halves(y):
    return (_pack_bf16_pairs(y[:, 0:PACK_W], y[:, PACK_W:2 * PACK_W]),
            _pack_bf16_pairs(y[:, 2 * PACK_W:3 * PACK_W], y[:, 3 * PACK_W:4 * PACK_W]))


def _unpack_row_halves(a, b):
    return [*_unpack_bf16_pairs(a), *_unpack_bf16_pairs(b)]


def _merge_kernel(attn_ref, mo_ref, bg_ref, x_ref, mod_ref, wua_ref, wum_ref, wo_ref, g2_ref, wr_ref, br_ref,
                  x1_ref, ha_ref, hb_ref, rt_ref, ids_ref):
    tm = x_ref.shape[1]
    nparts = tm // MERGE_ROWS
    parts = [pl.ds(i * MERGE_ROWS, MERGE_ROWS) for i in range(nparts)]
    wr = wr_ref[...].astype(BF16)
    merged, h2s = {}, {}
    for step in range(nparts + 2):
        if step < nparts:
            h = parts[step]
            up_a = _dot(attn_ref[0, h, :], wua_ref[...])
            up_m = _dot(mo_ref[0, h, :], wum_ref[...])
            merged[step] = (bg_ref[0, h, 0:D_MODEL].astype(F32) * up_a
                            + bg_ref[0, h, D_MODEL:2 * D_MODEL].astype(F32) * up_m).astype(BF16)
        if 0 <= step - 1 < nparts:
            h = parts[step - 1]
            x1 = x_ref[0, h, :] + mod_ref[0, 2:3, :] * _dot(merged.pop(step - 1), wo_ref[...])
            x1_ref[0, h, :] = x1
            h2s[step - 1] = _rms_mod(x1, g2_ref[...], mod_ref[0, 4:5, :], mod_ref[0, 3:4, :])
        if 0 <= step - 2 < nparts:
            h = parts[step - 2]
            h2 = h2s.pop(step - 2)
            logits = _dot(h2.astype(BF16), wr) + br_ref[...]
            ha_ref[0, h, :], hb_ref[0, h, :] = _pack_row_halves(h2)
            route = _route(logits)
            rt_ref[0, h, :] = route[:, 0:ROUTE_W]
            ids_ref[0, :, h] = route.T[0:SUBLANES, :]


def _merge(attn, mo, bg, x, mod, wua, wum, wo, norm_g, w_router, b_router, tm):
    B, S, D = x.shape
    tok = lambda w: pl.BlockSpec((1, tm, w), lambda b, i: (b, i, 0))
    const2 = lambda a: pl.BlockSpec(a.shape, lambda b, i: (0, 0))
    g2 = norm_g.reshape(1, D)
    return pl.pallas_call(
        _merge_kernel,
        grid=(B, S // tm),
        in_specs=[
            tok(A_WIDTH), tok(M_WIDTH), tok(2 * D), tok(D),
            pl.BlockSpec((1, 6, D), lambda b, i: (b, 0, 0)),
            const2(wua), const2(wum), const2(wo), const2(g2), const2(w_router), const2(b_router),
        ],
        out_specs=(tok(D), tok(PACK_W), tok(PACK_W), tok(ROUTE_W),
                   pl.BlockSpec((1, SUBLANES, tm), lambda b, i: (b, 0, i))),
        out_shape=(
            jax.ShapeDtypeStruct((B, S, D), F32),
            jax.ShapeDtypeStruct((B, S, PACK_W), PACKED),
            jax.ShapeDtypeStruct((B, S, PACK_W), PACKED),
            jax.ShapeDtypeStruct((B, S, ROUTE_W), F32),
            jax.ShapeDtypeStruct((B, SUBLANES, S), F32),
        ),
        compiler_params=_compiler_params(("parallel", "parallel")),
        name="merge_route",
    )(attn, mo, bg, x, mod, wua, wum, wo, g2, w_router, b_router)


def _route_tables(ea, tmb):
    n = ea.shape[0]
    T = n // 2
    nblk = 2 * T // tmb + N_EXPERTS
    onehot = ea[None, :] == jnp.arange(N_EXPERTS, dtype=jnp.int32)[:, None]
    pieces = onehot.reshape(N_EXPERTS, n // LANES, LANES).astype(BF16)
    upto = (jnp.arange(LANES)[:, None] <= jnp.arange(LANES)[None, :]).astype(BF16)
    within = jnp.einsum("eps,st->ept", pieces, upto, preferred_element_type=F32).astype(jnp.int32)
    totals = within[:, :, -1]
    before = jnp.cumsum(totals, axis=1) - totals
    csum = (within + before[:, :, None]).reshape(N_EXPERTS, n)
    counts = before[:, -1] + totals[:, -1]
    padded = ((counts + tmb - 1) // tmb) * tmb
    ends = jnp.cumsum(padded)
    starts = ends - padded
    pos = jnp.sum(jnp.where(onehot, csum - 1 + starts[:, None], 0), axis=0).astype(jnp.int32)
    blk0 = jnp.arange(nblk, dtype=jnp.int32) * tmb
    bexp = jnp.minimum(jnp.sum((blk0[:, None] >= ends[None, :]).astype(jnp.int32), axis=1), N_EXPERTS - 1)
    nval = jnp.clip(starts[bexp] + counts[bexp] - blk0, 0, tmb)
    nval = jnp.where(blk0 < ends[-1], nval, 0).astype(jnp.int32)
    return pos, bexp.astype(jnp.int32), nval


def _sc_mesh():
    return plsc.VectorSubcoreMesh(core_axis_name="c", subcore_axis_name="s")


def _sc_dispatch(x, pos, n_rows):
    T = x.shape[0]
    nb = T // SC_WINDOW
    idx = pos.reshape(1, 2 * T)

    @pl.kernel(out_type=jax.ShapeDtypeStruct((n_rows, PACK_W), x.dtype), mesh=_sc_mesh(), scratch_types=[])
    def dispatch(x_hbm, i_hbm, o_hbm):
        def body(x_vmem, i_vmem):
            pltpu.sync_copy(x_vmem, o_hbm.at[i_vmem.at[0]])

        pltpu.emit_pipeline(
            body,
            grid=(2 * nb,),
            in_specs=[pl.BlockSpec((SC_WINDOW, PACK_W), index_map=lambda i: (i % nb, 0)),
                      pl.BlockSpec((1, SC_WINDOW), index_map=lambda i: (0, i))],
            out_specs=[],
            core_axis_name=("c", "s"),
            dimension_semantics=(pltpu.PARALLEL,),
        )(x_hbm, i_hbm)

    return dispatch(x, idx)


def _sc_combine(y, pos):
    n = pos.shape[0]
    idx = pos.reshape(1, n)

    @pl.kernel(out_type=jax.ShapeDtypeStruct((n, PACK_W), y.dtype), mesh=_sc_mesh(), scratch_types=[])
    def combine(y_hbm, i_hbm, o_hbm):
        def body(i_vmem, o_vmem):
            pltpu.sync_copy(y_hbm.at[i_vmem.at[0]], o_vmem)

        pltpu.emit_pipeline(
            body,
            grid=(n // SC_WINDOW,),
            in_specs=[pl.BlockSpec((1, SC_WINDOW), index_map=lambda i: (0, i))],
            out_specs=[pl.BlockSpec((SC_WINDOW, PACK_W), index_map=lambda i: (i, 0))],
            core_axis_name=("c", "s"),
            dimension_semantics=(pltpu.PARALLEL,),
        )(i_hbm, o_hbm)

    return combine(y, idx)


def _experts_kernel(bexp_ref, nval_ref, run_ref, nxt_ref, xa_ref, xb_ref, wg_hbm, wu_hbm, wd_hbm, ya_ref, yb_ref,
                    wg_f, wu_f, wd_f, wg_s, wu_s, wd_s, sems):
    i = pl.program_id(0)
    nv = nval_ref[i]
    new_expert = (i == 0) | (bexp_ref[i] != bexp_ref[jnp.maximum(i - 1, 0)])

    def weight_copies(e, slot):
        pairs = ((wg_hbm, wg_f), (wu_hbm, wu_f), (wd_hbm, wd_f))
        return [pltpu.make_async_copy(src.at[e], dst.at[slot], sems.at[slot, k]) for k, (src, dst) in enumerate(pairs)]

    @pl.when((nv > 0) & new_expert)
    def _():
        slot = lax.rem(run_ref[i], 2)

        @pl.when(i == 0)
        def _():
            for c in weight_copies(bexp_ref[i], slot):
                c.start()

        for c in weight_copies(bexp_ref[i], slot):
            c.wait()

        @pl.when(nxt_ref[i] >= 0)
        def _():
            for c in weight_copies(nxt_ref[i], 1 - slot):
                c.start()

        wg_s[...] = wg_f[slot].astype(BF16)
        wu_s[...] = wu_f[slot].astype(BF16)
        wd_s[...] = wd_f[slot].astype(BF16)

    @pl.when(nv > 0)
    def _():
        keep = lax.broadcasted_iota(jnp.int32, xa_ref.shape, 0) < nv
        pieces = [jnp.where(keep, piece, 0.0).astype(BF16) for piece in _unpack_row_halves(xa_ref[...], xb_ref[...])]
        x = jnp.concatenate(pieces, axis=1)
        half = x.shape[0] // 2
        gus = [(_dot(x[r:r + half], wg_s[...]), _dot(x[r:r + half], wu_s[...])) for r in (0, half)]
        hes = [((g * _sigmoid(g)) * u).astype(BF16) for g, u in gus]
        for r, he in zip((0, half), hes):
            ya_ref[r:r + half, :], yb_ref[r:r + half, :] = _pack_row_halves(_dot(he, wd_s[...]))


def _experts(xa, xb, bexp, nval, wg, wu, wd, tmb):
    n = xa.shape[0]
    nblk = n // tmb
    blk = jnp.arange(nblk, dtype=jnp.int32)
    starts_run = (nval > 0) & ((blk == 0) | (bexp != jnp.roll(bexp, 1)))
    run = (jnp.cumsum(starts_run.astype(jnp.int32)) - 1).astype(jnp.int32)
    first_after = lax.cummin(jnp.where(starts_run, blk, nblk), reverse=True)
    first_after = jnp.concatenate([first_after[1:], jnp.full((1,), nblk, jnp.int32)])
    nxt = jnp.where(first_after < nblk, bexp[jnp.minimum(first_after, nblk - 1)], -1).astype(jnp.int32)

    row = lambda: pl.BlockSpec((tmb, PACK_W), lambda i, *_: (i, 0))
    hbm = pl.BlockSpec(memory_space=pl.ANY)
    grid_spec = pltpu.PrefetchScalarGridSpec(
        num_scalar_prefetch=4,
        grid=(nblk,),
        in_specs=[row(), row(), hbm, hbm, hbm],
        out_specs=[row(), row()],
        scratch_shapes=[pltpu.VMEM((2, D_MODEL, D_EXPERT), F32), pltpu.VMEM((2, D_MODEL, D_EXPERT), F32),
                        pltpu.VMEM((2, D_EXPERT, D_MODEL), F32),
                        pltpu.VMEM((D_MODEL, D_EXPERT), BF16), pltpu.VMEM((D_MODEL, D_EXPERT), BF16),
                        pltpu.VMEM((D_EXPERT, D_MODEL), BF16),
                        pltpu.SemaphoreType.DMA((2, 3))],
    )
    out = jax.ShapeDtypeStruct((n, PACK_W), xa.dtype)
    return pl.pallas_call(
        _experts_kernel, grid_spec=grid_spec, out_shape=(out, out),
        compiler_params=_compiler_params(("arbitrary",)),
        name="experts",
    )(bexp, nval, run, nxt, xa, xb, wg, wu, wd)


def _finish_kernel(a0_ref, b0_ref, a1_ref, b1_ref, rt_ref, x1_ref, mod_ref, fg_ref, o_ref):
    y0 = jnp.concatenate(_unpack_row_halves(a0_ref[...], b0_ref[...]), axis=1)
    y1 = jnp.concatenate(_unpack_row_halves(a1_ref[...], b1_ref[...]), axis=1)
    rt = rt_ref[0]
    moe = rt[:, 2:3] * y0 + rt[:, 3:4] * y1
    xo = x1_ref[0] + mod_ref[0, 5:6, :] * moe
    ms = jnp.mean(xo * xo, axis=-1, keepdims=True)
    o_ref[0] = xo * lax.rsqrt(ms + EPS) * fg_ref[...]


def _finish(ca, cb, rt, x1, mod, final_g, tm):
    B, S, D = x1.shape
    nt = S // tm
    second = B * nt
    half = lambda k: pl.BlockSpec((tm, PACK_W), lambda b, i: (k * second + b * nt + i, 0))
    tok = lambda w: pl.BlockSpec((1, tm, w), lambda b, i: (b, i, 0))
    fg = final_g.reshape(1, D)
    return pl.pallas_call(
        _finish_kernel,
        grid=(B, nt),
        in_specs=[half(0), half(0), half(1), half(1), tok(ROUTE_W), tok(D),
                  pl.BlockSpec((1, 6, D), lambda b, i: (b, 0, 0)),
                  pl.BlockSpec((1, D), lambda b, i: (0, 0))],
        out_specs=tok(D),
        out_shape=jax.ShapeDtypeStruct((B, S, D), F32),
        compiler_params=_compiler_params(("parallel", "parallel")),
        name="finish",
    )(ca, cb, ca, cb, rt, x1, mod, fg)


def _moe_tables(ids):
    B, _, S = ids.shape
    ea = ids[:, 0:2, :].astype(jnp.int32).transpose(1, 0, 2).reshape(2 * B * S)
    return _route_tables(ea, EXPERT_ROWS)


def _moe_dispatch(ha, hb, tables):
    B, S, _ = ha.shape
    T = B * S
    pos, bexp, nval = tables
    n_rows = 2 * T + N_EXPERTS * EXPERT_ROWS
    xa = _sc_dispatch(ha.reshape(T, PACK_W), pos, n_rows)
    xb = _sc_dispatch(hb.reshape(T, PACK_W), pos, n_rows)
    return xa, xb, pos, bexp, nval


def _rope_tables(S):
    half = ROPE_DIM // 2
    inv_freq = 1.0 / (ROPE_THETA ** (jnp.arange(half, dtype=F32) * 2.0 / ROPE_DIM))
    ang = jnp.arange(S, dtype=F32)[:, None] * inv_freq[None, :]
    cos, sin = jnp.cos(ang), jnp.sin(ang)
    zeros = jnp.zeros((S, A_HEAD_DIM - ROPE_DIM), F32)
    z8 = jnp.zeros((S, half), F32)
    cos_h = jnp.concatenate([cos, cos, jnp.ones_like(zeros)], axis=-1)
    slo_h = jnp.concatenate([-sin, z8, zeros], axis=-1)
    shi_h = jnp.concatenate([z8, sin, zeros], axis=-1)
    rep = LANES // A_HEAD_DIM
    return tuple(jnp.tile(t, (1, rep)) for t in (cos_h, slo_h, shi_h))


def _mixing(x, mod, p, tabs):
    B, S, D = x.shape
    mod = mod.reshape(B, 6, D)
    q, kx, vx, mq, mk, vt, og, bg, grow = _in_proj(
        x, mod, p["norm1_g"], p["w_main"], p["w_mo"], p["w_bg"], p["w_t"], p["m_gate_b"], tabs,
        tm=min(IN_PROJ_TILE, S))
    attn = _attention(q, kx, vx, p["attn_sink"])
    mo = _mlstm(mq, mk, vt, og, grow, p["conv_wq"], p["conv_wk"], p["head_norm_g_t"])
    x1, ha, hb, rt, ids = _merge(attn, mo, bg, x, mod, p["w_up_attn"], p["w_up_mlstm"], p["w_out"], p["norm2_g"],
                                 p["w_router"], p["b_router"], tm=min(MERGE_TILE, S))
    return dict(x1=x1, rt=rt, mod=mod, ha=ha, hb=hb, tables=_moe_tables(ids))


def kernel(x_prompt, x_sample, c_prompt, c_sample, ada_w, ada_b, norm1_g, w_in, conv_w, m_gate_b, attn_sink,
           head_norm_g, w_up_attn, w_up_mlstm, w_out, norm2_g, rg_w, rg_b, re_w, re_b, w_gate, w_up, w_down,
           final_norm_g):
    assert ada_w.shape[0] == 1, "single-layer trunk"
    w_in0 = w_in[0]
    w_g = w_in0[:, OFF_MG:OFF_BG]
    pad = LANES - N_EXPERTS - N_GROUPS
    p = dict(
        ada_w=ada_w[0], ada_b=ada_b[0], norm1_g=norm1_g[0],
        w_main=w_in0[:, :OFF_MV].astype(BF16), w_mo=w_in0[:, OFF_MO:OFF_MG].astype(BF16),
        w_bg=w_in0[:, OFF_BG:].astype(BF16),
        w_t=jnp.concatenate([w_in0[:, OFF_MV:OFF_MO], _gates_head_major(w_g)], axis=1).T.astype(BF16),
        m_gate_b=_gates_head_major(m_gate_b[0]), attn_sink=attn_sink[0],
        conv_wq=conv_w[0, :, :M_WIDTH], conv_wk=conv_w[0, :, M_WIDTH:],
        head_norm_g_t=jnp.broadcast_to(head_norm_g[0][:, None], (M_WIDTH, LANES)),
        w_up_attn=w_up_attn[0].astype(BF16), w_up_mlstm=w_up_mlstm[0].astype(BF16), w_out=w_out[0].astype(BF16),
        norm2_g=norm2_g[0],
        w_router=jnp.pad(jnp.concatenate([re_w[0], rg_w[0]], axis=1), ((0, 0), (0, pad))),
        b_router=jnp.pad(jnp.concatenate([re_b[0], rg_b[0]]), (0, pad)).reshape(1, LANES),
        w_gate=w_gate[0], w_up=w_up[0], w_down=w_down[0],
        final_norm_g=final_norm_g,
    )
    tabs = _rope_tables(x_prompt.shape[1])
    nbp = x_prompt.shape[0]
    mod = _ada_mod(jnp.concatenate([c_prompt, c_sample], axis=0), p["ada_w"], p["ada_b"])
    xs, mods = [x_prompt, x_sample], [mod[:nbp], mod[nbp:]]

    def experts(g):
        xa, xb, _, bexp, nval = g["routed"]
        return _experts(xa, xb, bexp, nval, p["w_gate"], p["w_up"], p["w_down"], EXPERT_ROWS)

    first, second = sorted(range(2), key=lambda i: -xs[i].shape[0])
    groups = {first: _mixing(xs[first], mods[first], p, tabs)}
    (groups[first]["x1"], groups[first]["tables"]), x_second = lax.optimization_barrier(
        ((groups[first]["x1"], groups[first]["tables"]), xs[second]))
    groups[second] = _mixing(x_second, mods[second], p, tabs)
    for g in groups.values():
        g["routed"] = _moe_dispatch(g["ha"], g["hb"], g["tables"])
    ys = {first: experts(groups[first])}
    ys[first], groups[second]["routed"] = lax.optimization_barrier((ys[first], groups[second]["routed"]))
    ys[second] = experts(groups[second])
    combined = {i: tuple(_sc_combine(y, groups[i]["routed"][2]) for y in ys[i]) for i in range(2)}

    def finish(i, c):
        g = groups[i]
        return _finish(*c, g["rt"], g["x1"], g["mod"], p["final_norm_g"], tm=min(FINISH_TILE, g["x1"].shape[1]))

    outs = {first: finish(first, combined[first])}
    outs[first], combined[second] = lax.optimization_barrier((outs[first], combined[second]))
    outs[second] = finish(second, combined[second])
    return outs[0], outs[1]
```
